```python
import jax, jax.numpy as jnp
from jax import lax
import numpy as np

D_MODEL = 1024
BATCH = 4
SEQ = 8192
DEPTH = 1

HG_HEADS = 8
HG_KEY_DIM = 128
HG_VAL_DIM = D_MODEL // HG_HEADS
HG_CHUNK = 64
FOX_HEADS = 16
FOX_HEAD_DIM = 64
FOX_BLOCK = 128
FOX_F_BIAS_INIT = 2.0
D_FF = 2816
CONV_WIDTH = 3
EPS = 1e-6

HG_QK = HG_HEADS * HG_KEY_DIM
HG_V = HG_HEADS * HG_VAL_DIM
FOX_W = FOX_HEADS * FOX_HEAD_DIM
SPLITS = (HG_QK, HG_QK, HG_V, HG_V, FOX_W, FOX_W, FOX_W, FOX_HEADS, D_MODEL, D_MODEL)
D_IN = sum(SPLITS)

kernel_name = 'hgrn2_fox_gated_hybrid_block'


def rms_norm(x, g):
    xf = x.astype(jnp.float32)
    y = xf * lax.rsqrt(jnp.mean(xf * xf, axis=-1, keepdims=True) + EPS)
    return (y * g.astype(jnp.float32)).astype(x.dtype)


def hgrn2_mixer(q, f_logit, i, g, lb, g_norm):
    B, S, _ = q.shape
    n_chunks = S // HG_CHUNK
    f32 = jnp.float32
    q = jax.nn.silu(q.astype(f32))
    f = lb + (1.0 - lb) * jax.nn.sigmoid(f_logit.astype(f32))
    k = 1.0 - f
    log_f = jnp.log(f)

    def chunks(t, d):
        return t.astype(f32).reshape(B, n_chunks, HG_CHUNK, HG_HEADS, d).transpose(1, 0, 3, 2, 4)

    xs = (chunks(q, HG_KEY_DIM), chunks(k, HG_KEY_DIM), chunks(log_f, HG_KEY_DIM), chunks(i, HG_VAL_DIM))
    causal = jnp.tril(jnp.ones((HG_CHUNK, HG_CHUNK), dtype=bool))[:, :, None]

    def step(state, inp):
        qc, kc, gc, vc = inp
        b = jnp.cumsum(gc, axis=2)
        o_inter = jnp.einsum('bhtk,bhkv->bhtv', qc * jnp.exp(b), state)
        rel = b[:, :, :, None, :] - b[:, :, None, :, :]
        decay = jnp.exp(jnp.where(causal, rel, -jnp.inf))
        scores = jnp.einsum('bhtsk,bhsk->bhts', qc[:, :, :, None, :] * decay, kc)
        o_intra = jnp.einsum('bhts,bhsv->bhtv', scores, vc)
        b_end = b[:, :, -1:, :]
        state = jnp.exp(b_end[:, :, 0, :])[..., None] * state + jnp.einsum('bhsk,bhsv->bhkv', kc * jnp.exp(b_end - b), vc)
        return state, o_inter + o_intra

    state0 = jnp.zeros((B, HG_HEADS, HG_KEY_DIM, HG_VAL_DIM), f32)
    _, o = lax.scan(step, state0, xs)
    o = o.transpose(1, 0, 3, 2, 4).reshape(B, S, HG_HEADS, HG_VAL_DIM)
    o = rms_norm(o, g_norm) * jax.nn.silu(g.astype(f32)).reshape(B, S, HG_HEADS, HG_VAL_DIM)
    return o.reshape(B, S, HG_V).astype(i.dtype)


def fox_mixer(q, k, v, f_logit, f_bias):
    B, S, _ = q.shape
    n_blocks = S // FOX_BLOCK
    f32 = jnp.float32
    q = q.reshape(B, S, FOX_HEADS, FOX_HEAD_DIM) * FOX_HEAD_DIM ** -0.5
    k = k.reshape(B, S, FOX_HEADS, FOX_HEAD_DIM)
    v = v.reshape(B, S, FOX_HEADS, FOX_HEAD_DIM)
    log_f = jax.nn.log_sigmoid(f_logit.astype(f32) + f_bias.astype(f32))
    c = jnp.cumsum(log_f, axis=1).transpose(0, 2, 1)
    q_blocks = q.reshape(B, n_blocks, FOX_BLOCK, FOX_HEADS, FOX_HEAD_DIM).transpose(1, 0, 2, 3, 4)
    c_blocks = c.reshape(B, FOX_HEADS, n_blocks, FOX_BLOCK).transpose(2, 0, 1, 3)
    key_pos = jnp.arange(S)

    def attend(args):
        blk, qb, cb = args
        logits = jnp.einsum('bqhd,bshd->bhqs', qb, k).astype(f32)
        logits = logits + (cb[..., None] - c[:, :, None, :])
        q_pos = blk * FOX_BLOCK + jnp.arange(FOX_BLOCK)
        mask = key_pos[None, :] <= q_pos[:, None]
        p = jax.nn.softmax(jnp.where(mask, logits, -jnp.inf), axis=-1)
        return jnp.einsum('bhqs,bshd->bqhd', p.astype(v.dtype), v)

    o = lax.map(attend, (jnp.arange(n_blocks), q_blocks, c_blocks))
    return o.transpose(1, 0, 2, 3, 4).reshape(B, S, FOX_W)


def conv_glu_ffn(x, w_up, conv_w, conv_b, w_down):
    S = x.shape[1]
    u = x @ w_up
    u_pad = jnp.pad(u, ((0, 0), (CONV_WIDTH - 1, 0), (0, 0)))
    acc = conv_b
    for j in range(CONV_WIDTH):
        acc = acc + conv_w[j] * u_pad[:, j:j + S]
    gate, val = jnp.split(acc, 2, axis=-1)
    return (jax.nn.gelu(gate, approximate=False) * val) @ w_down


def setup_inputs(seed: int = 0) -> dict:
    key = jax.random.key(seed)
    ks = jax.random.split(key, 16)
    f32 = jnp.float32

    def dense(k, shape, fan_in):
        return jax.random.normal(k, shape, f32) * fan_in ** -0.5

    def gain(k, shape):
        return 1.0 + 0.02 * jax.random.normal(k, shape, f32)

    return {
        'x': jax.random.normal(ks[0], (BATCH, SEQ, D_MODEL), f32),
        'norm_mix': gain(ks[1], (DEPTH, D_MODEL)),
        'w_in': dense(ks[2], (DEPTH, D_MODEL, D_IN), D_MODEL),
        'fox_f_bias': FOX_F_BIAS_INIT + 0.5 * jax.random.normal(ks[3], (DEPTH, FOX_HEADS), f32),
        'hg_lb_logits': 0.1 * jax.random.normal(ks[4], (DEPTH + 1, HG_QK), f32),
        'hg_norm': gain(ks[5], (DEPTH, HG_VAL_DIM)),
        'w_branch_a': dense(ks[6], (DEPTH, HG_V, D_MODEL), HG_V),
        'w_branch_b': dense(ks[7], (DEPTH, FOX_W, D_MODEL), FOX_W),
        'w_out': dense(ks[8], (DEPTH, D_MODEL, D_MODEL), D_MODEL),
        'norm_ffn': gain(ks[9], (DEPTH, D_MODEL)),
        'w_up': dense(ks[10], (DEPTH, D_MODEL, 2 * D_FF), D_MODEL),
        'conv_w': dense(ks[11], (DEPTH, CONV_WIDTH, 2 * D_FF), CONV_WIDTH),
        'conv_b': 0.02 * jax.random.normal(ks[12], (DEPTH, 2 * D_FF), f32),
        'w_down': dense(ks[13], (DEPTH, D_FF, D_MODEL), D_FF),
        'norm_final': gain(ks[14], (D_MODEL,)),
    }


def reference(x, norm_mix, w_in, fox_f_bias, hg_lb_logits, hg_norm, w_branch_a, w_branch_b, w_out, norm_ffn, w_up, conv_w, conv_b, w_down, norm_final):
    h = x
    lb_all = jnp.cumsum(jax.nn.softmax(hg_lb_logits.astype(jnp.float32), axis=0), axis=0)
    cuts = np.cumsum(SPLITS)[:-1].tolist()
    for layer in range(DEPTH):
        n = rms_norm(h, norm_mix[layer])
        proj = n @ w_in[layer]
        hq, hf, hi, hg, fq, fk, fv, ff, ga, gb = jnp.split(proj, cuts, axis=-1)
        o_a = hgrn2_mixer(hq, hf, hi, hg, lb_all[layer], hg_norm[layer])
        o_b = fox_mixer(fq, fk, fv, ff, fox_f_bias[layer])
        merged = jax.nn.sigmoid(ga) * (o_a @ w_branch_a[layer]) + jax.nn.sigmoid(gb) * (o_b @ w_branch_b[layer])
        h = h + merged @ w_out[layer]
        h = h + conv_glu_ffn(rms_norm(h, norm_ffn[layer]), w_up[layer], conv_w[layer], conv_b[layer], w_down[layer])
    return rms_norm(h, norm_final)
```

```python
import functools

import jax
import jax.numpy as jnp
from jax import lax
from jax.experimental import pallas as pl
from jax.experimental.pallas import tpu as pltpu

F32 = jnp.float32
BF16 = jnp.bfloat16

EPS = 1e-6
HG_HEADS = 8
HG_DIM = 128
HG_CHUNK = 64
HG_SUB = 16
FOX_HEADS = 16
FOX_DIM = 64
CONV_WIDTH = 3
HALO = 8
NEG_INF = float("-inf")

VMEM_LIMIT = 56 * 1024 * 1024


def _cparams(sem):
    return pltpu.CompilerParams(dimension_semantics=sem, vmem_limit_bytes=VMEM_LIMIT)


def _sigmoid(x):
    return 1.0 / (1.0 + jnp.exp(-x))


def _silu(x):
    return x * _sigmoid(x)


def _log_sigmoid(x):
    return jnp.minimum(x, 0.0) - jnp.log1p(jnp.exp(-jnp.abs(x)))


def _split3(x):
    x1 = x.astype(BF16)
    r1 = x - x1.astype(F32)
    x2 = r1.astype(BF16)
    x3 = (r1 - x2.astype(F32)).astype(BF16)
    return x1, x2, x3


def _rmsnorm_kernel(x_ref, g_ref, o_ref):
    x = x_ref[...]
    y = x * lax.rsqrt(jnp.mean(x * x, axis=-1, keepdims=True) + EPS)
    o_ref[...] = (y * g_ref[...]).astype(o_ref.dtype)


def _rmsnorm(x2d, gain, tm):
    t, d = x2d.shape
    return pl.pallas_call(
        _rmsnorm_kernel,
        grid=(t // tm,),
        in_specs=[pl.BlockSpec((tm, d), lambda i: (i, 0)), pl.BlockSpec((1, d), lambda i: (0, 0))],
        out_specs=pl.BlockSpec((tm, d), lambda i: (i, 0)),
        out_shape=jax.ShapeDtypeStruct((t, d), BF16),
        compiler_params=_cparams(("parallel",)),
        name="rmsnorm_in",
    )(x2d, gain.reshape(1, d))


def _hg_proj_kernel(n_ref, wq_ref, wf_ref, wi_ref, wg_ref, lbl_ref, q_ref, lf_ref, k_ref, v_ref, g_ref):
    n = n_ref[...]
    dot = lambda w: jnp.dot(n, w[...], preferred_element_type=F32)
    q_ref[...] = _silu(dot(wq_ref))
    lbl = lbl_ref[...]
    e = jnp.exp(lbl - jnp.max(lbl, axis=0, keepdims=True))
    lb = e[0:1] / jnp.sum(e, axis=0, keepdims=True)
    sig = _sigmoid(dot(wf_ref))
    f = lb + (1.0 - lb) * sig
    lf_ref[...] = jnp.log(f)
    k_ref[...] = 1.0 - f
    v_ref[...] = dot(wi_ref)
    g_ref[...] = _silu(dot(wg_ref))


def _hg_proj(n2d, wq, wf, wi, wg, lb_logits, tm, tn):
    t, d = n2d.shape
    w = wq.shape[1]
    wspec = pl.BlockSpec((d, tn), lambda i, j: (0, j))
    ospec = pl.BlockSpec((tm, tn), lambda i, j: (i, j))
    oshape = jax.ShapeDtypeStruct((t, w), F32)
    return pl.pallas_call(
        _hg_proj_kernel,
        grid=(t // tm, w // tn),
        in_specs=[pl.BlockSpec((tm, d), lambda i, j: (i, 0)), wspec, wspec, wspec, wspec,
                  pl.BlockSpec((lb_logits.shape[0], tn), lambda i, j: (0, j))],
        out_specs=[ospec] * 5,
        out_shape=[oshape] * 5,
        compiler_params=_cparams(("parallel", "arbitrary")),
        name="hgrn_proj",
    )(n2d, wq, wf, wi, wg, lb_logits)


def _fox_proj_kernel(n_ref, wq_ref, wk_ref, wv_ref, q_ref, k_ref, v_ref):
    n = n_ref[...]
    dot = lambda w: jnp.dot(n, w[...], preferred_element_type=F32)
    q_ref[...] = (dot(wq_ref) * (FOX_DIM ** -0.5)).astype(q_ref.dtype)
    k_ref[...] = dot(wk_ref).astype(k_ref.dtype)
    v_ref[...] = dot(wv_ref).astype(v_ref.dtype)


def _fox_proj(n2d, wq, wk, wv, tm, tn):
    t, d = n2d.shape
    w = wq.shape[1]
    wspec = pl.BlockSpec((d, tn), lambda i, j: (0, j))
    ospec = pl.BlockSpec((tm, tn), lambda i, j: (i, j))
    oshape = jax.ShapeDtypeStruct((t, w), BF16)
    return pl.pallas_call(
        _fox_proj_kernel,
        grid=(t // tm, w // tn),
        in_specs=[pl.BlockSpec((tm, d), lambda i, j: (i, 0)), wspec, wspec, wspec],
        out_specs=[ospec] * 3,
        out_shape=[oshape] * 3,
        compiler_params=_cparams(("parallel", "arbitrary")),
        name="fox_proj",
    )(n2d, wq, wk, wv)


def _gate_proj_kernel(n_ref, wa_ref, wb_ref, ga_ref, gb_ref):
    n = n_ref[...]
    ga_ref[...] = _sigmoid(jnp.dot(n, wa_ref[...], preferred_element_type=F32))
    gb_ref[...] = _sigmoid(jnp.dot(n, wb_ref[...], preferred_element_type=F32))


def _gate_proj(n2d, wa, wb, tm, tn):
    t, d = n2d.shape
    w = wa.shape[1]
    wspec = pl.BlockSpec((d, tn), lambda i, j: (0, j))
    ospec = pl.BlockSpec((tm, tn), lambda i, j: (i, j))
    oshape = jax.ShapeDtypeStruct((t, w), F32)
    return pl.pallas_call(
        _gate_proj_kernel,
        grid=(t // tm, w // tn),
        in_specs=[pl.BlockSpec((tm, d), lambda i, j: (i, 0)), wspec, wspec],
        out_specs=[ospec] * 2,
        out_shape=[oshape] * 2,
        compiler_params=_cparams(("parallel", "arbitrary")),
        name="gate_proj",
    )(n2d, wa, wb)


def _fgate_kernel(n_ref, w_ref, wt_ref, bias_row_ref, bias_col_ref, ccol_ref, crow_ref, carry_col, carry_row):
    @pl.when(pl.program_id(1) == 0)
    def _():
        carry_col[...] = jnp.zeros_like(carry_col)
        carry_row[...] = jnp.zeros_like(carry_row)

    n = n_ref[...]
    tm = n.shape[0]
    r = lax.broadcasted_iota(jnp.int32, (tm, tm), 0)
    c = lax.broadcasted_iota(jnp.int32, (tm, tm), 1)
    lower = (c <= r).astype(BF16)
    upper = (r <= c).astype(BF16)

    lf = _log_sigmoid(jnp.dot(n, w_ref[...], preferred_element_type=F32) + bias_row_ref[...])
    cs = carry_col[...]
    for piece in _split3(lf):
        cs = cs + jnp.dot(lower, piece, preferred_element_type=F32)
    ccol_ref[0] = cs
    carry_col[...] = cs[tm - 1:tm, :]

    lft = _log_sigmoid(
        lax.dot_general(wt_ref[...], n, (((1,), (1,)), ((), ())), preferred_element_type=F32)
        + bias_col_ref[...])
    cst = carry_row[...]
    for piece in _split3(lft):
        cst = cst + jnp.dot(piece, upper, preferred_element_type=F32)
    crow_ref[0] = cst
    carry_row[...] = cst[:, tm - 1:tm]


def _fgate(n3d, w, wt, bias, tm):
    b, s, d = n3d.shape
    h = w.shape[1]
    return pl.pallas_call(
        _fgate_kernel,
        grid=(b, s // tm),
        in_specs=[pl.BlockSpec((None, tm, d), lambda i, j: (i, j, 0)),
                  pl.BlockSpec((d, h), lambda i, j: (0, 0)),
                  pl.BlockSpec((h, d), lambda i, j: (0, 0)),
                  pl.BlockSpec((1, h), lambda i, j: (0, 0)),
                  pl.BlockSpec((h, 1), lambda i, j: (0, 0))],
        out_specs=[pl.BlockSpec((1, tm, h), lambda i, j: (i, j, 0)),
                   pl.BlockSpec((1, h, tm), lambda i, j: (i, 0, j))],
        out_shape=[jax.ShapeDtypeStruct((b, s, h), F32), jax.ShapeDtypeStruct((b, h, s), F32)],
        scratch_shapes=[pltpu.VMEM((1, h), F32), pltpu.VMEM((h, 1), F32)],
        compiler_params=_cparams(("parallel", "arbitrary")),
        name="fox_fgate_cumsum",
    )(n3d, w, wt, bias.reshape(1, h), bias.reshape(h, 1))


def _hgrn_chunk(q, lf, k, v, state_t):
    c = HG_CHUNK
    n_sub = c // HG_SUB
    r_i = lax.broadcasted_iota(jnp.int32, (c, c), 0)
    c_i = lax.broadcasted_iota(jnp.int32, (c, c), 1)
    lower = (c_i <= r_i).astype(BF16)
    b = jnp.zeros((c, HG_DIM), F32)
    for piece in _split3(lf):
        b = b + jnp.dot(lower, piece, preferred_element_type=F32)

    qe = (q * jnp.exp(b)).astype(BF16)
    o = lax.dot_general(qe, state_t.astype(BF16), (((1,), (1,)), ((), ())), preferred_element_type=F32)

    o_parts = []
    t_iota = lax.broadcasted_iota(jnp.int32, (HG_SUB, HG_DIM), 0)
    for i in range(n_sub):
        lo = i * HG_SUB
        q_i = q[lo:lo + HG_SUB]
        b_i = b[lo:lo + HG_SUB]
        o_i = o[lo:lo + HG_SUB]
        if i > 0:
            ref = b[lo - 1:lo]
            q_t = (q_i * jnp.exp(b_i - ref)).astype(BF16)
            k_t = (k[:lo] * jnp.exp(ref - b[:lo])).astype(BF16)
            sc = lax.dot_general(q_t, k_t, (((1,), (1,)), ((), ())), preferred_element_type=F32)
            o_i = o_i + jnp.dot(sc.astype(BF16), v[:lo].astype(BF16), preferred_element_type=F32)
        for s in range(HG_SUB):
            row = lo + s
            rel = jnp.where(t_iota >= s, b_i - b[row:row + 1], NEG_INF)
            p = q_i * jnp.exp(rel) * k[row:row + 1]
            o_i = o_i + jnp.sum(p, axis=-1, keepdims=True) * v[row:row + 1]
        o_parts.append(o_i)
    o = jnp.concatenate(o_parts, axis=0)

    b_end = b[c - 1:c]
    k_dec = (k * jnp.exp(b_end - b)).astype(BF16)
    upd = lax.dot_general(v.astype(BF16), k_dec, (((0,), (0,)), ((), ())), preferred_element_type=F32)
    return o, state_t * jnp.exp(b_end) + upd


def _hgrn_kernel(q_ref, lf_ref, k_ref, v_ref, g_ref, gn_ref, o_ref, state_ref, *, n_chunks):
    @pl.when(pl.program_id(2) == 0)
    def _():
        state_ref[...] = jnp.zeros_like(state_ref)

    def body(ci, carry):
        rows = pl.ds(pl.multiple_of(ci * HG_CHUNK, HG_CHUNK), HG_CHUNK)
        o, new_state = _hgrn_chunk(q_ref[rows, :], lf_ref[rows, :], k_ref[rows, :], v_ref[rows, :], state_ref[...])
        state_ref[...] = new_state
        y = o * lax.rsqrt(jnp.mean(o * o, axis=-1, keepdims=True) + EPS)
        o_ref[rows, :] = (y * gn_ref[...] * g_ref[rows, :]).astype(o_ref.dtype)
        return carry

    lax.fori_loop(0, n_chunks, body, 0)


def _hgrn(q, lf, k, v, g, g_norm, tc):
    b, s, w = q.shape
    heads = w // HG_DIM
    spec = pl.BlockSpec((None, tc, HG_DIM), lambda bi, hi, ti: (bi, ti, hi))
    return pl.pallas_call(
        functools.partial(_hgrn_kernel, n_chunks=tc // HG_CHUNK),
        grid=(b, heads, s // tc),
        in_specs=[spec] * 5 + [pl.BlockSpec((1, HG_DIM), lambda bi, hi, ti: (0, 0))],
        out_specs=spec,
        out_shape=jax.ShapeDtypeStruct((b, s, w), BF16),
        scratch_shapes=[pltpu.VMEM((HG_DIM, HG_DIM), F32)],
        compiler_params=_cparams(("parallel", "parallel", "arbitrary")),
        name="hgrn2_mixer",
    )(q, lf, k, v, g, g_norm.reshape(1, HG_DIM))


def _fox_kernel(q_ref, k_ref, v_ref, ccol_ref, crow_ref, o_ref, m_ref, l_ref, acc_ref, cq_ref, *, tq, tk):
    hp = pl.program_id(1)
    qi = pl.program_id(2)
    ki = pl.program_id(3)
    n_kv = pl.num_programs(3)
    lanes = 2 * FOX_DIM

    @pl.when(ki == 0)
    def _():
        m_ref[...] = jnp.full_like(m_ref, NEG_INF)
        l_ref[...] = jnp.zeros_like(l_ref)
        acc_ref[...] = jnp.zeros_like(acc_ref)
        cc = ccol_ref[...]
        head = lax.broadcasted_iota(jnp.int32, cc.shape, 1)
        for a in range(2):
            col = jnp.sum(jnp.where(head == 2 * hp + a, cc, 0.0), axis=-1, keepdims=True)
            cq_ref[a] = jnp.broadcast_to(col, (tq, lanes))

    def step(masked):
        q = q_ref[...]
        k = k_ref[...]
        v = v_ref[...]
        lane = lax.broadcasted_iota(jnp.int32, q.shape, 1)
        if masked:
            t_pos = qi * tq + lax.broadcasted_iota(jnp.int32, (tq, tk), 0)
            s_pos = ki * tk + lax.broadcasted_iota(jnp.int32, (tq, tk), 1)
            causal = s_pos <= t_pos
        for a in range(2):
            in_head = (lane >= a * FOX_DIM) & (lane < (a + 1) * FOX_DIM)
            q_a = jnp.where(in_head, q, jnp.zeros_like(q))
            s = lax.dot_general(q_a, k, (((1,), (1,)), ((), ())), preferred_element_type=F32)
            s = s + (cq_ref[a][:, 0:1] - crow_ref[pl.ds(2 * hp + a, 1), :])
            if masked:
                s = jnp.where(causal, s, NEG_INF)
            m_prev = m_ref[a]
            m_new = jnp.maximum(m_prev, jnp.max(s, axis=-1, keepdims=True))
            alpha = jnp.exp(m_prev - m_new)
            p = jnp.exp(s - m_new[:, 0:1])
            l_ref[a] = alpha * l_ref[a] + jnp.sum(p, axis=-1, keepdims=True)
            acc_ref[a] = alpha * acc_ref[a] + jnp.dot(p.astype(v.dtype), v, preferred_element_type=F32)
            m_ref[a] = m_new

    last = (qi * tq + tq - 1) // tk
    first_masked = (qi * tq) // tk

    @pl.when(ki < first_masked)
    def _():
        step(False)

    @pl.when((ki >= first_masked) & (ki <= last))
    def _():
        step(True)

    @pl.when(ki == n_kv - 1)
    def _():
        lane = lax.broadcasted_iota(jnp.int32, (tq, lanes), 1)
        o_a = acc_ref[0] / l_ref[0]
        o_b = acc_ref[1] / l_ref[1]
        o_ref[...] = jnp.where(lane < FOX_DIM, o_a, o_b).astype(o_ref.dtype)


def _fox_attention(q, k, v, c_col, c_row, tq, tk):
    b, s, w = q.shape
    heads = c_col.shape[-1]
    lanes = 2 * FOX_DIM
    n_pairs = w // lanes

    def kv_map(bi, hi, qi, ki):
        return (bi, jnp.minimum(ki, (qi * tq + tq - 1) // tk), hi)

    def crow_map(bi, hi, qi, ki):
        return (bi, 0, jnp.minimum(ki, (qi * tq + tq - 1) // tk))

    return pl.pallas_call(
        functools.partial(_fox_kernel, tq=tq, tk=tk),
        grid=(b, n_pairs, s // tq, s // tk),
        in_specs=[pl.BlockSpec((None, tq, lanes), lambda bi, hi, qi, ki: (bi, qi, hi)),
                  pl.BlockSpec((None, tk, lanes), kv_map),
                  pl.BlockSpec((None, tk, lanes), kv_map),
                  pl.BlockSpec((None, tq, heads), lambda bi, hi, qi, ki: (bi, qi, 0)),
                  pl.BlockSpec((None, heads, tk), crow_map)],
        out_specs=pl.BlockSpec((None, tq, lanes), lambda bi, hi, qi, ki: (bi, qi, hi)),
        out_shape=jax.ShapeDtypeStruct((b, s, w), BF16),
        scratch_shapes=[pltpu.VMEM((2, tq, lanes), F32), pltpu.VMEM((2, tq, lanes), F32),
                        pltpu.VMEM((2, tq, lanes), F32), pltpu.VMEM((2, tq, lanes), F32)],
        compiler_params=_cparams(("parallel", "parallel", "parallel", "arbitrary")),
        name="fox_attention",
    )(q, k, v, c_col, c_row)


def _merge_kernel(x_ref, oa_ref, ob_ref, ga_ref, gb_ref, wa_ref, wb_ref, wo_ref, gn_ref, h_ref, n_ref):
    ya = jnp.dot(oa_ref[...], wa_ref[...], preferred_element_type=F32)
    yb = jnp.dot(ob_ref[...], wb_ref[...], preferred_element_type=F32)
    merged = ga_ref[...] * ya + gb_ref[...] * yb
    h = x_ref[...] + jnp.dot(merged.astype(BF16), wo_ref[...], preferred_element_type=F32)
    h_ref[...] = h
    y = h * lax.rsqrt(jnp.mean(h * h, axis=-1, keepdims=True) + EPS)
    n_ref[...] = (y * gn_ref[...]).astype(n_ref.dtype)


def _merge(x2d, oa, ob, ga, gb, wa, wb, wo, gain, tm):
    t, d = x2d.shape
    row = lambda width: pl.BlockSpec((tm, width), lambda i: (i, 0))
    full = lambda a: pl.BlockSpec(a.shape, lambda i: (0, 0))
    return pl.pallas_call(
        _merge_kernel,
        grid=(t // tm,),
        in_specs=[row(d), row(oa.shape[1]), row(ob.shape[1]), row(d), row(d), full(wa), full(wb), full(wo),
                  pl.BlockSpec((1, d), lambda i: (0, 0))],
        out_specs=[row(d), row(d)],
        out_shape=[jax.ShapeDtypeStruct((t, d), F32), jax.ShapeDtypeStruct((t, d), BF16)],
        compiler_params=_cparams(("parallel",)),
        name="merge_outproj",
    )(x2d, oa, ob, ga, gb, wa, wb, wo, gain.reshape(1, d))


def _ffn_kernel(n_ref, halo_ref, h_ref, wg_ref, wv_ref, cwg_ref, cwv_ref, cbg_ref, cbv_ref, wd_ref, gn_ref,
                o_ref, next_ref, acc_ref, *, tm, seq):
    i = pl.program_id(0)
    j = pl.program_id(1)

    @pl.when(j == 0)
    def _():
        at_start = (i * tm) % seq == 0
        next_ref[0:HALO, :] = jnp.where(at_start, jnp.zeros_like(halo_ref[...]), halo_ref[...])
        next_ref[HALO:, :] = n_ref[...]
        acc_ref[...] = jnp.zeros_like(acc_ref)

    n_ext = next_ref[...]

    def conv(w_ref, cw_ref, cb_ref):
        u = jnp.dot(n_ext, w_ref[...], preferred_element_type=F32)
        out = cb_ref[...]
        for tap in range(CONV_WIDTH):
            lo = HALO - (CONV_WIDTH - 1) + tap
            out = out + cw_ref[tap:tap + 1, :] * u[lo:lo + tm]
        return out

    gate = conv(wg_ref, cwg_ref, cbg_ref)
    val = conv(wv_ref, cwv_ref, cbv_ref)
    act = 0.5 * gate * (1.0 + lax.erf(gate * (2.0 ** -0.5))) * val
    acc_ref[...] += jnp.dot(act.astype(BF16), wd_ref[...], preferred_element_type=F32)

    @pl.when(j == pl.num_programs(1) - 1)
    def _():
        h = h_ref[...] + acc_ref[...]
        y = h * lax.rsqrt(jnp.mean(h * h, axis=-1, keepdims=True) + EPS)
        o_ref[...] = (y * gn_ref[...]).astype(o_ref.dtype)


def _ffn(n2d, h2d, w_up, conv_w, conv_b, w_down, gain, seq, tm, tf):
    t, d = n2d.shape
    d_ff = w_down.shape[0]
    n_f = d_ff // tf
    halo_blocks = tm // HALO
    cb = conv_b.reshape(1, 2 * d_ff)
    return pl.pallas_call(
        functools.partial(_ffn_kernel, tm=tm, seq=seq),
        grid=(t // tm, n_f),
        in_specs=[pl.BlockSpec((tm, d), lambda i, j: (i, 0)),
                  pl.BlockSpec((HALO, d), lambda i, j: (jnp.maximum(i * halo_blocks - 1, 0), 0)),
                  pl.BlockSpec((tm, d), lambda i, j: (i, 0)),
                  pl.BlockSpec((d, tf), lambda i, j: (0, j)),
                  pl.BlockSpec((d, tf), lambda i, j: (0, n_f + j)),
                  pl.BlockSpec((CONV_WIDTH, tf), lambda i, j: (0, j)),
                  pl.BlockSpec((CONV_WIDTH, tf), lambda i, j: (0, n_f + j)),
                  pl.BlockSpec((1, tf), lambda i, j: (0, j)),
                  pl.BlockSpec((1, tf), lambda i, j: (0, n_f + j)),
                  pl.BlockSpec((tf, d), lambda i, j: (j, 0)),
                  pl.BlockSpec((1, d), lambda i, j: (0, 0))],
        out_specs=pl.BlockSpec((tm, d), lambda i, j: (i, 0)),
        out_shape=jax.ShapeDtypeStruct((t, d), F32),
        scratch_shapes=[pltpu.VMEM((HALO + tm, d), BF16), pltpu.VMEM((tm, d), F32)],
        compiler_params=_cparams(("parallel", "arbitrary")),
        name="conv_glu_ffn",
    )(n2d, n2d, h2d, w_up, w_up, conv_w, conv_w, cb, cb, w_down, gain.reshape(1, d))


def _tile(n, pref):
    return pref if n % pref == 0 else n


def kernel(x, norm_mix, w_in, fox_f_bias, hg_lb_logits, hg_norm, w_branch_a, w_branch_b, w_out, norm_ffn, w_up,
           conv_w, conv_b, w_down, norm_final):
    b, s, d = x.shape
    assert norm_mix.shape[0] == 1, "single-layer trunk"
    t = b * s
    hg_w = HG_HEADS * HG_DIM
    fox_w = FOX_HEADS * FOX_DIM
    cuts = [hg_w, 2 * hg_w, 3 * hg_w, 4 * hg_w, 4 * hg_w + fox_w, 4 * hg_w + 2 * fox_w, 4 * hg_w + 3 * fox_w,
            4 * hg_w + 3 * fox_w + FOX_HEADS, 4 * hg_w + 3 * fox_w + FOX_HEADS + d]
    assert w_in.shape[-1] == cuts[-1] + d
    w_bf = w_in[0].astype(BF16)
    w_hq, w_hf, w_hi, w_hg, w_fq, w_fk, w_fv, w_ff, w_ga, w_gb = jnp.split(w_bf, cuts, axis=-1)

    x2d = x.reshape(t, d)
    tm = _tile(t, 512)
    n2d = _rmsnorm(x2d, norm_mix[0], _tile(t, 1024))

    hq, hlf, hk, hv, hgate = _hg_proj(n2d, w_hq, w_hf, w_hi, w_hg, hg_lb_logits, tm, _tile(hg_w, 256))
    fq, fk, fv = _fox_proj(n2d, w_fq, w_fk, w_fv, tm, _tile(fox_w, 512))
    ga, gb = _gate_proj(n2d, w_ga, w_gb, tm, _tile(d, 512))
    c_col, c_row = _fgate(n2d.reshape(b, s, d), w_ff, w_ff.T, fox_f_bias[0], _tile(s, 256))

    r3 = lambda a: a.reshape(b, s, a.shape[-1])
    o_a = _hgrn(r3(hq), r3(hlf), r3(hk), r3(hv), r3(hgate), hg_norm[0], _tile(s, 512))
    o_b = _fox_attention(r3(fq), r3(fk), r3(fv), c_col, c_row, _tile(s, 512), _tile(s, 512))

    h1, n_ffn = _merge(x2d, o_a.reshape(t, hg_w), o_b.reshape(t, fox_w), ga, gb, w_branch_a[0].astype(BF16),
                       w_branch_b[0].astype(BF16), w_out[0].astype(BF16), norm_ffn[0], _tile(t, 256))
    out = _ffn(n_ffn, h1, w_up[0].astype(BF16), conv_w[0], conv_b[0], w_down[0].astype(BF16), norm_final,
               s, _tile(s, 512), 256)
    return out.reshape(b, s, d)
```

```python
import functools
import math

import jax
import jax.numpy as jnp
from jax import lax
from jax.experimental import pallas as pl
from jax.experimental.pallas import tpu as pltpu

F32 = jnp.float32
BF16 = jnp.bfloat16

EPS = 1e-6
HG_HEADS = 8
HG_DIM = 128
HG_CHUNK = 64
HG_SUB = 16
FOX_HEADS = 16
FOX_DIM = 64
CONV_WIDTH = 3
HALO = 8
NEG_INF = float("-inf")
LOG2E = math.log2(math.e)
LANES = 128

VMEM_LIMIT = 56 * 1024 * 1024


def _cparams(sem):
    return pltpu.CompilerParams(dimension_semantics=sem, vmem_limit_bytes=VMEM_LIMIT)


def _sigmoid(x):
    return 1.0 / (1.0 + jnp.exp(-x))


def _silu(x):
    return x * _sigmoid(x)


def _log_sigmoid(x):
    return jnp.minimum(x, 0.0) - jnp.log1p(jnp.exp(-jnp.abs(x)))


def _split3(x):
    x1 = x.astype(BF16)
    r1 = x - x1.astype(F32)
    x2 = r1.astype(BF16)
    x3 = (r1 - x2.astype(F32)).astype(BF16)
    return x1, x2, x3


def _rmsnorm_kernel(x_ref, g_ref, o_ref):
    x = x_ref[...]
    y = x * lax.rsqrt(jnp.mean(x * x, axis=-1, keepdims=True) + EPS)
    o_ref[...] = (y * g_ref[...]).astype(o_ref.dtype)


def _rmsnorm(x2d, gain, tm):
    t, d = x2d.shape
    return pl.pallas_call(
        _rmsnorm_kernel,
        grid=(t // tm,),
        in_specs=[pl.BlockSpec((tm, d), lambda i: (i, 0)), pl.BlockSpec((1, d), lambda i: (0, 0))],
        out_specs=pl.BlockSpec((tm, d), lambda i: (i, 0)),
        out_shape=jax.ShapeDtypeStruct((t, d), BF16),
        compiler_params=_cparams(("parallel",)),
        name="rmsnorm_in",
    )(x2d, gain.reshape(1, d))


def _hg_proj_kernel(n_ref, wq_ref, wf_ref, wi_ref, wg_ref, lbl_ref, q_ref, lf_ref, k_ref, v_ref, g_ref):
    n = n_ref[...]
    dot = lambda w: jnp.dot(n, w[...], preferred_element_type=F32)
    q_ref[...] = _silu(dot(wq_ref))
    lbl = lbl_ref[...]
    e = jnp.exp(lbl - jnp.max(lbl, axis=0, keepdims=True))
    lb = e[0:1] / jnp.sum(e, axis=0, keepdims=True)
    sig = _sigmoid(dot(wf_ref))
    f = lb + (1.0 - lb) * sig
    lf_ref[...] = jnp.log(f)
    k_ref[...] = 1.0 - f
    v_ref[...] = dot(wi_ref)
    g_ref[...] = _silu(dot(wg_ref))


def _hg_proj(n2d, wq, wf, wi, wg, lb_logits, tm, tn):
    t, d = n2d.shape
    w = wq.shape[1]
    wspec = pl.BlockSpec((d, tn), lambda i, j: (0, j))
    ospec = pl.BlockSpec((tm, tn), lambda i, j: (i, j))
    oshape = jax.ShapeDtypeStruct((t, w), F32)
    return pl.pallas_call(
        _hg_proj_kernel,
        grid=(t // tm, w // tn),
        in_specs=[pl.BlockSpec((tm, d), lambda i, j: (i, 0)), wspec, wspec, wspec, wspec,
                  pl.BlockSpec((lb_logits.shape[0], tn), lambda i, j: (0, j))],
        out_specs=[ospec] * 5,
        out_shape=[oshape] * 5,
        compiler_params=_cparams(("parallel", "arbitrary")),
        name="hgrn_proj",
    )(n2d, wq, wf, wi, wg, lb_logits)


def _fox_proj_kernel(n_ref, wq_ref, wk_ref, wv_ref, q_ref, k_ref, v_ref):
    n = n_ref[...]
    dot = lambda w: jnp.dot(n, w[...], preferred_element_type=F32)
    q_ref[...] = (dot(wq_ref) * (FOX_DIM ** -0.5 * LOG2E)).astype(q_ref.dtype)
    k_ref[...] = dot(wk_ref).astype(k_ref.dtype)
    v_ref[...] = dot(wv_ref).astype(v_ref.dtype)


def _fox_proj(n2d, wq, wk, wv, tm, tn):
    t, d = n2d.shape
    w = wq.shape[1]
    wspec = pl.BlockSpec((d, tn), lambda i, j: (0, j))
    ospec = pl.BlockSpec((tm, tn), lambda i, j: (i, j))
    oshape = jax.ShapeDtypeStruct((t, w), BF16)
    return pl.pallas_call(
        _fox_proj_kernel,
        grid=(t // tm, w // tn),
        in_specs=[pl.BlockSpec((tm, d), lambda i, j: (i, 0)), wspec, wspec, wspec],
        out_specs=[ospec] * 3,
        out_shape=[oshape] * 3,
        compiler_params=_cparams(("parallel", "arbitrary")),
        name="fox_proj",
    )(n2d, wq, wk, wv)


def _gate_proj_kernel(n_ref, wa_ref, wb_ref, ga_ref, gb_ref):
    n = n_ref[...]
    ga_ref[...] = _sigmoid(jnp.dot(n, wa_ref[...], preferred_element_type=F32))
    gb_ref[...] = _sigmoid(jnp.dot(n, wb_ref[...], preferred_element_type=F32))


def _gate_proj(n2d, wa, wb, tm, tn):
    t, d = n2d.shape
    w = wa.shape[1]
    wspec = pl.BlockSpec((d, tn), lambda i, j: (0, j))
    ospec = pl.BlockSpec((tm, tn), lambda i, j: (i, j))
    oshape = jax.ShapeDtypeStruct((t, w), F32)
    return pl.pallas_call(
        _gate_proj_kernel,
        grid=(t // tm, w // tn),
        in_specs=[pl.BlockSpec((tm, d), lambda i, j: (i, 0)), wspec, wspec],
        out_specs=[ospec] * 2,
        out_shape=[oshape] * 2,
        compiler_params=_cparams(("parallel", "arbitrary")),
        name="gate_proj",
    )(n2d, wa, wb)


def _fgate_kernel(n_ref, w_ref, bias_ref, c_ref, carry_ref):
    @pl.when(pl.program_id(1) == 0)
    def _():
        carry_ref[...] = jnp.zeros_like(carry_ref)

    n = n_ref[...]
    tm = n.shape[0]
    r = lax.broadcasted_iota(jnp.int32, (tm, tm), 0)
    c = lax.broadcasted_iota(jnp.int32, (tm, tm), 1)
    lower = (c <= r).astype(BF16)
    lf = _log_sigmoid(jnp.dot(n, w_ref[...], preferred_element_type=F32) + bias_ref[...])
    cs = carry_ref[...]
    for piece in _split3(lf):
        cs = cs + jnp.dot(lower, piece, preferred_element_type=F32)
    c_ref[...] = cs
    carry_ref[...] = cs[tm - 1:tm, :]


def _fgate(n3d, w, bias, tm):
    b, s, d = n3d.shape
    h = w.shape[1]
    return pl.pallas_call(
        _fgate_kernel,
        grid=(b, s // tm),
        in_specs=[pl.BlockSpec((None, tm, d), lambda i, j: (i, j, 0)),
                  pl.BlockSpec((d, h), lambda i, j: (0, 0)),
                  pl.BlockSpec((1, h), lambda i, j: (0, 0))],
        out_specs=pl.BlockSpec((None, tm, h), lambda i, j: (i, j, 0)),
        out_shape=jax.ShapeDtypeStruct((b, s, h), F32),
        scratch_shapes=[pltpu.VMEM((1, h), F32)],
        compiler_params=_cparams(("parallel", "arbitrary")),
        name="fox_fgate_cumsum",
    )(n3d, w, bias.reshape(1, h))


def _hgrn_chunk(q, lf, k, v, state_t):
    c = HG_CHUNK
    n_sub = c // HG_SUB
    r_i = lax.broadcasted_iota(jnp.int32, (c, c), 0)
    c_i = lax.broadcasted_iota(jnp.int32, (c, c), 1)
    lower = (c_i <= r_i).astype(BF16)
    b = jnp.zeros((c, HG_DIM), F32)
    for piece in _split3(lf):
        b = b + jnp.dot(lower, piece, preferred_element_type=F32)

    qe = (q * jnp.exp(b)).astype(BF16)
    o = lax.dot_general(qe, state_t.astype(BF16), (((1,), (1,)), ((), ())), preferred_element_type=F32)

    o_parts = []
    t_iota = lax.broadcasted_iota(jnp.int32, (HG_SUB, HG_DIM), 0)
    for i in range(n_sub):
        lo = i * HG_SUB
        q_i = q[lo:lo + HG_SUB]
        b_i = b[lo:lo + HG_SUB]
        o_i = o[lo:lo + HG_SUB]
        if i > 0:
            ref = b[lo - 1:lo]
            q_t = (q_i * jnp.exp(b_i - ref)).astype(BF16)
            k_t = (k[:lo] * jnp.exp(ref - b[:lo])).astype(BF16)
            sc = lax.dot_general(q_t, k_t, (((1,), (1,)), ((), ())), preferred_element_type=F32)
            o_i = o_i + jnp.dot(sc.astype(BF16), v[:lo].astype(BF16), preferred_element_type=F32)
        for s in range(HG_SUB):
            row = lo + s
            rel = jnp.where(t_iota >= s, b_i - b[row:row + 1], NEG_INF)
            p = q_i * jnp.exp(rel) * k[row:row + 1]
            o_i = o_i + jnp.sum(p, axis=-1, keepdims=True) * v[row:row + 1]
        o_parts.append(o_i)
    o = jnp.concatenate(o_parts, axis=0)

    b_end = b[c - 1:c]
    k_dec = (k * jnp.exp(b_end - b)).astype(BF16)
    upd = lax.dot_general(v.astype(BF16), k_dec, (((0,), (0,)), ((), ())), preferred_element_type=F32)
    return o, state_t * jnp.exp(b_end) + upd


def _hgrn_kernel(q_ref, lf_ref, k_ref, v_ref, g_ref, gn_ref, o_ref, state_ref, *, n_chunks):
    @pl.when(pl.program_id(2) == 0)
    def _():
        state_ref[...] = jnp.zeros_like(state_ref)

    def body(ci, carry):
        rows = pl.ds(pl.multiple_of(ci * HG_CHUNK, HG_CHUNK), HG_CHUNK)
        o, new_state = _hgrn_chunk(q_ref[rows, :], lf_ref[rows, :], k_ref[rows, :], v_ref[rows, :], state_ref[...])
        state_ref[...] = new_state
        y = o * lax.rsqrt(jnp.mean(o * o, axis=-1, keepdims=True) + EPS)
        o_ref[rows, :] = (y * gn_ref[...] * g_ref[rows, :]).astype(o_ref.dtype)
        return carry

    lax.fori_loop(0, n_chunks, body, 0)


def _hgrn(q, lf, k, v, g, g_norm, tc):
    b, s, w = q.shape
    heads = w // HG_DIM
    spec = pl.BlockSpec((None, tc, HG_DIM), lambda bi, hi, ti: (bi, ti, hi))
    return pl.pallas_call(
        functools.partial(_hgrn_kernel, n_chunks=tc // HG_CHUNK),
        grid=(b, heads, s // tc),
        in_specs=[spec] * 5 + [pl.BlockSpec((1, HG_DIM), lambda bi, hi, ti: (0, 0))],
        out_specs=spec,
        out_shape=jax.ShapeDtypeStruct((b, s, w), BF16),
        scratch_shapes=[pltpu.VMEM((HG_DIM, HG_DIM), F32)],
        compiler_params=_cparams(("parallel", "parallel", "arbitrary")),
        name="hgrn2_mixer",
    )(q, lf, k, v, g, g_norm.reshape(1, HG_DIM))


AUG_C = FOX_DIM
AUG_ONE = FOX_DIM + 3
AUG_END = FOX_DIM + 6


def _fox_pack_kernel(q_ref, k_ref, v_ref, c_ref, qa_ref, ka_ref, va_ref):
    hp = pl.program_id(1)
    cc = c_ref[...]
    head = lax.broadcasted_iota(jnp.int32, cc.shape, 1)
    tm = cc.shape[0]
    lane = lax.broadcasted_iota(jnp.int32, (tm, LANES), 1)
    in_data = lane < FOX_DIM
    for a in range(2):
        c_h = jnp.sum(jnp.where(head == 2 * hp + a, cc, 0.0), axis=-1, keepdims=True) * LOG2E
        c1, c2, c3 = (p.astype(F32) for p in _split3(c_h))
        pieces = jnp.where(lane == AUG_C, c1, jnp.where(lane == AUG_C + 1, c2, c3))
        shift = lambda x: x if a == 0 else pltpu.roll(x, FOX_DIM, 1)
        q = shift(q_ref[...].astype(F32))
        k = shift(k_ref[...].astype(F32))
        v = shift(v_ref[...].astype(F32))
        q_tail = jnp.where(lane < AUG_ONE, pieces, jnp.where(lane < AUG_END, 1.0, 0.0))
        k_tail = jnp.where(lane < AUG_ONE, 1.0,
                           jnp.where(lane < AUG_END, -pltpu.roll(pieces, AUG_ONE - AUG_C, 1), 0.0))
        v_tail = jnp.where(lane == FOX_DIM, 1.0, 0.0)
        qa_ref[a] = jnp.where(in_data, q, q_tail).astype(qa_ref.dtype)
        ka_ref[a] = jnp.where(in_data, k, k_tail).astype(ka_ref.dtype)
        va_ref[a] = jnp.where(in_data, v, v_tail).astype(va_ref.dtype)


def _fox_pack(q, k, v, c, tm):
    b, s, w = q.shape
    heads = c.shape[-1]
    pair = pl.BlockSpec((None, tm, LANES), lambda bi, hi, ti: (bi, ti, hi))
    out = pl.BlockSpec((None, 2, tm, LANES), lambda bi, hi, ti: (bi, hi, ti, 0))
    oshape = jax.ShapeDtypeStruct((b, heads, s, LANES), BF16)
    return pl.pallas_call(
        _fox_pack_kernel,
        grid=(b, heads // 2, s // tm),
        in_specs=[pair, pair, pair, pl.BlockSpec((None, tm, heads), lambda bi, hi, ti: (bi, ti, 0))],
        out_specs=[out] * 3,
        out_shape=[oshape] * 3,
        compiler_params=_cparams(("parallel", "parallel", "parallel")),
        name="fox_pack",
    )(q, k, v, c)


def _fox_kernel(q_ref, k_ref, v_ref, o_ref, m_ref, acc_ref, *, tq, tk):
    qi = pl.program_id(2)
    m_ref[...] = jnp.full_like(m_ref, NEG_INF)
    acc_ref[...] = jnp.zeros_like(acc_ref)

    def block(row0, n_rows, kv0, masked):
        rows = pl.ds(row0, n_rows)
        cols = pl.ds(kv0, tk)
        if masked:
            causal = (lax.broadcasted_iota(jnp.int32, (n_rows, tk), 1)
                      <= lax.broadcasted_iota(jnp.int32, (n_rows, tk), 0))
        for a in range(2):
            s = lax.dot_general(q_ref[a, rows, :], k_ref[a, cols, :], (((1,), (1,)), ((), ())),
                                preferred_element_type=F32)
            if masked:
                s = jnp.where(causal, s, NEG_INF)
            m_prev = m_ref[a, rows, :]
            m_new = jnp.maximum(m_prev, jnp.max(s, axis=-1, keepdims=True))
            alpha = jnp.exp2(m_prev - m_new)
            p = jnp.exp2(s - m_new[:, 0:1])
            acc_ref[a, rows, :] = alpha * acc_ref[a, rows, :] + jnp.dot(
                p.astype(BF16), v_ref[a, cols, :], preferred_element_type=F32)
            m_ref[a, rows, :] = m_new

    n_diag = tq // tk
    n_full = qi * n_diag

    def body(ki, carry):
        block(0, tq, pl.multiple_of(ki * tk, tk), False)
        return carry

    lax.fori_loop(0, n_full, body, 0)

    for j in range(n_diag):
        kv0 = pl.multiple_of((n_full + j) * tk, tk)
        block(j * tk, tk, kv0, True)
        if j + 1 < n_diag:
            block((j + 1) * tk, tq - (j + 1) * tk, kv0, False)

    lane = lax.broadcasted_iota(jnp.int32, (tq, LANES), 1)
    acc_a = acc_ref[0]
    acc_b = acc_ref[1]
    o_a = acc_a / acc_a[:, FOX_DIM:FOX_DIM + 1]
    o_b = acc_b / acc_b[:, FOX_DIM:FOX_DIM + 1]
    o_ref[...] = jnp.where(lane < FOX_DIM, o_a, pltpu.roll(o_b, FOX_DIM, 1)).astype(o_ref.dtype)


def _fox_attention(q, k, v, tq, tk):
    b, h, s, _ = q.shape
    kv_spec = pl.BlockSpec((None, 2, s, LANES), lambda bi, hi, qi: (bi, hi, 0, 0))
    return pl.pallas_call(
        functools.partial(_fox_kernel, tq=tq, tk=tk),
        grid=(b, h // 2, s // tq),
        in_specs=[pl.BlockSpec((None, 2, tq, LANES), lambda bi, hi, qi: (bi, hi, qi, 0)), kv_spec, kv_spec],
        out_specs=pl.BlockSpec((None, tq, LANES), lambda bi, hi, qi: (bi, qi, hi)),
        out_shape=jax.ShapeDtypeStruct((b, s, (h // 2) * LANES), BF16),
        scratch_shapes=[pltpu.VMEM((2, tq, LANES), F32), pltpu.VMEM((2, tq, LANES), F32)],
        compiler_params=_cparams(("parallel", "parallel", "arbitrary")),
        name="fox_attention",
    )(q, k, v)


def _merge_kernel(x_ref, oa_ref, ob_ref, ga_ref, gb_ref, wa_ref, wb_ref, wo_ref, gn_ref, h_ref, n_ref):
    ya = jnp.dot(oa_ref[...], wa_ref[...], preferred_element_type=F32)
    yb = jnp.dot(ob_ref[...], wb_ref[...], preferred_element_type=F32)
    merged = ga_ref[...] * ya + gb_ref[...] * yb
    h = x_ref[...] + jnp.dot(merged.astype(BF16), wo_ref[...], preferred_element_type=F32)
    h_ref[...] = h
    y = h * lax.rsqrt(jnp.mean(h * h, axis=-1, keepdims=True) + EPS)
    n_ref[...] = (y * gn_ref[...]).astype(n_ref.dtype)


def _merge(x2d, oa, ob, ga, gb, wa, wb, wo, gain, tm):
    t, d = x2d.shape
    row = lambda width: pl.BlockSpec((tm, width), lambda i: (i, 0))
    full = lambda a: pl.BlockSpec(a.shape, lambda i: (0, 0))
    return pl.pallas_call(
        _merge_kernel,
        grid=(t // tm,),
        in_specs=[row(d), row(oa.shape[1]), row(ob.shape[1]), row(d), row(d), full(wa), full(wb), full(wo),
                  pl.BlockSpec((1, d), lambda i: (0, 0))],
        out_specs=[row(d), row(d)],
        out_shape=[jax.ShapeDtypeStruct((t, d), F32), jax.ShapeDtypeStruct((t, d), BF16)],
        compiler_params=_cparams(("parallel",)),
        name="merge_outproj",
    )(x2d, oa, ob, ga, gb, wa, wb, wo, gain.reshape(1, d))


def _ffn_kernel(n_ref, halo_ref, h_ref, wg_ref, wv_ref, cwg_ref, cwv_ref, cbg_ref, cbv_ref, wd_ref, gn_ref,
                o_ref, next_ref, acc_ref, *, tm, seq):
    i = pl.program_id(0)
    j = pl.program_id(1)

    @pl.when(j == 0)
    def _():
        at_start = (i * tm) % seq == 0
        next_ref[0:HALO, :] = jnp.where(at_start, jnp.zeros_like(halo_ref[...]), halo_ref[...])
        next_ref[HALO:, :] = n_ref[...]
        acc_ref[...] = jnp.zeros_like(acc_ref)

    n_ext = next_ref[...]

    def conv(w_ref, cw_ref, cb_ref):
        u = jnp.dot(n_ext, w_ref[...], preferred_element_type=F32)
        out = cb_ref[...]
        for tap in range(CONV_WIDTH):
            lo = HALO - (CONV_WIDTH - 1) + tap
            out = out + cw_ref[tap:tap + 1, :] * u[lo:lo + tm]
        return out

    gate = conv(wg_ref, cwg_ref, cbg_ref)
    val = conv(wv_ref, cwv_ref, cbv_ref)
    act = 0.5 * gate * (1.0 + lax.erf(gate * (2.0 ** -0.5))) * val
    acc_ref[...] += jnp.dot(act.astype(BF16), wd_ref[...], preferred_element_type=F32)

    @pl.when(j == pl.num_programs(1) - 1)
    def _():
        h = h_ref[...] + acc_ref[...]
        y = h * lax.rsqrt(jnp.mean(h * h, axis=-1, keepdims=True) + EPS)
        o_ref[...] = (y * gn_ref[...]).astype(o_ref.dtype)


def _ffn(n2d, h2d, w_up, conv_w, conv_b, w_down, gain, seq, tm, tf):
    t, d = n2d.shape
    d_ff = w_down.shape[0]
    n_f = d_ff // tf
    halo_blocks = tm // HALO
    cb = conv_b.reshape(1, 2 * d_ff)
    return pl.pallas_call(
        functools.partial(_ffn_kernel, tm=tm, seq=seq),
        grid=(t // tm, n_f),
        in_specs=[pl.BlockSpec((tm, d), lambda i, j: (i, 0)),
                  pl.BlockSpec((HALO, d), lambda i, j: (jnp.maximum(i * halo_blocks - 1, 0), 0)),
                  pl.BlockSpec((tm, d), lambda i, j: (i, 0)),
                  pl.BlockSpec((d, tf), lambda i, j: (0, j)),
                  pl.BlockSpec((d, tf), lambda i, j: (0, n_f + j)),
                  pl.BlockSpec((CONV_WIDTH, tf), lambda i, j: (0, j)),
                  pl.BlockSpec((CONV_WIDTH, tf), lambda i, j: (0, n_f + j)),
                  pl.BlockSpec((1, tf), lambda i, j: (0, j)),
                  pl.BlockSpec((1, tf), lambda i, j: (0, n_f + j)),
                  pl.BlockSpec((tf, d), lambda i, j: (j, 0)),
                  pl.BlockSpec((1, d), lambda i, j: (0, 0))],
        out_specs=pl.BlockSpec((tm, d), lambda i, j: (i, 0)),
        out_shape=jax.ShapeDtypeStruct((t, d), F32),
        scratch_shapes=[pltpu.VMEM((HALO + tm, d), BF16), pltpu.VMEM((tm, d), F32)],
        compiler_params=_cparams(("parallel", "arbitrary")),
        name="conv_glu_ffn",
    )(n2d, n2d, h2d, w_up, w_up, conv_w, conv_w, cb, cb, w_down, gain.reshape(1, d))


def _tile(n, pref):
    return pref if n % pref == 0 else n


def kernel(x, norm_mix, w_in, fox_f_bias, hg_lb_logits, hg_norm, w_branch_a, w_branch_b, w_out, norm_ffn, w_up,
           conv_w, conv_b, w_down, norm_final):
    b, s, d = x.shape
    assert norm_mix.shape[0] == 1, "single-layer trunk"
    t = b * s
    hg_w = HG_HEADS * HG_DIM
    fox_w = FOX_HEADS * FOX_DIM
    cuts = [hg_w, 2 * hg_w, 3 * hg_w, 4 * hg_w, 4 * hg_w + fox_w, 4 * hg_w + 2 * fox_w, 4 * hg_w + 3 * fox_w,
            4 * hg_w + 3 * fox_w + FOX_HEADS, 4 * hg_w + 3 * fox_w + FOX_HEADS + d]
    assert w_in.shape[-1] == cuts[-1] + d
    w_bf = w_in[0].astype(BF16)
    w_hq, w_hf, w_hi, w_hg, w_fq, w_fk, w_fv, w_ff, w_ga, w_gb = jnp.split(w_bf, cuts, axis=-1)

    x2d = x.reshape(t, d)
    tm = _tile(t, 512)
    n2d = _rmsnorm(x2d, norm_mix[0], _tile(t, 1024))

    hq, hlf, hk, hv, hgate = _hg_proj(n2d, w_hq, w_hf, w_hi, w_hg, hg_lb_logits, tm, _tile(hg_w, 256))
    fq, fk, fv = _fox_proj(n2d, w_fq, w_fk, w_fv, tm, _tile(fox_w, 512))
    ga, gb = _gate_proj(n2d, w_ga, w_gb, tm, _tile(d, 512))
    c_cum = _fgate(n2d.reshape(b, s, d), w_ff, fox_f_bias[0], _tile(s, 256))

    r3 = lambda a: a.reshape(b, s, a.shape[-1])
    o_a = _hgrn(r3(hq), r3(hlf), r3(hk), r3(hv), r3(hgate), hg_norm[0], _tile(s, 512))
    q_aug, k_aug, v_aug = _fox_pack(r3(fq), r3(fk), r3(fv), c_cum, _tile(s, 512))
    tk = _tile(s, 512)
    o_b = _fox_attention(q_aug, k_aug, v_aug, _tile(s, 4 * tk), tk)

    h1, n_ffn = _merge(x2d, o_a.reshape(t, hg_w), o_b.reshape(t, fox_w), ga, gb, w_branch_a[0].astype(BF16),
                       w_branch_b[0].astype(BF16), w_out[0].astype(BF16), norm_ffn[0], _tile(t, 256))
    out = _ffn(n_ffn, h1, w_up[0].astype(BF16), conv_w[0], conv_b[0], w_down[0].astype(BF16), norm_final,
               s, _tile(s, 512), 256)
    return out.reshape(b, s, d)
```

```python
import functools
import math

import jax
import jax.numpy as jnp
from jax import lax
from jax.experimental import pallas as pl
from jax.experimental.pallas import tpu as pltpu

F32 = jnp.float32
BF16 = jnp.bfloat16

EPS = 1e-6
HG_HEADS = 8
HG_DIM = 128
HG_CHUNK = 64
HG_SUB = 16
HG_MAX_EXPONENT = 64.0
FOX_HEADS = 16
FOX_DIM = 64
CONV_WIDTH = 3
HALO = 8
NEG_INF = float("-inf")
LOG2E = math.log2(math.e)
LANES = 128

VMEM_LIMIT = 56 * 1024 * 1024


def _cparams(sem):
    return pltpu.CompilerParams(dimension_semantics=sem, vmem_limit_bytes=VMEM_LIMIT)


def _sigmoid(x):
    return 1.0 / (1.0 + jnp.exp(-x))


def _silu(x):
    return x * _sigmoid(x)


def _log_sigmoid(x):
    return jnp.minimum(x, 0.0) - jnp.log1p(jnp.exp(-jnp.abs(x)))


def _split3(x):
    x1 = x.astype(BF16)
    r1 = x - x1.astype(F32)
    x2 = r1.astype(BF16)
    x3 = (r1 - x2.astype(F32)).astype(BF16)
    return x1, x2, x3


def _rmsnorm_kernel(x_ref, g_ref, o_ref):
    x = x_ref[...]
    y = x * lax.rsqrt(jnp.mean(x * x, axis=-1, keepdims=True) + EPS)
    o_ref[...] = (y * g_ref[...]).astype(o_ref.dtype)


def _rmsnorm(x2d, gain, tm):
    t, d = x2d.shape
    return pl.pallas_call(
        _rmsnorm_kernel,
        grid=(t // tm,),
        in_specs=[pl.BlockSpec((tm, d), lambda i: (i, 0)), pl.BlockSpec((1, d), lambda i: (0, 0))],
        out_specs=pl.BlockSpec((tm, d), lambda i: (i, 0)),
        out_shape=jax.ShapeDtypeStruct((t, d), BF16),
        compiler_params=_cparams(("parallel",)),
        name="rmsnorm_in",
    )(x2d, gain.reshape(1, d))


def _hg_proj_kernel(n_ref, wq_ref, wf_ref, wi_ref, wg_ref, lbl_ref, q_ref, lf_ref, k_ref, v_ref, g_ref):
    n = n_ref[...]
    dot = lambda w: jnp.dot(n, w[...], preferred_element_type=F32)
    q_ref[...] = _silu(dot(wq_ref))
    lbl = lbl_ref[...]
    e = jnp.exp(lbl - jnp.max(lbl, axis=0, keepdims=True))
    lb = e[0:1] / jnp.sum(e, axis=0, keepdims=True)
    sig = _sigmoid(dot(wf_ref))
    f = lb + (1.0 - lb) * sig
    lf_ref[...] = jnp.log(f)
    k_ref[...] = 1.0 - f
    v_ref[...] = dot(wi_ref)
    g_ref[...] = _silu(dot(wg_ref))


def _hg_proj(n2d, wq, wf, wi, wg, lb_logits, tm, tn):
    t, d = n2d.shape
    w = wq.shape[1]
    wspec = pl.BlockSpec((d, tn), lambda i, j: (0, j))
    ospec = pl.BlockSpec((tm, tn), lambda i, j: (i, j))
    oshape = jax.ShapeDtypeStruct((t, w), F32)
    return pl.pallas_call(
        _hg_proj_kernel,
        grid=(t // tm, w // tn),
        in_specs=[pl.BlockSpec((tm, d), lambda i, j: (i, 0)), wspec, wspec, wspec, wspec,
                  pl.BlockSpec((lb_logits.shape[0], tn), lambda i, j: (0, j))],
        out_specs=[ospec] * 5,
        out_shape=[oshape] * 5,
        compiler_params=_cparams(("parallel", "arbitrary")),
        name="hgrn_proj",
    )(n2d, wq, wf, wi, wg, lb_logits)


def _fox_proj_kernel(n_ref, wq_ref, wk_ref, wv_ref, q_ref, k_ref, v_ref):
    n = n_ref[...]
    dot = lambda w: jnp.dot(n, w[...], preferred_element_type=F32)
    q_ref[...] = (dot(wq_ref) * (FOX_DIM ** -0.5 * LOG2E)).astype(q_ref.dtype)
    k_ref[...] = dot(wk_ref).astype(k_ref.dtype)
    v_ref[...] = dot(wv_ref).astype(v_ref.dtype)


def _fox_proj(n2d, wq, wk, wv, tm, tn):
    t, d = n2d.shape
    w = wq.shape[1]
    wspec = pl.BlockSpec((d, tn), lambda i, j: (0, j))
    ospec = pl.BlockSpec((tm, tn), lambda i, j: (i, j))
    oshape = jax.ShapeDtypeStruct((t, w), BF16)
    return pl.pallas_call(
        _fox_proj_kernel,
        grid=(t // tm, w // tn),
        in_specs=[pl.BlockSpec((tm, d), lambda i, j: (i, 0)), wspec, wspec, wspec],
        out_specs=[ospec] * 3,
        out_shape=[oshape] * 3,
        compiler_params=_cparams(("parallel", "arbitrary")),
        name="fox_proj",
    )(n2d, wq, wk, wv)


def _gate_proj_kernel(n_ref, wa_ref, wb_ref, ga_ref, gb_ref):
    n = n_ref[...]
    ga_ref[...] = _sigmoid(jnp.dot(n, wa_ref[...], preferred_element_type=F32))
    gb_ref[...] = _sigmoid(jnp.dot(n, wb_ref[...], preferred_element_type=F32))


def _gate_proj(n2d, wa, wb, tm, tn):
    t, d = n2d.shape
    w = wa.shape[1]
    wspec = pl.BlockSpec((d, tn), lambda i, j: (0, j))
    ospec = pl.BlockSpec((tm, tn), lambda i, j: (i, j))
    oshape = jax.ShapeDtypeStruct((t, w), F32)
    return pl.pallas_call(
        _gate_proj_kernel,
        grid=(t // tm, w // tn),
        in_specs=[pl.BlockSpec((tm, d), lambda i, j: (i, 0)), wspec, wspec],
        out_specs=[ospec] * 2,
        out_shape=[oshape] * 2,
        compiler_params=_cparams(("parallel", "arbitrary")),
        name="gate_proj",
    )(n2d, wa, wb)


def _fgate_kernel(n_ref, w_ref, bias_ref, c_ref, carry_ref):
    @pl.when(pl.program_id(1) == 0)
    def _():
        carry_ref[...] = jnp.zeros_like(carry_ref)

    n = n_ref[...]
    tm = n.shape[0]
    r = lax.broadcasted_iota(jnp.int32, (tm, tm), 0)
    c = lax.broadcasted_iota(jnp.int32, (tm, tm), 1)
    lower = (c <= r).astype(BF16)
    lf = _log_sigmoid(jnp.dot(n, w_ref[...], preferred_element_type=F32) + bias_ref[...])
    cs = carry_ref[...]
    for piece in _split3(lf):
        cs = cs + jnp.dot(lower, piece, preferred_element_type=F32)
    c_ref[...] = cs
    carry_ref[...] = cs[tm - 1:tm, :]


def _fgate(n3d, w, bias, tm):
    b, s, d = n3d.shape
    h = w.shape[1]
    return pl.pallas_call(
        _fgate_kernel,
        grid=(b, s // tm),
        in_specs=[pl.BlockSpec((None, tm, d), lambda i, j: (i, j, 0)),
                  pl.BlockSpec((d, h), lambda i, j: (0, 0)),
                  pl.BlockSpec((1, h), lambda i, j: (0, 0))],
        out_specs=pl.BlockSpec((None, tm, h), lambda i, j: (i, j, 0)),
        out_shape=jax.ShapeDtypeStruct((b, s, h), F32),
        scratch_shapes=[pltpu.VMEM((1, h), F32)],
        compiler_params=_cparams(("parallel", "arbitrary")),
        name="fox_fgate_cumsum",
    )(n3d, w, bias.reshape(1, h))


def _causal_chunk():
    r_i = lax.broadcasted_iota(jnp.int32, (HG_CHUNK, HG_CHUNK), 0)
    c_i = lax.broadcasted_iota(jnp.int32, (HG_CHUNK, HG_CHUNK), 1)
    return c_i <= r_i


def _chunk_cumsum(lf):
    lower = _causal_chunk().astype(BF16)
    b = jnp.zeros(lf.shape, F32)
    for piece in _split3(lf):
        b = b + jnp.dot(lower, piece, preferred_element_type=F32)
    return b


def _hgrn_chunk_single_ref(q, lf, k, v, state_t):
    nt = (((1,), (1,)), ((), ()))
    b = _chunk_cumsum(lf)
    q_t = (q * jnp.exp(b)).astype(BF16)
    k_up = k * jnp.exp(-b)
    sc = lax.dot_general(q_t, k_up.astype(BF16), nt, preferred_element_type=F32)
    sc = jnp.where(_causal_chunk(), sc, 0.0)
    v_bf = v.astype(BF16)
    o = lax.dot_general(q_t, state_t.astype(BF16), nt, preferred_element_type=F32)
    o = o + jnp.dot(sc.astype(BF16), v_bf, preferred_element_type=F32)
    dec = jnp.exp(b[HG_CHUNK - 1:HG_CHUNK])
    upd = lax.dot_general(v_bf, (k_up * dec).astype(BF16), (((0,), (0,)), ((), ())), preferred_element_type=F32)
    return o, state_t * dec + upd


def _hgrn_chunk(q, lf, k, v, state_t):
    c = HG_CHUNK
    n_sub = c // HG_SUB
    b = _chunk_cumsum(lf)

    qe = (q * jnp.exp(b)).astype(BF16)
    o = lax.dot_general(qe, state_t.astype(BF16), (((1,), (1,)), ((), ())), preferred_element_type=F32)

    o_parts = []
    t_iota = lax.broadcasted_iota(jnp.int32, (HG_SUB, HG_DIM), 0)
    for i in range(n_sub):
        lo = i * HG_SUB
        q_i = q[lo:lo + HG_SUB]
        b_i = b[lo:lo + HG_SUB]
        o_i = o[lo:lo + HG_SUB]
        if i > 0:
            ref = b[lo - 1:lo]
            q_t = (q_i * jnp.exp(b_i - ref)).astype(BF16)
            k_t = (k[:lo] * jnp.exp(ref - b[:lo])).astype(BF16)
            sc = lax.dot_general(q_t, k_t, (((1,), (1,)), ((), ())), preferred_element_type=F32)
            o_i = o_i + jnp.dot(sc.astype(BF16), v[:lo].astype(BF16), preferred_element_type=F32)
        for s in range(HG_SUB):
            row = lo + s
            rel = jnp.where(t_iota >= s, b_i - b[row:row + 1], NEG_INF)
            p = q_i * jnp.exp(rel) * k[row:row + 1]
            o_i = o_i + jnp.sum(p, axis=-1, keepdims=True) * v[row:row + 1]
        o_parts.append(o_i)
    o = jnp.concatenate(o_parts, axis=0)

    b_end = b[c - 1:c]
    k_dec = (k * jnp.exp(b_end - b)).astype(BF16)
    upd = lax.dot_general(v.astype(BF16), k_dec, (((0,), (0,)), ((), ())), preferred_element_type=F32)
    return o, state_t * jnp.exp(b_end) + upd


def _hgrn_kernel(q_ref, lf_ref, k_ref, v_ref, g_ref, gn_ref, o_ref, state_ref, *, n_chunks):
    @pl.when(pl.program_id(2) == 0)
    def _():
        state_ref[...] = jnp.zeros_like(state_ref)

    def run_chunk(chunk_fn, rows, state):
        o, new_state = chunk_fn(q_ref[rows, :], lf_ref[rows, :], k_ref[rows, :], v_ref[rows, :], state)
        y = o * lax.rsqrt(jnp.mean(o * o, axis=-1, keepdims=True) + EPS)
        o_ref[rows, :] = (y * gn_ref[...] * g_ref[rows, :]).astype(o_ref.dtype)
        return new_state

    single_ref_ok = jnp.min(lf_ref[...]) * HG_CHUNK > -HG_MAX_EXPONENT

    @pl.when(single_ref_ok)
    def _():
        state = state_ref[...]
        for ci in range(n_chunks):
            state = run_chunk(_hgrn_chunk_single_ref, pl.ds(ci * HG_CHUNK, HG_CHUNK), state)
        state_ref[...] = state

    @pl.when(jnp.logical_not(single_ref_ok))
    def _():
        def body(ci, carry):
            rows = pl.ds(pl.multiple_of(ci * HG_CHUNK, HG_CHUNK), HG_CHUNK)
            state_ref[...] = run_chunk(_hgrn_chunk, rows, state_ref[...])
            return carry

        lax.fori_loop(0, n_chunks, body, 0)


def _hgrn(q, lf, k, v, g, g_norm, tc):
    b, s, w = q.shape
    heads = w // HG_DIM
    spec = pl.BlockSpec((None, tc, HG_DIM), lambda bi, hi, ti: (bi, ti, hi))
    return pl.pallas_call(
        functools.partial(_hgrn_kernel, n_chunks=tc // HG_CHUNK),
        grid=(b, heads, s // tc),
        in_specs=[spec] * 5 + [pl.BlockSpec((1, HG_DIM), lambda bi, hi, ti: (0, 0))],
        out_specs=spec,
        out_shape=jax.ShapeDtypeStruct((b, s, w), BF16),
        scratch_shapes=[pltpu.VMEM((HG_DIM, HG_DIM), F32)],
        compiler_params=_cparams(("parallel", "parallel", "arbitrary")),
        name="hgrn2_mixer",
    )(q, lf, k, v, g, g_norm.reshape(1, HG_DIM))


AUG_C = FOX_DIM
AUG_ONE = FOX_DIM + 3
AUG_END = FOX_DIM + 6


def _fox_pack_kernel(q_ref, k_ref, v_ref, c_ref, qa_ref, ka_ref, va_ref):
    hp = pl.program_id(1)
    cc = c_ref[...]
    head = lax.broadcasted_iota(jnp.int32, cc.shape, 1)
    tm = cc.shape[0]
    lane = lax.broadcasted_iota(jnp.int32, (tm, LANES), 1)
    in_data = lane < FOX_DIM
    for a in range(2):
        c_h = jnp.sum(jnp.where(head == 2 * hp + a, cc, 0.0), axis=-1, keepdims=True) * LOG2E
        c1, c2, c3 = (p.astype(F32) for p in _split3(c_h))
        pieces = jnp.where(lane == AUG_C, c1, jnp.where(lane == AUG_C + 1, c2, c3))
        shift = lambda x: x if a == 0 else pltpu.roll(x, FOX_DIM, 1)
        q = shift(q_ref[...].astype(F32))
        k = shift(k_ref[...].astype(F32))
        v = shift(v_ref[...].astype(F32))
        q_tail = jnp.where(lane < AUG_ONE, pieces, jnp.where(lane < AUG_END, 1.0, 0.0))
        k_tail = jnp.where(lane < AUG_ONE, 1.0,
                           jnp.where(lane < AUG_END, -pltpu.roll(pieces, AUG_ONE - AUG_C, 1), 0.0))
        v_tail = jnp.where(lane == FOX_DIM, 1.0, 0.0)
        qa_ref[a] = jnp.where(in_data, q, q_tail).astype(qa_ref.dtype)
        ka_ref[a] = jnp.where(in_data, k, k_tail).astype(ka_ref.dtype)
        va_ref[a] = jnp.where(in_data, v, v_tail).astype(va_ref.dtype)


def _fox_pack(q, k, v, c, tm):
    b, s, w = q.shape
    heads = c.shape[-1]
    pair = pl.BlockSpec((None, tm, LANES), lambda bi, hi, ti: (bi, ti, hi))
    out = pl.BlockSpec((None, 2, tm, LANES), lambda bi, hi, ti: (bi, hi, ti, 0))
    oshape = jax.ShapeDtypeStruct((b, heads, s, LANES), BF16)
    return pl.pallas_call(
        _fox_pack_kernel,
        grid=(b, heads // 2, s // tm),
        in_specs=[pair, pair, pair, pl.BlockSpec((None, tm, heads), lambda bi, hi, ti: (bi, ti, 0))],
        out_specs=[out] * 3,
        out_shape=[oshape] * 3,
        compiler_params=_cparams(("parallel", "parallel", "parallel")),
        name="fox_pack",
    )(q, k, v, c)


def _fox_kernel(q_ref, k_ref, v_ref, o_ref, m_ref, acc_ref, *, tq, tk):
    qi = pl.program_id(2)
    m_ref[...] = jnp.full_like(m_ref, NEG_INF)
    acc_ref[...] = jnp.zeros_like(acc_ref)

    def block(row0, n_rows, kv0, masked):
        rows = pl.ds(row0, n_rows)
        cols = pl.ds(kv0, tk)
        if masked:
            causal = (lax.broadcasted_iota(jnp.int32, (n_rows, tk), 1)
                      <= lax.broadcasted_iota(jnp.int32, (n_rows, tk), 0))
        for a in range(2):
            s = lax.dot_general(q_ref[a, rows, :], k_ref[a, cols, :], (((1,), (1,)), ((), ())),
                                preferred_element_type=F32)
            if masked:
                s = jnp.where(causal, s, NEG_INF)
            m_prev = m_ref[a, rows, :]
            m_new = jnp.maximum(m_prev, jnp.max(s, axis=-1, keepdims=True))
            alpha = jnp.exp2(m_prev - m_new)
            p = jnp.exp2(s - m_new[:, 0:1])
            acc_ref[a, rows, :] = alpha * acc_ref[a, rows, :] + jnp.dot(
                p.astype(BF16), v_ref[a, cols, :], preferred_element_type=F32)
            m_ref[a, rows, :] = m_new

    n_diag = tq // tk
    n_full = qi * n_diag

    def body(ki, carry):
        block(0, tq, pl.multiple_of(ki * tk, tk), False)
        return carry

    lax.fori_loop(0, n_full, body, 0)

    for j in range(n_diag):
        kv0 = pl.multiple_of((n_full + j) * tk, tk)
        block(j * tk, tk, kv0, True)
        if j + 1 < n_diag:
            block((j + 1) * tk, tq - (j + 1) * tk, kv0, False)

    lane = lax.broadcasted_iota(jnp.int32, (tq, LANES), 1)
    acc_a = acc_ref[0]
    acc_b = acc_ref[1]
    o_a = acc_a / acc_a[:, FOX_DIM:FOX_DIM + 1]
    o_b = acc_b / acc_b[:, FOX_DIM:FOX_DIM + 1]
    o_ref[...] = jnp.where(lane < FOX_DIM, o_a, pltpu.roll(o_b, FOX_DIM, 1)).astype(o_ref.dtype)


def _fox_attention(q, k, v, tq, tk):
    b, h, s, _ = q.shape
    kv_spec = pl.BlockSpec((None, 2, s, LANES), lambda bi, hi, qi: (bi, hi, 0, 0))
    return pl.pallas_call(
        functools.partial(_fox_kernel, tq=tq, tk=tk),
        grid=(b, h // 2, s // tq),
        in_specs=[pl.BlockSpec((None, 2, tq, LANES), lambda bi, hi, qi: (bi, hi, qi, 0)), kv_spec, kv_spec],
        out_specs=pl.BlockSpec((None, tq, LANES), lambda bi, hi, qi: (bi, qi, hi)),
        out_shape=jax.ShapeDtypeStruct((b, s, (h // 2) * LANES), BF16),
        scratch_shapes=[pltpu.VMEM((2, tq, LANES), F32), pltpu.VMEM((2, tq, LANES), F32)],
        compiler_params=_cparams(("parallel", "parallel", "arbitrary")),
        name="fox_attention",
    )(q, k, v)


def _merge_kernel(x_ref, oa_ref, ob_ref, ga_ref, gb_ref, wa_ref, wb_ref, wo_ref, gn_ref, h_ref, n_ref):
    ya = jnp.dot(oa_ref[...], wa_ref[...], preferred_element_type=F32)
    yb = jnp.dot(ob_ref[...], wb_ref[...], preferred_element_type=F32)
    merged = ga_ref[...] * ya + gb_ref[...] * yb
    h = x_ref[...] + jnp.dot(merged.astype(BF16), wo_ref[...], preferred_element_type=F32)
    h_ref[...] = h
    y = h * lax.rsqrt(jnp.mean(h * h, axis=-1, keepdims=True) + EPS)
    n_ref[...] = (y * gn_ref[...]).astype(n_ref.dtype)


def _merge(x2d, oa, ob, ga, gb, wa, wb, wo, gain, tm):
    t, d = x2d.shape
    row = lambda width: pl.BlockSpec((tm, width), lambda i: (i, 0))
    full = lambda a: pl.BlockSpec(a.shape, lambda i: (0, 0))
    return pl.pallas_call(
        _merge_kernel,
        grid=(t // tm,),
        in_specs=[row(d), row(oa.shape[1]), row(ob.shape[1]), row(d), row(d), full(wa), full(wb), full(wo),
                  pl.BlockSpec((1, d), lambda i: (0, 0))],
        out_specs=[row(d), row(d)],
        out_shape=[jax.ShapeDtypeStruct((t, d), F32), jax.ShapeDtypeStruct((t, d), BF16)],
        compiler_params=_cparams(("parallel",)),
        name="merge_outproj",
    )(x2d, oa, ob, ga, gb, wa, wb, wo, gain.reshape(1, d))


def _ffn_kernel(n_ref, halo_ref, h_ref, wg_ref, wv_ref, cwg_ref, cwv_ref, cbg_ref, cbv_ref, wd_ref, gn_ref,
                o_ref, next_ref, acc_ref, *, tm, seq):
    i = pl.program_id(0)
    j = pl.program_id(1)

    @pl.when(j == 0)
    def _():
        at_start = (i * tm) % seq == 0
        next_ref[0:HALO, :] = jnp.where(at_start, jnp.zeros_like(halo_ref[...]), halo_ref[...])
        next_ref[HALO:, :] = n_ref[...]
        acc_ref[...] = jnp.zeros_like(acc_ref)

    n_ext = next_ref[...]

    def conv(w_ref, cw_ref, cb_ref):
        u = jnp.dot(n_ext, w_ref[...], preferred_element_type=F32)
        out = cb_ref[...]
        for tap in range(CONV_WIDTH):
            lo = HALO - (CONV_WIDTH - 1) + tap
            out = out + cw_ref[tap:tap + 1, :] * u[lo:lo + tm]
        return out

    gate = conv(wg_ref, cwg_ref, cbg_ref)
    val = conv(wv_ref, cwv_ref, cbv_ref)
    act = 0.5 * gate * (1.0 + lax.erf(gate * (2.0 ** -0.5))) * val
    acc_ref[...] += jnp.dot(act.astype(BF16), wd_ref[...], preferred_element_type=F32)

    @pl.when(j == pl.num_programs(1) - 1)
    def _():
        h = h_ref[...] + acc_ref[...]
        y = h * lax.rsqrt(jnp.mean(h * h, axis=-1, keepdims=True) + EPS)
        o_ref[...] = (y * gn_ref[...]).astype(o_ref.dtype)


def _ffn(n2d, h2d, w_up, conv_w, conv_b, w_down, gain, seq, tm, tf):
    t, d = n2d.shape
    d_ff = w_down.shape[0]
    n_f = d_ff // tf
    halo_blocks = tm // HALO
    cb = conv_b.reshape(1, 2 * d_ff)
    return pl.pallas_call(
        functools.partial(_ffn_kernel, tm=tm, seq=seq),
        grid=(t // tm, n_f),
        in_specs=[pl.BlockSpec((tm, d), lambda i, j: (i, 0)),
                  pl.BlockSpec((HALO, d), lambda i, j: (jnp.maximum(i * halo_blocks - 1, 0), 0)),
                  pl.BlockSpec((tm, d), lambda i, j: (i, 0)),
                  pl.BlockSpec((d, tf), lambda i, j: (0, j)),
                  pl.BlockSpec((d, tf), lambda i, j: (0, n_f + j)),
                  pl.BlockSpec((CONV_WIDTH, tf), lambda i, j: (0, j)),
                  pl.BlockSpec((CONV_WIDTH, tf), lambda i, j: (0, n_f + j)),
                  pl.BlockSpec((1, tf), lambda i, j: (0, j)),
                  pl.BlockSpec((1, tf), lambda i, j: (0, n_f + j)),
                  pl.BlockSpec((tf, d), lambda i, j: (j, 0)),
                  pl.BlockSpec((1, d), lambda i, j: (0, 0))],
        out_specs=pl.BlockSpec((tm, d), lambda i, j: (i, 0)),
        out_shape=jax.ShapeDtypeStruct((t, d), F32),
        scratch_shapes=[pltpu.VMEM((HALO + tm, d), BF16), pltpu.VMEM((tm, d), F32)],
        compiler_params=_cparams(("parallel", "arbitrary")),
        name="conv_glu_ffn",
    )(n2d, n2d, h2d, w_up, w_up, conv_w, conv_w, cb, cb, w_down, gain.reshape(1, d))


def _tile(n, pref):
    return pref if n % pref == 0 else n


def kernel(x, norm_mix, w_in, fox_f_bias, hg_lb_logits, hg_norm, w_branch_a, w_branch_b, w_out, norm_ffn, w_up,
           conv_w, conv_b, w_down, norm_final):
    b, s, d = x.shape
    assert norm_mix.shape[0] == 1, "single-layer trunk"
    t = b * s
    hg_w = HG_HEADS * HG_DIM
    fox_w = FOX_HEADS * FOX_DIM
    cuts = [hg_w, 2 * hg_w, 3 * hg_w, 4 * hg_w, 4 * hg_w + fox_w, 4 * hg_w + 2 * fox_w, 4 * hg_w + 3 * fox_w,
            4 * hg_w + 3 * fox_w + FOX_HEADS, 4 * hg_w + 3 * fox_w + FOX_HEADS + d]
    assert w_in.shape[-1] == cuts[-1] + d
    w_bf = w_in[0].astype(BF16)
    w_hq, w_hf, w_hi, w_hg, w_fq, w_fk, w_fv, w_ff, w_ga, w_gb = jnp.split(w_bf, cuts, axis=-1)

    x2d = x.reshape(t, d)
    tm = _tile(t, 512)
    n2d = _rmsnorm(x2d, norm_mix[0], _tile(t, 1024))

    hq, hlf, hk, hv, hgate = _hg_proj(n2d, w_hq, w_hf, w_hi, w_hg, hg_lb_logits, tm, _tile(hg_w, 256))
    fq, fk, fv = _fox_proj(n2d, w_fq, w_fk, w_fv, tm, _tile(fox_w, 512))
    ga, gb = _gate_proj(n2d, w_ga, w_gb, tm, _tile(d, 512))
    c_cum = _fgate(n2d.reshape(b, s, d), w_ff, fox_f_bias[0], _tile(s, 256))

    r3 = lambda a: a.reshape(b, s, a.shape[-1])
    o_a = _hgrn(r3(hq), r3(hlf), r3(hk), r3(hv), r3(hgate), hg_norm[0], _tile(s, 512))
    q_aug, k_aug, v_aug = _fox_pack(r3(fq), r3(fk), r3(fv), c_cum, _tile(s, 512))
    tk = _tile(s, 512)
    o_b = _fox_attention(q_aug, k_aug, v_aug, _tile(s, 4 * tk), tk)

    h1, n_ffn = _merge(x2d, o_a.reshape(t, hg_w), o_b.reshape(t, fox_w), ga, gb, w_branch_a[0].astype(BF16),
                       w_branch_b[0].astype(BF16), w_out[0].astype(BF16), norm_ffn[0], _tile(t, 256))
    out = _ffn(n_ffn, h1, w_up[0].astype(BF16), conv_w[0], conv_b[0], w_down[0].astype(BF16), norm_final,
               s, _tile(s, 512), 256)
    return out.reshape(b, s, d)
```

```python
import functools
import math

import jax
import jax.numpy as jnp
import numpy as np
from jax import lax
from jax.experimental import pallas as pl
from jax.experimental.pallas import tpu as pltpu

F32 = jnp.float32
BF16 = jnp.bfloat16

EPS = 1e-6
HG_HEADS = 8
HG_DIM = 128
HG_CHUNK = 64
HG_SUB = 16
HG_MAX_EXPONENT = 64.0
FOX_HEADS = 16
FOX_DIM = 64
CONV_WIDTH = 3
HALO = 8
NEG_INF = float("-inf")
LOG2E = math.log2(math.e)
LANES = 128

VMEM_LIMIT = 56 * 1024 * 1024


def _cparams(sem):
    return pltpu.CompilerParams(dimension_semantics=sem, vmem_limit_bytes=VMEM_LIMIT)


def _sigmoid(x):
    return 1.0 / (1.0 + jnp.exp(-x))


def _silu(x):
    return x * _sigmoid(x)


def _log_sigmoid(x):
    return jnp.minimum(x, 0.0) - jnp.log1p(jnp.exp(-jnp.abs(x)))


def _split3(x):
    x1 = x.astype(BF16)
    r1 = x - x1.astype(F32)
    x2 = r1.astype(BF16)
    x3 = (r1 - x2.astype(F32)).astype(BF16)
    return x1, x2, x3


def _rmsnorm_kernel(x_ref, g_ref, o_ref):
    x = x_ref[...]
    y = x * lax.rsqrt(jnp.mean(x * x, axis=-1, keepdims=True) + EPS)
    o_ref[...] = (y * g_ref[...]).astype(o_ref.dtype)


def _rmsnorm(x2d, gain, tm):
    t, d = x2d.shape
    return pl.pallas_call(
        _rmsnorm_kernel,
        grid=(t // tm,),
        in_specs=[pl.BlockSpec((tm, d), lambda i: (i, 0)), pl.BlockSpec((1, d), lambda i: (0, 0))],
        out_specs=pl.BlockSpec((tm, d), lambda i: (i, 0)),
        out_shape=jax.ShapeDtypeStruct((t, d), BF16),
        compiler_params=_cparams(("parallel",)),
        name="rmsnorm_in",
    )(x2d, gain.reshape(1, d))


def _hg_proj_kernel(n_ref, wq_ref, wf_ref, wi_ref, wg_ref, lbl_ref, q_ref, lf_ref, k_ref, v_ref, g_ref):
    n = n_ref[...]
    dot = lambda w: jnp.dot(n, w[...], preferred_element_type=F32)
    q_ref[...] = _silu(dot(wq_ref))
    lbl = lbl_ref[...]
    e = jnp.exp(lbl - jnp.max(lbl, axis=0, keepdims=True))
    lb = e[0:1] / jnp.sum(e, axis=0, keepdims=True)
    sig = _sigmoid(dot(wf_ref))
    f = lb + (1.0 - lb) * sig
    lf_ref[...] = jnp.log(f)
    k_ref[...] = 1.0 - f
    v_ref[...] = dot(wi_ref)
    g_ref[...] = _silu(dot(wg_ref))


def _hg_proj(n2d, wq, wf, wi, wg, lb_logits, tm, tn):
    t, d = n2d.shape
    w = wq.shape[1]
    wspec = pl.BlockSpec((d, tn), lambda i, j: (0, j))
    ospec = pl.BlockSpec((tm, tn), lambda i, j: (i, j))
    oshape = jax.ShapeDtypeStruct((t, w), F32)
    return pl.pallas_call(
        _hg_proj_kernel,
        grid=(t // tm, w // tn),
        in_specs=[pl.BlockSpec((tm, d), lambda i, j: (i, 0)), wspec, wspec, wspec, wspec,
                  pl.BlockSpec((lb_logits.shape[0], tn), lambda i, j: (0, j))],
        out_specs=[ospec] * 5,
        out_shape=[oshape] * 5,
        compiler_params=_cparams(("parallel", "arbitrary")),
        name="hgrn_proj",
    )(n2d, wq, wf, wi, wg, lb_logits)


def _fox_proj_kernel(n_ref, wq_ref, wk_ref, wv_ref, q_ref, k_ref, v_ref):
    n = n_ref[...]
    dot = lambda w: jnp.dot(n, w[...], preferred_element_type=F32)
    q_ref[...] = (dot(wq_ref) * (FOX_DIM ** -0.5 * LOG2E)).astype(q_ref.dtype)
    k_ref[...] = dot(wk_ref).astype(k_ref.dtype)
    v_ref[...] = dot(wv_ref).astype(v_ref.dtype)


def _fox_proj(n2d, wq, wk, wv, tm, tn):
    t, d = n2d.shape
    w = wq.shape[1]
    wspec = pl.BlockSpec((d, tn), lambda i, j: (0, j))
    ospec = pl.BlockSpec((tm, tn), lambda i, j: (i, j))
    oshape = jax.ShapeDtypeStruct((t, w), BF16)
    return pl.pallas_call(
        _fox_proj_kernel,
        grid=(t // tm, w // tn),
        in_specs=[pl.BlockSpec((tm, d), lambda i, j: (i, 0)), wspec, wspec, wspec],
        out_specs=[ospec] * 3,
        out_shape=[oshape] * 3,
        compiler_params=_cparams(("parallel", "arbitrary")),
        name="fox_proj",
    )(n2d, wq, wk, wv)


def _gate_proj_kernel(n_ref, wa_ref, wb_ref, ga_ref, gb_ref):
    n = n_ref[...]
    ga_ref[...] = _sigmoid(jnp.dot(n, wa_ref[...], preferred_element_type=F32))
    gb_ref[...] = _sigmoid(jnp.dot(n, wb_ref[...], preferred_element_type=F32))


def _gate_proj(n2d, wa, wb, tm, tn):
    t, d = n2d.shape
    w = wa.shape[1]
    wspec = pl.BlockSpec((d, tn), lambda i, j: (0, j))
    ospec = pl.BlockSpec((tm, tn), lambda i, j: (i, j))
    oshape = jax.ShapeDtypeStruct((t, w), F32)
    return pl.pallas_call(
        _gate_proj_kernel,
        grid=(t // tm, w // tn),
        in_specs=[pl.BlockSpec((tm, d), lambda i, j: (i, 0)), wspec, wspec],
        out_specs=[ospec] * 2,
        out_shape=[oshape] * 2,
        compiler_params=_cparams(("parallel", "arbitrary")),
        name="gate_proj",
    )(n2d, wa, wb)


CP_WIDTH = 4 * FOX_HEADS


def _fgate_kernel(n_ref, w_ref, bias_ref, spread_ref, one_ref, c_ref, cp_ref, carry_ref):
    @pl.when(pl.program_id(1) == 0)
    def _():
        carry_ref[...] = jnp.zeros_like(carry_ref)

    n = n_ref[...]
    tm = n.shape[0]
    r = lax.broadcasted_iota(jnp.int32, (tm, tm), 0)
    c = lax.broadcasted_iota(jnp.int32, (tm, tm), 1)
    lower = (c <= r).astype(BF16)
    lf = _log_sigmoid(jnp.dot(n, w_ref[...], preferred_element_type=F32) + bias_ref[...])
    cs = carry_ref[...]
    for piece in _split3(lf):
        cs = cs + jnp.dot(lower, piece, preferred_element_type=F32)
    carry_ref[...] = cs[tm - 1:tm, :]
    c2 = cs * LOG2E
    c_ref[...] = c2
    cp = one_ref[...]
    for i, piece in enumerate(_split3(c2)):
        cp = cp + jnp.dot(piece, spread_ref[i], preferred_element_type=F32)
    cp_ref[...] = cp.astype(cp_ref.dtype)


def _fgate(n3d, w, bias, tm):
    b, s, d = n3d.shape
    h = w.shape[1]
    spread = np.zeros((3, h, CP_WIDTH), np.float32)
    for i in range(3):
        spread[i, np.arange(h), i * h + np.arange(h)] = 1.0
    one = np.zeros((1, CP_WIDTH), np.float32)
    one[0, 3 * h] = 1.0
    return pl.pallas_call(
        _fgate_kernel,
        grid=(b, s // tm),
        in_specs=[pl.BlockSpec((None, tm, d), lambda i, j: (i, j, 0)),
                  pl.BlockSpec((d, h), lambda i, j: (0, 0)),
                  pl.BlockSpec((1, h), lambda i, j: (0, 0)),
                  pl.BlockSpec((3, h, CP_WIDTH), lambda i, j: (0, 0, 0)),
                  pl.BlockSpec((1, CP_WIDTH), lambda i, j: (0, 0))],
        out_specs=[pl.BlockSpec((None, tm, h), lambda i, j: (i, j, 0)),
                   pl.BlockSpec((None, tm, CP_WIDTH), lambda i, j: (i, j, 0))],
        out_shape=[jax.ShapeDtypeStruct((b, s, h), F32), jax.ShapeDtypeStruct((b, s, CP_WIDTH), BF16)],
        scratch_shapes=[pltpu.VMEM((1, h), F32)],
        compiler_params=_cparams(("parallel", "arbitrary")),
        name="fox_fgate_cumsum",
    )(n3d, w, bias.reshape(1, h), jnp.asarray(spread, BF16), jnp.asarray(one))


def _causal_chunk():
    r_i = lax.broadcasted_iota(jnp.int32, (HG_CHUNK, HG_CHUNK), 0)
    c_i = lax.broadcasted_iota(jnp.int32, (HG_CHUNK, HG_CHUNK), 1)
    return c_i <= r_i


def _chunk_cumsum(lf):
    lower = _causal_chunk().astype(BF16)
    b = jnp.zeros(lf.shape, F32)
    for piece in _split3(lf):
        b = b + jnp.dot(lower, piece, preferred_element_type=F32)
    return b


def _hgrn_chunk_single_ref(q, lf, k, v, state_t):
    nt = (((1,), (1,)), ((), ()))
    b = _chunk_cumsum(lf)
    q_t = (q * jnp.exp(b)).astype(BF16)
    k_up = k * jnp.exp(-b)
    sc = lax.dot_general(q_t, k_up.astype(BF16), nt, preferred_element_type=F32)
    sc = jnp.where(_causal_chunk(), sc, 0.0)
    v_bf = v.astype(BF16)
    o = lax.dot_general(q_t, state_t.astype(BF16), nt, preferred_element_type=F32)
    o = o + jnp.dot(sc.astype(BF16), v_bf, preferred_element_type=F32)
    dec = jnp.exp(b[HG_CHUNK - 1:HG_CHUNK])
    upd = lax.dot_general(v_bf, (k_up * dec).astype(BF16), (((0,), (0,)), ((), ())), preferred_element_type=F32)
    return o, state_t * dec + upd


def _hgrn_chunk(q, lf, k, v, state_t):
    c = HG_CHUNK
    n_sub = c // HG_SUB
    b = _chunk_cumsum(lf)

    qe = (q * jnp.exp(b)).astype(BF16)
    o = lax.dot_general(qe, state_t.astype(BF16), (((1,), (1,)), ((), ())), preferred_element_type=F32)

    o_parts = []
    t_iota = lax.broadcasted_iota(jnp.int32, (HG_SUB, HG_DIM), 0)
    for i in range(n_sub):
        lo = i * HG_SUB
        q_i = q[lo:lo + HG_SUB]
        b_i = b[lo:lo + HG_SUB]
        o_i = o[lo:lo + HG_SUB]
        if i > 0:
            ref = b[lo - 1:lo]
            q_t = (q_i * jnp.exp(b_i - ref)).astype(BF16)
            k_t = (k[:lo] * jnp.exp(ref - b[:lo])).astype(BF16)
            sc = lax.dot_general(q_t, k_t, (((1,), (1,)), ((), ())), preferred_element_type=F32)
            o_i = o_i + jnp.dot(sc.astype(BF16), v[:lo].astype(BF16), preferred_element_type=F32)
        for s in range(HG_SUB):
            row = lo + s
            rel = jnp.where(t_iota >= s, b_i - b[row:row + 1], NEG_INF)
            p = q_i * jnp.exp(rel) * k[row:row + 1]
            o_i = o_i + jnp.sum(p, axis=-1, keepdims=True) * v[row:row + 1]
        o_parts.append(o_i)
    o = jnp.concatenate(o_parts, axis=0)

    b_end = b[c - 1:c]
    k_dec = (k * jnp.exp(b_end - b)).astype(BF16)
    upd = lax.dot_general(v.astype(BF16), k_dec, (((0,), (0,)), ((), ())), preferred_element_type=F32)
    return o, state_t * jnp.exp(b_end) + upd


def _hgrn_kernel(q_ref, lf_ref, k_ref, v_ref, g_ref, gn_ref, o_ref, state_ref, *, n_chunks):
    @pl.when(pl.program_id(2) == 0)
    def _():
        state_ref[...] = jnp.zeros_like(state_ref)

    def run_chunk(chunk_fn, rows, state):
        o, new_state = chunk_fn(q_ref[rows, :], lf_ref[rows, :], k_ref[rows, :], v_ref[rows, :], state)
        y = o * lax.rsqrt(jnp.mean(o * o, axis=-1, keepdims=True) + EPS)
        o_ref[rows, :] = (y * gn_ref[...] * g_ref[rows, :]).astype(o_ref.dtype)
        return new_state

    single_ref_ok = jnp.min(lf_ref[...]) * HG_CHUNK > -HG_MAX_EXPONENT

    @pl.when(single_ref_ok)
    def _():
        state = state_ref[...]
        for ci in range(n_chunks):
            state = run_chunk(_hgrn_chunk_single_ref, pl.ds(ci * HG_CHUNK, HG_CHUNK), state)
        state_ref[...] = state

    @pl.when(jnp.logical_not(single_ref_ok))
    def _():
        def body(ci, carry):
            rows = pl.ds(pl.multiple_of(ci * HG_CHUNK, HG_CHUNK), HG_CHUNK)
            state_ref[...] = run_chunk(_hgrn_chunk, rows, state_ref[...])
            return carry

        lax.fori_loop(0, n_chunks, body, 0)


def _hgrn(q, lf, k, v, g, g_norm, tc):
    b, s, w = q.shape
    heads = w // HG_DIM
    spec = pl.BlockSpec((None, tc, HG_DIM), lambda bi, hi, ti: (bi, ti, hi))
    return pl.pallas_call(
        functools.partial(_hgrn_kernel, n_chunks=tc // HG_CHUNK),
        grid=(b, heads, s // tc),
        in_specs=[spec] * 5 + [pl.BlockSpec((1, HG_DIM), lambda bi, hi, ti: (0, 0))],
        out_specs=spec,
        out_shape=jax.ShapeDtypeStruct((b, s, w), BF16),
        scratch_shapes=[pltpu.VMEM((HG_DIM, HG_DIM), F32)],
        compiler_params=_cparams(("parallel", "parallel", "arbitrary")),
        name="hgrn2_mixer",
    )(q, lf, k, v, g, g_norm.reshape(1, HG_DIM))


AUG_C = FOX_DIM
AUG_ONE = FOX_DIM + 3
AUG_QN = FOX_DIM + 6
NORM_MARGIN = 1.02
SKIP_LOG2 = 150.0


def _placement_constants(heads):
    pq = np.zeros((CP_WIDTH, heads * LANES), np.float32)
    pk = np.zeros((CP_WIDTH, heads * LANES), np.float32)
    pn = np.zeros((heads, heads * LANES), np.float32)
    hs = np.zeros((heads * FOX_DIM, heads), np.float32)
    for h in range(heads):
        for i in range(3):
            pq[i * heads + h, h * LANES + AUG_C + i] = 1.0
            pk[i * heads + h, h * LANES + AUG_ONE + i] = -1.0
        pq[3 * heads, h * LANES + AUG_ONE:h * LANES + AUG_ONE + 3] = 1.0
        pk[3 * heads, h * LANES + AUG_C:h * LANES + AUG_C + 3] = 1.0
        pn[h, h * LANES + AUG_QN] = 1.0
        hs[h * FOX_DIM:(h + 1) * FOX_DIM, h] = 1.0
    return tuple(jnp.asarray(a, BF16) for a in (pq, pk, pn, hs))


def _fox_pack_kernel(q_ref, k_ref, v_ref, cp_ref, c_ref, pq_ref, pk_ref, pn_ref, hs_ref,
                     qa_ref, ka_ref, va_ref, st_ref):
    q = q_ref[...]
    k = k_ref[...]
    v = v_ref[...]
    cp = cp_ref[...]
    tm = q.shape[0]
    heads = qa_ref.shape[0]
    hs = hs_ref[...]
    qn = jnp.sqrt(jnp.dot(q * q, hs, preferred_element_type=F32)) * NORM_MARGIN
    kn = jnp.sqrt(jnp.dot(k * k, hs, preferred_element_type=F32)) * NORM_MARGIN
    tail_q = (jnp.dot(cp, pq_ref[...], preferred_element_type=F32)
              + jnp.dot(qn.astype(BF16), pn_ref[...], preferred_element_type=F32)).astype(BF16)
    tail_k = jnp.dot(cp, pk_ref[...], preferred_element_type=F32).astype(BF16)
    lane = lax.broadcasted_iota(jnp.int32, (tm, LANES), 1)
    in_data = lane < FOX_DIM
    tail_v = jnp.where(lane == FOX_DIM, 1.0, 0.0).astype(BF16)
    for h in range(heads):
        pair = slice((h // 2) * LANES, (h // 2 + 1) * LANES)
        mine = slice(h * LANES, (h + 1) * LANES)
        data = (lambda x: x[:, pair]) if h % 2 == 0 else (lambda x: pltpu.roll(x[:, pair], FOX_DIM, 1))
        qa_ref[h] = jnp.where(in_data, data(q), tail_q[:, mine])
        ka_ref[h] = jnp.where(in_data, data(k), tail_k[:, mine])
        va_ref[h] = jnp.where(in_data, data(v), tail_v)
    st_ref[0:1, :] = jnp.max(kn, axis=0, keepdims=True)
    st_ref[1:2, :] = c_ref[tm - 1:tm, :]


def _fox_pack(q, k, v, cp, c, tm):
    b, s, w = q.shape
    heads = c.shape[-1]
    consts = _placement_constants(heads)
    row = lambda width: pl.BlockSpec((None, tm, width), lambda bi, ti: (bi, ti, 0))
    full = lambda a: pl.BlockSpec(a.shape, lambda bi, ti: (0, 0))
    out = pl.BlockSpec((None, heads, tm, LANES), lambda bi, ti: (bi, 0, ti, 0))
    oshape = jax.ShapeDtypeStruct((b, heads, s, LANES), BF16)
    return pl.pallas_call(
        _fox_pack_kernel,
        grid=(b, s // tm),
        in_specs=[row(w), row(w), row(w), row(CP_WIDTH), row(heads)] + [full(a) for a in consts],
        out_specs=[out] * 3 + [pl.BlockSpec((None, None, 2, heads), lambda bi, ti: (bi, ti, 0, 0))],
        out_shape=[oshape] * 3 + [jax.ShapeDtypeStruct((b, s // tm, 2, heads), F32)],
        compiler_params=_cparams(("parallel", "parallel")),
        name="fox_pack",
    )(q, k, v, cp, c, *consts)


def _fox_kernel(kmax_ref, clast_ref, q_ref, k_ref, v_ref, o_ref, m_ref, acc_ref, *, tq, tk):
    qi = pl.program_id(2)
    row_a = pl.program_id(0) * (2 * pl.num_programs(1)) + 2 * pl.program_id(1)
    m_ref[...] = jnp.full_like(m_ref, NEG_INF)
    acc_ref[...] = jnp.zeros_like(acc_ref)

    def block(row0, n_rows, kv0, masked):
        rows = pl.ds(row0, n_rows)
        cols = pl.ds(kv0, tk)
        if masked:
            causal = (lax.broadcasted_iota(jnp.int32, (n_rows, tk), 1)
                      <= lax.broadcasted_iota(jnp.int32, (n_rows, tk), 0))
        for a in range(2):
            s = lax.dot_general(q_ref[a, rows, :], k_ref[a, cols, :], (((1,), (1,)), ((), ())),
                                preferred_element_type=F32)
            if masked:
                s = jnp.where(causal, s, NEG_INF)
            m_prev = m_ref[a, rows, :]
            m_new = jnp.maximum(m_prev, jnp.max(s, axis=-1, keepdims=True))
            alpha = jnp.exp2(m_prev - m_new)
            p = jnp.exp2(s - m_new[:, 0:1])
            acc_ref[a, rows, :] = alpha * acc_ref[a, rows, :] + jnp.dot(
                p.astype(BF16), v_ref[a, cols, :], preferred_element_type=F32)
            m_ref[a, rows, :] = m_new

    n_diag = tq // tk
    n_full = qi * n_diag

    for j in range(n_diag):
        kv0 = pl.multiple_of((n_full + j) * tk, tk)
        block(j * tk, tk, kv0, True)
        if j + 1 < n_diag:
            block((j + 1) * tk, tq - (j + 1) * tk, kv0, False)

    n_tiles = kmax_ref.shape[1]
    slack = []
    for a in range(2):
        kmax = kmax_ref[row_a + a, 0]
        for j in range(1, n_tiles):
            kmax = jnp.maximum(kmax, kmax_ref[row_a + a, j])
        tail = q_ref[a, :, AUG_C:AUG_QN + 1].astype(F32)
        c2_t = tail[:, 0:1] + tail[:, 1:2] + tail[:, 2:3]
        qn_t = tail[:, AUG_QN - AUG_C:AUG_QN - AUG_C + 1]
        slack.append(jnp.max(qn_t * kmax + c2_t - m_ref[a, :, 0:1]))

    def tile_is_live(j):
        live = [slack[a] - clast_ref[row_a + a, j] > -SKIP_LOG2 for a in range(2)]
        return live[0] | live[1]

    def cond(j):
        return (j >= 0) & tile_is_live(jnp.maximum(j, 0))

    def body(j):
        block(0, tq, pl.multiple_of(j * tk, tk), False)
        return j - 1

    lax.while_loop(cond, body, n_full - 1)

    lane = lax.broadcasted_iota(jnp.int32, (tq, LANES), 1)
    acc_a = acc_ref[0]
    acc_b = acc_ref[1]
    o_a = acc_a / acc_a[:, FOX_DIM:FOX_DIM + 1]
    o_b = acc_b / acc_b[:, FOX_DIM:FOX_DIM + 1]
    o_ref[...] = jnp.where(lane < FOX_DIM, o_a, pltpu.roll(o_b, FOX_DIM, 1)).astype(o_ref.dtype)


def _fox_attention(q, k, v, kmax, clast, tq, tk):
    b, h, s, _ = q.shape
    kv_spec = pl.BlockSpec((None, 2, s, LANES), lambda bi, hi, qi: (bi, hi, 0, 0))
    smem = pl.BlockSpec(memory_space=pltpu.SMEM)
    return pl.pallas_call(
        functools.partial(_fox_kernel, tq=tq, tk=tk),
        grid=(b, h // 2, s // tq),
        in_specs=[smem, smem,
                  pl.BlockSpec((None, 2, tq, LANES), lambda bi, hi, qi: (bi, hi, qi, 0)), kv_spec, kv_spec],
        out_specs=pl.BlockSpec((None, tq, LANES), lambda bi, hi, qi: (bi, qi, hi)),
        out_shape=jax.ShapeDtypeStruct((b, s, (h // 2) * LANES), BF16),
        scratch_shapes=[pltpu.VMEM((2, tq, LANES), F32), pltpu.VMEM((2, tq, LANES), F32)],
        compiler_params=_cparams(("parallel", "parallel", "arbitrary")),
        name="fox_attention",
    )(kmax, clast, q, k, v)


def _merge_kernel(x_ref, oa_ref, ob_ref, ga_ref, gb_ref, wa_ref, wb_ref, wo_ref, gn_ref, h_ref, n_ref):
    ya = jnp.dot(oa_ref[...], wa_ref[...], preferred_element_type=F32)
    yb = jnp.dot(ob_ref[...], wb_ref[...], preferred_element_type=F32)
    merged = ga_ref[...] * ya + gb_ref[...] * yb
    h = x_ref[...] + jnp.dot(merged.astype(BF16), wo_ref[...], preferred_element_type=F32)
    h_ref[...] = h
    y = h * lax.rsqrt(jnp.mean(h * h, axis=-1, keepdims=True) + EPS)
    n_ref[...] = (y * gn_ref[...]).astype(n_ref.dtype)


def _merge(x2d, oa, ob, ga, gb, wa, wb, wo, gain, tm):
    t, d = x2d.shape
    row = lambda width: pl.BlockSpec((tm, width), lambda i: (i, 0))
    full = lambda a: pl.BlockSpec(a.shape, lambda i: (0, 0))
    return pl.pallas_call(
        _merge_kernel,
        grid=(t // tm,),
        in_specs=[row(d), row(oa.shape[1]), row(ob.shape[1]), row(d), row(d), full(wa), full(wb), full(wo),
                  pl.BlockSpec((1, d), lambda i: (0, 0))],
        out_specs=[row(d), row(d)],
        out_shape=[jax.ShapeDtypeStruct((t, d), F32), jax.ShapeDtypeStruct((t, d), BF16)],
        compiler_params=_cparams(("parallel",)),
        name="merge_outproj",
    )(x2d, oa, ob, ga, gb, wa, wb, wo, gain.reshape(1, d))


def _ffn_kernel(n_ref, halo_ref, h_ref, wg_ref, wv_ref, cwg_ref, cwv_ref, cbg_ref, cbv_ref, wd_ref, gn_ref,
                o_ref, next_ref, acc_ref, *, tm, seq):
    i = pl.program_id(0)
    j = pl.program_id(1)

    @pl.when(j == 0)
    def _():
        at_start = (i * tm) % seq == 0
        next_ref[0:HALO, :] = jnp.where(at_start, jnp.zeros_like(halo_ref[...]), halo_ref[...])
        next_ref[HALO:, :] = n_ref[...]
        acc_ref[...] = jnp.zeros_like(acc_ref)

    n_ext = next_ref[...]

    def conv(w_ref, cw_ref, cb_ref):
        u = jnp.dot(n_ext, w_ref[...], preferred_element_type=F32)
        out = cb_ref[...]
        for tap in range(CONV_WIDTH):
            lo = HALO - (CONV_WIDTH - 1) + tap
            out = out + cw_ref[tap:tap + 1, :] * u[lo:lo + tm]
        return out

    gate = conv(wg_ref, cwg_ref, cbg_ref)
    val = conv(wv_ref, cwv_ref, cbv_ref)
    act = 0.5 * gate * (1.0 + lax.erf(gate * (2.0 ** -0.5))) * val
    acc_ref[...] += jnp.dot(act.astype(BF16), wd_ref[...], preferred_element_type=F32)

    @pl.when(j == pl.num_programs(1) - 1)
    def _():
        h = h_ref[...] + acc_ref[...]
        y = h * lax.rsqrt(jnp.mean(h * h, axis=-1, keepdims=True) + EPS)
        o_ref[...] = (y * gn_ref[...]).astype(o_ref.dtype)


def _ffn(n2d, h2d, w_up, conv_w, conv_b, w_down, gain, seq, tm, tf):
    t, d = n2d.shape
    d_ff = w_down.shape[0]
    n_f = d_ff // tf
    halo_blocks = tm // HALO
    cb = conv_b.reshape(1, 2 * d_ff)
    return pl.pallas_call(
        functools.partial(_ffn_kernel, tm=tm, seq=seq),
        grid=(t // tm, n_f),
        in_specs=[pl.BlockSpec((tm, d), lambda i, j: (i, 0)),
                  pl.BlockSpec((HALO, d), lambda i, j: (jnp.maximum(i * halo_blocks - 1, 0), 0)),
                  pl.BlockSpec((tm, d), lambda i, j: (i, 0)),
                  pl.BlockSpec((d, tf), lambda i, j: (0, j)),
                  pl.BlockSpec((d, tf), lambda i, j: (0, n_f + j)),
                  pl.BlockSpec((CONV_WIDTH, tf), lambda i, j: (0, j)),
                  pl.BlockSpec((CONV_WIDTH, tf), lambda i, j: (0, n_f + j)),
                  pl.BlockSpec((1, tf), lambda i, j: (0, j)),
                  pl.BlockSpec((1, tf), lambda i, j: (0, n_f + j)),
                  pl.BlockSpec((tf, d), lambda i, j: (j, 0)),
                  pl.BlockSpec((1, d), lambda i, j: (0, 0))],
        out_specs=pl.BlockSpec((tm, d), lambda i, j: (i, 0)),
        out_shape=jax.ShapeDtypeStruct((t, d), F32),
        scratch_shapes=[pltpu.VMEM((HALO + tm, d), BF16), pltpu.VMEM((tm, d), F32)],
        compiler_params=_cparams(("parallel", "arbitrary")),
        name="conv_glu_ffn",
    )(n2d, n2d, h2d, w_up, w_up, conv_w, conv_w, cb, cb, w_down, gain.reshape(1, d))


def _tile(n, pref):
    return pref if n % pref == 0 else n


def kernel(x, norm_mix, w_in, fox_f_bias, hg_lb_logits, hg_norm, w_branch_a, w_branch_b, w_out, norm_ffn, w_up,
           conv_w, conv_b, w_down, norm_final):
    b, s, d = x.shape
    assert norm_mix.shape[0] == 1, "single-layer trunk"
    t = b * s
    hg_w = HG_HEADS * HG_DIM
    fox_w = FOX_HEADS * FOX_DIM
    cuts = [hg_w, 2 * hg_w, 3 * hg_w, 4 * hg_w, 4 * hg_w + fox_w, 4 * hg_w + 2 * fox_w, 4 * hg_w + 3 * fox_w,
            4 * hg_w + 3 * fox_w + FOX_HEADS, 4 * hg_w + 3 * fox_w + FOX_HEADS + d]
    assert w_in.shape[-1] == cuts[-1] + d
    w_bf = w_in[0].astype(BF16)
    w_hq, w_hf, w_hi, w_hg, w_fq, w_fk, w_fv, w_ff, w_ga, w_gb = jnp.split(w_bf, cuts, axis=-1)

    x2d = x.reshape(t, d)
    tm = _tile(t, 512)
    n2d = _rmsnorm(x2d, norm_mix[0], _tile(t, 1024))

    hq, hlf, hk, hv, hgate = _hg_proj(n2d, w_hq, w_hf, w_hi, w_hg, hg_lb_logits, tm, _tile(hg_w, 256))
    fq, fk, fv = _fox_proj(n2d, w_fq, w_fk, w_fv, tm, _tile(fox_w, 512))
    ga, gb = _gate_proj(n2d, w_ga, w_gb, tm, _tile(d, 512))
    c2, c_pieces = _fgate(n2d.reshape(b, s, d), w_ff, fox_f_bias[0], _tile(s, 256))

    r3 = lambda a: a.reshape(b, s, a.shape[-1])
    o_a = _hgrn(r3(hq), r3(hlf), r3(hk), r3(hv), r3(hgate), hg_norm[0], _tile(s, 512))
    tk = _tile(s, 512)
    q_aug, k_aug, v_aug, kv_stats = _fox_pack(r3(fq), r3(fk), r3(fv), c_pieces, c2, tk)
    kv_stats = kv_stats.transpose(2, 0, 3, 1).reshape(2, b * FOX_HEADS, s // tk)
    o_b = _fox_attention(q_aug, k_aug, v_aug, kv_stats[0], kv_stats[1], _tile(s, 4 * tk), tk)

    h1, n_ffn = _merge(x2d, o_a.reshape(t, hg_w), o_b.reshape(t, fox_w), ga, gb, w_branch_a[0].astype(BF16),
                       w_branch_b[0].astype(BF16), w_out[0].astype(BF16), norm_ffn[0], _tile(t, 256))
    out = _ffn(n_ffn, h1, w_up[0].astype(BF16), conv_w[0], conv_b[0], w_down[0].astype(BF16), norm_final,
               s, _tile(s, 512), 256)
    return out.reshape(b, s, d)
```

```python
import functools
import math

import jax
import jax.numpy as jnp
import numpy as np
from jax import lax
from jax.experimental import pallas as pl
from jax.experimental.pallas import tpu as pltpu

F32 = jnp.float32
BF16 = jnp.bfloat16

EPS = 1e-6
HG_HEADS = 8
HG_DIM = 128
HG_CHUNK = 64
HG_SUB = 16
HG_MAX_EXPONENT = 64.0
FOX_HEADS = 16
FOX_DIM = 64
CONV_WIDTH = 3
HALO = 8
NEG_INF = float("-inf")
LOG2E = math.log2(math.e)
LANES = 128

VMEM_LIMIT = 56 * 1024 * 1024


def _cparams(sem):
    return pltpu.CompilerParams(dimension_semantics=sem, vmem_limit_bytes=VMEM_LIMIT)


def _sigmoid(x):
    return 1.0 / (1.0 + jnp.exp(-x))


def _silu(x):
    return x * _sigmoid(x)


def _log_sigmoid(x):
    return jnp.minimum(x, 0.0) - jnp.log1p(jnp.exp(-jnp.abs(x)))


def _split3(x):
    x1 = x.astype(BF16)
    r1 = x - x1.astype(F32)
    x2 = r1.astype(BF16)
    x3 = (r1 - x2.astype(F32)).astype(BF16)
    return x1, x2, x3


def _rmsnorm_kernel(x_ref, g_ref, o_ref):
    x = x_ref[...]
    y = x * lax.rsqrt(jnp.mean(x * x, axis=-1, keepdims=True) + EPS)
    o_ref[...] = (y * g_ref[...]).astype(o_ref.dtype)


def _rmsnorm(x2d, gain, tm):
    t, d = x2d.shape
    return pl.pallas_call(
        _rmsnorm_kernel,
        grid=(t // tm,),
        in_specs=[pl.BlockSpec((tm, d), lambda i: (i, 0)), pl.BlockSpec((1, d), lambda i: (0, 0))],
        out_specs=pl.BlockSpec((tm, d), lambda i: (i, 0)),
        out_shape=jax.ShapeDtypeStruct((t, d), BF16),
        compiler_params=_cparams(("parallel",)),
        name="rmsnorm_in",
    )(x2d, gain.reshape(1, d))


def _hg_proj_kernel(n_ref, wq_ref, wf_ref, wi_ref, wg_ref, lbl_ref, q_ref, lf_ref, k_ref, v_ref, g_ref):
    n = n_ref[...]
    dot = lambda w: jnp.dot(n, w[...], preferred_element_type=F32)
    q_ref[...] = _silu(dot(wq_ref))
    lbl = lbl_ref[...]
    e = jnp.exp(lbl - jnp.max(lbl, axis=0, keepdims=True))
    lb = e[0:1] / jnp.sum(e, axis=0, keepdims=True)
    sig = _sigmoid(dot(wf_ref))
    f = lb + (1.0 - lb) * sig
    lf_ref[...] = jnp.log(f)
    k_ref[...] = 1.0 - f
    v_ref[...] = dot(wi_ref)
    g_ref[...] = _silu(dot(wg_ref))


def _hg_proj(n2d, wq, wf, wi, wg, lb_logits, tm, tn):
    t, d = n2d.shape
    w = wq.shape[1]
    wspec = pl.BlockSpec((d, tn), lambda i, j: (0, j))
    ospec = pl.BlockSpec((tm, tn), lambda i, j: (i, j))
    oshape = jax.ShapeDtypeStruct((t, w), F32)
    return pl.pallas_call(
        _hg_proj_kernel,
        grid=(t // tm, w // tn),
        in_specs=[pl.BlockSpec((tm, d), lambda i, j: (i, 0)), wspec, wspec, wspec, wspec,
                  pl.BlockSpec((lb_logits.shape[0], tn), lambda i, j: (0, j))],
        out_specs=[ospec] * 5,
        out_shape=[oshape] * 5,
        compiler_params=_cparams(("parallel", "arbitrary")),
        name="hgrn_proj",
    )(n2d, wq, wf, wi, wg, lb_logits)


def _fox_proj_kernel(n_ref, wq_ref, wk_ref, wv_ref, q_ref, k_ref, v_ref):
    n = n_ref[...]
    dot = lambda w: jnp.dot(n, w[...], preferred_element_type=F32)
    q_ref[...] = (dot(wq_ref) * (FOX_DIM ** -0.5 * LOG2E)).astype(q_ref.dtype)
    k_ref[...] = dot(wk_ref).astype(k_ref.dtype)
    v_ref[...] = dot(wv_ref).astype(v_ref.dtype)


def _fox_proj(n2d, wq, wk, wv, tm, tn):
    t, d = n2d.shape
    w = wq.shape[1]
    wspec = pl.BlockSpec((d, tn), lambda i, j: (0, j))
    ospec = pl.BlockSpec((tm, tn), lambda i, j: (i, j))
    oshape = jax.ShapeDtypeStruct((t, w), BF16)
    return pl.pallas_call(
        _fox_proj_kernel,
        grid=(t // tm, w // tn),
        in_specs=[pl.BlockSpec((tm, d), lambda i, j: (i, 0)), wspec, wspec, wspec],
        out_specs=[ospec] * 3,
        out_shape=[oshape] * 3,
        compiler_params=_cparams(("parallel", "arbitrary")),
        name="fox_proj",
    )(n2d, wq, wk, wv)


def _gate_proj_kernel(n_ref, wa_ref, wb_ref, ga_ref, gb_ref):
    n = n_ref[...]
    ga_ref[...] = _sigmoid(jnp.dot(n, wa_ref[...], preferred_element_type=F32))
    gb_ref[...] = _sigmoid(jnp.dot(n, wb_ref[...], preferred_element_type=F32))


def _gate_proj(n2d, wa, wb, tm, tn):
    t, d = n2d.shape
    w = wa.shape[1]
    wspec = pl.BlockSpec((d, tn), lambda i, j: (0, j))
    ospec = pl.BlockSpec((tm, tn), lambda i, j: (i, j))
    oshape = jax.ShapeDtypeStruct((t, w), F32)
    return pl.pallas_call(
        _gate_proj_kernel,
        grid=(t // tm, w // tn),
        in_specs=[pl.BlockSpec((tm, d), lambda i, j: (i, 0)), wspec, wspec],
        out_specs=[ospec] * 2,
        out_shape=[oshape] * 2,
        compiler_params=_cparams(("parallel", "arbitrary")),
        name="gate_proj",
    )(n2d, wa, wb)


CP_WIDTH = 4 * FOX_HEADS


def _fgate_kernel(n_ref, w_ref, bias_ref, spread_ref, one_ref, c_ref, cp_ref, carry_ref):
    @pl.when(pl.program_id(1) == 0)
    def _():
        carry_ref[...] = jnp.zeros_like(carry_ref)

    n = n_ref[...]
    tm = n.shape[0]
    r = lax.broadcasted_iota(jnp.int32, (tm, tm), 0)
    c = lax.broadcasted_iota(jnp.int32, (tm, tm), 1)
    lower = (c <= r).astype(BF16)
    lf = _log_sigmoid(jnp.dot(n, w_ref[...], preferred_element_type=F32) + bias_ref[...])
    cs = carry_ref[...]
    for piece in _split3(lf):
        cs = cs + jnp.dot(lower, piece, preferred_element_type=F32)
    carry_ref[...] = cs[tm - 1:tm, :]
    c2 = cs * LOG2E
    c_ref[...] = c2
    cp = one_ref[...]
    for i, piece in enumerate(_split3(c2)):
        cp = cp + jnp.dot(piece, spread_ref[i], preferred_element_type=F32)
    cp_ref[...] = cp.astype(cp_ref.dtype)


def _fgate(n3d, w, bias, tm):
    b, s, d = n3d.shape
    h = w.shape[1]
    spread = np.zeros((3, h, CP_WIDTH), np.float32)
    for i in range(3):
        spread[i, np.arange(h), i * h + np.arange(h)] = 1.0
    one = np.zeros((1, CP_WIDTH), np.float32)
    one[0, 3 * h] = 1.0
    return pl.pallas_call(
        _fgate_kernel,
        grid=(b, s // tm),
        in_specs=[pl.BlockSpec((None, tm, d), lambda i, j: (i, j, 0)),
                  pl.BlockSpec((d, h), lambda i, j: (0, 0)),
                  pl.BlockSpec((1, h), lambda i, j: (0, 0)),
                  pl.BlockSpec((3, h, CP_WIDTH), lambda i, j: (0, 0, 0)),
                  pl.BlockSpec((1, CP_WIDTH), lambda i, j: (0, 0))],
        out_specs=[pl.BlockSpec((None, tm, h), lambda i, j: (i, j, 0)),
                   pl.BlockSpec((None, tm, CP_WIDTH), lambda i, j: (i, j, 0))],
        out_shape=[jax.ShapeDtypeStruct((b, s, h), F32), jax.ShapeDtypeStruct((b, s, CP_WIDTH), BF16)],
        scratch_shapes=[pltpu.VMEM((1, h), F32)],
        compiler_params=_cparams(("parallel", "arbitrary")),
        name="fox_fgate_cumsum",
    )(n3d, w, bias.reshape(1, h), jnp.asarray(spread, BF16), jnp.asarray(one))


def _causal_chunk():
    r_i = lax.broadcasted_iota(jnp.int32, (HG_CHUNK, HG_CHUNK), 0)
    c_i = lax.broadcasted_iota(jnp.int32, (HG_CHUNK, HG_CHUNK), 1)
    return c_i <= r_i


def _chunk_cumsum(lf):
    lower = _causal_chunk().astype(BF16)
    b = jnp.zeros(lf.shape, F32)
    for piece in _split3(lf):
        b = b + jnp.dot(lower, piece, preferred_element_type=F32)
    return b


def _hgrn_chunk_single_ref(q, lf, k, v, state_t):
    nt = (((1,), (1,)), ((), ()))
    b = _chunk_cumsum(lf)
    q_t = (q * jnp.exp(b)).astype(BF16)
    k_up = k * jnp.exp(-b)
    sc = lax.dot_general(q_t, k_up.astype(BF16), nt, preferred_element_type=F32)
    sc = jnp.where(_causal_chunk(), sc, 0.0)
    v_bf = v.astype(BF16)
    o = lax.dot_general(q_t, state_t.astype(BF16), nt, preferred_element_type=F32)
    o = o + jnp.dot(sc.astype(BF16), v_bf, preferred_element_type=F32)
    dec = jnp.exp(b[HG_CHUNK - 1:HG_CHUNK])
    upd = lax.dot_general(v_bf, (k_up * dec).astype(BF16), (((0,), (0,)), ((), ())), preferred_element_type=F32)
    return o, state_t * dec + upd


def _hgrn_chunk(q, lf, k, v, state_t):
    c = HG_CHUNK
    n_sub = c // HG_SUB
    b = _chunk_cumsum(lf)

    qe = (q * jnp.exp(b)).astype(BF16)
    o = lax.dot_general(qe, state_t.astype(BF16), (((1,), (1,)), ((), ())), preferred_element_type=F32)

    o_parts = []
    t_iota = lax.broadcasted_iota(jnp.int32, (HG_SUB, HG_DIM), 0)
    for i in range(n_sub):
        lo = i * HG_SUB
        q_i = q[lo:lo + HG_SUB]
        b_i = b[lo:lo + HG_SUB]
        o_i = o[lo:lo + HG_SUB]
        if i > 0:
            ref = b[lo - 1:lo]
            q_t = (q_i * jnp.exp(b_i - ref)).astype(BF16)
            k_t = (k[:lo] * jnp.exp(ref - b[:lo])).astype(BF16)
            sc = lax.dot_general(q_t, k_t, (((1,), (1,)), ((), ())), preferred_element_type=F32)
            o_i = o_i + jnp.dot(sc.astype(BF16), v[:lo].astype(BF16), preferred_element_type=F32)
        for s in range(HG_SUB):
            row = lo + s
            rel = jnp.where(t_iota >= s, b_i - b[row:row + 1], NEG_INF)
            p = q_i * jnp.exp(rel) * k[row:row + 1]
            o_i = o_i + jnp.sum(p, axis=-1, keepdims=True) * v[row:row + 1]
        o_parts.append(o_i)
    o = jnp.concatenate(o_parts, axis=0)

    b_end = b[c - 1:c]
    k_dec = (k * jnp.exp(b_end - b)).astype(BF16)
    upd = lax.dot_general(v.astype(BF16), k_dec, (((0,), (0,)), ((), ())), preferred_element_type=F32)
    return o, state_t * jnp.exp(b_end) + upd


def _hgrn_kernel(q_ref, lf_ref, k_ref, v_ref, g_ref, gn_ref, o_ref, state_ref, *, n_chunks):
    @pl.when(pl.program_id(2) == 0)
    def _():
        state_ref[...] = jnp.zeros_like(state_ref)

    def run_chunk(chunk_fn, rows, state):
        o, new_state = chunk_fn(q_ref[rows, :], lf_ref[rows, :], k_ref[rows, :], v_ref[rows, :], state)
        y = o * lax.rsqrt(jnp.mean(o * o, axis=-1, keepdims=True) + EPS)
        o_ref[rows, :] = (y * gn_ref[...] * g_ref[rows, :]).astype(o_ref.dtype)
        return new_state

    single_ref_ok = jnp.min(lf_ref[...]) * HG_CHUNK > -HG_MAX_EXPONENT

    @pl.when(single_ref_ok)
    def _():
        state = state_ref[...]
        for ci in range(n_chunks):
            state = run_chunk(_hgrn_chunk_single_ref, pl.ds(ci * HG_CHUNK, HG_CHUNK), state)
        state_ref[...] = state

    @pl.when(jnp.logical_not(single_ref_ok))
    def _():
        def body(ci, carry):
            rows = pl.ds(pl.multiple_of(ci * HG_CHUNK, HG_CHUNK), HG_CHUNK)
            state_ref[...] = run_chunk(_hgrn_chunk, rows, state_ref[...])
            return carry

        lax.fori_loop(0, n_chunks, body, 0)


def _hgrn(q, lf, k, v, g, g_norm, tc):
    b, s, w = q.shape
    heads = w // HG_DIM
    spec = pl.BlockSpec((None, tc, HG_DIM), lambda bi, hi, ti: (bi, ti, hi))
    return pl.pallas_call(
        functools.partial(_hgrn_kernel, n_chunks=tc // HG_CHUNK),
        grid=(b, heads, s // tc),
        in_specs=[spec] * 5 + [pl.BlockSpec((1, HG_DIM), lambda bi, hi, ti: (0, 0))],
        out_specs=spec,
        out_shape=jax.ShapeDtypeStruct((b, s, w), BF16),
        scratch_shapes=[pltpu.VMEM((HG_DIM, HG_DIM), F32)],
        compiler_params=_cparams(("parallel", "parallel", "arbitrary")),
        name="hgrn2_mixer",
    )(q, lf, k, v, g, g_norm.reshape(1, HG_DIM))


AUG_C = FOX_DIM
AUG_ONE = FOX_DIM + 3
AUG_QN = FOX_DIM + 6
NORM_MARGIN = 1.02
SKIP_LOG2 = 150.0


def _placement_constants(heads):
    pq = np.zeros((CP_WIDTH, heads * LANES), np.float32)
    pk = np.zeros((CP_WIDTH, heads * LANES), np.float32)
    pn = np.zeros((heads, heads * LANES), np.float32)
    hs = np.zeros((heads * FOX_DIM, heads), np.float32)
    for h in range(heads):
        for i in range(3):
            pq[i * heads + h, h * LANES + AUG_C + i] = 1.0
            pk[i * heads + h, h * LANES + AUG_ONE + i] = -1.0
        pq[3 * heads, h * LANES + AUG_ONE:h * LANES + AUG_ONE + 3] = 1.0
        pk[3 * heads, h * LANES + AUG_C:h * LANES + AUG_C + 3] = 1.0
        pn[h, h * LANES + AUG_QN] = 1.0
        hs[h * FOX_DIM:(h + 1) * FOX_DIM, h] = 1.0
    return tuple(jnp.asarray(a, BF16) for a in (pq, pk, pn, hs))


def _fox_pack_kernel(q_ref, k_ref, v_ref, cp_ref, c_ref, pq_ref, pk_ref, pn_ref, hs_ref,
                     qa_ref, ka_ref, va_ref, st_ref):
    q = q_ref[...]
    k = k_ref[...]
    v = v_ref[...]
    cp = cp_ref[...]
    tm = q.shape[0]
    heads = qa_ref.shape[0]
    hs = hs_ref[...]
    qn = jnp.sqrt(jnp.dot(q * q, hs, preferred_element_type=F32)) * NORM_MARGIN
    kn = jnp.sqrt(jnp.dot(k * k, hs, preferred_element_type=F32)) * NORM_MARGIN
    tail_q = (jnp.dot(cp, pq_ref[...], preferred_element_type=F32)
              + jnp.dot(qn.astype(BF16), pn_ref[...], preferred_element_type=F32)).astype(BF16)
    tail_k = jnp.dot(cp, pk_ref[...], preferred_element_type=F32).astype(BF16)
    lane = lax.broadcasted_iota(jnp.int32, (tm, LANES), 1)
    in_data = lane < FOX_DIM
    tail_v = jnp.where(lane == FOX_DIM, 1.0, 0.0).astype(BF16)
    for h in range(heads):
        pair = slice((h // 2) * LANES, (h // 2 + 1) * LANES)
        mine = slice(h * LANES, (h + 1) * LANES)
        data = (lambda x: x[:, pair]) if h % 2 == 0 else (lambda x: pltpu.roll(x[:, pair], FOX_DIM, 1))
        qa_ref[h] = jnp.where(in_data, data(q), tail_q[:, mine])
        ka_ref[h] = jnp.where(in_data, data(k), tail_k[:, mine])
        va_ref[h] = jnp.where(in_data, data(v), tail_v)
    st_ref[0:1, :] = jnp.max(kn, axis=0, keepdims=True)
    st_ref[1:2, :] = c_ref[tm - 1:tm, :]


def _fox_pack(q, k, v, cp, c, tm):
    b, s, w = q.shape
    heads = c.shape[-1]
    consts = _placement_constants(heads)
    row = lambda width: pl.BlockSpec((None, tm, width), lambda bi, ti: (bi, ti, 0))
    full = lambda a: pl.BlockSpec(a.shape, lambda bi, ti: (0, 0))
    out = pl.BlockSpec((None, heads, tm, LANES), lambda bi, ti: (bi, 0, ti, 0))
    oshape = jax.ShapeDtypeStruct((b, heads, s, LANES), BF16)
    return pl.pallas_call(
        _fox_pack_kernel,
        grid=(b, s // tm),
        in_specs=[row(w), row(w), row(w), row(CP_WIDTH), row(heads)] + [full(a) for a in consts],
        out_specs=[out] * 3 + [pl.BlockSpec((None, None, 2, heads), lambda bi, ti: (bi, ti, 0, 0))],
        out_shape=[oshape] * 3 + [jax.ShapeDtypeStruct((b, s // tm, 2, heads), F32)],
        compiler_params=_cparams(("parallel", "parallel")),
        name="fox_pack",
    )(q, k, v, cp, c, *consts)


def _fox_kernel(kmax_ref, clast_ref, q_ref, k_ref, v_ref, o_ref, m_ref, acc_ref, *, tq, tk):
    qi = pl.program_id(2)
    row_a = pl.program_id(0) * (2 * pl.num_programs(1)) + 2 * pl.program_id(1)
    m_ref[...] = jnp.full_like(m_ref, NEG_INF)
    acc_ref[...] = jnp.zeros_like(acc_ref)

    def block(row0, n_rows, kv0, masked):
        rows = pl.ds(row0, n_rows)
        cols = pl.ds(kv0, tk)
        if masked:
            causal = (lax.broadcasted_iota(jnp.int32, (n_rows, tk), 1)
                      <= lax.broadcasted_iota(jnp.int32, (n_rows, tk), 0))
        for a in range(2):
            s = lax.dot_general(q_ref[a, rows, :], k_ref[a, cols, :], (((1,), (1,)), ((), ())),
                                preferred_element_type=F32)
            if masked:
                s = jnp.where(causal, s, NEG_INF)
            m_prev = m_ref[a, rows, :]
            m_new = jnp.maximum(m_prev, jnp.max(s, axis=-1, keepdims=True))
            alpha = jnp.exp2(m_prev - m_new)
            p = jnp.exp2(s - m_new[:, 0:1])
            acc_ref[a, rows, :] = alpha * acc_ref[a, rows, :] + jnp.dot(
                p.astype(BF16), v_ref[a, cols, :], preferred_element_type=F32)
            m_ref[a, rows, :] = m_new

    n_diag = tq // tk
    n_full = qi * n_diag

    for j in range(n_diag):
        block(j * tk, tq - j * tk, pl.multiple_of((n_full + j) * tk, tk), True)

    n_tiles = kmax_ref.shape[1]
    slack = []
    for a in range(2):
        kmax = kmax_ref[row_a + a, 0]
        for j in range(1, n_tiles):
            kmax = jnp.maximum(kmax, kmax_ref[row_a + a, j])
        sel_row = lax.broadcasted_iota(jnp.int32, (LANES, LANES), 0)
        sel = jnp.where(sel_row == AUG_QN, kmax * NORM_MARGIN,
                        jnp.where((sel_row >= AUG_C) & (sel_row < AUG_ONE), 1.0, 0.0)).astype(BF16)
        bound = jnp.dot(q_ref[a], sel, preferred_element_type=F32)
        slack.append(jnp.max(bound - m_ref[a]))

    def tile_is_live(j):
        live = [slack[a] - clast_ref[row_a + a, j] > -SKIP_LOG2 for a in range(2)]
        return live[0] | live[1]

    def cond(j):
        return (j >= 0) & tile_is_live(jnp.maximum(j, 0))

    def body(j):
        block(0, tq, pl.multiple_of(j * tk, tk), False)
        return j - 1

    lax.while_loop(cond, body, n_full - 1)

    lane = lax.broadcasted_iota(jnp.int32, (tq, LANES), 1)
    acc_a = acc_ref[0]
    acc_b = acc_ref[1]
    o_a = acc_a / acc_a[:, FOX_DIM:FOX_DIM + 1]
    o_b = acc_b / acc_b[:, FOX_DIM:FOX_DIM + 1]
    o_ref[...] = jnp.where(lane < FOX_DIM, o_a, pltpu.roll(o_b, FOX_DIM, 1)).astype(o_ref.dtype)


def _fox_attention(q, k, v, kmax, clast, tq, tk):
    b, h, s, _ = q.shape
    kv_spec = pl.BlockSpec((None, 2, s, LANES), lambda bi, hi, qi: (bi, hi, 0, 0))
    smem = pl.BlockSpec(memory_space=pltpu.SMEM)
    return pl.pallas_call(
        functools.partial(_fox_kernel, tq=tq, tk=tk),
        grid=(b, h // 2, s // tq),
        in_specs=[smem, smem,
                  pl.BlockSpec((None, 2, tq, LANES), lambda bi, hi, qi: (bi, hi, qi, 0)), kv_spec, kv_spec],
        out_specs=pl.BlockSpec((None, tq, LANES), lambda bi, hi, qi: (bi, qi, hi)),
        out_shape=jax.ShapeDtypeStruct((b, s, (h // 2) * LANES), BF16),
        scratch_shapes=[pltpu.VMEM((2, tq, LANES), F32), pltpu.VMEM((2, tq, LANES), F32)],
        compiler_params=_cparams(("parallel", "parallel", "arbitrary")),
        name="fox_attention",
    )(kmax, clast, q, k, v)


def _merge_kernel(x_ref, oa_ref, ob_ref, ga_ref, gb_ref, wa_ref, wb_ref, wo_ref, gn_ref, h_ref, n_ref):
    ya = jnp.dot(oa_ref[...], wa_ref[...], preferred_element_type=F32)
    yb = jnp.dot(ob_ref[...], wb_ref[...], preferred_element_type=F32)
    merged = ga_ref[...] * ya + gb_ref[...] * yb
    h = x_ref[...] + jnp.dot(merged.astype(BF16), wo_ref[...], preferred_element_type=F32)
    h_ref[...] = h
    y = h * lax.rsqrt(jnp.mean(h * h, axis=-1, keepdims=True) + EPS)
    n_ref[...] = (y * gn_ref[...]).astype(n_ref.dtype)


def _merge(x2d, oa, ob, ga, gb, wa, wb, wo, gain, tm):
    t, d = x2d.shape
    row = lambda width: pl.BlockSpec((tm, width), lambda i: (i, 0))
    full = lambda a: pl.BlockSpec(a.shape, lambda i: (0, 0))
    return pl.pallas_call(
        _merge_kernel,
        grid=(t // tm,),
        in_specs=[row(d), row(oa.shape[1]), row(ob.shape[1]), row(d), row(d), full(wa), full(wb), full(wo),
                  pl.BlockSpec((1, d), lambda i: (0, 0))],
        out_specs=[row(d), row(d)],
        out_shape=[jax.ShapeDtypeStruct((t, d), F32), jax.ShapeDtypeStruct((t, d), BF16)],
        compiler_params=_cparams(("parallel",)),
        name="merge_outproj",
    )(x2d, oa, ob, ga, gb, wa, wb, wo, gain.reshape(1, d))


def _ffn_kernel(n_ref, halo_ref, h_ref, wg_ref, wv_ref, cwg_ref, cwv_ref, cbg_ref, cbv_ref, wd_ref, gn_ref,
                o_ref, next_ref, acc_ref, *, tm, seq):
    i = pl.program_id(0)
    j = pl.program_id(1)

    @pl.when(j == 0)
    def _():
        at_start = (i * tm) % seq == 0
        next_ref[0:HALO, :] = jnp.where(at_start, jnp.zeros_like(halo_ref[...]), halo_ref[...])
        next_ref[HALO:, :] = n_ref[...]
        acc_ref[...] = jnp.zeros_like(acc_ref)

    n_ext = next_ref[...]

    def conv(w_ref, cw_ref, cb_ref):
        u = jnp.dot(n_ext, w_ref[...], preferred_element_type=F32)
        out = cb_ref[...]
        for tap in range(CONV_WIDTH):
            lo = HALO - (CONV_WIDTH - 1) + tap
            out = out + cw_ref[tap:tap + 1, :] * u[lo:lo + tm]
        return out

    gate = conv(wg_ref, cwg_ref, cbg_ref)
    val = conv(wv_ref, cwv_ref, cbv_ref)
    act = 0.5 * gate * (1.0 + lax.erf(gate * (2.0 ** -0.5))) * val
    acc_ref[...] += jnp.dot(act.astype(BF16), wd_ref[...], preferred_element_type=F32)

    @pl.when(j == pl.num_programs(1) - 1)
    def _():
        h = h_ref[...] + acc_ref[...]
        y = h * lax.rsqrt(jnp.mean(h * h, axis=-1, keepdims=True) + EPS)
        o_ref[...] = (y * gn_ref[...]).astype(o_ref.dtype)


def _ffn(n2d, h2d, w_up, conv_w, conv_b, w_down, gain, seq, tm, tf):
    t, d = n2d.shape
    d_ff = w_down.shape[0]
    n_f = d_ff // tf
    halo_blocks = tm // HALO
    cb = conv_b.reshape(1, 2 * d_ff)
    return pl.pallas_call(
        functools.partial(_ffn_kernel, tm=tm, seq=seq),
        grid=(t // tm, n_f),
        in_specs=[pl.BlockSpec((tm, d), lambda i, j: (i, 0)),
                  pl.BlockSpec((HALO, d), lambda i, j: (jnp.maximum(i * halo_blocks - 1, 0), 0)),
                  pl.BlockSpec((tm, d), lambda i, j: (i, 0)),
                  pl.BlockSpec((d, tf), lambda i, j: (0, j)),
                  pl.BlockSpec((d, tf), lambda i, j: (0, n_f + j)),
                  pl.BlockSpec((CONV_WIDTH, tf), lambda i, j: (0, j)),
                  pl.BlockSpec((CONV_WIDTH, tf), lambda i, j: (0, n_f + j)),
                  pl.BlockSpec((1, tf), lambda i, j: (0, j)),
                  pl.BlockSpec((1, tf), lambda i, j: (0, n_f + j)),
                  pl.BlockSpec((tf, d), lambda i, j: (j, 0)),
                  pl.BlockSpec((1, d), lambda i, j: (0, 0))],
        out_specs=pl.BlockSpec((tm, d), lambda i, j: (i, 0)),
        out_shape=jax.ShapeDtypeStruct((t, d), F32),
        scratch_shapes=[pltpu.VMEM((HALO + tm, d), BF16), pltpu.VMEM((tm, d), F32)],
        compiler_params=_cparams(("parallel", "arbitrary")),
        name="conv_glu_ffn",
    )(n2d, n2d, h2d, w_up, w_up, conv_w, conv_w, cb, cb, w_down, gain.reshape(1, d))


def _tile(n, pref):
    return pref if n % pref == 0 else n


def kernel(x, norm_mix, w_in, fox_f_bias, hg_lb_logits, hg_norm, w_branch_a, w_branch_b, w_out, norm_ffn, w_up,
           conv_w, conv_b, w_down, norm_final):
    b, s, d = x.shape
    assert norm_mix.shape[0] == 1, "single-layer trunk"
    t = b * s
    hg_w = HG_HEADS * HG_DIM
    fox_w = FOX_HEADS * FOX_DIM
    cuts = [hg_w, 2 * hg_w, 3 * hg_w, 4 * hg_w, 4 * hg_w + fox_w, 4 * hg_w + 2 * fox_w, 4 * hg_w + 3 * fox_w,
            4 * hg_w + 3 * fox_w + FOX_HEADS, 4 * hg_w + 3 * fox_w + FOX_HEADS + d]
    assert w_in.shape[-1] == cuts[-1] + d
    w_bf = w_in[0].astype(BF16)
    w_hq, w_hf, w_hi, w_hg, w_fq, w_fk, w_fv, w_ff, w_ga, w_gb = jnp.split(w_bf, cuts, axis=-1)

    x2d = x.reshape(t, d)
    tm = _tile(t, 512)
    n2d = _rmsnorm(x2d, norm_mix[0], _tile(t, 1024))

    hq, hlf, hk, hv, hgate = _hg_proj(n2d, w_hq, w_hf, w_hi, w_hg, hg_lb_logits, tm, _tile(hg_w, 256))
    fq, fk, fv = _fox_proj(n2d, w_fq, w_fk, w_fv, tm, _tile(fox_w, 512))
    ga, gb = _gate_proj(n2d, w_ga, w_gb, tm, _tile(d, 512))
    c2, c_pieces = _fgate(n2d.reshape(b, s, d), w_ff, fox_f_bias[0], _tile(s, 256))

    r3 = lambda a: a.reshape(b, s, a.shape[-1])
    o_a = _hgrn(r3(hq), r3(hlf), r3(hk), r3(hv), r3(hgate), hg_norm[0], _tile(s, 512))
    tk = _tile(s, 512)
    q_aug, k_aug, v_aug, kv_stats = _fox_pack(r3(fq), r3(fk), r3(fv), c_pieces, c2, tk)
    kv_stats = kv_stats.transpose(2, 0, 3, 1).reshape(2, b * FOX_HEADS, s // tk)
    o_b = _fox_attention(q_aug, k_aug, v_aug, kv_stats[0], kv_stats[1], _tile(s, 4 * tk), tk)

    h1, n_ffn = _merge(x2d, o_a.reshape(t, hg_w), o_b.reshape(t, fox_w), ga, gb, w_branch_a[0].astype(BF16),
                       w_branch_b[0].astype(BF16), w_out[0].astype(BF16), norm_ffn[0], _tile(t, 256))
    out = _ffn(n_ffn, h1, w_up[0].astype(BF16), conv_w[0], conv_b[0], w_down[0].astype(BF16), norm_final,
               s, _tile(s, 512), 256)
    return out.reshape(b, s, d)
```

```python
import functools
import math

import jax
import jax.numpy as jnp
import numpy as np
from jax import lax
from jax.experimental import pallas as pl
from jax.experimental.pallas import tpu as pltpu

F32 = jnp.float32
BF16 = jnp.bfloat16

EPS = 1e-6
HG_HEADS = 8
HG_DIM = 128
HG_CHUNK = 64
HG_SUB = 16
HG_MAX_EXPONENT = 64.0
FOX_HEADS = 16
FOX_DIM = 64
CONV_WIDTH = 3
HALO = 8
NEG_INF = float("-inf")
LOG2E = math.log2(math.e)
LANES = 128

VMEM_LIMIT = 56 * 1024 * 1024


def _cparams(sem):
    return pltpu.CompilerParams(dimension_semantics=sem, vmem_limit_bytes=VMEM_LIMIT)


def _sigmoid(x):
    return 1.0 / (1.0 + jnp.exp(-x))


def _silu(x):
    return x * _sigmoid(x)


def _log_sigmoid(x):
    return jnp.minimum(x, 0.0) - jnp.log1p(jnp.exp(-jnp.abs(x)))


def _split3(x):
    x1 = x.astype(BF16)
    r1 = x - x1.astype(F32)
    x2 = r1.astype(BF16)
    x3 = (r1 - x2.astype(F32)).astype(BF16)
    return x1, x2, x3


SEG_HQ, SEG_HF, SEG_HI, SEG_HG, SEG_FQ, SEG_FK, SEG_FV, SEG_GA, SEG_GB = range(9)
N_SEG = 9


def _in_proj_kernel(x_ref, gain_ref, w_ref, lbl_ref, fbias_ref, hq_ref, hlf_ref, hk_ref, hv_ref, hg_ref,
                    fq_ref, fk_ref, fv_ref, ga_ref, gb_ref, flf_ref, n_ref, *, tn):
    x = x_ref[...]
    n_ref[...] = (x * lax.rsqrt(jnp.mean(x * x, axis=-1, keepdims=True) + EPS) * gain_ref[...]).astype(BF16)
    n = n_ref[...]
    width = hq_ref.shape[1]
    lbl = lbl_ref[...]
    e = jnp.exp(lbl - jnp.max(lbl, axis=0, keepdims=True))
    lb = e[0:1] / jnp.sum(e, axis=0, keepdims=True)

    def chunks(seg):
        for c in range(width // tn):
            cols = slice(c * tn, (c + 1) * tn)
            yield cols, jnp.dot(n, w_ref[:, seg * width + c * tn:seg * width + (c + 1) * tn],
                                preferred_element_type=F32)

    def emit(seg, ref, fn):
        for cols, acc in chunks(seg):
            ref[:, cols] = fn(acc).astype(ref.dtype)

    emit(SEG_HQ, hq_ref, _silu)
    for cols, acc in chunks(SEG_HF):
        f = lb[:, cols] + (1.0 - lb[:, cols]) * _sigmoid(acc)
        hlf_ref[:, cols] = jnp.log(f)
        hk_ref[:, cols] = (1.0 - f).astype(hk_ref.dtype)
    emit(SEG_HI, hv_ref, lambda a: a)
    emit(SEG_HG, hg_ref, _silu)
    emit(SEG_FQ, fq_ref, lambda a: a * (FOX_DIM ** -0.5 * LOG2E))
    emit(SEG_FK, fk_ref, lambda a: a)
    emit(SEG_FV, fv_ref, lambda a: a)
    emit(SEG_GA, ga_ref, _sigmoid)
    emit(SEG_GB, gb_ref, _sigmoid)
    ff = jnp.dot(n, w_ref[:, N_SEG * width:], preferred_element_type=F32)
    flf_ref[...] = _log_sigmoid(ff + fbias_ref[...])


def _in_proj(x2d, gain, w_all, lb_logits, f_bias, tm, tn):
    t, d = x2d.shape
    heads = f_bias.shape[0]
    once = lambda a: pl.BlockSpec(a.shape, lambda i: (0,) * a.ndim, pipeline_mode=pl.Buffered(1))
    row = lambda width: pl.BlockSpec((tm, width), lambda i: (i, 0))
    seg_dtypes = [BF16, F32, BF16, BF16, BF16, BF16, BF16, BF16, BF16, BF16]
    gain2, bias2 = gain.reshape(1, d), f_bias.reshape(1, heads)
    return pl.pallas_call(
        functools.partial(_in_proj_kernel, tn=tn),
        grid=(t // tm,),
        in_specs=[row(d), once(gain2), once(w_all), once(lb_logits), once(bias2)],
        out_specs=[row(d)] * len(seg_dtypes) + [row(heads)],
        out_shape=[jax.ShapeDtypeStruct((t, d), dt) for dt in seg_dtypes] + [jax.ShapeDtypeStruct((t, heads), F32)],
        scratch_shapes=[pltpu.VMEM((tm, d), BF16)],
        compiler_params=_cparams(("parallel",)),
        name="in_proj",
    )(x2d, gain2, w_all, lb_logits, bias2)


CP_WIDTH = 4 * FOX_HEADS


def _fgate_kernel(lf_ref, spread_ref, one_ref, c_ref, cp_ref, carry_ref):
    @pl.when(pl.program_id(1) == 0)
    def _():
        carry_ref[...] = jnp.zeros_like(carry_ref)

    lf = lf_ref[...]
    tm = lf.shape[0]
    r = lax.broadcasted_iota(jnp.int32, (tm, tm), 0)
    c = lax.broadcasted_iota(jnp.int32, (tm, tm), 1)
    lower = (c <= r).astype(BF16)
    cs = carry_ref[...]
    for piece in _split3(lf):
        cs = cs + jnp.dot(lower, piece, preferred_element_type=F32)
    carry_ref[...] = cs[tm - 1:tm, :]
    c2 = cs * LOG2E
    c_ref[...] = c2
    cp = one_ref[...]
    for i, piece in enumerate(_split3(c2)):
        cp = cp + jnp.dot(piece, spread_ref[i], preferred_element_type=F32)
    cp_ref[...] = cp.astype(cp_ref.dtype)


def _fgate(lf3d, tm):
    b, s, h = lf3d.shape
    spread = np.zeros((3, h, CP_WIDTH), np.float32)
    for i in range(3):
        spread[i, np.arange(h), i * h + np.arange(h)] = 1.0
    one = np.zeros((1, CP_WIDTH), np.float32)
    one[0, 3 * h] = 1.0
    return pl.pallas_call(
        _fgate_kernel,
        grid=(b, s // tm),
        in_specs=[pl.BlockSpec((None, tm, h), lambda i, j: (i, j, 0)),
                  pl.BlockSpec((3, h, CP_WIDTH), lambda i, j: (0, 0, 0)),
                  pl.BlockSpec((1, CP_WIDTH), lambda i, j: (0, 0))],
        out_specs=[pl.BlockSpec((None, tm, h), lambda i, j: (i, j, 0)),
                   pl.BlockSpec((None, tm, CP_WIDTH), lambda i, j: (i, j, 0))],
        out_shape=[jax.ShapeDtypeStruct((b, s, h), F32), jax.ShapeDtypeStruct((b, s, CP_WIDTH), BF16)],
        scratch_shapes=[pltpu.VMEM((1, h), F32)],
        compiler_params=_cparams(("parallel", "arbitrary")),
        name="fox_fgate_cumsum",
    )(lf3d, jnp.asarray(spread, BF16), jnp.asarray(one))


def _causal_chunk():
    r_i = lax.broadcasted_iota(jnp.int32, (HG_CHUNK, HG_CHUNK), 0)
    c_i = lax.broadcasted_iota(jnp.int32, (HG_CHUNK, HG_CHUNK), 1)
    return c_i <= r_i


def _chunk_cumsum(lf):
    lower = _causal_chunk().astype(BF16)
    b = jnp.zeros(lf.shape, F32)
    for piece in _split3(lf):
        b = b + jnp.dot(lower, piece, preferred_element_type=F32)
    return b


def _hgrn_chunk_single_ref(q, lf, k, v, state_t):
    nt = (((1,), (1,)), ((), ()))
    b = _chunk_cumsum(lf)
    q_t = (q * jnp.exp(b)).astype(BF16)
    k_up = k * jnp.exp(-b)
    sc = lax.dot_general(q_t, k_up.astype(BF16), nt, preferred_element_type=F32)
    sc = jnp.where(_causal_chunk(), sc, 0.0)
    v_bf = v.astype(BF16)
    o = lax.dot_general(q_t, state_t.astype(BF16), nt, preferred_element_type=F32)
    o = o + jnp.dot(sc.astype(BF16), v_bf, preferred_element_type=F32)
    dec = jnp.exp(b[HG_CHUNK - 1:HG_CHUNK])
    upd = lax.dot_general(v_bf, (k_up * dec).astype(BF16), (((0,), (0,)), ((), ())), preferred_element_type=F32)
    return o, state_t * dec + upd


def _hgrn_chunk(q, lf, k, v, state_t):
    c = HG_CHUNK
    n_sub = c // HG_SUB
    b = _chunk_cumsum(lf)

    qe = (q * jnp.exp(b)).astype(BF16)
    o = lax.dot_general(qe, state_t.astype(BF16), (((1,), (1,)), ((), ())), preferred_element_type=F32)

    o_parts = []
    t_iota = lax.broadcasted_iota(jnp.int32, (HG_SUB, HG_DIM), 0)
    for i in range(n_sub):
        lo = i * HG_SUB
        q_i = q[lo:lo + HG_SUB]
        b_i = b[lo:lo + HG_SUB]
        o_i = o[lo:lo + HG_SUB]
        if i > 0:
            ref = b[lo - 1:lo]
            q_t = (q_i * jnp.exp(b_i - ref)).astype(BF16)
            k_t = (k[:lo] * jnp.exp(ref - b[:lo])).astype(BF16)
            sc = lax.dot_general(q_t, k_t, (((1,), (1,)), ((), ())), preferred_element_type=F32)
            o_i = o_i + jnp.dot(sc.astype(BF16), v[:lo].astype(BF16), preferred_element_type=F32)
        for s in range(HG_SUB):
            row = lo + s
            rel = jnp.where(t_iota >= s, b_i - b[row:row + 1], NEG_INF)
            p = q_i * jnp.exp(rel) * k[row:row + 1]
            o_i = o_i + jnp.sum(p, axis=-1, keepdims=True) * v[row:row + 1]
        o_parts.append(o_i)
    o = jnp.concatenate(o_parts, axis=0)

    b_end = b[c - 1:c]
    k_dec = (k * jnp.exp(b_end - b)).astype(BF16)
    upd = lax.dot_general(v.astype(BF16), k_dec, (((0,), (0,)), ((), ())), preferred_element_type=F32)
    return o, state_t * jnp.exp(b_end) + upd


def _hgrn_kernel(q_ref, lf_ref, k_ref, v_ref, g_ref, gn_ref, o_ref, state_ref, *, n_chunks):
    @pl.when(pl.program_id(2) == 0)
    def _():
        state_ref[...] = jnp.zeros_like(state_ref)

    def run_chunk(chunk_fn, rows, state):
        load = lambda ref: ref[rows, :].astype(F32)
        o, new_state = chunk_fn(load(q_ref), lf_ref[rows, :], load(k_ref), load(v_ref), state)
        y = o * lax.rsqrt(jnp.mean(o * o, axis=-1, keepdims=True) + EPS)
        o_ref[rows, :] = (y * gn_ref[...] * load(g_ref)).astype(o_ref.dtype)
        return new_state

    single_ref_ok = jnp.min(lf_ref[...]) * HG_CHUNK > -HG_MAX_EXPONENT

    @pl.when(single_ref_ok)
    def _():
        state = state_ref[...]
        for ci in range(n_chunks):
            state = run_chunk(_hgrn_chunk_single_ref, pl.ds(ci * HG_CHUNK, HG_CHUNK), state)
        state_ref[...] = state

    @pl.when(jnp.logical_not(single_ref_ok))
    def _():
        def body(ci, carry):
            rows = pl.ds(pl.multiple_of(ci * HG_CHUNK, HG_CHUNK), HG_CHUNK)
            state_ref[...] = run_chunk(_hgrn_chunk, rows, state_ref[...])
            return carry

        lax.fori_loop(0, n_chunks, body, 0)


def _hgrn(q, lf, k, v, g, g_norm, tc):
    b, s, w = q.shape
    heads = w // HG_DIM
    spec = pl.BlockSpec((None, tc, HG_DIM), lambda bi, hi, ti: (bi, ti, hi))
    return pl.pallas_call(
        functools.partial(_hgrn_kernel, n_chunks=tc // HG_CHUNK),
        grid=(b, heads, s // tc),
        in_specs=[spec] * 5 + [pl.BlockSpec((1, HG_DIM), lambda bi, hi, ti: (0, 0))],
        out_specs=spec,
        out_shape=jax.ShapeDtypeStruct((b, s, w), BF16),
        scratch_shapes=[pltpu.VMEM((HG_DIM, HG_DIM), F32)],
        compiler_params=_cparams(("parallel", "parallel", "arbitrary")),
        name="hgrn2_mixer",
    )(q, lf, k, v, g, g_norm.reshape(1, HG_DIM))


AUG_C = FOX_DIM
AUG_ONE = FOX_DIM + 3
AUG_QN = FOX_DIM + 6
NORM_MARGIN = 1.02
SKIP_LOG2 = 150.0


def _placement_constants(heads):
    pq = np.zeros((CP_WIDTH, heads * LANES), np.float32)
    pk = np.zeros((CP_WIDTH, heads * LANES), np.float32)
    pn = np.zeros((heads, heads * LANES), np.float32)
    hs = np.zeros((heads * FOX_DIM, heads), np.float32)
    for h in range(heads):
        for i in range(3):
            pq[i * heads + h, h * LANES + AUG_C + i] = 1.0
            pk[i * heads + h, h * LANES + AUG_ONE + i] = -1.0
        pq[3 * heads, h * LANES + AUG_ONE:h * LANES + AUG_ONE + 3] = 1.0
        pk[3 * heads, h * LANES + AUG_C:h * LANES + AUG_C + 3] = 1.0
        pn[h, h * LANES + AUG_QN] = 1.0
        hs[h * FOX_DIM:(h + 1) * FOX_DIM, h] = 1.0
    return tuple(jnp.asarray(a, BF16) for a in (pq, pk, pn, hs))


def _fox_pack_kernel(q_ref, k_ref, v_ref, cp_ref, c_ref, pq_ref, pk_ref, pn_ref, hs_ref,
                     qa_ref, ka_ref, va_ref, st_ref):
    q = q_ref[...]
    k = k_ref[...]
    v = v_ref[...]
    cp = cp_ref[...]
    tm = q.shape[0]
    heads = qa_ref.shape[0]
    hs = hs_ref[...]
    qn = jnp.sqrt(jnp.dot(q * q, hs, preferred_element_type=F32)) * NORM_MARGIN
    kn = jnp.sqrt(jnp.dot(k * k, hs, preferred_element_type=F32)) * NORM_MARGIN
    tail_q = (jnp.dot(cp, pq_ref[...], preferred_element_type=F32)
              + jnp.dot(qn.astype(BF16), pn_ref[...], preferred_element_type=F32)).astype(BF16)
    tail_k = jnp.dot(cp, pk_ref[...], preferred_element_type=F32).astype(BF16)
    lane = lax.broadcasted_iota(jnp.int32, (tm, LANES), 1)
    in_data = lane < FOX_DIM
    tail_v = jnp.where(lane == FOX_DIM, 1.0, 0.0).astype(BF16)
    for h in range(heads):
        pair = slice((h // 2) * LANES, (h // 2 + 1) * LANES)
        mine = slice(h * LANES, (h + 1) * LANES)
        data = (lambda x: x[:, pair]) if h % 2 == 0 else (lambda x: pltpu.roll(x[:, pair], FOX_DIM, 1))
        qa_ref[h] = jnp.where(in_data, data(q), tail_q[:, mine])
        ka_ref[h] = jnp.where(in_data, data(k), tail_k[:, mine])
        va_ref[h] = jnp.where(in_data, data(v), tail_v)
    st_ref[0:1, :] = jnp.max(kn, axis=0, keepdims=True)
    st_ref[1:2, :] = c_ref[tm - 1:tm, :]


def _fox_pack(q, k, v, cp, c, tm):
    b, s, w = q.shape
    heads = c.shape[-1]
    consts = _placement_constants(heads)
    row = lambda width: pl.BlockSpec((None, tm, width), lambda bi, ti: (bi, ti, 0))
    full = lambda a: pl.BlockSpec(a.shape, lambda bi, ti: (0, 0))
    out = pl.BlockSpec((None, heads, tm, LANES), lambda bi, ti: (bi, 0, ti, 0))
    oshape = jax.ShapeDtypeStruct((b, heads, s, LANES), BF16)
    return pl.pallas_call(
        _fox_pack_kernel,
        grid=(b, s // tm),
        in_specs=[row(w), row(w), row(w), row(CP_WIDTH), row(heads)] + [full(a) for a in consts],
        out_specs=[out] * 3 + [pl.BlockSpec((None, None, 2, heads), lambda bi, ti: (bi, ti, 0, 0))],
        out_shape=[oshape] * 3 + [jax.ShapeDtypeStruct((b, s // tm, 2, heads), F32)],
        compiler_params=_cparams(("parallel", "parallel")),
        name="fox_pack",
    )(q, k, v, cp, c, *consts)


def _fox_kernel(kmax_ref, clast_ref, q_ref, k_ref, v_ref, o_ref, m_ref, acc_ref, *, tq, tk):
    qi = pl.program_id(2)
    row_a = pl.program_id(0) * (2 * pl.num_programs(1)) + 2 * pl.program_id(1)
    m_ref[...] = jnp.full_like(m_ref, NEG_INF)
    acc_ref[...] = jnp.zeros_like(acc_ref)

    def block(row0, n_rows, kv0, masked):
        rows = pl.ds(row0, n_rows)
        cols = pl.ds(kv0, tk)
        if masked:
            causal = (lax.broadcasted_iota(jnp.int32, (n_rows, tk), 1)
                      <= lax.broadcasted_iota(jnp.int32, (n_rows, tk), 0))
        for a in range(2):
            s = lax.dot_general(q_ref[a, rows, :], k_ref[a, cols, :], (((1,), (1,)), ((), ())),
                                preferred_element_type=F32)
            if masked:
                s = jnp.where(causal, s, NEG_INF)
            m_prev = m_ref[a, rows, :]
            m_new = jnp.maximum(m_prev, jnp.max(s, axis=-1, keepdims=True))
            alpha = jnp.exp2(m_prev - m_new)
            p = jnp.exp2(s - m_new[:, 0:1])
            acc_ref[a, rows, :] = alpha * acc_ref[a, rows, :] + jnp.dot(
                p.astype(BF16), v_ref[a, cols, :], preferred_element_type=F32)
            m_ref[a, rows, :] = m_new

    n_diag = tq // tk
    n_full = qi * n_diag

    for j in range(n_diag):
        block(j * tk, tq - j * tk, pl.multiple_of((n_full + j) * tk, tk), True)

    n_tiles = kmax_ref.shape[1]
    slack = []
    for a in range(2):
        kmax = kmax_ref[row_a + a, 0]
        for j in range(1, n_tiles):
            kmax = jnp.maximum(kmax, kmax_ref[row_a + a, j])
        sel_row = lax.broadcasted_iota(jnp.int32, (LANES, LANES), 0)
        sel = jnp.where(sel_row == AUG_QN, kmax * NORM_MARGIN,
                        jnp.where((sel_row >= AUG_C) & (sel_row < AUG_ONE), 1.0, 0.0)).astype(BF16)
        bound = jnp.dot(q_ref[a], sel, preferred_element_type=F32)
        slack.append(jnp.max(bound - m_ref[a]))

    def tile_is_live(j):
        live = [slack[a] - clast_ref[row_a + a, j] > -SKIP_LOG2 for a in range(2)]
        return live[0] | live[1]

    def cond(j):
        return (j >= 0) & tile_is_live(jnp.maximum(j, 0))

    def body(j):
        block(0, tq, pl.multiple_of(j * tk, tk), False)
        return j - 1

    lax.while_loop(cond, body, n_full - 1)

    lane = lax.broadcasted_iota(jnp.int32, (tq, LANES), 1)
    acc_a = acc_ref[0]
    acc_b = acc_ref[1]
    o_a = acc_a / acc_a[:, FOX_DIM:FOX_DIM + 1]
    o_b = acc_b / acc_b[:, FOX_DIM:FOX_DIM + 1]
    o_ref[...] = jnp.where(lane < FOX_DIM, o_a, pltpu.roll(o_b, FOX_DIM, 1)).astype(o_ref.dtype)


def _fox_attention(q, k, v, kmax, clast, tq, tk):
    b, h, s, _ = q.shape
    kv_spec = pl.BlockSpec((None, 2, s, LANES), lambda bi, hi, qi: (bi, hi, 0, 0))
    smem = pl.BlockSpec(memory_space=pltpu.SMEM)
    return pl.pallas_call(
        functools.partial(_fox_kernel, tq=tq, tk=tk),
        grid=(b, h // 2, s // tq),
        in_specs=[smem, smem,
                  pl.BlockSpec((None, 2, tq, LANES), lambda bi, hi, qi: (bi, hi, qi, 0)), kv_spec, kv_spec],
        out_specs=pl.BlockSpec((None, tq, LANES), lambda bi, hi, qi: (bi, qi, hi)),
        out_shape=jax.ShapeDtypeStruct((b, s, (h // 2) * LANES), BF16),
        scratch_shapes=[pltpu.VMEM((2, tq, LANES), F32), pltpu.VMEM((2, tq, LANES), F32)],
        compiler_params=_cparams(("parallel", "parallel", "arbitrary")),
        name="fox_attention",
    )(kmax, clast, q, k, v)


def _merge_kernel(x_ref, oa_ref, ob_ref, ga_ref, gb_ref, wa_ref, wb_ref, wo_ref, gn_ref, h_ref, n_ref):
    ya = jnp.dot(oa_ref[...], wa_ref[...], preferred_element_type=F32)
    yb = jnp.dot(ob_ref[...], wb_ref[...], preferred_element_type=F32)
    merged = ga_ref[...] * ya + gb_ref[...] * yb
    h = x_ref[...] + jnp.dot(merged.astype(BF16), wo_ref[...], preferred_element_type=F32)
    h_ref[...] = h
    y = h * lax.rsqrt(jnp.mean(h * h, axis=-1, keepdims=True) + EPS)
    n_ref[...] = (y * gn_ref[...]).astype(n_ref.dtype)


def _merge(x2d, oa, ob, ga, gb, wa, wb, wo, gain, tm):
    t, d = x2d.shape
    row = lambda width: pl.BlockSpec((tm, width), lambda i: (i, 0))
    full = lambda a: pl.BlockSpec(a.shape, lambda i: (0, 0))
    return pl.pallas_call(
        _merge_kernel,
        grid=(t // tm,),
        in_specs=[row(d), row(oa.shape[1]), row(ob.shape[1]), row(d), row(d), full(wa), full(wb), full(wo),
                  pl.BlockSpec((1, d), lambda i: (0, 0))],
        out_specs=[row(d), row(d)],
        out_shape=[jax.ShapeDtypeStruct((t, d), F32), jax.ShapeDtypeStruct((t, d), BF16)],
        compiler_params=_cparams(("parallel",)),
        name="merge_outproj",
    )(x2d, oa, ob, ga, gb, wa, wb, wo, gain.reshape(1, d))


def _ffn_kernel(n_ref, halo_ref, h_ref, wu_ref, cw_ref, cb_ref, wd_ref, gn_ref, o_ref, next_ref, act_ref,
                *, tm, seq, tf):
    i = pl.program_id(0)
    at_start = (i * tm) % seq == 0
    next_ref[0:HALO, :] = jnp.where(at_start, jnp.zeros_like(halo_ref[...]), halo_ref[...])
    next_ref[HALO:, :] = n_ref[...]
    n_ext = next_ref[...]
    d_ff = act_ref.shape[1]

    def conv(col0):
        cols = pl.ds(col0, tf)
        u = jnp.dot(n_ext, wu_ref[:, cols], preferred_element_type=F32)
        out = cb_ref[:, cols]
        for tap in range(CONV_WIDTH):
            lo = HALO - (CONV_WIDTH - 1) + tap
            out = out + cw_ref[tap:tap + 1, cols] * u[lo:lo + tm]
        return out

    for c in range(d_ff // tf):
        gate = conv(c * tf)
        val = conv(d_ff + c * tf)
        act = 0.5 * gate * (1.0 + lax.erf(gate * (2.0 ** -0.5))) * val
        act_ref[:, c * tf:(c + 1) * tf] = act.astype(act_ref.dtype)

    h = h_ref[...] + jnp.dot(act_ref[...], wd_ref[...], preferred_element_type=F32)
    y = h * lax.rsqrt(jnp.mean(h * h, axis=-1, keepdims=True) + EPS)
    o_ref[...] = (y * gn_ref[...]).astype(o_ref.dtype)


def _ffn(n2d, h2d, w_up, conv_w, conv_b, w_down, gain, seq, tm, tf):
    t, d = n2d.shape
    d_ff = w_down.shape[0]
    halo_blocks = tm // HALO
    cb = conv_b.reshape(1, 2 * d_ff)
    once = lambda a: pl.BlockSpec(a.shape, lambda i: (0,) * a.ndim, pipeline_mode=pl.Buffered(1))
    return pl.pallas_call(
        functools.partial(_ffn_kernel, tm=tm, seq=seq, tf=tf),
        grid=(t // tm,),
        in_specs=[pl.BlockSpec((tm, d), lambda i: (i, 0)),
                  pl.BlockSpec((HALO, d), lambda i: (jnp.maximum(i * halo_blocks - 1, 0), 0)),
                  pl.BlockSpec((tm, d), lambda i: (i, 0)),
                  once(w_up), once(conv_w), once(cb), once(w_down),
                  pl.BlockSpec((1, d), lambda i: (0, 0))],
        out_specs=pl.BlockSpec((tm, d), lambda i: (i, 0)),
        out_shape=jax.ShapeDtypeStruct((t, d), F32),
        scratch_shapes=[pltpu.VMEM((HALO + tm, d), BF16), pltpu.VMEM((tm, d_ff), BF16)],
        compiler_params=_cparams(("parallel",)),
        name="conv_glu_ffn",
    )(n2d, n2d, h2d, w_up, conv_w, cb, w_down, gain.reshape(1, d))


def _tile(n, pref):
    return pref if n % pref == 0 else n


def kernel(x, norm_mix, w_in, fox_f_bias, hg_lb_logits, hg_norm, w_branch_a, w_branch_b, w_out, norm_ffn, w_up,
           conv_w, conv_b, w_down, norm_final):
    b, s, d = x.shape
    assert norm_mix.shape[0] == 1, "single-layer trunk"
    t = b * s
    hg_w = HG_HEADS * HG_DIM
    fox_w = FOX_HEADS * FOX_DIM
    cuts = [hg_w, 2 * hg_w, 3 * hg_w, 4 * hg_w, 4 * hg_w + fox_w, 4 * hg_w + 2 * fox_w, 4 * hg_w + 3 * fox_w,
            4 * hg_w + 3 * fox_w + FOX_HEADS, 4 * hg_w + 3 * fox_w + FOX_HEADS + d]
    assert w_in.shape[-1] == cuts[-1] + d
    assert hg_w == d and fox_w == d, "the projection kernel walks equal-width segments"
    w_bf = w_in[0].astype(BF16)
    w_hq, w_hf, w_hi, w_hg, w_fq, w_fk, w_fv, w_ff, w_ga, w_gb = jnp.split(w_bf, cuts, axis=-1)
    w_all = jnp.concatenate([w_hq, w_hf, w_hi, w_hg, w_fq, w_fk, w_fv, w_ga, w_gb, w_ff], axis=-1)

    x2d = x.reshape(t, d)
    hq, hlf, hk, hv, hgate, fq, fk, fv, ga, gb, f_lf = _in_proj(
        x2d, norm_mix[0], w_all, hg_lb_logits, fox_f_bias[0], _tile(t, 512), 256)
    c2, c_pieces = _fgate(f_lf.reshape(b, s, FOX_HEADS), _tile(s, 256))

    r3 = lambda a: a.reshape(b, s, a.shape[-1])
    o_a = _hgrn(r3(hq), r3(hlf), r3(hk), r3(hv), r3(hgate), hg_norm[0], _tile(s, 512))
    tk = _tile(s, 512)
    q_aug, k_aug, v_aug, kv_stats = _fox_pack(r3(fq), r3(fk), r3(fv), c_pieces, c2, tk)
    kv_stats = kv_stats.transpose(2, 0, 3, 1).reshape(2, b * FOX_HEADS, s // tk)
    o_b = _fox_attention(q_aug, k_aug, v_aug, kv_stats[0], kv_stats[1], _tile(s, 4 * tk), tk)

    h1, n_ffn = _merge(x2d, o_a.reshape(t, hg_w), o_b.reshape(t, fox_w), ga, gb, w_branch_a[0].astype(BF16),
                       w_branch_b[0].astype(BF16), w_out[0].astype(BF16), norm_ffn[0], _tile(t, 256))
    out = _ffn(n_ffn, h1, w_up[0].astype(BF16), conv_w[0], conv_b[0], w_down[0].astype(BF16), norm_final,
               s, _tile(s, 512), 256)
    return out.reshape(b, s, d)
```

```python
import functools
import math

import jax
import jax.numpy as jnp
import numpy as np
from jax import lax
from jax.experimental import pallas as pl
from jax.experimental.pallas import tpu as pltpu

F32 = jnp.float32
BF16 = jnp.bfloat16

EPS = 1e-6
HG_HEADS = 8
HG_DIM = 128
HG_CHUNK = 64
HG_SUB = 16
HG_MAX_EXPONENT = 64.0
FOX_HEADS = 16
FOX_DIM = 64
CONV_WIDTH = 3
HALO = 8
NEG_INF = float("-inf")
LOG2E = math.log2(math.e)
LANES = 128

VMEM_LIMIT = 56 * 1024 * 1024


def _cparams(sem):
    return pltpu.CompilerParams(dimension_semantics=sem, vmem_limit_bytes=VMEM_LIMIT)


def _sigmoid(x):
    return 1.0 / (1.0 + jnp.exp(-x))


def _silu(x):
    return x * _sigmoid(x)


def _log_sigmoid(x):
    return jnp.minimum(x, 0.0) - jnp.log1p(jnp.exp(-jnp.abs(x)))


def _split3(x):
    x1 = x.astype(BF16)
    r1 = x - x1.astype(F32)
    x2 = r1.astype(BF16)
    x3 = (r1 - x2.astype(F32)).astype(BF16)
    return x1, x2, x3


SEG_HQ, SEG_HF, SEG_HI, SEG_HG, SEG_FQ, SEG_FK, SEG_FV, SEG_GA, SEG_GB = range(9)
N_SEG = 9


def _in_proj_kernel(x_ref, gain_ref, w_ref, lbl_ref, fbias_ref, hq_ref, hlf_ref, hk_ref, hv_ref, hg_ref,
                    fq_ref, fk_ref, fv_ref, ga_ref, gb_ref, flf_ref, n_ref, *, tn):
    x = x_ref[...]
    n_ref[...] = (x * lax.rsqrt(jnp.mean(x * x, axis=-1, keepdims=True) + EPS) * gain_ref[...]).astype(BF16)
    n = n_ref[...]
    width = hq_ref.shape[1]
    lbl = lbl_ref[...]
    e = jnp.exp(lbl - jnp.max(lbl, axis=0, keepdims=True))
    lb = e[0:1] / jnp.sum(e, axis=0, keepdims=True)

    def chunks(seg):
        for c in range(width // tn):
            cols = slice(c * tn, (c + 1) * tn)
            yield cols, jnp.dot(n, w_ref[:, seg * width + c * tn:seg * width + (c + 1) * tn],
                                preferred_element_type=F32)

    def emit(seg, ref, fn):
        for cols, acc in chunks(seg):
            ref[:, cols] = fn(acc).astype(ref.dtype)

    emit(SEG_HQ, hq_ref, _silu)
    for cols, acc in chunks(SEG_HF):
        f = lb[:, cols] + (1.0 - lb[:, cols]) * _sigmoid(acc)
        hlf_ref[:, cols] = jnp.log(f)
        hk_ref[:, cols] = (1.0 - f).astype(hk_ref.dtype)
    emit(SEG_HI, hv_ref, lambda a: a)
    emit(SEG_HG, hg_ref, _silu)
    emit(SEG_FQ, fq_ref, lambda a: a * (FOX_DIM ** -0.5 * LOG2E))
    emit(SEG_FK, fk_ref, lambda a: a)
    emit(SEG_FV, fv_ref, lambda a: a)
    emit(SEG_GA, ga_ref, _sigmoid)
    emit(SEG_GB, gb_ref, _sigmoid)
    ff = jnp.dot(n, w_ref[:, N_SEG * width:], preferred_element_type=F32)
    flf_ref[...] = _log_sigmoid(ff + fbias_ref[...])


def _in_proj(x2d, gain, w_all, lb_logits, f_bias, tm, tn):
    t, d = x2d.shape
    heads = f_bias.shape[0]
    once = lambda a: pl.BlockSpec(a.shape, lambda i: (0,) * a.ndim, pipeline_mode=pl.Buffered(1))
    row = lambda width: pl.BlockSpec((tm, width), lambda i: (i, 0))
    seg_dtypes = [BF16, F32, BF16, BF16, BF16, BF16, BF16, BF16, BF16, BF16]
    gain2, bias2 = gain.reshape(1, d), f_bias.reshape(1, heads)
    return pl.pallas_call(
        functools.partial(_in_proj_kernel, tn=tn),
        grid=(t // tm,),
        in_specs=[row(d), once(gain2), once(w_all), once(lb_logits), once(bias2)],
        out_specs=[row(d)] * len(seg_dtypes) + [row(heads)],
        out_shape=[jax.ShapeDtypeStruct((t, d), dt) for dt in seg_dtypes] + [jax.ShapeDtypeStruct((t, heads), F32)],
        scratch_shapes=[pltpu.VMEM((tm, d), BF16)],
        compiler_params=_cparams(("parallel",)),
        name="in_proj",
    )(x2d, gain2, w_all, lb_logits, bias2)


CP_WIDTH = 4 * FOX_HEADS


def _fgate_kernel(lf_ref, spread_ref, one_ref, c_ref, cp_ref, carry_ref):
    @pl.when(pl.program_id(1) == 0)
    def _():
        carry_ref[...] = jnp.zeros_like(carry_ref)

    lf = lf_ref[...]
    tm = lf.shape[0]
    r = lax.broadcasted_iota(jnp.int32, (tm, tm), 0)
    c = lax.broadcasted_iota(jnp.int32, (tm, tm), 1)
    lower = (c <= r).astype(BF16)
    cs = carry_ref[...]
    for piece in _split3(lf):
        cs = cs + jnp.dot(lower, piece, preferred_element_type=F32)
    carry_ref[...] = cs[tm - 1:tm, :]
    c2 = cs * LOG2E
    c_ref[...] = c2
    cp = one_ref[...]
    for i, piece in enumerate(_split3(c2)):
        cp = cp + jnp.dot(piece, spread_ref[i], preferred_element_type=F32)
    cp_ref[...] = cp.astype(cp_ref.dtype)


def _fgate(lf3d, tm):
    b, s, h = lf3d.shape
    spread = np.zeros((3, h, CP_WIDTH), np.float32)
    for i in range(3):
        spread[i, np.arange(h), i * h + np.arange(h)] = 1.0
    one = np.zeros((1, CP_WIDTH), np.float32)
    one[0, 3 * h] = 1.0
    return pl.pallas_call(
        _fgate_kernel,
        grid=(b, s // tm),
        in_specs=[pl.BlockSpec((None, tm, h), lambda i, j: (i, j, 0)),
                  pl.BlockSpec((3, h, CP_WIDTH), lambda i, j: (0, 0, 0)),
                  pl.BlockSpec((1, CP_WIDTH), lambda i, j: (0, 0))],
        out_specs=[pl.BlockSpec((None, tm, h), lambda i, j: (i, j, 0)),
                   pl.BlockSpec((None, tm, CP_WIDTH), lambda i, j: (i, j, 0))],
        out_shape=[jax.ShapeDtypeStruct((b, s, h), F32), jax.ShapeDtypeStruct((b, s, CP_WIDTH), BF16)],
        scratch_shapes=[pltpu.VMEM((1, h), F32)],
        compiler_params=_cparams(("parallel", "arbitrary")),
        name="fox_fgate_cumsum",
    )(lf3d, jnp.asarray(spread, BF16), jnp.asarray(one))


def _causal_chunk():
    r_i = lax.broadcasted_iota(jnp.int32, (HG_CHUNK, HG_CHUNK), 0)
    c_i = lax.broadcasted_iota(jnp.int32, (HG_CHUNK, HG_CHUNK), 1)
    return c_i <= r_i


def _chunk_cumsum(lf):
    lower = _causal_chunk().astype(BF16)
    b = jnp.zeros(lf.shape, F32)
    for piece in _split3(lf):
        b = b + jnp.dot(lower, piece, preferred_element_type=F32)
    return b


def _hgrn_chunk_single_ref(q, b, k, v, state):
    c = HG_CHUNK
    q_t = (q * jnp.exp(b)).astype(BF16)
    k_up = k * jnp.exp(-b)
    sc = lax.dot_general(q_t, k_up.astype(BF16), (((1,), (1,)), ((), ())), preferred_element_type=F32)
    sc = jnp.where(_causal_chunk(), sc, 0.0).astype(BF16)
    v_bf = v.astype(BF16)
    o = jnp.dot(jnp.concatenate([q_t, sc], axis=1), jnp.concatenate([state.astype(BF16), v_bf], axis=0),
                preferred_element_type=F32)
    b_end = b[c - 1:c]
    k_end = k_up * jnp.exp(b_end)
    tr = jnp.concatenate([k_end, jnp.broadcast_to(b_end, (HG_DIM - c, HG_DIM))], axis=0).T
    upd = jnp.dot(tr[:, :c].astype(BF16), v_bf, preferred_element_type=F32)
    return o, state * jnp.exp(tr[:, c:c + 1]) + upd


def _hgrn_chunk(q, b, k, v, state):
    c = HG_CHUNK
    n_sub = c // HG_SUB
    state_t = state.T

    qe = (q * jnp.exp(b)).astype(BF16)
    o = lax.dot_general(qe, state_t.astype(BF16), (((1,), (1,)), ((), ())), preferred_element_type=F32)

    o_parts = []
    t_iota = lax.broadcasted_iota(jnp.int32, (HG_SUB, HG_DIM), 0)
    for i in range(n_sub):
        lo = i * HG_SUB
        q_i = q[lo:lo + HG_SUB]
        b_i = b[lo:lo + HG_SUB]
        o_i = o[lo:lo + HG_SUB]
        if i > 0:
            ref = b[lo - 1:lo]
            q_t = (q_i * jnp.exp(b_i - ref)).astype(BF16)
            k_t = (k[:lo] * jnp.exp(ref - b[:lo])).astype(BF16)
            sc = lax.dot_general(q_t, k_t, (((1,), (1,)), ((), ())), preferred_element_type=F32)
            o_i = o_i + jnp.dot(sc.astype(BF16), v[:lo].astype(BF16), preferred_element_type=F32)
        for s in range(HG_SUB):
            row = lo + s
            rel = jnp.where(t_iota >= s, b_i - b[row:row + 1], NEG_INF)
            p = q_i * jnp.exp(rel) * k[row:row + 1]
            o_i = o_i + jnp.sum(p, axis=-1, keepdims=True) * v[row:row + 1]
        o_parts.append(o_i)
    o = jnp.concatenate(o_parts, axis=0)

    b_end = b[c - 1:c]
    k_dec = (k * jnp.exp(b_end - b)).astype(BF16)
    upd = lax.dot_general(v.astype(BF16), k_dec, (((0,), (0,)), ((), ())), preferred_element_type=F32)
    return o, (state_t * jnp.exp(b_end) + upd).T


def _hgrn_kernel(q_ref, lf_ref, k_ref, v_ref, g_ref, gn_ref, o_ref, state_ref, *, n_chunks):
    @pl.when(pl.program_id(2) == 0)
    def _():
        state_ref[...] = jnp.zeros_like(state_ref)

    def run_chunk(chunk_fn, rows, b, state):
        load = lambda ref: ref[rows, :].astype(F32)
        o, new_state = chunk_fn(load(q_ref), b, load(k_ref), load(v_ref), state)
        y = o * lax.rsqrt(jnp.mean(o * o, axis=-1, keepdims=True) + EPS)
        o_ref[rows, :] = (y * gn_ref[...] * load(g_ref)).astype(o_ref.dtype)
        return new_state

    single_ref_ok = jnp.min(lf_ref[...]) * HG_CHUNK > -HG_MAX_EXPONENT

    @pl.when(single_ref_ok)
    def _():
        chunk_rows = [pl.ds(ci * HG_CHUNK, HG_CHUNK) for ci in range(n_chunks)]
        b_all = _chunk_cumsum(jnp.concatenate([lf_ref[rows, :] for rows in chunk_rows], axis=1))
        state = state_ref[...]
        for ci, rows in enumerate(chunk_rows):
            state = run_chunk(_hgrn_chunk_single_ref, rows, b_all[:, ci * HG_DIM:(ci + 1) * HG_DIM], state)
        state_ref[...] = state

    @pl.when(jnp.logical_not(single_ref_ok))
    def _():
        def body(ci, carry):
            rows = pl.ds(pl.multiple_of(ci * HG_CHUNK, HG_CHUNK), HG_CHUNK)
            state_ref[...] = run_chunk(_hgrn_chunk, rows, _chunk_cumsum(lf_ref[rows, :]), state_ref[...])
            return carry

        lax.fori_loop(0, n_chunks, body, 0)


def _hgrn(q, lf, k, v, g, g_norm, tc):
    b, s, w = q.shape
    heads = w // HG_DIM
    spec = pl.BlockSpec((None, tc, HG_DIM), lambda bi, hi, ti: (bi, ti, hi))
    return pl.pallas_call(
        functools.partial(_hgrn_kernel, n_chunks=tc // HG_CHUNK),
        grid=(b, heads, s // tc),
        in_specs=[spec] * 5 + [pl.BlockSpec((1, HG_DIM), lambda bi, hi, ti: (0, 0))],
        out_specs=spec,
        out_shape=jax.ShapeDtypeStruct((b, s, w), BF16),
        scratch_shapes=[pltpu.VMEM((HG_DIM, HG_DIM), F32)],
        compiler_params=_cparams(("parallel", "parallel", "arbitrary")),
        name="hgrn2_mixer",
    )(q, lf, k, v, g, g_norm.reshape(1, HG_DIM))


AUG_C = FOX_DIM
AUG_ONE = FOX_DIM + 3
AUG_QN = FOX_DIM + 6
NORM_MARGIN = 1.02
SKIP_LOG2 = 150.0


def _placement_constants(heads):
    pq = np.zeros((CP_WIDTH, heads * LANES), np.float32)
    pk = np.zeros((CP_WIDTH, heads * LANES), np.float32)
    pn = np.zeros((heads, heads * LANES), np.float32)
    hs = np.zeros((heads * FOX_DIM, heads), np.float32)
    for h in range(heads):
        for i in range(3):
            pq[i * heads + h, h * LANES + AUG_C + i] = 1.0
            pk[i * heads + h, h * LANES + AUG_ONE + i] = -1.0
        pq[3 * heads, h * LANES + AUG_ONE:h * LANES + AUG_ONE + 3] = 1.0
        pk[3 * heads, h * LANES + AUG_C:h * LANES + AUG_C + 3] = 1.0
        pn[h, h * LANES + AUG_QN] = 1.0
        hs[h * FOX_DIM:(h + 1) * FOX_DIM, h] = 1.0
    return tuple(jnp.asarray(a, BF16) for a in (pq, pk, pn, hs))


def _fox_pack_kernel(q_ref, k_ref, v_ref, cp_ref, c_ref, pq_ref, pk_ref, pn_ref, hs_ref,
                     qa_ref, ka_ref, va_ref, st_ref):
    q = q_ref[...]
    k = k_ref[...]
    v = v_ref[...]
    cp = cp_ref[...]
    tm = q.shape[0]
    heads = qa_ref.shape[0]
    hs = hs_ref[...]
    qn = jnp.sqrt(jnp.dot(q * q, hs, preferred_element_type=F32)) * NORM_MARGIN
    kn = jnp.sqrt(jnp.dot(k * k, hs, preferred_element_type=F32)) * NORM_MARGIN
    tail_q = (jnp.dot(cp, pq_ref[...], preferred_element_type=F32)
              + jnp.dot(qn.astype(BF16), pn_ref[...], preferred_element_type=F32)).astype(BF16)
    tail_k = jnp.dot(cp, pk_ref[...], preferred_element_type=F32).astype(BF16)
    lane = lax.broadcasted_iota(jnp.int32, (tm, LANES), 1)
    in_data = lane < FOX_DIM
    tail_v = jnp.where(lane == FOX_DIM, 1.0, 0.0).astype(BF16)
    for h in range(heads):
        pair = slice((h // 2) * LANES, (h // 2 + 1) * LANES)
        mine = slice(h * LANES, (h + 1) * LANES)
        data = (lambda x: x[:, pair]) if h % 2 == 0 else (lambda x: pltpu.roll(x[:, pair], FOX_DIM, 1))
        qa_ref[h] = jnp.where(in_data, data(q), tail_q[:, mine])
        ka_ref[h] = jnp.where(in_data, data(k), tail_k[:, mine])
        va_ref[h] = jnp.where(in_data, data(v), tail_v)
    st_ref[0:1, :] = jnp.max(kn, axis=0, keepdims=True)
    st_ref[1:2, :] = c_ref[tm - 1:tm, :]


def _fox_pack(q, k, v, cp, c, tm):
    b, s, w = q.shape
    heads = c.shape[-1]
    consts = _placement_constants(heads)
    row = lambda width: pl.BlockSpec((None, tm, width), lambda bi, ti: (bi, ti, 0))
    full = lambda a: pl.BlockSpec(a.shape, lambda bi, ti: (0, 0))
    out = pl.BlockSpec((None, heads, tm, LANES), lambda bi, ti: (bi, 0, ti, 0))
    oshape = jax.ShapeDtypeStruct((b, heads, s, LANES), BF16)
    return pl.pallas_call(
        _fox_pack_kernel,
        grid=(b, s // tm),
        in_specs=[row(w), row(w), row(w), row(CP_WIDTH), row(heads)] + [full(a) for a in consts],
        out_specs=[out] * 3 + [pl.BlockSpec((None, None, 2, heads), lambda bi, ti: (bi, ti, 0, 0))],
        out_shape=[oshape] * 3 + [jax.ShapeDtypeStruct((b, s // tm, 2, heads), F32)],
        compiler_params=_cparams(("parallel", "parallel")),
        name="fox_pack",
    )(q, k, v, cp, c, *consts)


def _fox_kernel(kmax_ref, clast_ref, q_ref, k_ref, v_ref, o_ref, m_ref, acc_ref, *, tq, tk):
    qi = pl.program_id(2)
    row_a = pl.program_id(0) * (2 * pl.num_programs(1)) + 2 * pl.program_id(1)
    m_ref[...] = jnp.full_like(m_ref, NEG_INF)
    acc_ref[...] = jnp.zeros_like(acc_ref)

    def block(row0, n_rows, kv0, masked):
        rows = pl.ds(row0, n_rows)
        cols = pl.ds(kv0, tk)
        if masked:
            causal = (lax.broadcasted_iota(jnp.int32, (n_rows, tk), 1)
                      <= lax.broadcasted_iota(jnp.int32, (n_rows, tk), 0))
        for a in range(2):
            s = lax.dot_general(q_ref[a, rows, :], k_ref[a, cols, :], (((1,), (1,)), ((), ())),
                                preferred_element_type=F32)
            if masked:
                s = jnp.where(causal, s, NEG_INF)
            m_prev = m_ref[a, rows, :]
            m_new = jnp.maximum(m_prev, jnp.max(s, axis=-1, keepdims=True))
            alpha = jnp.exp2(m_prev - m_new)
            p = jnp.exp2(s - m_new[:, 0:1])
            acc_ref[a, rows, :] = alpha * acc_ref[a, rows, :] + jnp.dot(
                p.astype(BF16), v_ref[a, cols, :], preferred_element_type=F32)
            m_ref[a, rows, :] = m_new

    n_diag = tq // tk
    n_full = qi * n_diag

    for j in range(n_diag):
        block(j * tk, tq - j * tk, pl.multiple_of((n_full + j) * tk, tk), True)

    n_tiles = kmax_ref.shape[1]
    slack = []
    for a in range(2):
        kmax = kmax_ref[row_a + a, 0]
        for j in range(1, n_tiles):
            kmax = jnp.maximum(kmax, kmax_ref[row_a + a, j])
        sel_row = lax.broadcasted_iota(jnp.int32, (LANES, LANES), 0)
        sel = jnp.where(sel_row == AUG_QN, kmax * NORM_MARGIN,
                        jnp.where((sel_row >= AUG_C) & (sel_row < AUG_ONE), 1.0, 0.0)).astype(BF16)
        bound = jnp.dot(q_ref[a], sel, preferred_element_type=F32)
        slack.append(jnp.max(bound - m_ref[a]))

    def tile_is_live(j):
        live = [slack[a] - clast_ref[row_a + a, j] > -SKIP_LOG2 for a in range(2)]
        return live[0] | live[1]

    def cond(j):
        return (j >= 0) & tile_is_live(jnp.maximum(j, 0))

    def body(j):
        block(0, tq, pl.multiple_of(j * tk, tk), False)
        return j - 1

    lax.while_loop(cond, body, n_full - 1)

    lane = lax.broadcasted_iota(jnp.int32, (tq, LANES), 1)
    acc_a = acc_ref[0]
    acc_b = acc_ref[1]
    o_a = acc_a / acc_a[:, FOX_DIM:FOX_DIM + 1]
    o_b = acc_b / acc_b[:, FOX_DIM:FOX_DIM + 1]
    o_ref[...] = jnp.where(lane < FOX_DIM, o_a, pltpu.roll(o_b, FOX_DIM, 1)).astype(o_ref.dtype)


def _fox_attention(q, k, v, kmax, clast, tq, tk):
    b, h, s, _ = q.shape
    kv_spec = pl.BlockSpec((None, 2, s, LANES), lambda bi, hi, qi: (bi, hi, 0, 0))
    smem = pl.BlockSpec(memory_space=pltpu.SMEM)
    return pl.pallas_call(
        functools.partial(_fox_kernel, tq=tq, tk=tk),
        grid=(b, h // 2, s // tq),
        in_specs=[smem, smem,
                  pl.BlockSpec((None, 2, tq, LANES), lambda bi, hi, qi: (bi, hi, qi, 0)), kv_spec, kv_spec],
        out_specs=pl.BlockSpec((None, tq, LANES), lambda bi, hi, qi: (bi, qi, hi)),
        out_shape=jax.ShapeDtypeStruct((b, s, (h // 2) * LANES), BF16),
        scratch_shapes=[pltpu.VMEM((2, tq, LANES), F32), pltpu.VMEM((2, tq, LANES), F32)],
        compiler_params=_cparams(("parallel", "parallel", "arbitrary")),
        name="fox_attention",
    )(kmax, clast, q, k, v)


def _merge_kernel(x_ref, oa_ref, ob_ref, ga_ref, gb_ref, wa_ref, wb_ref, wo_ref, gn_ref, h_ref, n_ref):
    ya = jnp.dot(oa_ref[...], wa_ref[...], preferred_element_type=F32)
    yb = jnp.dot(ob_ref[...], wb_ref[...], preferred_element_type=F32)
    merged = ga_ref[...] * ya + gb_ref[...] * yb
    h = x_ref[...] + jnp.dot(merged.astype(BF16), wo_ref[...], preferred_element_type=F32)
    h_ref[...] = h
    y = h * lax.rsqrt(jnp.mean(h * h, axis=-1, keepdims=True) + EPS)
    n_ref[...] = (y * gn_ref[...]).astype(n_ref.dtype)


def _merge(x2d, oa, ob, ga, gb, wa, wb, wo, gain, tm):
    t, d = x2d.shape
    row = lambda width: pl.BlockSpec((tm, width), lambda i: (i, 0))
    full = lambda a: pl.BlockSpec(a.shape, lambda i: (0, 0))
    return pl.pallas_call(
        _merge_kernel,
        grid=(t // tm,),
        in_specs=[row(d), row(oa.shape[1]), row(ob.shape[1]), row(d), row(d), full(wa), full(wb), full(wo),
                  pl.BlockSpec((1, d), lambda i: (0, 0))],
        out_specs=[row(d), row(d)],
        out_shape=[jax.ShapeDtypeStruct((t, d), F32), jax.ShapeDtypeStruct((t, d), BF16)],
        compiler_params=_cparams(("parallel",)),
        name="merge_outproj",
    )(x2d, oa, ob, ga, gb, wa, wb, wo, gain.reshape(1, d))


def _ffn_kernel(n_ref, halo_ref, h_ref, wu_ref, cw_ref, cb_ref, wd_ref, gn_ref, o_ref, next_ref, act_ref,
                *, tm, seq, tf):
    i = pl.program_id(0)
    at_start = (i * tm) % seq == 0
    next_ref[0:HALO, :] = jnp.where(at_start, jnp.zeros_like(halo_ref[...]), halo_ref[...])
    next_ref[HALO:, :] = n_ref[...]
    n_ext = next_ref[...]
    d_ff = act_ref.shape[1]

    def conv(col0):
        cols = pl.ds(col0, tf)
        u = jnp.dot(n_ext, wu_ref[:, cols], preferred_element_type=F32)
        out = cb_ref[:, cols]
        for tap in range(CONV_WIDTH):
            lo = HALO - (CONV_WIDTH - 1) + tap
            out = out + cw_ref[tap:tap + 1, cols] * u[lo:lo + tm]
        return out

    for c in range(d_ff // tf):
        gate = conv(c * tf)
        val = conv(d_ff + c * tf)
        act = 0.5 * gate * (1.0 + lax.erf(gate * (2.0 ** -0.5))) * val
        act_ref[:, c * tf:(c + 1) * tf] = act.astype(act_ref.dtype)

    h = h_ref[...] + jnp.dot(act_ref[...], wd_ref[...], preferred_element_type=F32)
    y = h * lax.rsqrt(jnp.mean(h * h, axis=-1, keepdims=True) + EPS)
    o_ref[...] = (y * gn_ref[...]).astype(o_ref.dtype)


def _ffn(n2d, h2d, w_up, conv_w, conv_b, w_down, gain, seq, tm, tf):
    t, d = n2d.shape
    d_ff = w_down.shape[0]
    halo_blocks = tm // HALO
    cb = conv_b.reshape(1, 2 * d_ff)
    once = lambda a: pl.BlockSpec(a.shape, lambda i: (0,) * a.ndim, pipeline_mode=pl.Buffered(1))
    return pl.pallas_call(
        functools.partial(_ffn_kernel, tm=tm, seq=seq, tf=tf),
        grid=(t // tm,),
        in_specs=[pl.BlockSpec((tm, d), lambda i: (i, 0)),
                  pl.BlockSpec((HALO, d), lambda i: (jnp.maximum(i * halo_blocks - 1, 0), 0)),
                  pl.BlockSpec((tm, d), lambda i: (i, 0)),
                  once(w_up), once(conv_w), once(cb), once(w_down),
                  pl.BlockSpec((1, d), lambda i: (0, 0))],
        out_specs=pl.BlockSpec((tm, d), lambda i: (i, 0)),
        out_shape=jax.ShapeDtypeStruct((t, d), F32),
        scratch_shapes=[pltpu.VMEM((HALO + tm, d), BF16), pltpu.VMEM((tm, d_ff), BF16)],
        compiler_params=_cparams(("parallel",)),
        name="conv_glu_ffn",
    )(n2d, n2d, h2d, w_up, conv_w, cb, w_down, gain.reshape(1, d))


def _tile(n, pref):
    return pref if n % pref == 0 else n


def kernel(x, norm_mix, w_in, fox_f_bias, hg_lb_logits, hg_norm, w_branch_a, w_branch_b, w_out, norm_ffn, w_up,
           conv_w, conv_b, w_down, norm_final):
    b, s, d = x.shape
    assert norm_mix.shape[0] == 1, "single-layer trunk"
    t = b * s
    hg_w = HG_HEADS * HG_DIM
    fox_w = FOX_HEADS * FOX_DIM
    cuts = [hg_w, 2 * hg_w, 3 * hg_w, 4 * hg_w, 4 * hg_w + fox_w, 4 * hg_w + 2 * fox_w, 4 * hg_w + 3 * fox_w,
            4 * hg_w + 3 * fox_w + FOX_HEADS, 4 * hg_w + 3 * fox_w + FOX_HEADS + d]
    assert w_in.shape[-1] == cuts[-1] + d
    assert hg_w == d and fox_w == d, "the projection kernel walks equal-width segments"
    w_bf = w_in[0].astype(BF16)
    w_hq, w_hf, w_hi, w_hg, w_fq, w_fk, w_fv, w_ff, w_ga, w_gb = jnp.split(w_bf, cuts, axis=-1)
    w_all = jnp.concatenate([w_hq, w_hf, w_hi, w_hg, w_fq, w_fk, w_fv, w_ga, w_gb, w_ff], axis=-1)

    x2d = x.reshape(t, d)
    hq, hlf, hk, hv, hgate, fq, fk, fv, ga, gb, f_lf = _in_proj(
        x2d, norm_mix[0], w_all, hg_lb_logits, fox_f_bias[0], _tile(t, 512), 256)
    c2, c_pieces = _fgate(f_lf.reshape(b, s, FOX_HEADS), _tile(s, 256))

    r3 = lambda a: a.reshape(b, s, a.shape[-1])
    o_a = _hgrn(r3(hq), r3(hlf), r3(hk), r3(hv), r3(hgate), hg_norm[0], _tile(s, 1024))
    tk = _tile(s, 512)
    q_aug, k_aug, v_aug, kv_stats = _fox_pack(r3(fq), r3(fk), r3(fv), c_pieces, c2, tk)
    kv_stats = kv_stats.transpose(2, 0, 3, 1).reshape(2, b * FOX_HEADS, s // tk)
    o_b = _fox_attention(q_aug, k_aug, v_aug, kv_stats[0], kv_stats[1], _tile(s, 4 * tk), tk)

    h1, n_ffn = _merge(x2d, o_a.reshape(t, hg_w), o_b.reshape(t, fox_w), ga, gb, w_branch_a[0].astype(BF16),
                       w_branch_b[0].astype(BF16), w_out[0].astype(BF16), norm_ffn[0], _tile(t, 256))
    out = _ffn(n_ffn, h1, w_up[0].astype(BF16), conv_w[0], conv_b[0], w_down[0].astype(BF16), norm_final,
               s, _tile(s, 512), 256)
    return out.reshape(b, s, d)
```

```python
import functools
import math

import jax
import jax.numpy as jnp
import numpy as np
from jax import lax
from jax.experimental import pallas as pl
from jax.experimental.pallas import tpu as pltpu

F32 = jnp.float32
BF16 = jnp.bfloat16

EPS = 1e-6
HG_HEADS = 8
HG_DIM = 128
HG_CHUNK = 64
HG_SUB = 16
HG_MAX_EXPONENT = 64.0
FOX_HEADS = 16
FOX_DIM = 64
CONV_WIDTH = 3
HALO = 8
NEG_INF = float("-inf")
LOG2E = math.log2(math.e)
LANES = 128

VMEM_LIMIT = 56 * 1024 * 1024


def _cparams(sem):
    return pltpu.CompilerParams(dimension_semantics=sem, vmem_limit_bytes=VMEM_LIMIT)


def _sigmoid(x):
    return 1.0 / (1.0 + jnp.exp(-x))


def _silu(x):
    return x * _sigmoid(x)


def _log_sigmoid(x):
    return jnp.minimum(x, 0.0) - jnp.log1p(jnp.exp(-jnp.abs(x)))


def _split3(x):
    x1 = x.astype(BF16)
    r1 = x - x1.astype(F32)
    x2 = r1.astype(BF16)
    x3 = (r1 - x2.astype(F32)).astype(BF16)
    return x1, x2, x3


SEG_HQ, SEG_HF, SEG_HI, SEG_HG, SEG_FQ, SEG_FK, SEG_FV, SEG_GA, SEG_GB = range(9)
N_SEG = 9


def _in_proj_kernel(x_ref, gain_ref, w_ref, lbl_ref, fbias_ref, spread_ref, one_ref, hq_ref, hlf_ref, hk_ref,
                    hv_ref, hg_ref, fq_ref, fk_ref, fv_ref, ga_ref, gb_ref, c_ref, cp_ref, n_ref, carry_ref,
                    *, tn, seq):
    x = x_ref[...]
    n_ref[...] = (x * lax.rsqrt(jnp.mean(x * x, axis=-1, keepdims=True) + EPS) * gain_ref[...]).astype(BF16)
    n = n_ref[...]
    width = hq_ref.shape[1]
    lbl = lbl_ref[...]
    e = jnp.exp(lbl - jnp.max(lbl, axis=0, keepdims=True))
    lb = e[0:1] / jnp.sum(e, axis=0, keepdims=True)

    def chunks(seg):
        for c in range(width // tn):
            cols = slice(c * tn, (c + 1) * tn)
            yield cols, jnp.dot(n, w_ref[:, seg * width + c * tn:seg * width + (c + 1) * tn],
                                preferred_element_type=F32)

    def emit(seg, ref, fn):
        for cols, acc in chunks(seg):
            ref[:, cols] = fn(acc).astype(ref.dtype)

    emit(SEG_HQ, hq_ref, _silu)
    for cols, acc in chunks(SEG_HF):
        f = lb[:, cols] + (1.0 - lb[:, cols]) * _sigmoid(acc)
        hlf_ref[:, cols] = jnp.log(f)
        hk_ref[:, cols] = (1.0 - f).astype(hk_ref.dtype)
    emit(SEG_HI, hv_ref, lambda a: a)
    emit(SEG_HG, hg_ref, _silu)
    emit(SEG_FQ, fq_ref, lambda a: a * (FOX_DIM ** -0.5 * LOG2E))
    emit(SEG_FK, fk_ref, lambda a: a)
    emit(SEG_FV, fv_ref, lambda a: a)
    emit(SEG_GA, ga_ref, _sigmoid)
    emit(SEG_GB, gb_ref, _sigmoid)
    @pl.when((pl.program_id(0) * x.shape[0]) % seq == 0)
    def _():
        carry_ref[...] = jnp.zeros_like(carry_ref)

    lf = _log_sigmoid(jnp.dot(n, w_ref[:, N_SEG * width:], preferred_element_type=F32) + fbias_ref[...])
    tm = lf.shape[0]
    lower = (lax.broadcasted_iota(jnp.int32, (tm, tm), 1)
             <= lax.broadcasted_iota(jnp.int32, (tm, tm), 0)).astype(BF16)
    cs = carry_ref[...]
    for piece in _split3(lf):
        cs = cs + jnp.dot(lower, piece, preferred_element_type=F32)
    carry_ref[...] = cs[tm - 1:tm, :]
    c2 = cs * LOG2E
    c_ref[...] = c2
    cp = one_ref[...]
    for i, piece in enumerate(_split3(c2)):
        cp = cp + jnp.dot(piece, spread_ref[i], preferred_element_type=F32)
    cp_ref[...] = cp.astype(cp_ref.dtype)


CP_WIDTH = 4 * FOX_HEADS


def _in_proj(x2d, gain, w_all, lb_logits, f_bias, seq, tm, tn):
    t, d = x2d.shape
    heads = f_bias.shape[0]
    spread = np.zeros((3, heads, CP_WIDTH), np.float32)
    for i in range(3):
        spread[i, np.arange(heads), i * heads + np.arange(heads)] = 1.0
    one = np.zeros((1, CP_WIDTH), np.float32)
    one[0, 3 * heads] = 1.0
    once = lambda a: pl.BlockSpec(a.shape, lambda i: (0,) * a.ndim, pipeline_mode=pl.Buffered(1))
    row = lambda width: pl.BlockSpec((tm, width), lambda i: (i, 0))
    seg_dtypes = [BF16, F32, BF16, BF16, BF16, BF16, BF16, BF16, BF16, BF16]
    consts = (gain.reshape(1, d), w_all, lb_logits, f_bias.reshape(1, heads), jnp.asarray(spread, BF16),
              jnp.asarray(one))
    return pl.pallas_call(
        functools.partial(_in_proj_kernel, tn=tn, seq=seq),
        grid=(t // tm,),
        in_specs=[row(d)] + [once(a) for a in consts],
        out_specs=[row(d)] * len(seg_dtypes) + [row(heads), row(CP_WIDTH)],
        out_shape=([jax.ShapeDtypeStruct((t, d), dt) for dt in seg_dtypes]
                   + [jax.ShapeDtypeStruct((t, heads), F32), jax.ShapeDtypeStruct((t, CP_WIDTH), BF16)]),
        scratch_shapes=[pltpu.VMEM((tm, d), BF16), pltpu.VMEM((1, heads), F32)],
        compiler_params=_cparams(("arbitrary",)),
        name="in_proj",
    )(x2d, *consts)


def _causal_chunk():
    r_i = lax.broadcasted_iota(jnp.int32, (HG_CHUNK, HG_CHUNK), 0)
    c_i = lax.broadcasted_iota(jnp.int32, (HG_CHUNK, HG_CHUNK), 1)
    return c_i <= r_i


def _chunk_cumsum(lf):
    lower = _causal_chunk().astype(BF16)
    b = jnp.zeros(lf.shape, F32)
    for piece in _split3(lf):
        b = b + jnp.dot(lower, piece, preferred_element_type=F32)
    return b


def _hgrn_chunk_single_ref(q, b, k, v, state):
    c = HG_CHUNK
    q_t = (q * jnp.exp(b)).astype(BF16)
    k_up = k * jnp.exp(-b)
    sc = lax.dot_general(q_t, k_up.astype(BF16), (((1,), (1,)), ((), ())), preferred_element_type=F32)
    sc = jnp.where(_causal_chunk(), sc, 0.0).astype(BF16)
    v_bf = v.astype(BF16)
    o = jnp.dot(jnp.concatenate([q_t, sc], axis=1), jnp.concatenate([state.astype(BF16), v_bf], axis=0),
                preferred_element_type=F32)
    b_end = b[c - 1:c]
    k_end = k_up * jnp.exp(b_end)
    tr = jnp.concatenate([k_end, jnp.broadcast_to(b_end, (HG_DIM - c, HG_DIM))], axis=0).T
    upd = jnp.dot(tr[:, :c].astype(BF16), v_bf, preferred_element_type=F32)
    return o, state * jnp.exp(tr[:, c:c + 1]) + upd


def _hgrn_chunk(q, b, k, v, state):
    c = HG_CHUNK
    n_sub = c // HG_SUB
    state_t = state.T

    qe = (q * jnp.exp(b)).astype(BF16)
    o = lax.dot_general(qe, state_t.astype(BF16), (((1,), (1,)), ((), ())), preferred_element_type=F32)

    o_parts = []
    t_iota = lax.broadcasted_iota(jnp.int32, (HG_SUB, HG_DIM), 0)
    for i in range(n_sub):
        lo = i * HG_SUB
        q_i = q[lo:lo + HG_SUB]
        b_i = b[lo:lo + HG_SUB]
        o_i = o[lo:lo + HG_SUB]
        if i > 0:
            ref = b[lo - 1:lo]
            q_t = (q_i * jnp.exp(b_i - ref)).astype(BF16)
            k_t = (k[:lo] * jnp.exp(ref - b[:lo])).astype(BF16)
            sc = lax.dot_general(q_t, k_t, (((1,), (1,)), ((), ())), preferred_element_type=F32)
            o_i = o_i + jnp.dot(sc.astype(BF16), v[:lo].astype(BF16), preferred_element_type=F32)
        for s in range(HG_SUB):
            row = lo + s
            rel = jnp.where(t_iota >= s, b_i - b[row:row + 1], NEG_INF)
            p = q_i * jnp.exp(rel) * k[row:row + 1]
            o_i = o_i + jnp.sum(p, axis=-1, keepdims=True) * v[row:row + 1]
        o_parts.append(o_i)
    o = jnp.concatenate(o_parts, axis=0)

    b_end = b[c - 1:c]
    k_dec = (k * jnp.exp(b_end - b)).astype(BF16)
    upd = lax.dot_general(v.astype(BF16), k_dec, (((0,), (0,)), ((), ())), preferred_element_type=F32)
    return o, (state_t * jnp.exp(b_end) + upd).T


def _hgrn_kernel(q_ref, lf_ref, k_ref, v_ref, g_ref, gn_ref, o_ref, state_ref, *, n_chunks):
    @pl.when(pl.program_id(2) == 0)
    def _():
        state_ref[...] = jnp.zeros_like(state_ref)

    def run_chunk(chunk_fn, rows, b, state):
        load = lambda ref: ref[rows, :].astype(F32)
        o, new_state = chunk_fn(load(q_ref), b, load(k_ref), load(v_ref), state)
        y = o * lax.rsqrt(jnp.mean(o * o, axis=-1, keepdims=True) + EPS)
        o_ref[rows, :] = (y * gn_ref[...] * load(g_ref)).astype(o_ref.dtype)
        return new_state

    single_ref_ok = jnp.min(lf_ref[...]) * HG_CHUNK > -HG_MAX_EXPONENT

    @pl.when(single_ref_ok)
    def _():
        chunk_rows = [pl.ds(ci * HG_CHUNK, HG_CHUNK) for ci in range(n_chunks)]
        b_all = _chunk_cumsum(jnp.concatenate([lf_ref[rows, :] for rows in chunk_rows], axis=1))
        state = state_ref[...]
        for ci, rows in enumerate(chunk_rows):
            state = run_chunk(_hgrn_chunk_single_ref, rows, b_all[:, ci * HG_DIM:(ci + 1) * HG_DIM], state)
        state_ref[...] = state

    @pl.when(jnp.logical_not(single_ref_ok))
    def _():
        def body(ci, carry):
            rows = pl.ds(pl.multiple_of(ci * HG_CHUNK, HG_CHUNK), HG_CHUNK)
            state_ref[...] = run_chunk(_hgrn_chunk, rows, _chunk_cumsum(lf_ref[rows, :]), state_ref[...])
            return carry

        lax.fori_loop(0, n_chunks, body, 0)


def _hgrn(q, lf, k, v, g, g_norm, tc):
    b, s, w = q.shape
    heads = w // HG_DIM
    spec = pl.BlockSpec((None, tc, HG_DIM), lambda bi, hi, ti: (bi, ti, hi))
    return pl.pallas_call(
        functools.partial(_hgrn_kernel, n_chunks=tc // HG_CHUNK),
        grid=(b, heads, s // tc),
        in_specs=[spec] * 5 + [pl.BlockSpec((1, HG_DIM), lambda bi, hi, ti: (0, 0))],
        out_specs=spec,
        out_shape=jax.ShapeDtypeStruct((b, s, w), BF16),
        scratch_shapes=[pltpu.VMEM((HG_DIM, HG_DIM), F32)],
        compiler_params=_cparams(("parallel", "parallel", "arbitrary")),
        name="hgrn2_mixer",
    )(q, lf, k, v, g, g_norm.reshape(1, HG_DIM))


AUG_C = FOX_DIM
AUG_ONE = FOX_DIM + 3
AUG_QN = FOX_DIM + 6
NORM_MARGIN = 1.02
SKIP_LOG2 = 150.0


def _placement_constants(heads):
    pq = np.zeros((CP_WIDTH, heads * LANES), np.float32)
    pk = np.zeros((CP_WIDTH, heads * LANES), np.float32)
    pn = np.zeros((heads, heads * LANES), np.float32)
    hs = np.zeros((heads * FOX_DIM, heads), np.float32)
    for h in range(heads):
        for i in range(3):
            pq[i * heads + h, h * LANES + AUG_C + i] = 1.0
            pk[i * heads + h, h * LANES + AUG_ONE + i] = -1.0
        pq[3 * heads, h * LANES + AUG_ONE:h * LANES + AUG_ONE + 3] = 1.0
        pk[3 * heads, h * LANES + AUG_C:h * LANES + AUG_C + 3] = 1.0
        pn[h, h * LANES + AUG_QN] = 1.0
        hs[h * FOX_DIM:(h + 1) * FOX_DIM, h] = 1.0
    return tuple(jnp.asarray(a, BF16) for a in (pq, pk, pn, hs))


def _fox_pack_kernel(q_ref, k_ref, v_ref, cp_ref, c_ref, pq_ref, pk_ref, pn_ref, hs_ref,
                     qa_ref, ka_ref, va_ref, st_ref):
    q = q_ref[...]
    k = k_ref[...]
    v = v_ref[...]
    cp = cp_ref[...]
    tm = q.shape[0]
    heads = qa_ref.shape[0]
    hs = hs_ref[...]
    qn = jnp.sqrt(jnp.dot(q * q, hs, preferred_element_type=F32)) * NORM_MARGIN
    kn = jnp.sqrt(jnp.dot(k * k, hs, preferred_element_type=F32)) * NORM_MARGIN
    tail_q = (jnp.dot(cp, pq_ref[...], preferred_element_type=F32)
              + jnp.dot(qn.astype(BF16), pn_ref[...], preferred_element_type=F32)).astype(BF16)
    tail_k = jnp.dot(cp, pk_ref[...], preferred_element_type=F32).astype(BF16)
    lane = lax.broadcasted_iota(jnp.int32, (tm, LANES), 1)
    in_data = lane < FOX_DIM
    tail_v = jnp.where(lane == FOX_DIM, 1.0, 0.0).astype(BF16)
    for h in range(heads):
        pair = slice((h // 2) * LANES, (h // 2 + 1) * LANES)
        mine = slice(h * LANES, (h + 1) * LANES)
        data = (lambda x: x[:, pair]) if h % 2 == 0 else (lambda x: pltpu.roll(x[:, pair], FOX_DIM, 1))
        qa_ref[h] = jnp.where(in_data, data(q), tail_q[:, mine])
        ka_ref[h] = jnp.where(in_data, data(k), tail_k[:, mine])
        va_ref[h] = jnp.where(in_data, data(v), tail_v)
    st_ref[0:1, :] = jnp.max(kn, axis=0, keepdims=True)
    st_ref[1:2, :] = c_ref[tm - 1:tm, :]


def _fox_pack(q, k, v, cp, c, tm):
    b, s, w = q.shape
    heads = c.shape[-1]
    consts = _placement_constants(heads)
    row = lambda width: pl.BlockSpec((None, tm, width), lambda bi, ti: (bi, ti, 0))
    full = lambda a: pl.BlockSpec(a.shape, lambda bi, ti: (0, 0))
    out = pl.BlockSpec((None, heads, tm, LANES), lambda bi, ti: (bi, 0, ti, 0))
    oshape = jax.ShapeDtypeStruct((b, heads, s, LANES), BF16)
    return pl.pallas_call(
        _fox_pack_kernel,
        grid=(b, s // tm),
        in_specs=[row(w), row(w), row(w), row(CP_WIDTH), row(heads)] + [full(a) for a in consts],
        out_specs=[out] * 3 + [pl.BlockSpec((None, None, 2, heads), lambda bi, ti: (bi, ti, 0, 0))],
        out_shape=[oshape] * 3 + [jax.ShapeDtypeStruct((b, s // tm, 2, heads), F32)],
        compiler_params=_cparams(("parallel", "parallel")),
        name="fox_pack",
    )(q, k, v, cp, c, *consts)


def _fox_kernel(kmax_ref, clast_ref, q_ref, k_ref, v_ref, o_ref, m_ref, acc_ref, *, tq, tk):
    qi = pl.program_id(2)
    row_a = pl.program_id(0) * (2 * pl.num_programs(1)) + 2 * pl.program_id(1)
    m_ref[...] = jnp.full_like(m_ref, NEG_INF)
    acc_ref[...] = jnp.zeros_like(acc_ref)

    def block(row0, n_rows, kv0, masked):
        rows = pl.ds(row0, n_rows)
        cols = pl.ds(kv0, tk)
        if masked:
            causal = (lax.broadcasted_iota(jnp.int32, (n_rows, tk), 1)
                      <= lax.broadcasted_iota(jnp.int32, (n_rows, tk), 0))
        for a in range(2):
            s = lax.dot_general(q_ref[a, rows, :], k_ref[a, cols, :], (((1,), (1,)), ((), ())),
                                preferred_element_type=F32)
            if masked:
                s = jnp.where(causal, s, NEG_INF)
            m_prev = m_ref[a, rows, :]
            m_new = jnp.maximum(m_prev, jnp.max(s, axis=-1, keepdims=True))
            alpha = jnp.exp2(m_prev - m_new)
            p = jnp.exp2(s - m_new[:, 0:1])
            acc_ref[a, rows, :] = alpha * acc_ref[a, rows, :] + jnp.dot(
                p.astype(BF16), v_ref[a, cols, :], preferred_element_type=F32)
            m_ref[a, rows, :] = m_new

    n_diag = tq // tk
    n_full = qi * n_diag

    for j in range(n_diag):
        block(j * tk, tq - j * tk, pl.multiple_of((n_full + j) * tk, tk), True)

    n_tiles = kmax_ref.shape[1]
    slack = []
    for a in range(2):
        kmax = kmax_ref[row_a + a, 0]
        for j in range(1, n_tiles):
            kmax = jnp.maximum(kmax, kmax_ref[row_a + a, j])
        sel_row = lax.broadcasted_iota(jnp.int32, (LANES, LANES), 0)
        sel = jnp.where(sel_row == AUG_QN, kmax * NORM_MARGIN,
                        jnp.where((sel_row >= AUG_C) & (sel_row < AUG_ONE), 1.0, 0.0)).astype(BF16)
        bound = jnp.dot(q_ref[a], sel, preferred_element_type=F32)
        slack.append(jnp.max(bound - m_ref[a]))

    def tile_is_live(j):
        live = [slack[a] - clast_ref[row_a + a, j] > -SKIP_LOG2 for a in range(2)]
        return live[0] | live[1]

    def cond(j):
        return (j >= 0) & tile_is_live(jnp.maximum(j, 0))

    def body(j):
        block(0, tq, pl.multiple_of(j * tk, tk), False)
        return j - 1

    lax.while_loop(cond, body, n_full - 1)

    lane = lax.broadcasted_iota(jnp.int32, (tq, LANES), 1)
    acc_a = acc_ref[0]
    acc_b = acc_ref[1]
    o_a = acc_a / acc_a[:, FOX_DIM:FOX_DIM + 1]
    o_b = acc_b / acc_b[:, FOX_DIM:FOX_DIM + 1]
    o_ref[...] = jnp.where(lane < FOX_DIM, o_a, pltpu.roll(o_b, FOX_DIM, 1)).astype(o_ref.dtype)


def _fox_attention(q, k, v, kmax, clast, tq, tk):
    b, h, s, _ = q.shape
    kv_spec = pl.BlockSpec((None, 2, s, LANES), lambda bi, hi, qi: (bi, hi, 0, 0))
    smem = pl.BlockSpec(memory_space=pltpu.SMEM)
    return pl.pallas_call(
        functools.partial(_fox_kernel, tq=tq, tk=tk),
        grid=(b, h // 2, s // tq),
        in_specs=[smem, smem,
                  pl.BlockSpec((None, 2, tq, LANES), lambda bi, hi, qi: (bi, hi, qi, 0)), kv_spec, kv_spec],
        out_specs=pl.BlockSpec((None, tq, LANES), lambda bi, hi, qi: (bi, qi, hi)),
        out_shape=jax.ShapeDtypeStruct((b, s, (h // 2) * LANES), BF16),
        scratch_shapes=[pltpu.VMEM((2, tq, LANES), F32), pltpu.VMEM((2, tq, LANES), F32)],
        compiler_params=_cparams(("parallel", "parallel", "arbitrary")),
        name="fox_attention",
    )(kmax, clast, q, k, v)


def _merge_kernel(x_ref, oa_ref, ob_ref, ga_ref, gb_ref, wa_ref, wb_ref, wo_ref, gn_ref, h_ref, n_ref, mg_ref,
                  *, tn):
    oa = oa_ref[...]
    ob = ob_ref[...]
    for c in range(mg_ref.shape[1] // tn):
        cols = slice(c * tn, (c + 1) * tn)
        ya = jnp.dot(oa, wa_ref[:, cols], preferred_element_type=F32)
        yb = jnp.dot(ob, wb_ref[:, cols], preferred_element_type=F32)
        mg_ref[:, cols] = (ga_ref[:, cols] * ya + gb_ref[:, cols] * yb).astype(mg_ref.dtype)
    h = x_ref[...] + jnp.dot(mg_ref[...], wo_ref[...], preferred_element_type=F32)
    h_ref[...] = h
    y = h * lax.rsqrt(jnp.mean(h * h, axis=-1, keepdims=True) + EPS)
    n_ref[...] = (y * gn_ref[...]).astype(n_ref.dtype)


def _merge(x2d, oa, ob, ga, gb, wa, wb, wo, gain, tm, tn):
    t, d = x2d.shape
    row = lambda width: pl.BlockSpec((tm, width), lambda i: (i, 0))
    once = lambda a: pl.BlockSpec(a.shape, lambda i: (0,) * a.ndim, pipeline_mode=pl.Buffered(1))
    gain2 = gain.reshape(1, d)
    return pl.pallas_call(
        functools.partial(_merge_kernel, tn=tn),
        grid=(t // tm,),
        in_specs=[row(d), row(oa.shape[1]), row(ob.shape[1]), row(d), row(d), once(wa), once(wb), once(wo),
                  once(gain2)],
        out_specs=[row(d), row(d)],
        out_shape=[jax.ShapeDtypeStruct((t, d), F32), jax.ShapeDtypeStruct((t, d), BF16)],
        scratch_shapes=[pltpu.VMEM((tm, d), BF16)],
        compiler_params=_cparams(("parallel",)),
        name="merge_outproj",
    )(x2d, oa, ob, ga, gb, wa, wb, wo, gain2)


def _ffn_kernel(n_ref, halo_ref, h_ref, wu_ref, cw_ref, cb_ref, wd_ref, gn_ref, o_ref, next_ref, act_ref,
                *, tm, seq, tf):
    i = pl.program_id(0)
    at_start = (i * tm) % seq == 0
    next_ref[0:HALO, :] = jnp.where(at_start, jnp.zeros_like(halo_ref[...]), halo_ref[...])
    next_ref[HALO:, :] = n_ref[...]
    n_ext = next_ref[...]
    d_ff = act_ref.shape[1]

    def conv(col0):
        cols = pl.ds(col0, tf)
        u = jnp.dot(n_ext, wu_ref[:, cols], preferred_element_type=F32)
        out = cb_ref[:, cols]
        for tap in range(CONV_WIDTH):
            lo = HALO - (CONV_WIDTH - 1) + tap
            out = out + cw_ref[tap:tap + 1, cols] * u[lo:lo + tm]
        return out

    for c in range(d_ff // tf):
        gate = conv(c * tf)
        val = conv(d_ff + c * tf)
        act = 0.5 * gate * (1.0 + lax.erf(gate * (2.0 ** -0.5))) * val
        act_ref[:, c * tf:(c + 1) * tf] = act.astype(act_ref.dtype)

    h = h_ref[...] + jnp.dot(act_ref[...], wd_ref[...], preferred_element_type=F32)
    y = h * lax.rsqrt(jnp.mean(h * h, axis=-1, keepdims=True) + EPS)
    o_ref[...] = (y * gn_ref[...]).astype(o_ref.dtype)


def _ffn(n2d, h2d, w_up, conv_w, conv_b, w_down, gain, seq, tm, tf):
    t, d = n2d.shape
    d_ff = w_down.shape[0]
    halo_blocks = tm // HALO
    cb = conv_b.reshape(1, 2 * d_ff)
    once = lambda a: pl.BlockSpec(a.shape, lambda i: (0,) * a.ndim, pipeline_mode=pl.Buffered(1))
    return pl.pallas_call(
        functools.partial(_ffn_kernel, tm=tm, seq=seq, tf=tf),
        grid=(t // tm,),
        in_specs=[pl.BlockSpec((tm, d), lambda i: (i, 0)),
                  pl.BlockSpec((HALO, d), lambda i: (jnp.maximum(i * halo_blocks - 1, 0), 0)),
                  pl.BlockSpec((tm, d), lambda i: (i, 0)),
                  once(w_up), once(conv_w), once(cb), once(w_down),
                  pl.BlockSpec((1, d), lambda i: (0, 0))],
        out_specs=pl.BlockSpec((tm, d), lambda i: (i, 0)),
        out_shape=jax.ShapeDtypeStruct((t, d), F32),
        scratch_shapes=[pltpu.VMEM((HALO + tm, d), BF16), pltpu.VMEM((tm, d_ff), BF16)],
        compiler_params=_cparams(("parallel",)),
        name="conv_glu_ffn",
    )(n2d, n2d, h2d, w_up, conv_w, cb, w_down, gain.reshape(1, d))


def _tile(n, pref):
    return pref if n % pref == 0 else n


def kernel(x, norm_mix, w_in, fox_f_bias, hg_lb_logits, hg_norm, w_branch_a, w_branch_b, w_out, norm_ffn, w_up,
           conv_w, conv_b, w_down, norm_final):
    b, s, d = x.shape
    assert norm_mix.shape[0] == 1, "single-layer trunk"
    t = b * s
    hg_w = HG_HEADS * HG_DIM
    fox_w = FOX_HEADS * FOX_DIM
    cuts = [hg_w, 2 * hg_w, 3 * hg_w, 4 * hg_w, 4 * hg_w + fox_w, 4 * hg_w + 2 * fox_w, 4 * hg_w + 3 * fox_w,
            4 * hg_w + 3 * fox_w + FOX_HEADS, 4 * hg_w + 3 * fox_w + FOX_HEADS + d]
    assert w_in.shape[-1] == cuts[-1] + d
    assert hg_w == d and fox_w == d, "the projection kernel walks equal-width segments"
    w_bf = w_in[0].astype(BF16)
    w_hq, w_hf, w_hi, w_hg, w_fq, w_fk, w_fv, w_ff, w_ga, w_gb = jnp.split(w_bf, cuts, axis=-1)
    w_all = jnp.concatenate([w_hq, w_hf, w_hi, w_hg, w_fq, w_fk, w_fv, w_ga, w_gb, w_ff], axis=-1)

    x2d = x.reshape(t, d)
    hq, hlf, hk, hv, hgate, fq, fk, fv, ga, gb, c2, c_pieces = _in_proj(
        x2d, norm_mix[0], w_all, hg_lb_logits, fox_f_bias[0], s, _tile(s, 512), 256)

    r3 = lambda a: a.reshape(b, s, a.shape[-1])
    o_a = _hgrn(r3(hq), r3(hlf), r3(hk), r3(hv), r3(hgate), hg_norm[0], _tile(s, 1024))
    tk = _tile(s, 512)
    q_aug, k_aug, v_aug, kv_stats = _fox_pack(r3(fq), r3(fk), r3(fv), r3(c_pieces), r3(c2), tk)
    kv_stats = kv_stats.transpose(2, 0, 3, 1).reshape(2, b * FOX_HEADS, s // tk)
    o_b = _fox_attention(q_aug, k_aug, v_aug, kv_stats[0], kv_stats[1], _tile(s, 4 * tk), tk)

    h1, n_ffn = _merge(x2d, o_a.reshape(t, hg_w), o_b.reshape(t, fox_w), ga, gb, w_branch_a[0].astype(BF16),
                       w_branch_b[0].astype(BF16), w_out[0].astype(BF16), norm_ffn[0], _tile(t, 512), 256)
    out = _ffn(n_ffn, h1, w_up[0].astype(BF16), conv_w[0], conv_b[0], w_down[0].astype(BF16), norm_final,
               s, _tile(s, 512), 256)
    return out.reshape(b, s, d)
```

```python
import functools
import math

import jax
import jax.numpy as jnp
import numpy as np
from jax import lax
from jax.experimental import pallas as pl
from jax.experimental.pallas import tpu as pltpu

F32 = jnp.float32
BF16 = jnp.bfloat16

EPS = 1e-6
HG_HEADS = 8
HG_DIM = 128
HG_CHUNK = 64
HG_SUB = 16
HG_MAX_EXPONENT = 64.0
FOX_HEADS = 16
FOX_DIM = 64
CONV_WIDTH = 3
HALO = 8
NEG_INF = float("-inf")
LOG2E = math.log2(math.e)
LANES = 128

VMEM_LIMIT = 56 * 1024 * 1024


def _cparams(sem):
    return pltpu.CompilerParams(dimension_semantics=sem, vmem_limit_bytes=VMEM_LIMIT)


def _sigmoid(x):
    return 1.0 / (1.0 + jnp.exp(-x))


def _silu(x):
    return x * _sigmoid(x)


def _log_sigmoid(x):
    return jnp.minimum(x, 0.0) - jnp.log1p(jnp.exp(-jnp.abs(x)))


def _split3(x):
    x1 = x.astype(BF16)
    r1 = x - x1.astype(F32)
    x2 = r1.astype(BF16)
    x3 = (r1 - x2.astype(F32)).astype(BF16)
    return x1, x2, x3


SEG_HQ, SEG_HF, SEG_HI, SEG_HG, SEG_FQ, SEG_FK, SEG_FV, SEG_GA, SEG_GB = range(9)
N_SEG = 9


def _in_proj_kernel(x_ref, gain_ref, w_ref, lbl_ref, fbias_ref, spread_ref, one_ref, hq_ref, hlf_ref, hk_ref,
                    hv_ref, hg_ref, fq_ref, fk_ref, fv_ref, ga_ref, gb_ref, c_ref, cp_ref, n_ref, carry_ref,
                    *, tn, seq):
    x = x_ref[...]
    n_ref[...] = (x * lax.rsqrt(jnp.mean(x * x, axis=-1, keepdims=True) + EPS) * gain_ref[...]).astype(BF16)
    n = n_ref[...]
    width = hq_ref.shape[1]
    lbl = lbl_ref[...]
    e = jnp.exp(lbl - jnp.max(lbl, axis=0, keepdims=True))
    lb = e[0:1] / jnp.sum(e, axis=0, keepdims=True)

    def chunks(seg):
        for c in range(width // tn):
            cols = slice(c * tn, (c + 1) * tn)
            yield cols, jnp.dot(n, w_ref[:, seg * width + c * tn:seg * width + (c + 1) * tn],
                                preferred_element_type=F32)

    def emit(seg, ref, fn):
        for cols, acc in chunks(seg):
            ref[:, cols] = fn(acc).astype(ref.dtype)

    emit(SEG_HQ, hq_ref, _silu)
    for cols, acc in chunks(SEG_HF):
        f = lb[:, cols] + (1.0 - lb[:, cols]) * _sigmoid(acc)
        hlf_ref[:, cols] = jnp.log(f)
        hk_ref[:, cols] = (1.0 - f).astype(hk_ref.dtype)
    emit(SEG_HI, hv_ref, lambda a: a)
    emit(SEG_HG, hg_ref, _silu)
    emit(SEG_FQ, fq_ref, lambda a: a * (FOX_DIM ** -0.5 * LOG2E))
    emit(SEG_FK, fk_ref, lambda a: a)
    emit(SEG_FV, fv_ref, lambda a: a)
    emit(SEG_GA, ga_ref, _sigmoid)
    emit(SEG_GB, gb_ref, _sigmoid)
    @pl.when((pl.program_id(0) * x.shape[0]) % seq == 0)
    def _():
        carry_ref[...] = jnp.zeros_like(carry_ref)

    lf = _log_sigmoid(jnp.dot(n, w_ref[:, N_SEG * width:], preferred_element_type=F32) + fbias_ref[...])
    tm = lf.shape[0]
    lower = (lax.broadcasted_iota(jnp.int32, (tm, tm), 1)
             <= lax.broadcasted_iota(jnp.int32, (tm, tm), 0)).astype(BF16)
    cs = carry_ref[...]
    for piece in _split3(lf):
        cs = cs + jnp.dot(lower, piece, preferred_element_type=F32)
    carry_ref[...] = cs[tm - 1:tm, :]
    c2 = cs * LOG2E
    c_ref[...] = c2
    cp = one_ref[...]
    for i, piece in enumerate(_split3(c2)):
        cp = cp + jnp.dot(piece, spread_ref[i], preferred_element_type=F32)
    cp_ref[...] = cp.astype(cp_ref.dtype)


CP_WIDTH = 4 * FOX_HEADS


def _in_proj(x2d, gain, w_all, lb_logits, f_bias, seq, tm, tn):
    t, d = x2d.shape
    heads = f_bias.shape[0]
    spread = np.zeros((3, heads, CP_WIDTH), np.float32)
    for i in range(3):
        spread[i, np.arange(heads), i * heads + np.arange(heads)] = 1.0
    one = np.zeros((1, CP_WIDTH), np.float32)
    one[0, 3 * heads] = 1.0
    once = lambda a: pl.BlockSpec(a.shape, lambda i: (0,) * a.ndim, pipeline_mode=pl.Buffered(1))
    row = lambda width: pl.BlockSpec((tm, width), lambda i: (i, 0))
    seg_dtypes = [BF16, F32, BF16, BF16, BF16, BF16, BF16, BF16, BF16, BF16]
    consts = (gain.reshape(1, d), w_all, lb_logits, f_bias.reshape(1, heads), jnp.asarray(spread, BF16),
              jnp.asarray(one))
    return pl.pallas_call(
        functools.partial(_in_proj_kernel, tn=tn, seq=seq),
        grid=(t // tm,),
        in_specs=[row(d)] + [once(a) for a in consts],
        out_specs=[row(d)] * len(seg_dtypes) + [row(heads), row(CP_WIDTH)],
        out_shape=([jax.ShapeDtypeStruct((t, d), dt) for dt in seg_dtypes]
                   + [jax.ShapeDtypeStruct((t, heads), F32), jax.ShapeDtypeStruct((t, CP_WIDTH), BF16)]),
        scratch_shapes=[pltpu.VMEM((tm, d), BF16), pltpu.VMEM((1, heads), F32)],
        compiler_params=_cparams(("arbitrary",)),
        name="in_proj",
    )(x2d, *consts)


def _causal_chunk():
    r_i = lax.broadcasted_iota(jnp.int32, (HG_CHUNK, HG_CHUNK), 0)
    c_i = lax.broadcasted_iota(jnp.int32, (HG_CHUNK, HG_CHUNK), 1)
    return c_i <= r_i


def _chunk_cumsum(lf):
    lower = _causal_chunk().astype(BF16)
    b = jnp.zeros(lf.shape, F32)
    for piece in _split3(lf):
        b = b + jnp.dot(lower, piece, preferred_element_type=F32)
    return b


def _hgrn_chunk_single_ref(q, b, k, v, state):
    c = HG_CHUNK
    q_t = (q * jnp.exp(b)).astype(BF16)
    k_up = k * jnp.exp(-b)
    sc = lax.dot_general(q_t, k_up.astype(BF16), (((1,), (1,)), ((), ())), preferred_element_type=F32)
    sc = jnp.where(_causal_chunk(), sc, 0.0).astype(BF16)
    v_bf = v.astype(BF16)
    o = jnp.dot(jnp.concatenate([q_t, sc], axis=1), jnp.concatenate([state.astype(BF16), v_bf], axis=0),
                preferred_element_type=F32)
    b_end = b[c - 1:c]
    k_end = k_up * jnp.exp(b_end)
    tr = jnp.concatenate([k_end, jnp.broadcast_to(b_end, (HG_DIM - c, HG_DIM))], axis=0).T
    upd = jnp.dot(tr[:, :c].astype(BF16), v_bf, preferred_element_type=F32)
    return o, state * jnp.exp(tr[:, c:c + 1]) + upd


def _hgrn_chunk(q, b, k, v, state):
    c = HG_CHUNK
    n_sub = c // HG_SUB
    state_t = state.T

    qe = (q * jnp.exp(b)).astype(BF16)
    o = lax.dot_general(qe, state_t.astype(BF16), (((1,), (1,)), ((), ())), preferred_element_type=F32)

    o_parts = []
    t_iota = lax.broadcasted_iota(jnp.int32, (HG_SUB, HG_DIM), 0)
    for i in range(n_sub):
        lo = i * HG_SUB
        q_i = q[lo:lo + HG_SUB]
        b_i = b[lo:lo + HG_SUB]
        o_i = o[lo:lo + HG_SUB]
        if i > 0:
            ref = b[lo - 1:lo]
            q_t = (q_i * jnp.exp(b_i - ref)).astype(BF16)
            k_t = (k[:lo] * jnp.exp(ref - b[:lo])).astype(BF16)
            sc = lax.dot_general(q_t, k_t, (((1,), (1,)), ((), ())), preferred_element_type=F32)
            o_i = o_i + jnp.dot(sc.astype(BF16), v[:lo].astype(BF16), preferred_element_type=F32)
        for s in range(HG_SUB):
            row = lo + s
            rel = jnp.where(t_iota >= s, b_i - b[row:row + 1], NEG_INF)
            p = q_i * jnp.exp(rel) * k[row:row + 1]
            o_i = o_i + jnp.sum(p, axis=-1, keepdims=True) * v[row:row + 1]
        o_parts.append(o_i)
    o = jnp.concatenate(o_parts, axis=0)

    b_end = b[c - 1:c]
    k_dec = (k * jnp.exp(b_end - b)).astype(BF16)
    upd = lax.dot_general(v.astype(BF16), k_dec, (((0,), (0,)), ((), ())), preferred_element_type=F32)
    return o, (state_t * jnp.exp(b_end) + upd).T


def _hgrn_kernel(q_ref, lf_ref, k_ref, v_ref, g_ref, gn_ref, o_ref, state_ref, *, n_chunks):
    @pl.when(pl.program_id(2) == 0)
    def _():
        state_ref[...] = jnp.zeros_like(state_ref)

    def run_chunk(chunk_fn, rows, b, state):
        load = lambda ref: ref[rows, :].astype(F32)
        o, new_state = chunk_fn(load(q_ref), b, load(k_ref), load(v_ref), state)
        y = o * lax.rsqrt(jnp.mean(o * o, axis=-1, keepdims=True) + EPS)
        o_ref[rows, :] = (y * gn_ref[...] * load(g_ref)).astype(o_ref.dtype)
        return new_state

    single_ref_ok = jnp.min(lf_ref[...]) * HG_CHUNK > -HG_MAX_EXPONENT

    @pl.when(single_ref_ok)
    def _():
        chunk_rows = [pl.ds(ci * HG_CHUNK, HG_CHUNK) for ci in range(n_chunks)]
        b_all = _chunk_cumsum(jnp.concatenate([lf_ref[rows, :] for rows in chunk_rows], axis=1))
        state = state_ref[...]
        for ci, rows in enumerate(chunk_rows):
            state = run_chunk(_hgrn_chunk_single_ref, rows, b_all[:, ci * HG_DIM:(ci + 1) * HG_DIM], state)
        state_ref[...] = state

    @pl.when(jnp.logical_not(single_ref_ok))
    def _():
        def body(ci, carry):
            rows = pl.ds(pl.multiple_of(ci * HG_CHUNK, HG_CHUNK), HG_CHUNK)
            state_ref[...] = run_chunk(_hgrn_chunk, rows, _chunk_cumsum(lf_ref[rows, :]), state_ref[...])
            return carry

        lax.fori_loop(0, n_chunks, body, 0)


def _hgrn(q, lf, k, v, g, g_norm, tc):
    b, s, w = q.shape
    heads = w // HG_DIM
    spec = pl.BlockSpec((None, tc, HG_DIM), lambda bi, hi, ti: (bi, ti, hi))
    return pl.pallas_call(
        functools.partial(_hgrn_kernel, n_chunks=tc // HG_CHUNK),
        grid=(b, heads, s // tc),
        in_specs=[spec] * 5 + [pl.BlockSpec((1, HG_DIM), lambda bi, hi, ti: (0, 0))],
        out_specs=spec,
        out_shape=jax.ShapeDtypeStruct((b, s, w), BF16),
        scratch_shapes=[pltpu.VMEM((HG_DIM, HG_DIM), F32)],
        compiler_params=_cparams(("parallel", "parallel", "arbitrary")),
        name="hgrn2_mixer",
    )(q, lf, k, v, g, g_norm.reshape(1, HG_DIM))


AUG_C = FOX_DIM
AUG_ONE = FOX_DIM + 3
AUG_QN = FOX_DIM + 6
NORM_MARGIN = 1.02
SKIP_LOG2 = 150.0


def _placement_constants(heads):
    pq = np.zeros((CP_WIDTH, heads * LANES), np.float32)
    pk = np.zeros((CP_WIDTH, heads * LANES), np.float32)
    pn = np.zeros((heads, heads * LANES), np.float32)
    hs = np.zeros((heads * FOX_DIM, heads), np.float32)
    for h in range(heads):
        for i in range(3):
            pq[i * heads + h, h * LANES + AUG_C + i] = 1.0
            pk[i * heads + h, h * LANES + AUG_ONE + i] = -1.0
        pq[3 * heads, h * LANES + AUG_ONE:h * LANES + AUG_ONE + 3] = 1.0
        pk[3 * heads, h * LANES + AUG_C:h * LANES + AUG_C + 3] = 1.0
        pn[h, h * LANES + AUG_QN] = 1.0
        hs[h * FOX_DIM:(h + 1) * FOX_DIM, h] = 1.0
    return tuple(jnp.asarray(a, BF16) for a in (pq, pk, pn, hs))


def _fox_pack_kernel(q_ref, k_ref, v_ref, cp_ref, c_ref, pq_ref, pk_ref, pn_ref, hs_ref,
                     qa_ref, ka_ref, va_ref, st_ref):
    q = q_ref[...]
    k = k_ref[...]
    v = v_ref[...]
    cp = cp_ref[...]
    tm = q.shape[0]
    heads = qa_ref.shape[0]
    hs = hs_ref[...]
    qn = jnp.sqrt(jnp.dot(q * q, hs, preferred_element_type=F32)) * NORM_MARGIN
    kn = jnp.sqrt(jnp.dot(k * k, hs, preferred_element_type=F32)) * NORM_MARGIN
    tail_q = (jnp.dot(cp, pq_ref[...], preferred_element_type=F32)
              + jnp.dot(qn.astype(BF16), pn_ref[...], preferred_element_type=F32)).astype(BF16)
    tail_k = jnp.dot(cp, pk_ref[...], preferred_element_type=F32).astype(BF16)
    lane = lax.broadcasted_iota(jnp.int32, (tm, LANES), 1)
    in_data = lane < FOX_DIM
    tail_v = jnp.where(lane == FOX_DIM, 1.0, 0.0).astype(BF16)
    for h in range(heads):
        pair = slice((h // 2) * LANES, (h // 2 + 1) * LANES)
        mine = slice(h * LANES, (h + 1) * LANES)
        data = (lambda x: x[:, pair]) if h % 2 == 0 else (lambda x: pltpu.roll(x[:, pair], FOX_DIM, 1))
        qa_ref[h] = jnp.where(in_data, data(q), tail_q[:, mine])
        ka_ref[h] = jnp.where(in_data, data(k), tail_k[:, mine])
        va_ref[h] = jnp.where(in_data, data(v), tail_v)
    st_ref[0:1, :] = jnp.max(kn, axis=0, keepdims=True)
    st_ref[1:2, :] = c_ref[tm - 1:tm, :]


def _fox_pack(q, k, v, cp, c, tm):
    b, s, w = q.shape
    heads = c.shape[-1]
    consts = _placement_constants(heads)
    row = lambda width: pl.BlockSpec((None, tm, width), lambda bi, ti: (bi, ti, 0))
    full = lambda a: pl.BlockSpec(a.shape, lambda bi, ti: (0, 0))
    out = pl.BlockSpec((None, heads, tm, LANES), lambda bi, ti: (bi, 0, ti, 0))
    oshape = jax.ShapeDtypeStruct((b, heads, s, LANES), BF16)
    return pl.pallas_call(
        _fox_pack_kernel,
        grid=(b, s // tm),
        in_specs=[row(w), row(w), row(w), row(CP_WIDTH), row(heads)] + [full(a) for a in consts],
        out_specs=[out] * 3 + [pl.BlockSpec((None, None, 2, heads), lambda bi, ti: (bi, ti, 0, 0))],
        out_shape=[oshape] * 3 + [jax.ShapeDtypeStruct((b, s // tm, 2, heads), F32)],
        compiler_params=_cparams(("parallel", "parallel")),
        name="fox_pack",
    )(q, k, v, cp, c, *consts)


def _fox_kernel(kmax_ref, clast_ref, q_ref, k_ref, v_ref, o_ref, m_ref, acc_ref, *, tq, tk):
    qi = pl.program_id(2)
    row_a = pl.program_id(0) * (2 * pl.num_programs(1)) + 2 * pl.program_id(1)
    m_ref[...] = jnp.full_like(m_ref, NEG_INF)
    acc_ref[...] = jnp.zeros_like(acc_ref)

    def block(row0, n_rows, kv0, masked):
        rows = pl.ds(row0, n_rows)
        cols = pl.ds(kv0, tk)
        if masked:
            causal = (lax.broadcasted_iota(jnp.int32, (n_rows, tk), 1)
                      <= lax.broadcasted_iota(jnp.int32, (n_rows, tk), 0))
        for a in range(2):
            s = lax.dot_general(q_ref[a, rows, :], k_ref[a, cols, :], (((1,), (1,)), ((), ())),
                                preferred_element_type=F32)
            if masked:
                s = jnp.where(causal, s, NEG_INF)
            m_prev = m_ref[a, rows, :]
            m_new = jnp.maximum(m_prev, jnp.max(s, axis=-1, keepdims=True))
            alpha = jnp.exp2(m_prev - m_new)
            p = jnp.exp2(s - m_new[:, 0:1])
            acc_ref[a, rows, :] = alpha * acc_ref[a, rows, :] + jnp.dot(
                p.astype(BF16), v_ref[a, cols, :], preferred_element_type=F32)
            m_ref[a, rows, :] = m_new

    n_diag = tq // tk
    n_full = qi * n_diag

    for j in range(n_diag):
        block(j * tk, tq - j * tk, pl.multiple_of((n_full + j) * tk, tk), True)

    n_tiles = kmax_ref.shape[1]
    slack = []
    for a in range(2):
        kmax = kmax_ref[row_a + a, 0]
        for j in range(1, n_tiles):
            kmax = jnp.maximum(kmax, kmax_ref[row_a + a, j])
        sel_row = lax.broadcasted_iota(jnp.int32, (LANES, LANES), 0)
        sel = jnp.where(sel_row == AUG_QN, kmax * NORM_MARGIN,
                        jnp.where((sel_row >= AUG_C) & (sel_row < AUG_ONE), 1.0, 0.0)).astype(BF16)
        gap = jnp.dot(q_ref[a], sel, preferred_element_type=F32) - m_ref[a]
        slack.append([jnp.max(gap[g * tk:(g + 1) * tk]) for g in range(n_diag)])

    def live_groups(j):
        n = jnp.int32(0)
        for a in range(2):
            c_last = clast_ref[row_a + a, jnp.maximum(j, 0)]
            for g in range(n_diag):
                n = jnp.where(slack[a][g] - c_last > -SKIP_LOG2, jnp.maximum(n, g + 1), n)
        return jnp.where(j >= 0, n, 0)

    def body(carry):
        j, n_live = carry
        kv0 = pl.multiple_of(j * tk, tk)
        for n in range(1, n_diag + 1):
            @pl.when(n_live == n)
            def _():
                block(0, n * tk, kv0, False)
        return j - 1, live_groups(j - 1)

    lax.while_loop(lambda carry: carry[1] > 0, body, (n_full - 1, live_groups(n_full - 1)))

    lane = lax.broadcasted_iota(jnp.int32, (tq, LANES), 1)
    acc_a = acc_ref[0]
    acc_b = acc_ref[1]
    o_a = acc_a / acc_a[:, FOX_DIM:FOX_DIM + 1]
    o_b = acc_b / acc_b[:, FOX_DIM:FOX_DIM + 1]
    o_ref[...] = jnp.where(lane < FOX_DIM, o_a, pltpu.roll(o_b, FOX_DIM, 1)).astype(o_ref.dtype)


def _fox_attention(q, k, v, kmax, clast, tq, tk):
    b, h, s, _ = q.shape
    kv_spec = pl.BlockSpec((None, 2, s, LANES), lambda bi, hi, qi: (bi, hi, 0, 0))
    smem = pl.BlockSpec(memory_space=pltpu.SMEM)
    return pl.pallas_call(
        functools.partial(_fox_kernel, tq=tq, tk=tk),
        grid=(b, h // 2, s // tq),
        in_specs=[smem, smem,
                  pl.BlockSpec((None, 2, tq, LANES), lambda bi, hi, qi: (bi, hi, qi, 0)), kv_spec, kv_spec],
        out_specs=pl.BlockSpec((None, tq, LANES), lambda bi, hi, qi: (bi, qi, hi)),
        out_shape=jax.ShapeDtypeStruct((b, s, (h // 2) * LANES), BF16),
        scratch_shapes=[pltpu.VMEM((2, tq, LANES), F32), pltpu.VMEM((2, tq, LANES), F32)],
        compiler_params=_cparams(("parallel", "parallel", "arbitrary")),
        name="fox_attention",
    )(kmax, clast, q, k, v)


def _merge_kernel(x_ref, oa_ref, ob_ref, ga_ref, gb_ref, wa_ref, wb_ref, wo_ref, gn_ref, h_ref, n_ref, mg_ref,
                  *, tn):
    oa = oa_ref[...]
    ob = ob_ref[...]
    for c in range(mg_ref.shape[1] // tn):
        cols = slice(c * tn, (c + 1) * tn)
        ya = jnp.dot(oa, wa_ref[:, cols], preferred_element_type=F32)
        yb = jnp.dot(ob, wb_ref[:, cols], preferred_element_type=F32)
        mg_ref[:, cols] = (ga_ref[:, cols] * ya + gb_ref[:, cols] * yb).astype(mg_ref.dtype)
    h = x_ref[...] + jnp.dot(mg_ref[...], wo_ref[...], preferred_element_type=F32)
    h_ref[...] = h
    y = h * lax.rsqrt(jnp.mean(h * h, axis=-1, keepdims=True) + EPS)
    n_ref[...] = (y * gn_ref[...]).astype(n_ref.dtype)


def _merge(x2d, oa, ob, ga, gb, wa, wb, wo, gain, tm, tn):
    t, d = x2d.shape
    row = lambda width: pl.BlockSpec((tm, width), lambda i: (i, 0))
    once = lambda a: pl.BlockSpec(a.shape, lambda i: (0,) * a.ndim, pipeline_mode=pl.Buffered(1))
    gain2 = gain.reshape(1, d)
    return pl.pallas_call(
        functools.partial(_merge_kernel, tn=tn),
        grid=(t // tm,),
        in_specs=[row(d), row(oa.shape[1]), row(ob.shape[1]), row(d), row(d), once(wa), once(wb), once(wo),
                  once(gain2)],
        out_specs=[row(d), row(d)],
        out_shape=[jax.ShapeDtypeStruct((t, d), F32), jax.ShapeDtypeStruct((t, d), BF16)],
        scratch_shapes=[pltpu.VMEM((tm, d), BF16)],
        compiler_params=_cparams(("parallel",)),
        name="merge_outproj",
    )(x2d, oa, ob, ga, gb, wa, wb, wo, gain2)


def _ffn_kernel(n_ref, halo_ref, h_ref, wu_ref, cw_ref, cb_ref, wd_ref, gn_ref, o_ref, next_ref, act_ref,
                *, tm, seq, tf):
    i = pl.program_id(0)
    at_start = (i * tm) % seq == 0
    next_ref[0:HALO, :] = jnp.where(at_start, jnp.zeros_like(halo_ref[...]), halo_ref[...])
    next_ref[HALO:, :] = n_ref[...]
    n_ext = next_ref[...]
    d_ff = act_ref.shape[1]

    def conv(col0):
        cols = pl.ds(col0, tf)
        u = jnp.dot(n_ext, wu_ref[:, cols], preferred_element_type=F32)
        out = cb_ref[:, cols]
        for tap in range(CONV_WIDTH):
            lo = HALO - (CONV_WIDTH - 1) + tap
            out = out + cw_ref[tap:tap + 1, cols] * u[lo:lo + tm]
        return out

    for c in range(d_ff // tf):
        gate = conv(c * tf)
        val = conv(d_ff + c * tf)
        act = 0.5 * gate * (1.0 + lax.erf(gate * (2.0 ** -0.5))) * val
        act_ref[:, c * tf:(c + 1) * tf] = act.astype(act_ref.dtype)

    h = h_ref[...] + jnp.dot(act_ref[...], wd_ref[...], preferred_element_type=F32)
    y = h * lax.rsqrt(jnp.mean(h * h, axis=-1, keepdims=True) + EPS)
    o_ref[...] = (y * gn_ref[...]).astype(o_ref.dtype)


def _ffn(n2d, h2d, w_up, conv_w, conv_b, w_down, gain, seq, tm, tf):
    t, d = n2d.shape
    d_ff = w_down.shape[0]
    halo_blocks = tm // HALO
    cb = conv_b.reshape(1, 2 * d_ff)
    once = lambda a: pl.BlockSpec(a.shape, lambda i: (0,) * a.ndim, pipeline_mode=pl.Buffered(1))
    return pl.pallas_call(
        functools.partial(_ffn_kernel, tm=tm, seq=seq, tf=tf),
        grid=(t // tm,),
        in_specs=[pl.BlockSpec((tm, d), lambda i: (i, 0)),
                  pl.BlockSpec((HALO, d), lambda i: (jnp.maximum(i * halo_blocks - 1, 0), 0)),
                  pl.BlockSpec((tm, d), lambda i: (i, 0)),
                  once(w_up), once(conv_w), once(cb), once(w_down),
                  pl.BlockSpec((1, d), lambda i: (0, 0))],
        out_specs=pl.BlockSpec((tm, d), lambda i: (i, 0)),
        out_shape=jax.ShapeDtypeStruct((t, d), F32),
        scratch_shapes=[pltpu.VMEM((HALO + tm, d), BF16), pltpu.VMEM((tm, d_ff), BF16)],
        compiler_params=_cparams(("parallel",)),
        name="conv_glu_ffn",
    )(n2d, n2d, h2d, w_up, conv_w, cb, w_down, gain.reshape(1, d))


def _tile(n, pref):
    return pref if n % pref == 0 else n


def kernel(x, norm_mix, w_in, fox_f_bias, hg_lb_logits, hg_norm, w_branch_a, w_branch_b, w_out, norm_ffn, w_up,
           conv_w, conv_b, w_down, norm_final):
    b, s, d = x.shape
    assert norm_mix.shape[0] == 1, "single-layer trunk"
    t = b * s
    hg_w = HG_HEADS * HG_DIM
    fox_w = FOX_HEADS * FOX_DIM
    cuts = [hg_w, 2 * hg_w, 3 * hg_w, 4 * hg_w, 4 * hg_w + fox_w, 4 * hg_w + 2 * fox_w, 4 * hg_w + 3 * fox_w,
            4 * hg_w + 3 * fox_w + FOX_HEADS, 4 * hg_w + 3 * fox_w + FOX_HEADS + d]
    assert w_in.shape[-1] == cuts[-1] + d
    assert hg_w == d and fox_w == d, "the projection kernel walks equal-width segments"
    w_bf = w_in[0].astype(BF16)
    w_hq, w_hf, w_hi, w_hg, w_fq, w_fk, w_fv, w_ff, w_ga, w_gb = jnp.split(w_bf, cuts, axis=-1)
    w_all = jnp.concatenate([w_hq, w_hf, w_hi, w_hg, w_fq, w_fk, w_fv, w_ga, w_gb, w_ff], axis=-1)

    x2d = x.reshape(t, d)
    hq, hlf, hk, hv, hgate, fq, fk, fv, ga, gb, c2, c_pieces = _in_proj(
        x2d, norm_mix[0], w_all, hg_lb_logits, fox_f_bias[0], s, _tile(s, 512), 256)

    r3 = lambda a: a.reshape(b, s, a.shape[-1])
    o_a = _hgrn(r3(hq), r3(hlf), r3(hk), r3(hv), r3(hgate), hg_norm[0], _tile(s, 1024))
    tk = _tile(s, 512)
    q_aug, k_aug, v_aug, kv_stats = _fox_pack(r3(fq), r3(fk), r3(fv), r3(c_pieces), r3(c2), tk)
    kv_stats = kv_stats.transpose(2, 0, 3, 1).reshape(2, b * FOX_HEADS, s // tk)
    o_b = _fox_attention(q_aug, k_aug, v_aug, kv_stats[0], kv_stats[1], _tile(s, 4 * tk), tk)

    h1, n_ffn = _merge(x2d, o_a.reshape(t, hg_w), o_b.reshape(t, fox_w), ga, gb, w_branch_a[0].astype(BF16),
                       w_branch_b[0].astype(BF16), w_out[0].astype(BF16), norm_ffn[0], _tile(t, 512), 256)
    out = _ffn(n_ffn, h1, w_up[0].astype(BF16), conv_w[0], conv_b[0], w_down[0].astype(BF16), norm_final,
               s, _tile(s, 512), 256)
    return out.reshape(b, s, d)
```

```python
import functools
import math

import jax
import jax.numpy as jnp
import numpy as np
from jax import lax
from jax.experimental import pallas as pl
from jax.experimental.pallas import tpu as pltpu

F32 = jnp.float32
BF16 = jnp.bfloat16

EPS = 1e-6
HG_HEADS = 8
HG_DIM = 128
HG_CHUNK = 64
HG_SUB = 16
HG_MAX_EXPONENT = 64.0
FOX_HEADS = 16
FOX_DIM = 64
CONV_WIDTH = 3
HALO = 8
NEG_INF = float("-inf")
LOG2E = math.log2(math.e)
LANES = 128

VMEM_LIMIT = 56 * 1024 * 1024


def _cparams(sem):
    return pltpu.CompilerParams(dimension_semantics=sem, vmem_limit_bytes=VMEM_LIMIT)


def _sigmoid(x):
    return 1.0 / (1.0 + jnp.exp(-x))


def _silu(x):
    return x * _sigmoid(x)


def _log_sigmoid(x):
    return jnp.minimum(x, 0.0) - jnp.log1p(jnp.exp(-jnp.abs(x)))


def _split3(x):
    x1 = x.astype(BF16)
    r1 = x - x1.astype(F32)
    x2 = r1.astype(BF16)
    x3 = (r1 - x2.astype(F32)).astype(BF16)
    return x1, x2, x3


SEG_HQ, SEG_HF, SEG_HI, SEG_HG, SEG_FQ, SEG_FK, SEG_FV, SEG_GA, SEG_GB = range(9)
N_SEG = 9


def _in_proj_kernel(x_ref, gain_ref, w_ref, lbl_ref, fbias_ref, spread_ref, one_ref, hq_ref, hlf_ref, hk_ref,
                    hv_ref, hg_ref, fq_ref, fk_ref, fv_ref, ga_ref, gb_ref, c_ref, cp_ref, n_ref, carry_ref,
                    *, tn, seq):
    x = x_ref[...]
    n_ref[...] = (x * lax.rsqrt(jnp.mean(x * x, axis=-1, keepdims=True) + EPS) * gain_ref[...]).astype(BF16)
    n = n_ref[...]
    width = hq_ref.shape[1]
    lbl = lbl_ref[...]
    e = jnp.exp(lbl - jnp.max(lbl, axis=0, keepdims=True))
    lb = e[0:1] / jnp.sum(e, axis=0, keepdims=True)

    def chunks(seg):
        for c in range(width // tn):
            cols = slice(c * tn, (c + 1) * tn)
            yield cols, jnp.dot(n, w_ref[:, seg * width + c * tn:seg * width + (c + 1) * tn],
                                preferred_element_type=F32)

    def emit(seg, ref, fn):
        for cols, acc in chunks(seg):
            ref[:, cols] = fn(acc).astype(ref.dtype)

    emit(SEG_HQ, hq_ref, _silu)
    for cols, acc in chunks(SEG_HF):
        f = lb[:, cols] + (1.0 - lb[:, cols]) * _sigmoid(acc)
        hlf_ref[:, cols] = jnp.log(f)
        hk_ref[:, cols] = (1.0 - f).astype(hk_ref.dtype)
    emit(SEG_HI, hv_ref, lambda a: a)
    emit(SEG_HG, hg_ref, _silu)
    emit(SEG_FQ, fq_ref, lambda a: a * (FOX_DIM ** -0.5 * LOG2E))
    emit(SEG_FK, fk_ref, lambda a: a)
    emit(SEG_FV, fv_ref, lambda a: a)
    emit(SEG_GA, ga_ref, _sigmoid)
    emit(SEG_GB, gb_ref, _sigmoid)
    @pl.when((pl.program_id(0) * x.shape[0]) % seq == 0)
    def _():
        carry_ref[...] = jnp.zeros_like(carry_ref)

    lf = _log_sigmoid(jnp.dot(n, w_ref[:, N_SEG * width:], preferred_element_type=F32) + fbias_ref[...])
    tm = lf.shape[0]
    lower = (lax.broadcasted_iota(jnp.int32, (tm, tm), 1)
             <= lax.broadcasted_iota(jnp.int32, (tm, tm), 0)).astype(BF16)
    cs = carry_ref[...]
    for piece in _split3(lf):
        cs = cs + jnp.dot(lower, piece, preferred_element_type=F32)
    carry_ref[...] = cs[tm - 1:tm, :]
    c2 = cs * LOG2E
    c_ref[...] = c2
    cp = one_ref[...]
    for i, piece in enumerate(_split3(c2)):
        cp = cp + jnp.dot(piece, spread_ref[i], preferred_element_type=F32)
    cp_ref[...] = cp.astype(cp_ref.dtype)


CP_WIDTH = 4 * FOX_HEADS


def _in_proj(x2d, gain, w_all, lb_logits, f_bias, seq, tm, tn):
    t, d = x2d.shape
    heads = f_bias.shape[0]
    spread = np.zeros((3, heads, CP_WIDTH), np.float32)
    for i in range(3):
        spread[i, np.arange(heads), i * heads + np.arange(heads)] = 1.0
    one = np.zeros((1, CP_WIDTH), np.float32)
    one[0, 3 * heads] = 1.0
    once = lambda a: pl.BlockSpec(a.shape, lambda i: (0,) * a.ndim, pipeline_mode=pl.Buffered(1))
    row = lambda width: pl.BlockSpec((tm, width), lambda i: (i, 0))
    seg_dtypes = [BF16, F32, BF16, BF16, BF16, BF16, BF16, BF16, BF16, BF16]
    consts = (gain.reshape(1, d), w_all, lb_logits, f_bias.reshape(1, heads), jnp.asarray(spread, BF16),
              jnp.asarray(one))
    return pl.pallas_call(
        functools.partial(_in_proj_kernel, tn=tn, seq=seq),
        grid=(t // tm,),
        in_specs=[row(d)] + [once(a) for a in consts],
        out_specs=[row(d)] * len(seg_dtypes) + [row(heads), row(CP_WIDTH)],
        out_shape=([jax.ShapeDtypeStruct((t, d), dt) for dt in seg_dtypes]
                   + [jax.ShapeDtypeStruct((t, heads), F32), jax.ShapeDtypeStruct((t, CP_WIDTH), BF16)]),
        scratch_shapes=[pltpu.VMEM((tm, d), BF16), pltpu.VMEM((1, heads), F32)],
        compiler_params=_cparams(("arbitrary",)),
        name="in_proj",
    )(x2d, *consts)


def _causal_chunk():
    r_i = lax.broadcasted_iota(jnp.int32, (HG_CHUNK, HG_CHUNK), 0)
    c_i = lax.broadcasted_iota(jnp.int32, (HG_CHUNK, HG_CHUNK), 1)
    return c_i <= r_i


def _chunk_cumsum(lf):
    lower = _causal_chunk().astype(BF16)
    b = jnp.zeros(lf.shape, F32)
    for piece in _split3(lf):
        b = b + jnp.dot(lower, piece, preferred_element_type=F32)
    return b


def _hgrn_chunk_single_ref(q, b, k, v, state):
    c = HG_CHUNK
    q_t = (q * jnp.exp(b)).astype(BF16)
    k_up = k * jnp.exp(-b)
    sc = lax.dot_general(q_t, k_up.astype(BF16), (((1,), (1,)), ((), ())), preferred_element_type=F32)
    sc = jnp.where(_causal_chunk(), sc, 0.0).astype(BF16)
    v_bf = v.astype(BF16)
    o = jnp.dot(jnp.concatenate([q_t, sc], axis=1), jnp.concatenate([state.astype(BF16), v_bf], axis=0),
                preferred_element_type=F32)
    b_end = b[c - 1:c]
    k_end = k_up * jnp.exp(b_end)
    tr = jnp.concatenate([k_end, jnp.broadcast_to(b_end, (HG_DIM - c, HG_DIM))], axis=0).T
    upd = jnp.dot(tr[:, :c].astype(BF16), v_bf, preferred_element_type=F32)
    return o, state * jnp.exp(tr[:, c:c + 1]) + upd


def _hgrn_chunk(q, b, k, v, state):
    c = HG_CHUNK
    n_sub = c // HG_SUB
    state_t = state.T

    qe = (q * jnp.exp(b)).astype(BF16)
    o = lax.dot_general(qe, state_t.astype(BF16), (((1,), (1,)), ((), ())), preferred_element_type=F32)

    o_parts = []
    t_iota = lax.broadcasted_iota(jnp.int32, (HG_SUB, HG_DIM), 0)
    for i in range(n_sub):
        lo = i * HG_SUB
        q_i = q[lo:lo + HG_SUB]
        b_i = b[lo:lo + HG_SUB]
        o_i = o[lo:lo + HG_SUB]
        if i > 0:
            ref = b[lo - 1:lo]
            q_t = (q_i * jnp.exp(b_i - ref)).astype(BF16)
            k_t = (k[:lo] * jnp.exp(ref - b[:lo])).astype(BF16)
            sc = lax.dot_general(q_t, k_t, (((1,), (1,)), ((), ())), preferred_element_type=F32)
            o_i = o_i + jnp.dot(sc.astype(BF16), v[:lo].astype(BF16), preferred_element_type=F32)
        for s in range(HG_SUB):
            row = lo + s
            rel = jnp.where(t_iota >= s, b_i - b[row:row + 1], NEG_INF)
            p = q_i * jnp.exp(rel) * k[row:row + 1]
            o_i = o_i + jnp.sum(p, axis=-1, keepdims=True) * v[row:row + 1]
        o_parts.append(o_i)
    o = jnp.concatenate(o_parts, axis=0)

    b_end = b[c - 1:c]
    k_dec = (k * jnp.exp(b_end - b)).astype(BF16)
    upd = lax.dot_general(v.astype(BF16), k_dec, (((0,), (0,)), ((), ())), preferred_element_type=F32)
    return o, (state_t * jnp.exp(b_end) + upd).T


def _hgrn_kernel(q_ref, lf_ref, k_ref, v_ref, g_ref, gn_ref, o_ref, state_ref, *, n_chunks):
    @pl.when(pl.program_id(2) == 0)
    def _():
        state_ref[...] = jnp.zeros_like(state_ref)

    def run_chunk(chunk_fn, rows, b, state):
        load = lambda ref: ref[rows, :].astype(F32)
        o, new_state = chunk_fn(load(q_ref), b, load(k_ref), load(v_ref), state)
        y = o * lax.rsqrt(jnp.mean(o * o, axis=-1, keepdims=True) + EPS)
        o_ref[rows, :] = (y * gn_ref[...] * load(g_ref)).astype(o_ref.dtype)
        return new_state

    single_ref_ok = jnp.min(lf_ref[...]) * HG_CHUNK > -HG_MAX_EXPONENT

    @pl.when(single_ref_ok)
    def _():
        chunk_rows = [pl.ds(ci * HG_CHUNK, HG_CHUNK) for ci in range(n_chunks)]
        b_all = _chunk_cumsum(jnp.concatenate([lf_ref[rows, :] for rows in chunk_rows], axis=1))
        state = state_ref[...]
        for ci, rows in enumerate(chunk_rows):
            state = run_chunk(_hgrn_chunk_single_ref, rows, b_all[:, ci * HG_DIM:(ci + 1) * HG_DIM], state)
        state_ref[...] = state

    @pl.when(jnp.logical_not(single_ref_ok))
    def _():
        def body(ci, carry):
            rows = pl.ds(pl.multiple_of(ci * HG_CHUNK, HG_CHUNK), HG_CHUNK)
            state_ref[...] = run_chunk(_hgrn_chunk, rows, _chunk_cumsum(lf_ref[rows, :]), state_ref[...])
            return carry

        lax.fori_loop(0, n_chunks, body, 0)


def _hgrn(q, lf, k, v, g, g_norm, tc):
    b, s, w = q.shape
    heads = w // HG_DIM
    spec = pl.BlockSpec((None, tc, HG_DIM), lambda bi, hi, ti: (bi, ti, hi))
    return pl.pallas_call(
        functools.partial(_hgrn_kernel, n_chunks=tc // HG_CHUNK),
        grid=(b, heads, s // tc),
        in_specs=[spec] * 5 + [pl.BlockSpec((1, HG_DIM), lambda bi, hi, ti: (0, 0))],
        out_specs=spec,
        out_shape=jax.ShapeDtypeStruct((b, s, w), BF16),
        scratch_shapes=[pltpu.VMEM((HG_DIM, HG_DIM), F32)],
        compiler_params=_cparams(("parallel", "parallel", "arbitrary")),
        name="hgrn2_mixer",
    )(q, lf, k, v, g, g_norm.reshape(1, HG_DIM))


AUG_C = FOX_DIM
AUG_ONE = FOX_DIM + 3
AUG_QN = FOX_DIM + 6
NORM_MARGIN = 1.02
SKIP_LOG2 = 150.0


def _placement_constants(heads):
    pq = np.zeros((CP_WIDTH, heads * LANES), np.float32)
    pk = np.zeros((CP_WIDTH, heads * LANES), np.float32)
    pn = np.zeros((heads, heads * LANES), np.float32)
    hs = np.zeros((heads * FOX_DIM, heads), np.float32)
    for h in range(heads):
        for i in range(3):
            pq[i * heads + h, h * LANES + AUG_C + i] = 1.0
            pk[i * heads + h, h * LANES + AUG_ONE + i] = -1.0
        pq[3 * heads, h * LANES + AUG_ONE:h * LANES + AUG_ONE + 3] = 1.0
        pk[3 * heads, h * LANES + AUG_C:h * LANES + AUG_C + 3] = 1.0
        pn[h, h * LANES + AUG_QN] = 1.0
        hs[h * FOX_DIM:(h + 1) * FOX_DIM, h] = 1.0
    return tuple(jnp.asarray(a, BF16) for a in (pq, pk, pn, hs))


def _fox_pack_kernel(q_ref, k_ref, v_ref, cp_ref, c_ref, pq_ref, pk_ref, pn_ref, hs_ref,
                     qa_ref, ka_ref, va_ref, st_ref):
    q = q_ref[...]
    k = k_ref[...]
    v = v_ref[...]
    cp = cp_ref[...]
    tm = q.shape[0]
    heads = qa_ref.shape[0]
    hs = hs_ref[...]
    qn = jnp.sqrt(jnp.dot(q * q, hs, preferred_element_type=F32)) * NORM_MARGIN
    kn = jnp.sqrt(jnp.dot(k * k, hs, preferred_element_type=F32)) * NORM_MARGIN
    tail_q = (jnp.dot(cp, pq_ref[...], preferred_element_type=F32)
              + jnp.dot(qn.astype(BF16), pn_ref[...], preferred_element_type=F32)).astype(BF16)
    tail_k = jnp.dot(cp, pk_ref[...], preferred_element_type=F32).astype(BF16)
    lane = lax.broadcasted_iota(jnp.int32, (tm, LANES), 1)
    in_data = lane < FOX_DIM
    tail_v = jnp.where(lane == FOX_DIM, 1.0, 0.0).astype(BF16)
    for h in range(heads):
        pair = slice((h // 2) * LANES, (h // 2 + 1) * LANES)
        mine = slice(h * LANES, (h + 1) * LANES)
        data = (lambda x: x[:, pair]) if h % 2 == 0 else (lambda x: pltpu.roll(x[:, pair], FOX_DIM, 1))
        qa_ref[h] = jnp.where(in_data, data(q), tail_q[:, mine])
        ka_ref[h] = jnp.where(in_data, data(k), tail_k[:, mine])
        va_ref[h] = jnp.where(in_data, data(v), tail_v)
    st_ref[0:1, :] = jnp.max(kn, axis=0, keepdims=True)
    st_ref[1:2, :] = c_ref[tm - 1:tm, :]


def _fox_pack(q, k, v, cp, c, tm):
    b, s, w = q.shape
    heads = c.shape[-1]
    consts = _placement_constants(heads)
    row = lambda width: pl.BlockSpec((None, tm, width), lambda bi, ti: (bi, ti, 0))
    full = lambda a: pl.BlockSpec(a.shape, lambda bi, ti: (0, 0))
    out = pl.BlockSpec((None, heads, tm, LANES), lambda bi, ti: (bi, 0, ti, 0))
    oshape = jax.ShapeDtypeStruct((b, heads, s, LANES), BF16)
    return pl.pallas_call(
        _fox_pack_kernel,
        grid=(b, s // tm),
        in_specs=[row(w), row(w), row(w), row(CP_WIDTH), row(heads)] + [full(a) for a in consts],
        out_specs=[out] * 3 + [pl.BlockSpec((None, None, 2, heads), lambda bi, ti: (bi, ti, 0, 0))],
        out_shape=[oshape] * 3 + [jax.ShapeDtypeStruct((b, s // tm, 2, heads), F32)],
        compiler_params=_cparams(("parallel", "parallel")),
        name="fox_pack",
    )(q, k, v, cp, c, *consts)


def _fox_kernel(kmax_ref, clast_ref, q_ref, k_ref, v_ref, o_ref, m_ref, acc_ref, *, tq, tk):
    qi = pl.program_id(2)
    row_a = pl.program_id(0) * (2 * pl.num_programs(1)) + 2 * pl.program_id(1)
    m_ref[...] = jnp.full_like(m_ref, NEG_INF)
    acc_ref[...] = jnp.zeros_like(acc_ref)

    def block(row0, n_rows, kv0, masked):
        rows = pl.ds(row0, n_rows)
        cols = pl.ds(kv0, tk)
        if masked:
            causal = (lax.broadcasted_iota(jnp.int32, (n_rows, tk), 1)
                      <= lax.broadcasted_iota(jnp.int32, (n_rows, tk), 0))
        for a in range(2):
            s = lax.dot_general(q_ref[a, rows, :], k_ref[a, cols, :], (((1,), (1,)), ((), ())),
                                preferred_element_type=F32)
            if masked:
                s = jnp.where(causal, s, NEG_INF)
            m_prev = m_ref[a, rows, :]
            m_new = jnp.maximum(m_prev, jnp.max(s, axis=-1, keepdims=True))
            alpha = jnp.exp2(m_prev - m_new)
            p = jnp.exp2(s - jnp.tile(m_new, (1, tk // LANES)))
            acc_ref[a, rows, :] = alpha * acc_ref[a, rows, :] + jnp.dot(
                p.astype(BF16), v_ref[a, cols, :], preferred_element_type=F32)
            m_ref[a, rows, :] = m_new

    n_diag = tq // tk
    n_full = qi * n_diag

    for j in range(n_diag):
        block(j * tk, tq - j * tk, pl.multiple_of((n_full + j) * tk, tk), True)

    n_tiles = kmax_ref.shape[1]
    slack = []
    for a in range(2):
        kmax = kmax_ref[row_a + a, 0]
        for j in range(1, n_tiles):
            kmax = jnp.maximum(kmax, kmax_ref[row_a + a, j])
        sel_row = lax.broadcasted_iota(jnp.int32, (LANES, LANES), 0)
        sel = jnp.where(sel_row == AUG_QN, kmax * NORM_MARGIN,
                        jnp.where((sel_row >= AUG_C) & (sel_row < AUG_ONE), 1.0, 0.0)).astype(BF16)
        gap = jnp.dot(q_ref[a], sel, preferred_element_type=F32) - m_ref[a]
        slack.append([jnp.max(gap[g * tk:(g + 1) * tk]) for g in range(n_diag)])

    def live_groups(j):
        n = jnp.int32(0)
        for a in range(2):
            c_last = clast_ref[row_a + a, jnp.maximum(j, 0)]
            for g in range(n_diag):
                n = jnp.where(slack[a][g] - c_last > -SKIP_LOG2, jnp.maximum(n, g + 1), n)
        return jnp.where(j >= 0, n, 0)

    def body(carry):
        j, n_live = carry
        kv0 = pl.multiple_of(j * tk, tk)
        for n in range(1, n_diag + 1):
            @pl.when(n_live == n)
            def _():
                block(0, n * tk, kv0, False)
        return j - 1, live_groups(j - 1)

    lax.while_loop(lambda carry: carry[1] > 0, body, (n_full - 1, live_groups(n_full - 1)))

    lane = lax.broadcasted_iota(jnp.int32, (tq, LANES), 1)
    acc_a = acc_ref[0]
    acc_b = acc_ref[1]
    o_a = acc_a / acc_a[:, FOX_DIM:FOX_DIM + 1]
    o_b = acc_b / acc_b[:, FOX_DIM:FOX_DIM + 1]
    o_ref[...] = jnp.where(lane < FOX_DIM, o_a, pltpu.roll(o_b, FOX_DIM, 1)).astype(o_ref.dtype)


def _fox_attention(q, k, v, kmax, clast, tq, tk):
    b, h, s, _ = q.shape
    kv_spec = pl.BlockSpec((None, 2, s, LANES), lambda bi, hi, qi: (bi, hi, 0, 0))
    smem = pl.BlockSpec(memory_space=pltpu.SMEM)
    return pl.pallas_call(
        functools.partial(_fox_kernel, tq=tq, tk=tk),
        grid=(b, h // 2, s // tq),
        in_specs=[smem, smem,
                  pl.BlockSpec((None, 2, tq, LANES), lambda bi, hi, qi: (bi, hi, qi, 0)), kv_spec, kv_spec],
        out_specs=pl.BlockSpec((None, tq, LANES), lambda bi, hi, qi: (bi, qi, hi)),
        out_shape=jax.ShapeDtypeStruct((b, s, (h // 2) * LANES), BF16),
        scratch_shapes=[pltpu.VMEM((2, tq, LANES), F32), pltpu.VMEM((2, tq, LANES), F32)],
        compiler_params=_cparams(("parallel", "parallel", "arbitrary")),
        name="fox_attention",
    )(kmax, clast, q, k, v)


def _merge_kernel(x_ref, oa_ref, ob_ref, ga_ref, gb_ref, wa_ref, wb_ref, wo_ref, gn_ref, h_ref, n_ref, mg_ref,
                  *, tn):
    oa = oa_ref[...]
    ob = ob_ref[...]
    for c in range(mg_ref.shape[1] // tn):
        cols = slice(c * tn, (c + 1) * tn)
        ya = jnp.dot(oa, wa_ref[:, cols], preferred_element_type=F32)
        yb = jnp.dot(ob, wb_ref[:, cols], preferred_element_type=F32)
        mg_ref[:, cols] = (ga_ref[:, cols] * ya + gb_ref[:, cols] * yb).astype(mg_ref.dtype)
    h = x_ref[...] + jnp.dot(mg_ref[...], wo_ref[...], preferred_element_type=F32)
    h_ref[...] = h
    y = h * lax.rsqrt(jnp.mean(h * h, axis=-1, keepdims=True) + EPS)
    n_ref[...] = (y * gn_ref[...]).astype(n_ref.dtype)


def _merge(x2d, oa, ob, ga, gb, wa, wb, wo, gain, tm, tn):
    t, d = x2d.shape
    row = lambda width: pl.BlockSpec((tm, width), lambda i: (i, 0))
    once = lambda a: pl.BlockSpec(a.shape, lambda i: (0,) * a.ndim, pipeline_mode=pl.Buffered(1))
    gain2 = gain.reshape(1, d)
    return pl.pallas_call(
        functools.partial(_merge_kernel, tn=tn),
        grid=(t // tm,),
        in_specs=[row(d), row(oa.shape[1]), row(ob.shape[1]), row(d), row(d), once(wa), once(wb), once(wo),
                  once(gain2)],
        out_specs=[row(d), row(d)],
        out_shape=[jax.ShapeDtypeStruct((t, d), F32), jax.ShapeDtypeStruct((t, d), BF16)],
        scratch_shapes=[pltpu.VMEM((tm, d), BF16)],
        compiler_params=_cparams(("parallel",)),
        name="merge_outproj",
    )(x2d, oa, ob, ga, gb, wa, wb, wo, gain2)


def _ffn_kernel(n_ref, halo_ref, h_ref, wu_ref, cw_ref, cb_ref, wd_ref, gn_ref, o_ref, next_ref, act_ref,
                *, tm, seq, tf):
    i = pl.program_id(0)
    at_start = (i * tm) % seq == 0
    next_ref[0:HALO, :] = jnp.where(at_start, jnp.zeros_like(halo_ref[...]), halo_ref[...])
    next_ref[HALO:, :] = n_ref[...]
    n_ext = next_ref[...]
    d_ff = act_ref.shape[1]

    def conv(col0):
        cols = pl.ds(col0, tf)
        u = jnp.dot(n_ext, wu_ref[:, cols], preferred_element_type=F32)
        out = cb_ref[:, cols]
        for tap in range(CONV_WIDTH):
            lo = HALO - (CONV_WIDTH - 1) + tap
            out = out + cw_ref[tap:tap + 1, cols] * u[lo:lo + tm]
        return out

    for c in range(d_ff // tf):
        gate = conv(c * tf)
        val = conv(d_ff + c * tf)
        act = 0.5 * gate * (1.0 + lax.erf(gate * (2.0 ** -0.5))) * val
        act_ref[:, c * tf:(c + 1) * tf] = act.astype(act_ref.dtype)

    h = h_ref[...] + jnp.dot(act_ref[...], wd_ref[...], preferred_element_type=F32)
    y = h * lax.rsqrt(jnp.mean(h * h, axis=-1, keepdims=True) + EPS)
    o_ref[...] = (y * gn_ref[...]).astype(o_ref.dtype)


def _ffn(n2d, h2d, w_up, conv_w, conv_b, w_down, gain, seq, tm, tf):
    t, d = n2d.shape
    d_ff = w_down.shape[0]
    halo_blocks = tm // HALO
    cb = conv_b.reshape(1, 2 * d_ff)
    once = lambda a: pl.BlockSpec(a.shape, lambda i: (0,) * a.ndim, pipeline_mode=pl.Buffered(1))
    return pl.pallas_call(
        functools.partial(_ffn_kernel, tm=tm, seq=seq, tf=tf),
        grid=(t // tm,),
        in_specs=[pl.BlockSpec((tm, d), lambda i: (i, 0)),
                  pl.BlockSpec((HALO, d), lambda i: (jnp.maximum(i * halo_blocks - 1, 0), 0)),
                  pl.BlockSpec((tm, d), lambda i: (i, 0)),
                  once(w_up), once(conv_w), once(cb), once(w_down),
                  pl.BlockSpec((1, d), lambda i: (0, 0))],
        out_specs=pl.BlockSpec((tm, d), lambda i: (i, 0)),
        out_shape=jax.ShapeDtypeStruct((t, d), F32),
        scratch_shapes=[pltpu.VMEM((HALO + tm, d), BF16), pltpu.VMEM((tm, d_ff), BF16)],
        compiler_params=_cparams(("parallel",)),
        name="conv_glu_ffn",
    )(n2d, n2d, h2d, w_up, conv_w, cb, w_down, gain.reshape(1, d))


def _tile(n, pref):
    return pref if n % pref == 0 else n


def kernel(x, norm_mix, w_in, fox_f_bias, hg_lb_logits, hg_norm, w_branch_a, w_branch_b, w_out, norm_ffn, w_up,
           conv_w, conv_b, w_down, norm_final):
    b, s, d = x.shape
    assert norm_mix.shape[0] == 1, "single-layer trunk"
    t = b * s
    hg_w = HG_HEADS * HG_DIM
    fox_w = FOX_HEADS * FOX_DIM
    cuts = [hg_w, 2 * hg_w, 3 * hg_w, 4 * hg_w, 4 * hg_w + fox_w, 4 * hg_w + 2 * fox_w, 4 * hg_w + 3 * fox_w,
            4 * hg_w + 3 * fox_w + FOX_HEADS, 4 * hg_w + 3 * fox_w + FOX_HEADS + d]
    assert w_in.shape[-1] == cuts[-1] + d
    assert hg_w == d and fox_w == d, "the projection kernel walks equal-width segments"
    w_bf = w_in[0].astype(BF16)
    w_hq, w_hf, w_hi, w_hg, w_fq, w_fk, w_fv, w_ff, w_ga, w_gb = jnp.split(w_bf, cuts, axis=-1)
    w_all = jnp.concatenate([w_hq, w_hf, w_hi, w_hg, w_fq, w_fk, w_fv, w_ga, w_gb, w_ff], axis=-1)

    x2d = x.reshape(t, d)
    hq, hlf, hk, hv, hgate, fq, fk, fv, ga, gb, c2, c_pieces = _in_proj(
        x2d, norm_mix[0], w_all, hg_lb_logits, fox_f_bias[0], s, _tile(s, 512), 256)

    r3 = lambda a: a.reshape(b, s, a.shape[-1])
    o_a = _hgrn(r3(hq), r3(hlf), r3(hk), r3(hv), r3(hgate), hg_norm[0], _tile(s, 1024))
    tk = _tile(s, 512)
    q_aug, k_aug, v_aug, kv_stats = _fox_pack(r3(fq), r3(fk), r3(fv), r3(c_pieces), r3(c2), tk)
    kv_stats = kv_stats.transpose(2, 0, 3, 1).reshape(2, b * FOX_HEADS, s // tk)
    o_b = _fox_attention(q_aug, k_aug, v_aug, kv_stats[0], kv_stats[1], _tile(s, 4 * tk), tk)

    h1, n_ffn = _merge(x2d, o_a.reshape(t, hg_w), o_b.reshape(t, fox_w), ga, gb, w_branch_a[0].astype(BF16),
                       w_branch_b[0].astype(BF16), w_out[0].astype(BF16), norm_ffn[0], _tile(t, 512), 256)
    out = _ffn(n_ffn, h1, w_up[0].astype(BF16), conv_w[0], conv_b[0], w_down[0].astype(BF16), norm_final,
               s, _tile(s, 512), 256)
    return out.reshape(b, s, d)
```

```python
import functools
import math

import jax
import jax.numpy as jnp
import numpy as np
from jax import lax
from jax.experimental import pallas as pl
from jax.experimental.pallas import tpu as pltpu

F32 = jnp.float32
BF16 = jnp.bfloat16

EPS = 1e-6
HG_HEADS = 8
HG_DIM = 128
HG_CHUNK = 64
HG_SUB = 16
HG_MAX_EXPONENT = 64.0
FOX_HEADS = 16
FOX_DIM = 64
CONV_WIDTH = 3
HALO = 8
NEG_INF = float("-inf")
LOG2E = math.log2(math.e)
LANES = 128
SUBLANES = 8

VMEM_LIMIT = 56 * 1024 * 1024


def _cparams(sem):
    return pltpu.CompilerParams(dimension_semantics=sem, vmem_limit_bytes=VMEM_LIMIT)


def _sigmoid(x):
    return 1.0 / (1.0 + jnp.exp(-x))


def _silu(x):
    return x * _sigmoid(x)


def _log_sigmoid(x):
    return jnp.minimum(x, 0.0) - jnp.log1p(jnp.exp(-jnp.abs(x)))


def _split3(x):
    x1 = x.astype(BF16)
    r1 = x - x1.astype(F32)
    x2 = r1.astype(BF16)
    x3 = (r1 - x2.astype(F32)).astype(BF16)
    return x1, x2, x3


SEG_HQ, SEG_HF, SEG_HI, SEG_HG, SEG_FQ, SEG_FK, SEG_FV, SEG_GA, SEG_GB = range(9)
N_SEG = 9


def _in_proj_kernel(x_ref, gain_ref, w_ref, lbl_ref, fbias_ref, spread_ref, one_ref, hq_ref, hlf_ref, hk_ref,
                    hv_ref, hg_ref, fq_ref, fk_ref, fv_ref, ga_ref, gb_ref, c_ref, cp_ref, lfmin_ref, n_ref, carry_ref,
                    *, tn, seq):
    x = x_ref[...]
    n_ref[...] = (x * lax.rsqrt(jnp.mean(x * x, axis=-1, keepdims=True) + EPS) * gain_ref[...]).astype(BF16)
    n = n_ref[...]
    width = hq_ref.shape[1]
    lbl = lbl_ref[...]
    e = jnp.exp(lbl - jnp.max(lbl, axis=0, keepdims=True))
    lb = e[0:1] / jnp.sum(e, axis=0, keepdims=True)

    def chunks(seg):
        for c in range(width // tn):
            cols = slice(c * tn, (c + 1) * tn)
            yield cols, jnp.dot(n, w_ref[:, seg * width + c * tn:seg * width + (c + 1) * tn],
                                preferred_element_type=F32)

    def emit(seg, ref, fn):
        for cols, acc in chunks(seg):
            ref[:, cols] = fn(acc).astype(ref.dtype)

    @pl.when((pl.program_id(0) * x.shape[0]) % seq == 0)
    def _():
        carry_ref[...] = jnp.zeros_like(carry_ref)

    lf = _log_sigmoid(jnp.dot(n, w_ref[:, N_SEG * width:], preferred_element_type=F32) + fbias_ref[...])
    tm = lf.shape[0]
    lower = (lax.broadcasted_iota(jnp.int32, (tm, tm), 1)
             <= lax.broadcasted_iota(jnp.int32, (tm, tm), 0)).astype(BF16)
    cs = carry_ref[...]
    for piece in _split3(lf):
        cs = cs + jnp.dot(lower, piece, preferred_element_type=F32)
    carry_ref[...] = cs[tm - 1:tm, :]
    c2 = cs * LOG2E
    c_ref[...] = c2
    cp = one_ref[...]
    for i, piece in enumerate(_split3(c2)):
        cp = cp + jnp.dot(piece, spread_ref[i], preferred_element_type=F32)
    cp_ref[...] = cp.astype(cp_ref.dtype)

    emit(SEG_HQ, hq_ref, _silu)
    for cols, acc in chunks(SEG_HF):
        f = lb[:, cols] + (1.0 - lb[:, cols]) * _sigmoid(acc)
        lf_h = jnp.log(f)
        hlf_ref[:, cols] = lf_h
        hk_ref[:, cols] = (1.0 - f).astype(hk_ref.dtype)
        lfmin_ref[:, cols] = jnp.broadcast_to(jnp.min(lf_h, axis=0, keepdims=True), (lfmin_ref.shape[0], tn))
    emit(SEG_HI, hv_ref, lambda a: a)
    emit(SEG_HG, hg_ref, _silu)
    emit(SEG_FQ, fq_ref, lambda a: a * (FOX_DIM ** -0.5 * LOG2E))
    emit(SEG_FK, fk_ref, lambda a: a)
    emit(SEG_FV, fv_ref, lambda a: a)
    emit(SEG_GA, ga_ref, _sigmoid)
    emit(SEG_GB, gb_ref, _sigmoid)


CP_WIDTH = 4 * FOX_HEADS


def _in_proj(x2d, gain, w_all, lb_logits, f_bias, seq, tm, tn):
    t, d = x2d.shape
    heads = f_bias.shape[0]
    spread = np.zeros((3, heads, CP_WIDTH), np.float32)
    for i in range(3):
        spread[i, np.arange(heads), i * heads + np.arange(heads)] = 1.0
    one = np.zeros((1, CP_WIDTH), np.float32)
    one[0, 3 * heads] = 1.0
    once = lambda a: pl.BlockSpec(a.shape, lambda i: (0,) * a.ndim, pipeline_mode=pl.Buffered(1))
    row = lambda width: pl.BlockSpec((tm, width), lambda i: (i, 0))
    seg_dtypes = [BF16, F32, BF16, BF16, BF16, BF16, BF16, BF16, BF16, BF16]
    consts = (gain.reshape(1, d), w_all, lb_logits, f_bias.reshape(1, heads), jnp.asarray(spread, BF16),
              jnp.asarray(one))
    return pl.pallas_call(
        functools.partial(_in_proj_kernel, tn=tn, seq=seq),
        grid=(t // tm,),
        in_specs=[row(d)] + [once(a) for a in consts],
        out_specs=([row(d)] * len(seg_dtypes) + [row(heads), row(CP_WIDTH)]
                   + [pl.BlockSpec((None, SUBLANES, d), lambda i: (i, 0, 0))]),
        out_shape=([jax.ShapeDtypeStruct((t, d), dt) for dt in seg_dtypes]
                   + [jax.ShapeDtypeStruct((t, heads), F32), jax.ShapeDtypeStruct((t, CP_WIDTH), BF16),
                      jax.ShapeDtypeStruct((t // tm, SUBLANES, d), F32)]),
        scratch_shapes=[pltpu.VMEM((tm, d), BF16), pltpu.VMEM((1, heads), F32)],
        compiler_params=_cparams(("arbitrary",)),
        name="in_proj",
    )(x2d, *consts)


def _causal_chunk():
    r_i = lax.broadcasted_iota(jnp.int32, (HG_CHUNK, HG_CHUNK), 0)
    c_i = lax.broadcasted_iota(jnp.int32, (HG_CHUNK, HG_CHUNK), 1)
    return c_i <= r_i


def _chunk_cumsum(lf):
    lower = _causal_chunk().astype(BF16)
    b = jnp.zeros(lf.shape, F32)
    for piece in _split3(lf):
        b = b + jnp.dot(lower, piece, preferred_element_type=F32)
    return b


def _hgrn_chunk_single_ref(q, b, k, v, state):
    c = HG_CHUNK
    q_t = (q * jnp.exp(b)).astype(BF16)
    k_up = k * jnp.exp(-b)
    sc = lax.dot_general(q_t, k_up.astype(BF16), (((1,), (1,)), ((), ())), preferred_element_type=F32)
    sc = jnp.where(_causal_chunk(), sc, 0.0).astype(BF16)
    v_bf = v.astype(BF16)
    o = jnp.dot(jnp.concatenate([q_t, sc], axis=1), jnp.concatenate([state.astype(BF16), v_bf], axis=0),
                preferred_element_type=F32)
    b_end = b[c - 1:c]
    k_end = k_up * jnp.exp(b_end)
    tr = jnp.concatenate([k_end, jnp.broadcast_to(b_end, (HG_DIM - c, HG_DIM))], axis=0).T
    upd = jnp.dot(tr[:, :c].astype(BF16), v_bf, preferred_element_type=F32)
    return o, state * jnp.exp(tr[:, c:c + 1]) + upd


def _hgrn_chunk(q, b, k, v, state):
    c = HG_CHUNK
    n_sub = c // HG_SUB
    state_t = state.T

    qe = (q * jnp.exp(b)).astype(BF16)
    o = lax.dot_general(qe, state_t.astype(BF16), (((1,), (1,)), ((), ())), preferred_element_type=F32)

    o_parts = []
    t_iota = lax.broadcasted_iota(jnp.int32, (HG_SUB, HG_DIM), 0)
    for i in range(n_sub):
        lo = i * HG_SUB
        q_i = q[lo:lo + HG_SUB]
        b_i = b[lo:lo + HG_SUB]
        o_i = o[lo:lo + HG_SUB]
        if i > 0:
            ref = b[lo - 1:lo]
            q_t = (q_i * jnp.exp(b_i - ref)).astype(BF16)
            k_t = (k[:lo] * jnp.exp(ref - b[:lo])).astype(BF16)
            sc = lax.dot_general(q_t, k_t, (((1,), (1,)), ((), ())), preferred_element_type=F32)
            o_i = o_i + jnp.dot(sc.astype(BF16), v[:lo].astype(BF16), preferred_element_type=F32)
        for s in range(HG_SUB):
            row = lo + s
            rel = jnp.where(t_iota >= s, b_i - b[row:row + 1], NEG_INF)
            p = q_i * jnp.exp(rel) * k[row:row + 1]
            o_i = o_i + jnp.sum(p, axis=-1, keepdims=True) * v[row:row + 1]
        o_parts.append(o_i)
    o = jnp.concatenate(o_parts, axis=0)

    b_end = b[c - 1:c]
    k_dec = (k * jnp.exp(b_end - b)).astype(BF16)
    upd = lax.dot_general(v.astype(BF16), k_dec, (((0,), (0,)), ((), ())), preferred_element_type=F32)
    return o, (state_t * jnp.exp(b_end) + upd).T


def _hgrn_kernel(lfmin_ref, q_ref, lf_ref, k_ref, v_ref, g_ref, gn_ref, o_ref, state_ref, *, n_chunks):
    @pl.when(pl.program_id(2) == 0)
    def _():
        state_ref[...] = jnp.zeros_like(state_ref)

    def run_chunk(chunk_fn, rows, b, state):
        load = lambda ref: ref[rows, :].astype(F32)
        o, new_state = chunk_fn(load(q_ref), b, load(k_ref), load(v_ref), state)
        y = o * lax.rsqrt(jnp.mean(o * o, axis=-1, keepdims=True) + EPS)
        o_ref[rows, :] = (y * gn_ref[...] * load(g_ref)).astype(o_ref.dtype)
        return new_state

    lf_min = lfmin_ref[pl.program_id(0) * pl.num_programs(1) + pl.program_id(1), pl.program_id(2)]
    single_ref_ok = lf_min * HG_CHUNK > -HG_MAX_EXPONENT

    @pl.when(single_ref_ok)
    def _():
        chunk_rows = [pl.ds(ci * HG_CHUNK, HG_CHUNK) for ci in range(n_chunks)]
        b_all = _chunk_cumsum(jnp.concatenate([lf_ref[rows, :] for rows in chunk_rows], axis=1))
        state = state_ref[...]
        for ci, rows in enumerate(chunk_rows):
            state = run_chunk(_hgrn_chunk_single_ref, rows, b_all[:, ci * HG_DIM:(ci + 1) * HG_DIM], state)
        state_ref[...] = state

    @pl.when(jnp.logical_not(single_ref_ok))
    def _():
        def body(ci, carry):
            rows = pl.ds(pl.multiple_of(ci * HG_CHUNK, HG_CHUNK), HG_CHUNK)
            state_ref[...] = run_chunk(_hgrn_chunk, rows, _chunk_cumsum(lf_ref[rows, :]), state_ref[...])
            return carry

        lax.fori_loop(0, n_chunks, body, 0)


def _hgrn(lf_min, q, lf, k, v, g, g_norm, tc):
    b, s, w = q.shape
    heads = w // HG_DIM
    spec = pl.BlockSpec((None, tc, HG_DIM), lambda bi, hi, ti: (bi, ti, hi))
    return pl.pallas_call(
        functools.partial(_hgrn_kernel, n_chunks=tc // HG_CHUNK),
        grid=(b, heads, s // tc),
        in_specs=[pl.BlockSpec(memory_space=pltpu.SMEM)] + [spec] * 5
                 + [pl.BlockSpec((1, HG_DIM), lambda bi, hi, ti: (0, 0))],
        out_specs=spec,
        out_shape=jax.ShapeDtypeStruct((b, s, w), BF16),
        scratch_shapes=[pltpu.VMEM((HG_DIM, HG_DIM), F32)],
        compiler_params=_cparams(("parallel", "parallel", "arbitrary")),
        name="hgrn2_mixer",
    )(lf_min, q, lf, k, v, g, g_norm.reshape(1, HG_DIM))


AUG_C = FOX_DIM
AUG_ONE = FOX_DIM + 3
AUG_QN = FOX_DIM + 6
NORM_MARGIN = 1.02
SKIP_LOG2 = 150.0


def _placement_constants(heads):
    pq = np.zeros((CP_WIDTH, heads * LANES), np.float32)
    pk = np.zeros((CP_WIDTH, heads * LANES), np.float32)
    pn = np.zeros((heads, heads * LANES), np.float32)
    hs = np.zeros((heads * FOX_DIM, heads), np.float32)
    for h in range(heads):
        for i in range(3):
            pq[i * heads + h, h * LANES + AUG_C + i] = 1.0
            pk[i * heads + h, h * LANES + AUG_ONE + i] = -1.0
        pq[3 * heads, h * LANES + AUG_ONE:h * LANES + AUG_ONE + 3] = 1.0
        pk[3 * heads, h * LANES + AUG_C:h * LANES + AUG_C + 3] = 1.0
        pn[h, h * LANES + AUG_QN] = 1.0
        hs[h * FOX_DIM:(h + 1) * FOX_DIM, h] = 1.0
    return tuple(jnp.asarray(a, BF16) for a in (pq, pk, pn, hs))


def _fox_pack_kernel(q_ref, k_ref, v_ref, cp_ref, c_ref, pq_ref, pk_ref, pn_ref, hs_ref,
                     qa_ref, ka_ref, va_ref, st_ref):
    q = q_ref[...]
    k = k_ref[...]
    v = v_ref[...]
    cp = cp_ref[...]
    tm = q.shape[0]
    heads = qa_ref.shape[0]
    hs = hs_ref[...]
    qn = jnp.sqrt(jnp.dot(q * q, hs, preferred_element_type=F32)) * NORM_MARGIN
    kn = jnp.sqrt(jnp.dot(k * k, hs, preferred_element_type=F32)) * NORM_MARGIN
    tail_q = (jnp.dot(cp, pq_ref[...], preferred_element_type=F32)
              + jnp.dot(qn.astype(BF16), pn_ref[...], preferred_element_type=F32)).astype(BF16)
    tail_k = jnp.dot(cp, pk_ref[...], preferred_element_type=F32).astype(BF16)
    lane = lax.broadcasted_iota(jnp.int32, (tm, LANES), 1)
    in_data = lane < FOX_DIM
    tail_v = jnp.where(lane == FOX_DIM, 1.0, 0.0).astype(BF16)
    for h in range(heads):
        pair = slice((h // 2) * LANES, (h // 2 + 1) * LANES)
        mine = slice(h * LANES, (h + 1) * LANES)
        data = (lambda x: x[:, pair]) if h % 2 == 0 else (lambda x: pltpu.roll(x[:, pair], FOX_DIM, 1))
        qa_ref[h] = jnp.where(in_data, data(q), tail_q[:, mine])
        ka_ref[h] = jnp.where(in_data, data(k), tail_k[:, mine])
        va_ref[h] = jnp.where(in_data, data(v), tail_v)
    st_ref[0:1, :] = jnp.max(kn, axis=0, keepdims=True)
    st_ref[1:2, :] = c_ref[tm - 1:tm, :]


def _fox_pack(q, k, v, cp, c, tm):
    b, s, w = q.shape
    heads = c.shape[-1]
    consts = _placement_constants(heads)
    row = lambda width: pl.BlockSpec((None, tm, width), lambda bi, ti: (bi, ti, 0))
    full = lambda a: pl.BlockSpec(a.shape, lambda bi, ti: (0, 0))
    out = pl.BlockSpec((None, heads, tm, LANES), lambda bi, ti: (bi, 0, ti, 0))
    oshape = jax.ShapeDtypeStruct((b, heads, s, LANES), BF16)
    return pl.pallas_call(
        _fox_pack_kernel,
        grid=(b, s // tm),
        in_specs=[row(w), row(w), row(w), row(CP_WIDTH), row(heads)] + [full(a) for a in consts],
        out_specs=[out] * 3 + [pl.BlockSpec((None, None, 2, heads), lambda bi, ti: (bi, ti, 0, 0))],
        out_shape=[oshape] * 3 + [jax.ShapeDtypeStruct((b, s // tm, 2, heads), F32)],
        compiler_params=_cparams(("parallel", "parallel")),
        name="fox_pack",
    )(q, k, v, cp, c, *consts)


def _fox_kernel(kmax_ref, clast_ref, q_ref, k_ref, v_ref, o_ref, m_ref, acc_ref, *, tq, tk):
    qi = pl.program_id(2)
    row_a = pl.program_id(0) * (2 * pl.num_programs(1)) + 2 * pl.program_id(1)
    m_ref[...] = jnp.full_like(m_ref, NEG_INF)
    acc_ref[...] = jnp.zeros_like(acc_ref)

    def block(row0, n_rows, kv0, masked):
        rows = pl.ds(row0, n_rows)
        cols = pl.ds(kv0, tk)
        if masked:
            causal = (lax.broadcasted_iota(jnp.int32, (n_rows, tk), 1)
                      <= lax.broadcasted_iota(jnp.int32, (n_rows, tk), 0))
        for a in range(2):
            s = lax.dot_general(q_ref[a, rows, :], k_ref[a, cols, :], (((1,), (1,)), ((), ())),
                                preferred_element_type=F32)
            if masked:
                s = jnp.where(causal, s, NEG_INF)
            m_prev = m_ref[a, rows, :]
            m_new = jnp.maximum(m_prev, jnp.max(s, axis=-1, keepdims=True))
            alpha = jnp.exp2(m_prev - m_new)
            p = jnp.exp2(s - jnp.tile(m_new, (1, tk // LANES)))
            acc_ref[a, rows, :] = alpha * acc_ref[a, rows, :] + jnp.dot(
                p.astype(BF16), v_ref[a, cols, :], preferred_element_type=F32)
            m_ref[a, rows, :] = m_new

    n_diag = tq // tk
    n_full = qi * n_diag

    for j in range(n_diag):
        block(j * tk, tq - j * tk, pl.multiple_of((n_full + j) * tk, tk), True)

    n_tiles = kmax_ref.shape[1]
    slack = []
    for a in range(2):
        kmax = kmax_ref[row_a + a, 0]
        for j in range(1, n_tiles):
            kmax = jnp.maximum(kmax, kmax_ref[row_a + a, j])
        sel_row = lax.broadcasted_iota(jnp.int32, (LANES, LANES), 0)
        sel = jnp.where(sel_row == AUG_QN, kmax * NORM_MARGIN,
                        jnp.where((sel_row >= AUG_C) & (sel_row < AUG_ONE), 1.0, 0.0)).astype(BF16)
        gap = jnp.dot(q_ref[a], sel, preferred_element_type=F32) - m_ref[a]
        slack.append([jnp.max(gap[g * tk:(g + 1) * tk]) for g in range(n_diag)])

    def live_groups(j):
        n = jnp.int32(0)
        for a in range(2):
            c_last = clast_ref[row_a + a, jnp.maximum(j, 0)]
            for g in range(n_diag):
                n = jnp.where(slack[a][g] - c_last > -SKIP_LOG2, jnp.maximum(n, g + 1), n)
        return jnp.where(j >= 0, n, 0)

    def body(carry):
        j, n_live = carry
        kv0 = pl.multiple_of(j * tk, tk)
        for n in range(1, n_diag + 1):
            @pl.when(n_live == n)
            def _():
                block(0, n * tk, kv0, False)
        return j - 1, live_groups(j - 1)

    lax.while_loop(lambda carry: carry[1] > 0, body, (n_full - 1, live_groups(n_full - 1)))

    lane = lax.broadcasted_iota(jnp.int32, (tq, LANES), 1)
    acc_a = acc_ref[0]
    acc_b = acc_ref[1]
    o_a = acc_a / acc_a[:, FOX_DIM:FOX_DIM + 1]
    o_b = acc_b / acc_b[:, FOX_DIM:FOX_DIM + 1]
    o_ref[...] = jnp.where(lane < FOX_DIM, o_a, pltpu.roll(o_b, FOX_DIM, 1)).astype(o_ref.dtype)


def _fox_attention(q, k, v, kmax, clast, tq, tk):
    b, h, s, _ = q.shape
    kv_spec = pl.BlockSpec((None, 2, s, LANES), lambda bi, hi, qi: (bi, hi, 0, 0))
    smem = pl.BlockSpec(memory_space=pltpu.SMEM)
    return pl.pallas_call(
        functools.partial(_fox_kernel, tq=tq, tk=tk),
        grid=(b, h // 2, s // tq),
        in_specs=[smem, smem,
                  pl.BlockSpec((None, 2, tq, LANES), lambda bi, hi, qi: (bi, hi, qi, 0)), kv_spec, kv_spec],
        out_specs=pl.BlockSpec((None, tq, LANES), lambda bi, hi, qi: (bi, qi, hi)),
        out_shape=jax.ShapeDtypeStruct((b, s, (h // 2) * LANES), BF16),
        scratch_shapes=[pltpu.VMEM((2, tq, LANES), F32), pltpu.VMEM((2, tq, LANES), F32)],
        compiler_params=_cparams(("parallel", "parallel", "arbitrary")),
        name="fox_attention",
    )(kmax, clast, q, k, v)


def _merge_kernel(x_ref, oa_ref, ob_ref, ga_ref, gb_ref, wa_ref, wb_ref, wo_ref, gn_ref, h_ref, n_ref, mg_ref,
                  *, tn):
    oa = oa_ref[...]
    ob = ob_ref[...]
    for c in range(mg_ref.shape[1] // tn):
        cols = slice(c * tn, (c + 1) * tn)
        ya = jnp.dot(oa, wa_ref[:, cols], preferred_element_type=F32)
        yb = jnp.dot(ob, wb_ref[:, cols], preferred_element_type=F32)
        mg_ref[:, cols] = (ga_ref[:, cols] * ya + gb_ref[:, cols] * yb).astype(mg_ref.dtype)
    h = x_ref[...] + jnp.dot(mg_ref[...], wo_ref[...], preferred_element_type=F32)
    h_ref[...] = h
    y = h * lax.rsqrt(jnp.mean(h * h, axis=-1, keepdims=True) + EPS)
    n_ref[...] = (y * gn_ref[...]).astype(n_ref.dtype)


def _merge(x2d, oa, ob, ga, gb, wa, wb, wo, gain, tm, tn):
    t, d = x2d.shape
    row = lambda width: pl.BlockSpec((tm, width), lambda i: (i, 0))
    once = lambda a: pl.BlockSpec(a.shape, lambda i: (0,) * a.ndim, pipeline_mode=pl.Buffered(1))
    gain2 = gain.reshape(1, d)
    return pl.pallas_call(
        functools.partial(_merge_kernel, tn=tn),
        grid=(t // tm,),
        in_specs=[row(d), row(oa.shape[1]), row(ob.shape[1]), row(d), row(d), once(wa), once(wb), once(wo),
                  once(gain2)],
        out_specs=[row(d), row(d)],
        out_shape=[jax.ShapeDtypeStruct((t, d), F32), jax.ShapeDtypeStruct((t, d), BF16)],
        scratch_shapes=[pltpu.VMEM((tm, d), BF16)],
        compiler_params=_cparams(("parallel",)),
        name="merge_outproj",
    )(x2d, oa, ob, ga, gb, wa, wb, wo, gain2)


def _ffn_kernel(n_ref, halo_ref, h_ref, wu_ref, cw_ref, cb_ref, wd_ref, gn_ref, o_ref, next_ref, act_ref,
                *, tm, seq, tf):
    i = pl.program_id(0)
    at_start = (i * tm) % seq == 0
    next_ref[0:HALO, :] = jnp.where(at_start, jnp.zeros_like(halo_ref[...]), halo_ref[...])
    next_ref[HALO:, :] = n_ref[...]
    n_ext = next_ref[...]
    d_ff = act_ref.shape[1]

    def conv(col0):
        cols = pl.ds(col0, tf)
        u = jnp.dot(n_ext, wu_ref[:, cols], preferred_element_type=F32)
        out = cb_ref[:, cols]
        for tap in range(CONV_WIDTH):
            lo = HALO - (CONV_WIDTH - 1) + tap
            out = out + cw_ref[tap:tap + 1, cols] * u[lo:lo + tm]
        return out

    for c in range(d_ff // tf):
        gate = conv(c * tf)
        val = conv(d_ff + c * tf)
        act = 0.5 * gate * (1.0 + lax.erf(gate * (2.0 ** -0.5))) * val
        act_ref[:, c * tf:(c + 1) * tf] = act.astype(act_ref.dtype)

    h = h_ref[...] + jnp.dot(act_ref[...], wd_ref[...], preferred_element_type=F32)
    y = h * lax.rsqrt(jnp.mean(h * h, axis=-1, keepdims=True) + EPS)
    o_ref[...] = (y * gn_ref[...]).astype(o_ref.dtype)


def _ffn(n2d, h2d, w_up, conv_w, conv_b, w_down, gain, seq, tm, tf):
    t, d = n2d.shape
    d_ff = w_down.shape[0]
    halo_blocks = tm // HALO
    cb = conv_b.reshape(1, 2 * d_ff)
    once = lambda a: pl.BlockSpec(a.shape, lambda i: (0,) * a.ndim, pipeline_mode=pl.Buffered(1))
    return pl.pallas_call(
        functools.partial(_ffn_kernel, tm=tm, seq=seq, tf=tf),
        grid=(t // tm,),
        in_specs=[pl.BlockSpec((tm, d), lambda i: (i, 0)),
                  pl.BlockSpec((HALO, d), lambda i: (jnp.maximum(i * halo_blocks - 1, 0), 0)),
                  pl.BlockSpec((tm, d), lambda i: (i, 0)),
                  once(w_up), once(conv_w), once(cb), once(w_down),
                  pl.BlockSpec((1, d), lambda i: (0, 0))],
        out_specs=pl.BlockSpec((tm, d), lambda i: (i, 0)),
        out_shape=jax.ShapeDtypeStruct((t, d), F32),
        scratch_shapes=[pltpu.VMEM((HALO + tm, d), BF16), pltpu.VMEM((tm, d_ff), BF16)],
        compiler_params=_cparams(("parallel",)),
        name="conv_glu_ffn",
    )(n2d, n2d, h2d, w_up, conv_w, cb, w_down, gain.reshape(1, d))


def _tile(n, pref):
    return pref if n % pref == 0 else n


def kernel(x, norm_mix, w_in, fox_f_bias, hg_lb_logits, hg_norm, w_branch_a, w_branch_b, w_out, norm_ffn, w_up,
           conv_w, conv_b, w_down, norm_final):
    b, s, d = x.shape
    assert norm_mix.shape[0] == 1, "single-layer trunk"
    t = b * s
    hg_w = HG_HEADS * HG_DIM
    fox_w = FOX_HEADS * FOX_DIM
    cuts = [hg_w, 2 * hg_w, 3 * hg_w, 4 * hg_w, 4 * hg_w + fox_w, 4 * hg_w + 2 * fox_w, 4 * hg_w + 3 * fox_w,
            4 * hg_w + 3 * fox_w + FOX_HEADS, 4 * hg_w + 3 * fox_w + FOX_HEADS + d]
    assert w_in.shape[-1] == cuts[-1] + d
    assert hg_w == d and fox_w == d, "the projection kernel walks equal-width segments"
    w_bf = w_in[0].astype(BF16)
    w_hq, w_hf, w_hi, w_hg, w_fq, w_fk, w_fv, w_ff, w_ga, w_gb = jnp.split(w_bf, cuts, axis=-1)
    w_all = jnp.concatenate([w_hq, w_hf, w_hi, w_hg, w_fq, w_fk, w_fv, w_ga, w_gb, w_ff], axis=-1)

    x2d = x.reshape(t, d)
    tm_in = _tile(s, 512)
    hq, hlf, hk, hv, hgate, fq, fk, fv, ga, gb, c2, c_pieces, lf_colmin = _in_proj(
        x2d, norm_mix[0], w_all, hg_lb_logits, fox_f_bias[0], s, tm_in, 256)

    r3 = lambda a: a.reshape(b, s, a.shape[-1])
    tc = _tile(s, 2 * tm_in)
    lf_min = lf_colmin[:, 0, :].reshape(b, s // tc, tc // tm_in, HG_HEADS, HG_DIM).min(axis=(2, 4))
    lf_min = lf_min.transpose(0, 2, 1).reshape(b * HG_HEADS, s // tc)
    o_a = _hgrn(lf_min, r3(hq), r3(hlf), r3(hk), r3(hv), r3(hgate), hg_norm[0], tc)
    tk = _tile(s, 512)
    q_aug, k_aug, v_aug, kv_stats = _fox_pack(r3(fq), r3(fk), r3(fv), r3(c_pieces), r3(c2), tk)
    kv_stats = kv_stats.transpose(2, 0, 3, 1).reshape(2, b * FOX_HEADS, s // tk)
    o_b = _fox_attention(q_aug, k_aug, v_aug, kv_stats[0], kv_stats[1], _tile(s, 4 * tk), tk)

    h1, n_ffn = _merge(x2d, o_a.reshape(t, hg_w), o_b.reshape(t, fox_w), ga, gb, w_branch_a[0].astype(BF16),
                       w_branch_b[0].astype(BF16), w_out[0].astype(BF16), norm_ffn[0], _tile(t, 512), 256)
    out = _ffn(n_ffn, h1, w_up[0].astype(BF16), conv_w[0], conv_b[0], w_down[0].astype(BF16), norm_final,
               s, _tile(s, 512), 256)
    return out.reshape(b, s, d)
```

```python
import functools
import math
from typing import NamedTuple

import jax
import jax.numpy as jnp
import numpy as np
from jax import lax
from jax.experimental import pallas as pl
from jax.experimental.pallas import tpu as pltpu

F32 = jnp.float32
BF16 = jnp.bfloat16

EPS = 1e-6
HG_HEADS = 8
HG_DIM = 128
HG_CHUNK = 64
HG_SUB = 16
HG_MAX_EXPONENT = 64.0
FOX_HEADS = 16
FOX_DIM = 64
CONV_WIDTH = 3
HALO = 8
NEG_INF = float("-inf")
LOG2E = math.log2(math.e)
LANES = 128
SUBLANES = 8

VMEM_LIMIT = 56 * 1024 * 1024


def _cparams(sem):
    return pltpu.CompilerParams(dimension_semantics=sem, vmem_limit_bytes=VMEM_LIMIT)


def _sigmoid(x):
    return 1.0 / (1.0 + jnp.exp(-x))


def _silu(x):
    return x * _sigmoid(x)


def _log_sigmoid(x):
    return jnp.minimum(x, 0.0) - jnp.log1p(jnp.exp(-jnp.abs(x)))


def _split3(x):
    x1 = x.astype(BF16)
    r1 = x - x1.astype(F32)
    x2 = r1.astype(BF16)
    x3 = (r1 - x2.astype(F32)).astype(BF16)
    return x1, x2, x3


SEG_HQ, SEG_HF, SEG_HI, SEG_HG, SEG_FQ, SEG_FK, SEG_FV, SEG_GA, SEG_GB = range(9)
N_SEG = 9


def _in_proj_kernel(x_ref, gain_ref, w_ref, lbl_ref, fbias_ref, spread_ref, one_ref, hq_ref, hlf_ref, hk_ref,
                    hv_ref, hg_ref, fq_ref, fk_ref, fv_ref, ga_ref, gb_ref, c_ref, cp_ref, lfmin_ref, n_ref, carry_ref,
                    *, tn, seq):
    x = x_ref[...]
    n_ref[...] = (x * lax.rsqrt(jnp.mean(x * x, axis=-1, keepdims=True) + EPS) * gain_ref[...]).astype(BF16)
    n = n_ref[...]
    width = hq_ref.shape[1]
    lbl = lbl_ref[...]
    e = jnp.exp(lbl - jnp.max(lbl, axis=0, keepdims=True))
    lb = e[0:1] / jnp.sum(e, axis=0, keepdims=True)

    def chunks(seg):
        for c in range(width // tn):
            cols = slice(c * tn, (c + 1) * tn)
            yield cols, jnp.dot(n, w_ref[:, seg * width + c * tn:seg * width + (c + 1) * tn],
                                preferred_element_type=F32)

    def emit(seg, ref, fn):
        for cols, acc in chunks(seg):
            ref[:, cols] = fn(acc).astype(ref.dtype)

    @pl.when((pl.program_id(0) * x.shape[0]) % seq == 0)
    def _():
        carry_ref[...] = jnp.zeros_like(carry_ref)

    lf = _log_sigmoid(jnp.dot(n, w_ref[:, N_SEG * width:], preferred_element_type=F32) + fbias_ref[...])
    tm = lf.shape[0]
    lower = (lax.broadcasted_iota(jnp.int32, (tm, tm), 1)
             <= lax.broadcasted_iota(jnp.int32, (tm, tm), 0)).astype(BF16)
    cs = carry_ref[...]
    for piece in _split3(lf):
        cs = cs + jnp.dot(lower, piece, preferred_element_type=F32)
    carry_ref[...] = cs[tm - 1:tm, :]
    c2 = cs * LOG2E
    c_ref[...] = c2
    cp = one_ref[...]
    for i, piece in enumerate(_split3(c2)):
        cp = cp + jnp.dot(piece, spread_ref[i], preferred_element_type=F32)
    cp_ref[...] = cp.astype(cp_ref.dtype)

    emit(SEG_HQ, hq_ref, _silu)
    for cols, acc in chunks(SEG_HF):
        f = lb[:, cols] + (1.0 - lb[:, cols]) * _sigmoid(acc)
        lf_h = jnp.log(f)
        hlf_ref[:, cols] = lf_h
        hk_ref[:, cols] = (1.0 - f).astype(hk_ref.dtype)
        lfmin_ref[:, cols] = jnp.broadcast_to(jnp.min(lf_h, axis=0, keepdims=True), (lfmin_ref.shape[0], tn))
    emit(SEG_HI, hv_ref, lambda a: a)
    emit(SEG_HG, hg_ref, _silu)
    emit(SEG_FQ, fq_ref, lambda a: a * (FOX_DIM ** -0.5 * LOG2E))
    emit(SEG_FK, fk_ref, lambda a: a)
    emit(SEG_FV, fv_ref, lambda a: a)
    emit(SEG_GA, ga_ref, _sigmoid)
    emit(SEG_GB, gb_ref, _sigmoid)


CP_WIDTH = 4 * FOX_HEADS


def _in_proj(x2d, gain, w_all, lb_logits, f_bias, seq, tm, tn):
    t, d = x2d.shape
    heads = f_bias.shape[0]
    spread = np.zeros((3, heads, CP_WIDTH), np.float32)
    for i in range(3):
        spread[i, np.arange(heads), i * heads + np.arange(heads)] = 1.0
    one = np.zeros((1, CP_WIDTH), np.float32)
    one[0, 3 * heads] = 1.0
    once = lambda a: pl.BlockSpec(a.shape, lambda i: (0,) * a.ndim, pipeline_mode=pl.Buffered(1))
    row = lambda width: pl.BlockSpec((tm, width), lambda i: (i, 0))
    seg_dtypes = [BF16, F32, BF16, BF16, BF16, BF16, BF16, BF16, BF16, BF16]
    consts = (gain.reshape(1, d), w_all, lb_logits, f_bias.reshape(1, heads), jnp.asarray(spread, BF16),
              jnp.asarray(one))
    return pl.pallas_call(
        functools.partial(_in_proj_kernel, tn=tn, seq=seq),
        grid=(t // tm,),
        in_specs=[row(d)] + [once(a) for a in consts],
        out_specs=([row(d)] * len(seg_dtypes) + [row(heads), row(CP_WIDTH)]
                   + [pl.BlockSpec((None, SUBLANES, d), lambda i: (i, 0, 0))]),
        out_shape=([jax.ShapeDtypeStruct((t, d), dt) for dt in seg_dtypes]
                   + [jax.ShapeDtypeStruct((t, heads), F32), jax.ShapeDtypeStruct((t, CP_WIDTH), BF16),
                      jax.ShapeDtypeStruct((t // tm, SUBLANES, d), F32)]),
        scratch_shapes=[pltpu.VMEM((tm, d), BF16), pltpu.VMEM((1, heads), F32)],
        compiler_params=_cparams(("arbitrary",)),
        name="in_proj",
    )(x2d, *consts)


def _causal_chunk():
    r_i = lax.broadcasted_iota(jnp.int32, (HG_CHUNK, HG_CHUNK), 0)
    c_i = lax.broadcasted_iota(jnp.int32, (HG_CHUNK, HG_CHUNK), 1)
    return c_i <= r_i


def _chunk_cumsum(lf):
    lower = _causal_chunk().astype(BF16)
    b = jnp.zeros(lf.shape, F32)
    for piece in _split3(lf):
        b = b + jnp.dot(lower, piece, preferred_element_type=F32)
    return b


def _hgrn_chunk_single_ref(q, b, k, v, state):
    c = HG_CHUNK
    q_t = (q * jnp.exp(b)).astype(BF16)
    k_up = k * jnp.exp(-b)
    sc = lax.dot_general(q_t, k_up.astype(BF16), (((1,), (1,)), ((), ())), preferred_element_type=F32)
    sc = jnp.where(_causal_chunk(), sc, 0.0).astype(BF16)
    v_bf = v.astype(BF16)
    o = jnp.dot(jnp.concatenate([q_t, sc], axis=1), jnp.concatenate([state.astype(BF16), v_bf], axis=0),
                preferred_element_type=F32)
    b_end = b[c - 1:c]
    k_end = k_up * jnp.exp(b_end)
    tr = jnp.concatenate([k_end, jnp.broadcast_to(b_end, (HG_DIM - c, HG_DIM))], axis=0).T
    upd = jnp.dot(tr[:, :c].astype(BF16), v_bf, preferred_element_type=F32)
    return o, state * jnp.exp(tr[:, c:c + 1]) + upd


def _hgrn_chunk(q, b, k, v, state):
    c = HG_CHUNK
    n_sub = c // HG_SUB
    state_t = state.T

    qe = (q * jnp.exp(b)).astype(BF16)
    o = lax.dot_general(qe, state_t.astype(BF16), (((1,), (1,)), ((), ())), preferred_element_type=F32)

    o_parts = []
    t_iota = lax.broadcasted_iota(jnp.int32, (HG_SUB, HG_DIM), 0)
    for i in range(n_sub):
        lo = i * HG_SUB
        q_i = q[lo:lo + HG_SUB]
        b_i = b[lo:lo + HG_SUB]
        o_i = o[lo:lo + HG_SUB]
        if i > 0:
            ref = b[lo - 1:lo]
            q_t = (q_i * jnp.exp(b_i - ref)).astype(BF16)
            k_t = (k[:lo] * jnp.exp(ref - b[:lo])).astype(BF16)
            sc = lax.dot_general(q_t, k_t, (((1,), (1,)), ((), ())), preferred_element_type=F32)
            o_i = o_i + jnp.dot(sc.astype(BF16), v[:lo].astype(BF16), preferred_element_type=F32)
        for s in range(HG_SUB):
            row = lo + s
            rel = jnp.where(t_iota >= s, b_i - b[row:row + 1], NEG_INF)
            p = q_i * jnp.exp(rel) * k[row:row + 1]
            o_i = o_i + jnp.sum(p, axis=-1, keepdims=True) * v[row:row + 1]
        o_parts.append(o_i)
    o = jnp.concatenate(o_parts, axis=0)

    b_end = b[c - 1:c]
    k_dec = (k * jnp.exp(b_end - b)).astype(BF16)
    upd = lax.dot_general(v.astype(BF16), k_dec, (((0,), (0,)), ((), ())), preferred_element_type=F32)
    return o, (state_t * jnp.exp(b_end) + upd).T


def _hgrn_kernel(lfmin_ref, q_ref, lf_ref, k_ref, v_ref, g_ref, gn_ref, o_ref, state_ref, *, n_chunks):
    @pl.when(pl.program_id(2) == 0)
    def _():
        state_ref[...] = jnp.zeros_like(state_ref)

    def run_chunk(chunk_fn, rows, b, state):
        load = lambda ref: ref[rows, :].astype(F32)
        o, new_state = chunk_fn(load(q_ref), b, load(k_ref), load(v_ref), state)
        y = o * lax.rsqrt(jnp.mean(o * o, axis=-1, keepdims=True) + EPS)
        o_ref[rows, :] = (y * gn_ref[...] * load(g_ref)).astype(o_ref.dtype)
        return new_state

    lf_min = lfmin_ref[pl.program_id(0) * pl.num_programs(1) + pl.program_id(1), pl.program_id(2)]
    single_ref_ok = lf_min * HG_CHUNK > -HG_MAX_EXPONENT

    @pl.when(single_ref_ok)
    def _():
        chunk_rows = [pl.ds(ci * HG_CHUNK, HG_CHUNK) for ci in range(n_chunks)]
        b_all = _chunk_cumsum(jnp.concatenate([lf_ref[rows, :] for rows in chunk_rows], axis=1))
        state = state_ref[...]
        for ci, rows in enumerate(chunk_rows):
            state = run_chunk(_hgrn_chunk_single_ref, rows, b_all[:, ci * HG_DIM:(ci + 1) * HG_DIM], state)
        state_ref[...] = state

    @pl.when(jnp.logical_not(single_ref_ok))
    def _():
        def body(ci, carry):
            rows = pl.ds(pl.multiple_of(ci * HG_CHUNK, HG_CHUNK), HG_CHUNK)
            state_ref[...] = run_chunk(_hgrn_chunk, rows, _chunk_cumsum(lf_ref[rows, :]), state_ref[...])
            return carry

        lax.fori_loop(0, n_chunks, body, 0)


def _hgrn(lf_min, q, lf, k, v, g, g_norm, tc):
    b, s, w = q.shape
    heads = w // HG_DIM
    spec = pl.BlockSpec((None, tc, HG_DIM), lambda bi, hi, ti: (bi, ti, hi))
    return pl.pallas_call(
        functools.partial(_hgrn_kernel, n_chunks=tc // HG_CHUNK),
        grid=(b, heads, s // tc),
        in_specs=[pl.BlockSpec(memory_space=pltpu.SMEM)] + [spec] * 5
                 + [pl.BlockSpec((1, HG_DIM), lambda bi, hi, ti: (0, 0))],
        out_specs=spec,
        out_shape=jax.ShapeDtypeStruct((b, s, w), BF16),
        scratch_shapes=[pltpu.VMEM((HG_DIM, HG_DIM), F32)],
        compiler_params=_cparams(("parallel", "parallel", "arbitrary")),
        name="hgrn2_mixer",
    )(lf_min, q, lf, k, v, g, g_norm.reshape(1, HG_DIM))


AUG_C = FOX_DIM
AUG_ONE = FOX_DIM + 3
AUG_QN = FOX_DIM + 6
NORM_MARGIN = 1.02
SKIP_LOG2 = 150.0


def _placement_constants(heads):
    pq = np.zeros((CP_WIDTH + heads, heads * LANES), np.float32)
    pk = np.zeros((CP_WIDTH, heads * LANES), np.float32)
    hs = np.zeros((heads * FOX_DIM, heads), np.float32)
    for h in range(heads):
        for i in range(3):
            pq[i * heads + h, h * LANES + AUG_C + i] = 1.0
            pk[i * heads + h, h * LANES + AUG_ONE + i] = -1.0
        pq[3 * heads, h * LANES + AUG_ONE:h * LANES + AUG_ONE + 3] = 1.0
        pk[3 * heads, h * LANES + AUG_C:h * LANES + AUG_C + 3] = 1.0
        pq[CP_WIDTH + h, h * LANES + AUG_QN] = 1.0
        hs[h * FOX_DIM:(h + 1) * FOX_DIM, h] = 1.0
    return tuple(jnp.asarray(a, BF16) for a in (pq, pk, hs))


def _fox_pack_kernel(q_ref, k_ref, v_ref, cp_ref, c_ref, pq_ref, pk_ref, hs_ref, qa_ref, ka_ref, va_ref, st_ref):
    q = q_ref[...]
    k = k_ref[...]
    v = v_ref[...]
    cp = cp_ref[...]
    tm = q.shape[0]
    heads = qa_ref.shape[0]
    hs = hs_ref[...]
    qn = jnp.sqrt(jnp.dot(q * q, hs, preferred_element_type=F32)) * NORM_MARGIN
    kn = jnp.sqrt(jnp.dot(k * k, hs, preferred_element_type=F32)) * NORM_MARGIN
    tail_q = jnp.dot(jnp.concatenate([cp, qn.astype(BF16)], axis=1), pq_ref[...],
                     preferred_element_type=F32).astype(BF16)
    tail_k = jnp.dot(cp, pk_ref[...], preferred_element_type=F32).astype(BF16)
    lane = lax.broadcasted_iota(jnp.int32, (tm, LANES), 1)
    in_data = lane < FOX_DIM
    tail_v = jnp.where(lane == FOX_DIM, 1.0, 0.0).astype(BF16)
    for h in range(heads):
        pair = slice((h // 2) * LANES, (h // 2 + 1) * LANES)
        mine = slice(h * LANES, (h + 1) * LANES)
        data = (lambda x: x[:, pair]) if h % 2 == 0 else (lambda x: pltpu.roll(x[:, pair], FOX_DIM, 1))
        qa_ref[h] = jnp.where(in_data, data(q), tail_q[:, mine])
        ka_ref[h] = jnp.where(in_data, data(k), tail_k[:, mine])
        va_ref[h] = jnp.where(in_data, data(v), tail_v)
    st_ref[0:1, :] = jnp.max(kn, axis=0, keepdims=True)
    st_ref[1:2, :] = c_ref[tm - 1:tm, :]


def _fox_pack(q, k, v, cp, c, tm):
    b, s, w = q.shape
    heads = c.shape[-1]
    consts = _placement_constants(heads)
    row = lambda width: pl.BlockSpec((None, tm, width), lambda bi, ti: (bi, ti, 0))
    full = lambda a: pl.BlockSpec(a.shape, lambda bi, ti: (0, 0))
    out = pl.BlockSpec((None, heads, tm, LANES), lambda bi, ti: (bi, 0, ti, 0))
    oshape = jax.ShapeDtypeStruct((b, heads, s, LANES), BF16)
    return pl.pallas_call(
        _fox_pack_kernel,
        grid=(b, s // tm),
        in_specs=[row(w), row(w), row(w), row(CP_WIDTH), row(heads)] + [full(a) for a in consts],
        out_specs=[out] * 3 + [pl.BlockSpec((None, None, 2, heads), lambda bi, ti: (bi, ti, 0, 0))],
        out_shape=[oshape] * 3 + [jax.ShapeDtypeStruct((b, s // tm, 2, heads), F32)],
        compiler_params=_cparams(("parallel", "parallel")),
        name="fox_pack",
    )(q, k, v, cp, c, *consts)


def _fox_kernel(kmax_ref, clast_ref, q_ref, k_ref, v_ref, o_ref, m_ref, acc_ref, *, tq, tk):
    qi = pl.program_id(2)
    row_a = pl.program_id(0) * (2 * pl.num_programs(1)) + 2 * pl.program_id(1)
    m_ref[...] = jnp.full_like(m_ref, NEG_INF)
    acc_ref[...] = jnp.zeros_like(acc_ref)

    def block(row0, n_rows, kv0, masked):
        rows = pl.ds(row0, n_rows)
        cols = pl.ds(kv0, tk)
        if masked:
            causal = (lax.broadcasted_iota(jnp.int32, (n_rows, tk), 1)
                      <= lax.broadcasted_iota(jnp.int32, (n_rows, tk), 0))
        for a in range(2):
            s = lax.dot_general(q_ref[a, rows, :], k_ref[a, cols, :], (((1,), (1,)), ((), ())),
                                preferred_element_type=F32)
            if masked:
                s = jnp.where(causal, s, NEG_INF)
            m_prev = m_ref[a, rows, :]
            m_new = jnp.maximum(m_prev, jnp.max(s, axis=-1, keepdims=True))
            alpha = jnp.exp2(m_prev - m_new)
            p = jnp.exp2(s - jnp.tile(m_new, (1, tk // LANES)))
            acc_ref[a, rows, :] = alpha * acc_ref[a, rows, :] + jnp.dot(
                p.astype(BF16), v_ref[a, cols, :], preferred_element_type=F32)
            m_ref[a, rows, :] = m_new

    n_diag = tq // tk
    n_full = qi * n_diag

    for j in range(n_diag):
        block(j * tk, tq - j * tk, pl.multiple_of((n_full + j) * tk, tk), True)

    n_tiles = kmax_ref.shape[1]
    slack = []
    for a in range(2):
        kmax = kmax_ref[row_a + a, 0]
        for j in range(1, n_tiles):
            kmax = jnp.maximum(kmax, kmax_ref[row_a + a, j])
        sel_row = lax.broadcasted_iota(jnp.int32, (LANES, LANES), 0)
        sel = jnp.where(sel_row == AUG_QN, kmax * NORM_MARGIN,
                        jnp.where((sel_row >= AUG_C) & (sel_row < AUG_ONE), 1.0, 0.0)).astype(BF16)
        gap = jnp.dot(q_ref[a], sel, preferred_element_type=F32) - m_ref[a]
        slack.append([jnp.max(gap[g * tk:(g + 1) * tk]) for g in range(n_diag)])

    def live_groups(j):
        n = jnp.int32(0)
        for a in range(2):
            c_last = clast_ref[row_a + a, jnp.maximum(j, 0)]
            for g in range(n_diag):
                n = jnp.where(slack[a][g] - c_last > -SKIP_LOG2, jnp.maximum(n, g + 1), n)
        return jnp.where(j >= 0, n, 0)

    def body(carry):
        j, n_live = carry
        kv0 = pl.multiple_of(j * tk, tk)
        for n in range(1, n_diag + 1):
            @pl.when(n_live == n)
            def _():
                block(0, n * tk, kv0, False)
        return j - 1, live_groups(j - 1)

    lax.while_loop(lambda carry: carry[1] > 0, body, (n_full - 1, live_groups(n_full - 1)))

    lane = lax.broadcasted_iota(jnp.int32, (tq, LANES), 1)
    acc_a = acc_ref[0]
    acc_b = acc_ref[1]
    o_a = acc_a / acc_a[:, FOX_DIM:FOX_DIM + 1]
    o_b = acc_b / acc_b[:, FOX_DIM:FOX_DIM + 1]
    o_ref[...] = jnp.where(lane < FOX_DIM, o_a, pltpu.roll(o_b, FOX_DIM, 1)).astype(o_ref.dtype)


def _fox_attention(q, k, v, kmax, clast, tq, tk):
    b, h, s, _ = q.shape
    kv_spec = pl.BlockSpec((None, 2, s, LANES), lambda bi, hi, qi: (bi, hi, 0, 0))
    smem = pl.BlockSpec(memory_space=pltpu.SMEM)
    return pl.pallas_call(
        functools.partial(_fox_kernel, tq=tq, tk=tk),
        grid=(b, h // 2, s // tq),
        in_specs=[smem, smem,
                  pl.BlockSpec((None, 2, tq, LANES), lambda bi, hi, qi: (bi, hi, qi, 0)), kv_spec, kv_spec],
        out_specs=pl.BlockSpec((None, tq, LANES), lambda bi, hi, qi: (bi, qi, hi)),
        out_shape=jax.ShapeDtypeStruct((b, s, (h // 2) * LANES), BF16),
        scratch_shapes=[pltpu.VMEM((2, tq, LANES), F32), pltpu.VMEM((2, tq, LANES), F32)],
        compiler_params=_cparams(("parallel", "parallel", "arbitrary")),
        name="fox_attention",
    )(kmax, clast, q, k, v)


def _merge_kernel(x_ref, oa_ref, ob_ref, ga_ref, gb_ref, wa_ref, wb_ref, wo_ref, gn_ref, h_ref, n_ref, mg_ref,
                  *, tn):
    oa = oa_ref[...]
    ob = ob_ref[...]
    for c in range(mg_ref.shape[1] // tn):
        cols = slice(c * tn, (c + 1) * tn)
        ya = jnp.dot(oa, wa_ref[:, cols], preferred_element_type=F32)
        yb = jnp.dot(ob, wb_ref[:, cols], preferred_element_type=F32)
        mg_ref[:, cols] = (ga_ref[:, cols] * ya + gb_ref[:, cols] * yb).astype(mg_ref.dtype)
    h = x_ref[...] + jnp.dot(mg_ref[...], wo_ref[...], preferred_element_type=F32)
    h_ref[...] = h
    y = h * lax.rsqrt(jnp.mean(h * h, axis=-1, keepdims=True) + EPS)
    n_ref[...] = (y * gn_ref[...]).astype(n_ref.dtype)


def _merge(x2d, oa, ob, ga, gb, wa, wb, wo, gain, tm, tn):
    t, d = x2d.shape
    row = lambda width: pl.BlockSpec((tm, width), lambda i: (i, 0))
    once = lambda a: pl.BlockSpec(a.shape, lambda i: (0,) * a.ndim, pipeline_mode=pl.Buffered(1))
    gain2 = gain.reshape(1, d)
    return pl.pallas_call(
        functools.partial(_merge_kernel, tn=tn),
        grid=(t // tm,),
        in_specs=[row(d), row(oa.shape[1]), row(ob.shape[1]), row(d), row(d), once(wa), once(wb), once(wo),
                  once(gain2)],
        out_specs=[row(d), row(d)],
        out_shape=[jax.ShapeDtypeStruct((t, d), F32), jax.ShapeDtypeStruct((t, d), BF16)],
        scratch_shapes=[pltpu.VMEM((tm, d), BF16)],
        compiler_params=_cparams(("parallel",)),
        name="merge_outproj",
    )(x2d, oa, ob, ga, gb, wa, wb, wo, gain2)


def _ffn_kernel(n_ref, halo_ref, h_ref, wu_ref, cw_ref, cb_ref, wd_ref, gn_ref, o_ref, next_ref, act_ref,
                *, tm, seq, tf):
    i = pl.program_id(0)
    at_start = (i * tm) % seq == 0
    next_ref[0:HALO, :] = jnp.where(at_start, jnp.zeros_like(halo_ref[...]), halo_ref[...])
    next_ref[HALO:, :] = n_ref[...]
    n_ext = next_ref[...]
    d_ff = act_ref.shape[1]

    def conv(col0):
        cols = pl.ds(col0, tf)
        u = jnp.dot(n_ext, wu_ref[:, cols], preferred_element_type=F32)
        out = cb_ref[:, cols]
        for tap in range(CONV_WIDTH):
            lo = HALO - (CONV_WIDTH - 1) + tap
            out = out + cw_ref[tap:tap + 1, cols] * u[lo:lo + tm]
        return out

    for c in range(d_ff // tf):
        gate = conv(c * tf)
        val = conv(d_ff + c * tf)
        act = 0.5 * gate * (1.0 + lax.erf(gate * (2.0 ** -0.5))) * val
        act_ref[:, c * tf:(c + 1) * tf] = act.astype(act_ref.dtype)

    h = h_ref[...] + jnp.dot(act_ref[...], wd_ref[...], preferred_element_type=F32)
    y = h * lax.rsqrt(jnp.mean(h * h, axis=-1, keepdims=True) + EPS)
    o_ref[...] = (y * gn_ref[...]).astype(o_ref.dtype)


def _ffn(n2d, h2d, w_up, conv_w, conv_b, w_down, gain, seq, tm, tf):
    t, d = n2d.shape
    d_ff = w_down.shape[0]
    halo_blocks = tm // HALO
    cb = conv_b.reshape(1, 2 * d_ff)
    once = lambda a: pl.BlockSpec(a.shape, lambda i: (0,) * a.ndim, pipeline_mode=pl.Buffered(1))
    return pl.pallas_call(
        functools.partial(_ffn_kernel, tm=tm, seq=seq, tf=tf),
        grid=(t // tm,),
        in_specs=[pl.BlockSpec((tm, d), lambda i: (i, 0)),
                  pl.BlockSpec((HALO, d), lambda i: (jnp.maximum(i * halo_blocks - 1, 0), 0)),
                  pl.BlockSpec((tm, d), lambda i: (i, 0)),
                  once(w_up), once(conv_w), once(cb), once(w_down),
                  pl.BlockSpec((1, d), lambda i: (0, 0))],
        out_specs=pl.BlockSpec((tm, d), lambda i: (i, 0)),
        out_shape=jax.ShapeDtypeStruct((t, d), F32),
        scratch_shapes=[pltpu.VMEM((HALO + tm, d), BF16), pltpu.VMEM((tm, d_ff), BF16)],
        compiler_params=_cparams(("parallel",)),
        name="conv_glu_ffn",
    )(n2d, n2d, h2d, w_up, conv_w, cb, w_down, gain.reshape(1, d))


class _Tiles(NamedTuple):
    rows: int
    cols: int
    hgrn: int
    query: int


def _tiles(seq):
    fit = lambda pref, n=seq: pref if n % pref == 0 else n
    rows = fit(512)
    return _Tiles(rows=rows, cols=256, hgrn=fit(2 * rows), query=fit(4 * rows))


def kernel(x, norm_mix, w_in, fox_f_bias, hg_lb_logits, hg_norm, w_branch_a, w_branch_b, w_out, norm_ffn, w_up,
           conv_w, conv_b, w_down, norm_final):
    b, s, d = x.shape
    assert norm_mix.shape[0] == 1, "single-layer trunk"
    t = b * s
    tiles = _tiles(s)
    assert HG_HEADS * HG_DIM == d and FOX_HEADS * FOX_DIM == d, "the projection kernel walks equal-width segments"
    assert w_in.shape[-1] == N_SEG * d + FOX_HEADS
    f_lo, f_hi = (SEG_FV + 1) * d, (SEG_FV + 1) * d + FOX_HEADS
    w = w_in[0]
    w_all = jnp.concatenate([w[:, :f_lo], w[:, f_hi:], w[:, f_lo:f_hi]], axis=-1).astype(BF16)

    x2d = x.reshape(t, d)
    hq, hlf, hk, hv, hgate, fq, fk, fv, ga, gb, c2, c_pieces, lf_colmin = _in_proj(
        x2d, norm_mix[0], w_all, hg_lb_logits, fox_f_bias[0], s, tiles.rows, tiles.cols)

    r3 = lambda a: a.reshape(b, s, a.shape[-1])
    lf_min = lf_colmin[:, 0, :].reshape(b, s // tiles.hgrn, tiles.hgrn // tiles.rows, HG_HEADS, HG_DIM)
    lf_min = lf_min.min(axis=(2, 4)).transpose(0, 2, 1).reshape(b * HG_HEADS, s // tiles.hgrn)
    o_a = _hgrn(lf_min, r3(hq), r3(hlf), r3(hk), r3(hv), r3(hgate), hg_norm[0], tiles.hgrn)

    q_aug, k_aug, v_aug, kv_stats = _fox_pack(r3(fq), r3(fk), r3(fv), r3(c_pieces), r3(c2), tiles.rows)
    kv_stats = kv_stats.transpose(2, 0, 3, 1).reshape(2, b * FOX_HEADS, s // tiles.rows)
    o_b = _fox_attention(q_aug, k_aug, v_aug, kv_stats[0], kv_stats[1], tiles.query, tiles.rows)

    h1, n_ffn = _merge(x2d, o_a.reshape(t, d), o_b.reshape(t, d), ga, gb, w_branch_a[0].astype(BF16),
                       w_branch_b[0].astype(BF16), w_out[0].astype(BF16), norm_ffn[0], tiles.rows, tiles.cols)
    out = _ffn(n_ffn, h1, w_up[0].astype(BF16), conv_w[0], conv_b[0], w_down[0].astype(BF16), norm_final,
               s, tiles.rows, tiles.cols)
    return out.reshape(b, s, d)
```

```python
import functools
import math
from typing import NamedTuple

import jax
import jax.numpy as jnp
import numpy as np
from jax import lax
from jax.experimental import pallas as pl
from jax.experimental.pallas import tpu as pltpu

F32 = jnp.float32
BF16 = jnp.bfloat16

EPS = 1e-6
HG_HEADS = 8
HG_DIM = 128
HG_CHUNK = 64
HG_SUB = 16
HG_MAX_EXPONENT = 64.0
FOX_HEADS = 16
FOX_DIM = 64
CONV_WIDTH = 3
HALO = 8
NEG_INF = float("-inf")
LOG2E = math.log2(math.e)
LANES = 128
SUBLANES = 8

VMEM_LIMIT = 56 * 1024 * 1024


def _cparams(sem):
    return pltpu.CompilerParams(dimension_semantics=sem, vmem_limit_bytes=VMEM_LIMIT)


def _sigmoid(x):
    return 1.0 / (1.0 + jnp.exp(-x))


def _silu(x):
    return x * _sigmoid(x)


def _log_sigmoid(x):
    return jnp.minimum(x, 0.0) - jnp.log1p(jnp.exp(-jnp.abs(x)))


def _split3(x):
    x1 = x.astype(BF16)
    r1 = x - x1.astype(F32)
    x2 = r1.astype(BF16)
    x3 = (r1 - x2.astype(F32)).astype(BF16)
    return x1, x2, x3


SEG_HQ, SEG_HF, SEG_HI, SEG_HG, SEG_FQ, SEG_FK, SEG_FV, SEG_GA, SEG_GB = range(9)
N_MAIN = 7


def _in_proj_kernel(x_ref, gain_ref, w_ref, wgate_ref, wf_ref, lbl_ref, fbias_ref, spread_ref, one_ref,
                    hq_ref, hlf_ref, hk_ref, hv_ref, hg_ref, fq_ref, fk_ref, fv_ref, ga_ref, gb_ref, c_ref, cp_ref,
                    lfmin_ref, n_ref, carry_ref, *, tn, seq):
    x = x_ref[...]
    n_ref[...] = (x * lax.rsqrt(jnp.mean(x * x, axis=-1, keepdims=True) + EPS) * gain_ref[...]).astype(BF16)
    n = n_ref[...]
    width = hq_ref.shape[1]
    lbl = lbl_ref[...]
    e = jnp.exp(lbl - jnp.max(lbl, axis=0, keepdims=True))
    lb = e[0:1] / jnp.sum(e, axis=0, keepdims=True)

    def chunks(seg):
        ref, first = (w_ref, seg * width) if seg < N_MAIN else (wgate_ref, (seg - N_MAIN) * width)
        for c in range(width // tn):
            cols = slice(c * tn, (c + 1) * tn)
            yield cols, jnp.dot(n, ref[:, first + c * tn:first + (c + 1) * tn], preferred_element_type=F32)

    def emit(seg, ref, fn):
        for cols, acc in chunks(seg):
            ref[:, cols] = fn(acc).astype(ref.dtype)

    @pl.when((pl.program_id(0) * x.shape[0]) % seq == 0)
    def _():
        carry_ref[...] = jnp.zeros_like(carry_ref)

    lf = _log_sigmoid(jnp.dot(n, wf_ref[...], preferred_element_type=F32) + fbias_ref[...])
    tm = lf.shape[0]
    lower = (lax.broadcasted_iota(jnp.int32, (tm, tm), 1)
             <= lax.broadcasted_iota(jnp.int32, (tm, tm), 0)).astype(BF16)
    cs = carry_ref[...]
    for piece in _split3(lf):
        cs = cs + jnp.dot(lower, piece, preferred_element_type=F32)
    carry_ref[...] = cs[tm - 1:tm, :]
    c2 = cs * LOG2E
    c_ref[...] = c2
    cp = one_ref[...]
    for i, piece in enumerate(_split3(c2)):
        cp = cp + jnp.dot(piece, spread_ref[i], preferred_element_type=F32)
    cp_ref[...] = cp.astype(cp_ref.dtype)

    emit(SEG_HQ, hq_ref, _silu)
    for cols, acc in chunks(SEG_HF):
        f = lb[:, cols] + (1.0 - lb[:, cols]) * _sigmoid(acc)
        lf_h = jnp.log(f)
        hlf_ref[:, cols] = lf_h
        hk_ref[:, cols] = (1.0 - f).astype(hk_ref.dtype)
        lfmin_ref[:, cols] = jnp.broadcast_to(jnp.min(lf_h, axis=0, keepdims=True), (lfmin_ref.shape[0], tn))
    emit(SEG_HI, hv_ref, lambda a: a)
    emit(SEG_HG, hg_ref, _silu)
    emit(SEG_FQ, fq_ref, lambda a: a * (FOX_DIM ** -0.5 * LOG2E))
    emit(SEG_FK, fk_ref, lambda a: a)
    emit(SEG_FV, fv_ref, lambda a: a)
    emit(SEG_GA, ga_ref, _sigmoid)
    emit(SEG_GB, gb_ref, _sigmoid)


CP_WIDTH = 4 * FOX_HEADS


def _in_proj(x2d, gain, w_main, w_gates, w_forget, lb_logits, f_bias, seq, tm, tn):
    t, d = x2d.shape
    heads = f_bias.shape[0]
    spread = np.zeros((3, heads, CP_WIDTH), np.float32)
    for i in range(3):
        spread[i, np.arange(heads), i * heads + np.arange(heads)] = 1.0
    one = np.zeros((1, CP_WIDTH), np.float32)
    one[0, 3 * heads] = 1.0
    once = lambda a: pl.BlockSpec(a.shape, lambda i: (0,) * a.ndim, pipeline_mode=pl.Buffered(1))
    row = lambda width: pl.BlockSpec((tm, width), lambda i: (i, 0))
    seg_dtypes = [BF16, F32, BF16, BF16, BF16, BF16, BF16, BF16, BF16, BF16]
    consts = (gain.reshape(1, d), w_main, w_gates, w_forget, lb_logits, f_bias.reshape(1, heads),
              jnp.asarray(spread, BF16), jnp.asarray(one))
    return pl.pallas_call(
        functools.partial(_in_proj_kernel, tn=tn, seq=seq),
        grid=(t // tm,),
        in_specs=[row(d)] + [once(a) for a in consts],
        out_specs=([row(d)] * len(seg_dtypes) + [row(heads), row(CP_WIDTH)]
                   + [pl.BlockSpec((None, SUBLANES, d), lambda i: (i, 0, 0))]),
        out_shape=([jax.ShapeDtypeStruct((t, d), dt) for dt in seg_dtypes]
                   + [jax.ShapeDtypeStruct((t, heads), F32), jax.ShapeDtypeStruct((t, CP_WIDTH), BF16),
                      jax.ShapeDtypeStruct((t // tm, SUBLANES, d), F32)]),
        scratch_shapes=[pltpu.VMEM((tm, d), BF16), pltpu.VMEM((1, heads), F32)],
        compiler_params=_cparams(("arbitrary",)),
        name="in_proj",
    )(x2d, *consts)


def _causal_chunk():
    r_i = lax.broadcasted_iota(jnp.int32, (HG_CHUNK, HG_CHUNK), 0)
    c_i = lax.broadcasted_iota(jnp.int32, (HG_CHUNK, HG_CHUNK), 1)
    return c_i <= r_i


def _chunk_cumsum(lf):
    lower = _causal_chunk().astype(BF16)
    b = jnp.zeros(lf.shape, F32)
    for piece in _split3(lf):
        b = b + jnp.dot(lower, piece, preferred_element_type=F32)
    return b


def _hgrn_chunk_single_ref(q, b, k, v, state):
    c = HG_CHUNK
    q_t = (q * jnp.exp(b)).astype(BF16)
    k_up = k * jnp.exp(-b)
    sc = lax.dot_general(q_t, k_up.astype(BF16), (((1,), (1,)), ((), ())), preferred_element_type=F32)
    sc = jnp.where(_causal_chunk(), sc, 0.0).astype(BF16)
    v_bf = v.astype(BF16)
    o = jnp.dot(jnp.concatenate([q_t, sc], axis=1), jnp.concatenate([state.astype(BF16), v_bf], axis=0),
                preferred_element_type=F32)
    b_end = b[c - 1:c]
    k_end = k_up * jnp.exp(b_end)
    tr = jnp.concatenate([k_end, jnp.broadcast_to(b_end, (HG_DIM - c, HG_DIM))], axis=0).T
    upd = jnp.dot(tr[:, :c].astype(BF16), v_bf, preferred_element_type=F32)
    return o, state * jnp.exp(tr[:, c:c + 1]) + upd


def _hgrn_chunk(q, b, k, v, state):
    c = HG_CHUNK
    n_sub = c // HG_SUB
    state_t = state.T

    qe = (q * jnp.exp(b)).astype(BF16)
    o = lax.dot_general(qe, state_t.astype(BF16), (((1,), (1,)), ((), ())), preferred_element_type=F32)

    o_parts = []
    t_iota = lax.broadcasted_iota(jnp.int32, (HG_SUB, HG_DIM), 0)
    for i in range(n_sub):
        lo = i * HG_SUB
        q_i = q[lo:lo + HG_SUB]
        b_i = b[lo:lo + HG_SUB]
        o_i = o[lo:lo + HG_SUB]
        if i > 0:
            ref = b[lo - 1:lo]
            q_t = (q_i * jnp.exp(b_i - ref)).astype(BF16)
            k_t = (k[:lo] * jnp.exp(ref - b[:lo])).astype(BF16)
            sc = lax.dot_general(q_t, k_t, (((1,), (1,)), ((), ())), preferred_element_type=F32)
            o_i = o_i + jnp.dot(sc.astype(BF16), v[:lo].astype(BF16), preferred_element_type=F32)
        for s in range(HG_SUB):
            row = lo + s
            rel = jnp.where(t_iota >= s, b_i - b[row:row + 1], NEG_INF)
            p = q_i * jnp.exp(rel) * k[row:row + 1]
            o_i = o_i + jnp.sum(p, axis=-1, keepdims=True) * v[row:row + 1]
        o_parts.append(o_i)
    o = jnp.concatenate(o_parts, axis=0)

    b_end = b[c - 1:c]
    k_dec = (k * jnp.exp(b_end - b)).astype(BF16)
    upd = lax.dot_general(v.astype(BF16), k_dec, (((0,), (0,)), ((), ())), preferred_element_type=F32)
    return o, (state_t * jnp.exp(b_end) + upd).T


def _hgrn_kernel(lfmin_ref, q_ref, lf_ref, k_ref, v_ref, g_ref, gn_ref, o_ref, state_ref, *, n_chunks):
    @pl.when(pl.program_id(2) == 0)
    def _():
        state_ref[...] = jnp.zeros_like(state_ref)

    def run_chunk(chunk_fn, rows, b, state):
        load = lambda ref: ref[rows, :].astype(F32)
        o, new_state = chunk_fn(load(q_ref), b, load(k_ref), load(v_ref), state)
        y = o * lax.rsqrt(jnp.mean(o * o, axis=-1, keepdims=True) + EPS)
        o_ref[rows, :] = (y * gn_ref[...] * load(g_ref)).astype(o_ref.dtype)
        return new_state

    lf_min = lfmin_ref[pl.program_id(0) * pl.num_programs(1) + pl.program_id(1), pl.program_id(2)]
    single_ref_ok = lf_min * HG_CHUNK > -HG_MAX_EXPONENT

    @pl.when(single_ref_ok)
    def _():
        chunk_rows = [pl.ds(ci * HG_CHUNK, HG_CHUNK) for ci in range(n_chunks)]
        b_all = _chunk_cumsum(jnp.concatenate([lf_ref[rows, :] for rows in chunk_rows], axis=1))
        state = state_ref[...]
        for ci, rows in enumerate(chunk_rows):
            state = run_chunk(_hgrn_chunk_single_ref, rows, b_all[:, ci * HG_DIM:(ci + 1) * HG_DIM], state)
        state_ref[...] = state

    @pl.when(jnp.logical_not(single_ref_ok))
    def _():
        def body(ci, carry):
            rows = pl.ds(pl.multiple_of(ci * HG_CHUNK, HG_CHUNK), HG_CHUNK)
            state_ref[...] = run_chunk(_hgrn_chunk, rows, _chunk_cumsum(lf_ref[rows, :]), state_ref[...])
            return carry

        lax.fori_loop(0, n_chunks, body, 0)


def _hgrn(lf_min, q, lf, k, v, g, g_norm, tc):
    b, s, w = q.shape
    heads = w // HG_DIM
    spec = pl.BlockSpec((None, tc, HG_DIM), lambda bi, hi, ti: (bi, ti, hi))
    return pl.pallas_call(
        functools.partial(_hgrn_kernel, n_chunks=tc // HG_CHUNK),
        grid=(b, heads, s // tc),
        in_specs=[pl.BlockSpec(memory_space=pltpu.SMEM)] + [spec] * 5
                 + [pl.BlockSpec((1, HG_DIM), lambda bi, hi, ti: (0, 0))],
        out_specs=spec,
        out_shape=jax.ShapeDtypeStruct((b, s, w), BF16),
        scratch_shapes=[pltpu.VMEM((HG_DIM, HG_DIM), F32)],
        compiler_params=_cparams(("parallel", "parallel", "arbitrary")),
        name="hgrn2_mixer",
    )(lf_min, q, lf, k, v, g, g_norm.reshape(1, HG_DIM))


AUG_C = FOX_DIM
AUG_ONE = FOX_DIM + 3
AUG_QN = FOX_DIM + 6
NORM_MARGIN = 1.02
SKIP_LOG2 = 150.0


def _placement_constants(heads):
    pq = np.zeros((CP_WIDTH + heads, heads * LANES), np.float32)
    pk = np.zeros((CP_WIDTH, heads * LANES), np.float32)
    hs = np.zeros((heads * FOX_DIM, heads), np.float32)
    for h in range(heads):
        for i in range(3):
            pq[i * heads + h, h * LANES + AUG_C + i] = 1.0
            pk[i * heads + h, h * LANES + AUG_ONE + i] = -1.0
        pq[3 * heads, h * LANES + AUG_ONE:h * LANES + AUG_ONE + 3] = 1.0
        pk[3 * heads, h * LANES + AUG_C:h * LANES + AUG_C + 3] = 1.0
        pq[CP_WIDTH + h, h * LANES + AUG_QN] = 1.0
        hs[h * FOX_DIM:(h + 1) * FOX_DIM, h] = 1.0
    return tuple(jnp.asarray(a, BF16) for a in (pq, pk, hs))


def _fox_pack_kernel(q_ref, k_ref, v_ref, cp_ref, c_ref, pq_ref, pk_ref, hs_ref, qa_ref, ka_ref, va_ref, st_ref):
    q = q_ref[...]
    k = k_ref[...]
    v = v_ref[...]
    cp = cp_ref[...]
    tm = q.shape[0]
    heads = qa_ref.shape[0]
    hs = hs_ref[...]
    qn = jnp.sqrt(jnp.dot(q * q, hs, preferred_element_type=F32)) * NORM_MARGIN
    kn = jnp.sqrt(jnp.dot(k * k, hs, preferred_element_type=F32)) * NORM_MARGIN
    tail_q = jnp.dot(jnp.concatenate([cp, qn.astype(BF16)], axis=1), pq_ref[...],
                     preferred_element_type=F32).astype(BF16)
    tail_k = jnp.dot(cp, pk_ref[...], preferred_element_type=F32).astype(BF16)
    lane = lax.broadcasted_iota(jnp.int32, (tm, LANES), 1)
    in_data = lane < FOX_DIM
    tail_v = jnp.where(lane == FOX_DIM, 1.0, 0.0).astype(BF16)
    for h in range(heads):
        pair = slice((h // 2) * LANES, (h // 2 + 1) * LANES)
        mine = slice(h * LANES, (h + 1) * LANES)
        data = (lambda x: x[:, pair]) if h % 2 == 0 else (lambda x: pltpu.roll(x[:, pair], FOX_DIM, 1))
        qa_ref[h] = jnp.where(in_data, data(q), tail_q[:, mine])
        ka_ref[h] = jnp.where(in_data, data(k), tail_k[:, mine])
        va_ref[h] = jnp.where(in_data, data(v), tail_v)
    st_ref[0:1, :] = jnp.max(kn, axis=0, keepdims=True)
    st_ref[1:2, :] = c_ref[tm - 1:tm, :]


def _fox_pack(q, k, v, cp, c, tm):
    b, s, w = q.shape
    heads = c.shape[-1]
    consts = _placement_constants(heads)
    row = lambda width: pl.BlockSpec((None, tm, width), lambda bi, ti: (bi, ti, 0))
    full = lambda a: pl.BlockSpec(a.shape, lambda bi, ti: (0, 0))
    out = pl.BlockSpec((None, heads, tm, LANES), lambda bi, ti: (bi, 0, ti, 0))
    oshape = jax.ShapeDtypeStruct((b, heads, s, LANES), BF16)
    return pl.pallas_call(
        _fox_pack_kernel,
        grid=(b, s // tm),
        in_specs=[row(w), row(w), row(w), row(CP_WIDTH), row(heads)] + [full(a) for a in consts],
        out_specs=[out] * 3 + [pl.BlockSpec((None, None, 2, heads), lambda bi, ti: (bi, ti, 0, 0))],
        out_shape=[oshape] * 3 + [jax.ShapeDtypeStruct((b, s // tm, 2, heads), F32)],
        compiler_params=_cparams(("parallel", "parallel")),
        name="fox_pack",
    )(q, k, v, cp, c, *consts)


def _fox_kernel(kmax_ref, clast_ref, q_ref, k_ref, v_ref, o_ref, m_ref, acc_ref, *, tq, tk):
    qi = pl.program_id(2)
    row_a = pl.program_id(0) * (2 * pl.num_programs(1)) + 2 * pl.program_id(1)
    m_ref[...] = jnp.full_like(m_ref, NEG_INF)
    acc_ref[...] = jnp.zeros_like(acc_ref)

    def scores(row0, n_rows, kv0, masked):
        rows = pl.ds(row0, n_rows)
        cols = pl.ds(kv0, tk)
        if masked:
            causal = (lax.broadcasted_iota(jnp.int32, (n_rows, tk), 1)
                      <= lax.broadcasted_iota(jnp.int32, (n_rows, tk), 0))
        out = []
        for a in range(2):
            s = lax.dot_general(q_ref[a, rows, :], k_ref[a, cols, :], (((1,), (1,)), ((), ())),
                                preferred_element_type=F32)
            out.append(jnp.where(causal, s, NEG_INF) if masked else s)
        return out

    def update(row0, n_rows, kv0, s_pair):
        rows = pl.ds(row0, n_rows)
        cols = pl.ds(kv0, tk)
        for a, s in enumerate(s_pair):
            m_prev = m_ref[a, rows, :]
            m_new = jnp.maximum(m_prev, jnp.max(s, axis=-1, keepdims=True))
            alpha = jnp.exp2(m_prev - m_new)
            p = jnp.exp2(s - jnp.tile(m_new, (1, tk // LANES)))
            acc_ref[a, rows, :] = alpha * acc_ref[a, rows, :] + jnp.dot(
                p.astype(BF16), v_ref[a, cols, :], preferred_element_type=F32)
            m_ref[a, rows, :] = m_new

    n_diag = tq // tk
    n_full = qi * n_diag

    def block(row0, n_rows, kv0, masked):
        update(row0, n_rows, kv0, scores(row0, n_rows, kv0, masked))

    diag = [(j * tk, tq - j * tk, pl.multiple_of((n_full + j) * tk, tk)) for j in range(n_diag)]
    s_next = scores(*diag[0], True)
    for j in range(n_diag):
        s_cur = s_next
        if j + 1 < n_diag:
            s_next = scores(*diag[j + 1], True)
        update(*diag[j], s_cur)

    n_tiles = kmax_ref.shape[1]
    slack = []
    for a in range(2):
        kmax = kmax_ref[row_a + a, 0]
        for j in range(1, n_tiles):
            kmax = jnp.maximum(kmax, kmax_ref[row_a + a, j])
        sel_row = lax.broadcasted_iota(jnp.int32, (LANES, LANES), 0)
        sel = jnp.where(sel_row == AUG_QN, kmax * NORM_MARGIN,
                        jnp.where((sel_row >= AUG_C) & (sel_row < AUG_ONE), 1.0, 0.0)).astype(BF16)
        gap = jnp.dot(q_ref[a], sel, preferred_element_type=F32) - m_ref[a]
        slack.append([jnp.max(gap[g * tk:(g + 1) * tk]) for g in range(n_diag)])

    def live_groups(j):
        n = jnp.int32(0)
        for a in range(2):
            c_last = clast_ref[row_a + a, jnp.maximum(j, 0)]
            for g in range(n_diag):
                n = jnp.where(slack[a][g] - c_last > -SKIP_LOG2, jnp.maximum(n, g + 1), n)
        return jnp.where(j >= 0, n, 0)

    def body(carry):
        j, n_live = carry
        kv0 = pl.multiple_of(j * tk, tk)
        for n in range(1, n_diag + 1):
            @pl.when(n_live == n)
            def _():
                block(0, n * tk, kv0, False)
        return j - 1, live_groups(j - 1)

    lax.while_loop(lambda carry: carry[1] > 0, body, (n_full - 1, live_groups(n_full - 1)))

    lane = lax.broadcasted_iota(jnp.int32, (tq, LANES), 1)
    acc_a = acc_ref[0]
    acc_b = acc_ref[1]
    o_a = acc_a / acc_a[:, FOX_DIM:FOX_DIM + 1]
    o_b = acc_b / acc_b[:, FOX_DIM:FOX_DIM + 1]
    o_ref[...] = jnp.where(lane < FOX_DIM, o_a, pltpu.roll(o_b, FOX_DIM, 1)).astype(o_ref.dtype)


def _fox_attention(q, k, v, kmax, clast, tq, tk):
    b, h, s, _ = q.shape
    kv_spec = pl.BlockSpec((None, 2, s, LANES), lambda bi, hi, qi: (bi, hi, 0, 0))
    smem = pl.BlockSpec(memory_space=pltpu.SMEM)
    return pl.pallas_call(
        functools.partial(_fox_kernel, tq=tq, tk=tk),
        grid=(b, h // 2, s // tq),
        in_specs=[smem, smem,
                  pl.BlockSpec((None, 2, tq, LANES), lambda bi, hi, qi: (bi, hi, qi, 0)), kv_spec, kv_spec],
        out_specs=pl.BlockSpec((None, tq, LANES), lambda bi, hi, qi: (bi, qi, hi)),
        out_shape=jax.ShapeDtypeStruct((b, s, (h // 2) * LANES), BF16),
        scratch_shapes=[pltpu.VMEM((2, tq, LANES), F32), pltpu.VMEM((2, tq, LANES), F32)],
        compiler_params=_cparams(("parallel", "parallel", "arbitrary")),
        name="fox_attention",
    )(kmax, clast, q, k, v)


def _merge_kernel(x_ref, oa_ref, ob_ref, ga_ref, gb_ref, wa_ref, wb_ref, wo_ref, gn_ref, h_ref, n_ref, mg_ref,
                  *, tn):
    oa = oa_ref[...]
    ob = ob_ref[...]
    for c in range(mg_ref.shape[1] // tn):
        cols = slice(c * tn, (c + 1) * tn)
        ya = jnp.dot(oa, wa_ref[:, cols], preferred_element_type=F32)
        yb = jnp.dot(ob, wb_ref[:, cols], preferred_element_type=F32)
        mg_ref[:, cols] = (ga_ref[:, cols] * ya + gb_ref[:, cols] * yb).astype(mg_ref.dtype)
    h = x_ref[...] + jnp.dot(mg_ref[...], wo_ref[...], preferred_element_type=F32)
    h_ref[...] = h
    y = h * lax.rsqrt(jnp.mean(h * h, axis=-1, keepdims=True) + EPS)
    n_ref[...] = (y * gn_ref[...]).astype(n_ref.dtype)


def _merge(x2d, oa, ob, ga, gb, wa, wb, wo, gain, tm, tn):
    t, d = x2d.shape
    row = lambda width: pl.BlockSpec((tm, width), lambda i: (i, 0))
    once = lambda a: pl.BlockSpec(a.shape, lambda i: (0,) * a.ndim, pipeline_mode=pl.Buffered(1))
    gain2 = gain.reshape(1, d)
    return pl.pallas_call(
        functools.partial(_merge_kernel, tn=tn),
        grid=(t // tm,),
        in_specs=[row(d), row(oa.shape[1]), row(ob.shape[1]), row(d), row(d), once(wa), once(wb), once(wo),
                  once(gain2)],
        out_specs=[row(d), row(d)],
        out_shape=[jax.ShapeDtypeStruct((t, d), F32), jax.ShapeDtypeStruct((t, d), BF16)],
        scratch_shapes=[pltpu.VMEM((tm, d), BF16)],
        compiler_params=_cparams(("parallel",)),
        name="merge_outproj",
    )(x2d, oa, ob, ga, gb, wa, wb, wo, gain2)


def _ffn_kernel(n_ref, halo_ref, h_ref, wu_ref, cw_ref, cb_ref, wd_ref, gn_ref, o_ref, next_ref, act_ref,
                *, tm, seq, tf):
    i = pl.program_id(0)
    at_start = (i * tm) % seq == 0
    next_ref[0:HALO, :] = jnp.where(at_start, jnp.zeros_like(halo_ref[...]), halo_ref[...])
    next_ref[HALO:, :] = n_ref[...]
    n_ext = next_ref[...]
    d_ff = act_ref.shape[1]

    def conv(col0):
        cols = pl.ds(col0, tf)
        u = jnp.dot(n_ext, wu_ref[:, cols], preferred_element_type=F32)
        out = cb_ref[:, cols]
        for tap in range(CONV_WIDTH):
            lo = HALO - (CONV_WIDTH - 1) + tap
            out = out + cw_ref[tap:tap + 1, cols] * u[lo:lo + tm]
        return out

    for c in range(d_ff // tf):
        gate = conv(c * tf)
        val = conv(d_ff + c * tf)
        act = 0.5 * gate * (1.0 + lax.erf(gate * (2.0 ** -0.5))) * val
        act_ref[:, c * tf:(c + 1) * tf] = act.astype(act_ref.dtype)

    h = h_ref[...] + jnp.dot(act_ref[...], wd_ref[...], preferred_element_type=F32)
    y = h * lax.rsqrt(jnp.mean(h * h, axis=-1, keepdims=True) + EPS)
    o_ref[...] = (y * gn_ref[...]).astype(o_ref.dtype)


def _ffn(n2d, h2d, w_up, conv_w, conv_b, w_down, gain, seq, tm, tf):
    t, d = n2d.shape
    d_ff = w_down.shape[0]
    halo_blocks = tm // HALO
    cb = conv_b.reshape(1, 2 * d_ff)
    once = lambda a: pl.BlockSpec(a.shape, lambda i: (0,) * a.ndim, pipeline_mode=pl.Buffered(1))
    return pl.pallas_call(
        functools.partial(_ffn_kernel, tm=tm, seq=seq, tf=tf),
        grid=(t // tm,),
        in_specs=[pl.BlockSpec((tm, d), lambda i: (i, 0)),
                  pl.BlockSpec((HALO, d), lambda i: (jnp.maximum(i * halo_blocks - 1, 0), 0)),
                  pl.BlockSpec((tm, d), lambda i: (i, 0)),
                  once(w_up), once(conv_w), once(cb), once(w_down),
                  pl.BlockSpec((1, d), lambda i: (0, 0))],
        out_specs=pl.BlockSpec((tm, d), lambda i: (i, 0)),
        out_shape=jax.ShapeDtypeStruct((t, d), F32),
        scratch_shapes=[pltpu.VMEM((HALO + tm, d), BF16), pltpu.VMEM((tm, d_ff), BF16)],
        compiler_params=_cparams(("parallel",)),
        name="conv_glu_ffn",
    )(n2d, n2d, h2d, w_up, conv_w, cb, w_down, gain.reshape(1, d))


class _Tiles(NamedTuple):
    rows: int
    cols: int
    hgrn: int
    query: int


def _tiles(seq):
    fit = lambda pref, n=seq: pref if n % pref == 0 else n
    rows = fit(512)
    return _Tiles(rows=rows, cols=256, hgrn=fit(2 * rows), query=fit(4 * rows))


def kernel(x, norm_mix, w_in, fox_f_bias, hg_lb_logits, hg_norm, w_branch_a, w_branch_b, w_out, norm_ffn, w_up,
           conv_w, conv_b, w_down, norm_final):
    b, s, d = x.shape
    assert norm_mix.shape[0] == 1, "single-layer trunk"
    t = b * s
    tiles = _tiles(s)
    assert HG_HEADS * HG_DIM == d and FOX_HEADS * FOX_DIM == d, "the projection kernel walks equal-width segments"
    f_lo, f_hi = N_MAIN * d, N_MAIN * d + FOX_HEADS
    assert w_in.shape[-1] == f_hi + 2 * d
    w_main, w_forget, w_gates = (w_in[0, :, lo:hi].astype(BF16) for lo, hi in ((0, f_lo), (f_lo, f_hi), (f_hi, None)))

    x2d = x.reshape(t, d)
    hq, hlf, hk, hv, hgate, fq, fk, fv, ga, gb, c2, c_pieces, lf_colmin = _in_proj(
        x2d, norm_mix[0], w_main, w_gates, w_forget, hg_lb_logits, fox_f_bias[0], s, tiles.rows, tiles.cols)

    r3 = lambda a: a.reshape(b, s, a.shape[-1])
    lf_min = lf_colmin[:, 0, :].reshape(b, s // tiles.hgrn, tiles.hgrn // tiles.rows, HG_HEADS, HG_DIM)
    lf_min = lf_min.min(axis=(2, 4)).transpose(0, 2, 1).reshape(b * HG_HEADS, s // tiles.hgrn)
    o_a = _hgrn(lf_min, r3(hq), r3(hlf), r3(hk), r3(hv), r3(hgate), hg_norm[0], tiles.hgrn)

    q_aug, k_aug, v_aug, kv_stats = _fox_pack(r3(fq), r3(fk), r3(fv), r3(c_pieces), r3(c2), tiles.rows)
    kv_stats = kv_stats.transpose(2, 0, 3, 1).reshape(2, b * FOX_HEADS, s // tiles.rows)
    o_b = _fox_attention(q_aug, k_aug, v_aug, kv_stats[0], kv_stats[1], tiles.query, tiles.rows)

    h1, n_ffn = _merge(x2d, o_a.reshape(t, d), o_b.reshape(t, d), ga, gb, w_branch_a[0].astype(BF16),
                       w_branch_b[0].astype(BF16), w_out[0].astype(BF16), norm_ffn[0], tiles.rows, tiles.cols)
    out = _ffn(n_ffn, h1, w_up[0].astype(BF16), conv_w[0], conv_b[0], w_down[0].astype(BF16), norm_final,
               s, tiles.rows, tiles.cols)
    return out.reshape(b, s, d)
```

```python
import functools
import math
from typing import NamedTuple

import jax
import jax.numpy as jnp
import numpy as np
from jax import lax
from jax.experimental import pallas as pl
from jax.experimental.pallas import tpu as pltpu

F32 = jnp.float32
BF16 = jnp.bfloat16

EPS = 1e-6
HG_HEADS = 8
HG_DIM = 128
HG_CHUNK = 64
HG_SUB = 16
HG_MAX_EXPONENT = 64.0
FOX_HEADS = 16
FOX_DIM = 64
CONV_WIDTH = 3
HALO = 8
NEG_INF = float("-inf")
LOG2E = math.log2(math.e)
LANES = 128
SUBLANES = 8

VMEM_LIMIT = 56 * 1024 * 1024


def _cparams(sem):
    return pltpu.CompilerParams(dimension_semantics=sem, vmem_limit_bytes=VMEM_LIMIT)


def _sigmoid(x):
    return 1.0 / (1.0 + jnp.exp(-x))


def _silu(x):
    return x * _sigmoid(x)


def _log_sigmoid(x):
    return jnp.minimum(x, 0.0) - jnp.log1p(jnp.exp(-jnp.abs(x)))


def _split3(x):
    x1 = x.astype(BF16)
    r1 = x - x1.astype(F32)
    x2 = r1.astype(BF16)
    x3 = (r1 - x2.astype(F32)).astype(BF16)
    return x1, x2, x3


SEG_HQ, SEG_HF, SEG_HI, SEG_HG, SEG_FQ, SEG_FK, SEG_FV, SEG_GA, SEG_GB = range(9)
N_MAIN = 7


def _in_proj_kernel(x_ref, gain_ref, w_ref, wgate_ref, wf_ref, lbl_ref, fbias_ref, spread_ref, one_ref,
                    hq_ref, hlf_ref, hk_ref, hv_ref, hg_ref, fq_ref, fk_ref, va_ref, ga_ref, gb_ref, c_ref, cp_ref,
                    lfmin_ref, n_ref, carry_ref, *, tn, seq):
    x = x_ref[...]
    n_ref[...] = (x * lax.rsqrt(jnp.mean(x * x, axis=-1, keepdims=True) + EPS) * gain_ref[...]).astype(BF16)
    n = n_ref[...]
    width = hq_ref.shape[1]
    lbl = lbl_ref[...]
    e = jnp.exp(lbl - jnp.max(lbl, axis=0, keepdims=True))
    lb = e[0:1] / jnp.sum(e, axis=0, keepdims=True)

    def chunks(seg):
        ref, first = (w_ref, seg * width) if seg < N_MAIN else (wgate_ref, (seg - N_MAIN) * width)
        for c in range(width // tn):
            cols = slice(c * tn, (c + 1) * tn)
            yield cols, jnp.dot(n, ref[:, first + c * tn:first + (c + 1) * tn], preferred_element_type=F32)

    def emit(seg, ref, fn):
        for cols, acc in chunks(seg):
            ref[:, cols] = fn(acc).astype(ref.dtype)

    @pl.when((pl.program_id(0) * x.shape[0]) % seq == 0)
    def _():
        carry_ref[...] = jnp.zeros_like(carry_ref)

    lf = _log_sigmoid(jnp.dot(n, wf_ref[...], preferred_element_type=F32) + fbias_ref[...])
    tm = lf.shape[0]
    lower = (lax.broadcasted_iota(jnp.int32, (tm, tm), 1)
             <= lax.broadcasted_iota(jnp.int32, (tm, tm), 0)).astype(BF16)
    cs = carry_ref[...]
    for piece in _split3(lf):
        cs = cs + jnp.dot(lower, piece, preferred_element_type=F32)
    carry_ref[...] = cs[tm - 1:tm, :]
    c2 = cs * LOG2E
    c_ref[...] = c2
    cp = one_ref[...]
    for i, piece in enumerate(_split3(c2)):
        cp = cp + jnp.dot(piece, spread_ref[i], preferred_element_type=F32)
    cp_ref[...] = cp.astype(cp_ref.dtype)

    emit(SEG_HQ, hq_ref, _silu)
    for cols, acc in chunks(SEG_HF):
        f = lb[:, cols] + (1.0 - lb[:, cols]) * _sigmoid(acc)
        lf_h = jnp.log(f)
        hlf_ref[:, cols] = lf_h
        hk_ref[:, cols] = (1.0 - f).astype(hk_ref.dtype)
        lfmin_ref[:, cols] = jnp.broadcast_to(jnp.min(lf_h, axis=0, keepdims=True), (lfmin_ref.shape[0], tn))
    emit(SEG_HI, hv_ref, lambda a: a)
    emit(SEG_HG, hg_ref, _silu)
    emit(SEG_FQ, fq_ref, lambda a: a * (FOX_DIM ** -0.5 * LOG2E))
    emit(SEG_FK, fk_ref, lambda a: a)
    lane = lax.broadcasted_iota(jnp.int32, (tm, LANES), 1)
    tail_v = jnp.where(lane == FOX_DIM, 1.0, 0.0)
    for cols, acc in chunks(SEG_FV):
        for j in range(tn // FOX_DIM):
            pair = acc[:, (j // 2) * LANES:(j // 2 + 1) * LANES]
            data = pair if j % 2 == 0 else pltpu.roll(pair, FOX_DIM, 1)
            va_ref[cols.start // FOX_DIM + j] = jnp.where(lane < FOX_DIM, data, tail_v).astype(va_ref.dtype)
    emit(SEG_GA, ga_ref, _sigmoid)
    emit(SEG_GB, gb_ref, _sigmoid)


CP_WIDTH = 4 * FOX_HEADS


def _in_proj(x2d, gain, w_main, w_gates, w_forget, lb_logits, f_bias, seq, tm, tn):
    t, d = x2d.shape
    heads = f_bias.shape[0]
    spread = np.zeros((3, heads, CP_WIDTH), np.float32)
    for i in range(3):
        spread[i, np.arange(heads), i * heads + np.arange(heads)] = 1.0
    one = np.zeros((1, CP_WIDTH), np.float32)
    one[0, 3 * heads] = 1.0
    once = lambda a: pl.BlockSpec(a.shape, lambda i: (0,) * a.ndim, pipeline_mode=pl.Buffered(1))
    row = lambda width: pl.BlockSpec((tm, width), lambda i: (i, 0))
    flat = lambda dt: (row(d), jax.ShapeDtypeStruct((t, d), dt))
    tiles_per_seq = seq // tm
    v_aug = (pl.BlockSpec((None, heads, tm, LANES), lambda i: (i // tiles_per_seq, 0, i % tiles_per_seq, 0)),
             jax.ShapeDtypeStruct((t // seq, heads, seq, LANES), BF16))
    outs = [flat(BF16), flat(F32), flat(BF16), flat(BF16), flat(BF16),
            flat(BF16), flat(BF16), v_aug, flat(BF16), flat(BF16),
            (row(heads), jax.ShapeDtypeStruct((t, heads), F32)),
            (row(CP_WIDTH), jax.ShapeDtypeStruct((t, CP_WIDTH), BF16)),
            (pl.BlockSpec((None, SUBLANES, d), lambda i: (i, 0, 0)),
             jax.ShapeDtypeStruct((t // tm, SUBLANES, d), F32))]
    consts = (gain.reshape(1, d), w_main, w_gates, w_forget, lb_logits, f_bias.reshape(1, heads),
              jnp.asarray(spread, BF16), jnp.asarray(one))
    return pl.pallas_call(
        functools.partial(_in_proj_kernel, tn=tn, seq=seq),
        grid=(t // tm,),
        in_specs=[row(d)] + [once(a) for a in consts],
        out_specs=[spec for spec, _ in outs],
        out_shape=[shape for _, shape in outs],
        scratch_shapes=[pltpu.VMEM((tm, d), BF16), pltpu.VMEM((1, heads), F32)],
        compiler_params=_cparams(("arbitrary",)),
        name="in_proj",
    )(x2d, *consts)


def _causal_chunk():
    r_i = lax.broadcasted_iota(jnp.int32, (HG_CHUNK, HG_CHUNK), 0)
    c_i = lax.broadcasted_iota(jnp.int32, (HG_CHUNK, HG_CHUNK), 1)
    return c_i <= r_i


def _chunk_cumsum(lf):
    lower = _causal_chunk().astype(BF16)
    b = jnp.zeros(lf.shape, F32)
    for piece in _split3(lf):
        b = b + jnp.dot(lower, piece, preferred_element_type=F32)
    return b


def _hgrn_chunk_single_ref(q, b, k, v, state):
    c = HG_CHUNK
    q_t = (q * jnp.exp(b)).astype(BF16)
    k_up = k * jnp.exp(-b)
    sc = lax.dot_general(q_t, k_up.astype(BF16), (((1,), (1,)), ((), ())), preferred_element_type=F32)
    sc = jnp.where(_causal_chunk(), sc, 0.0).astype(BF16)
    v_bf = v.astype(BF16)
    o = jnp.dot(jnp.concatenate([q_t, sc], axis=1), jnp.concatenate([state.astype(BF16), v_bf], axis=0),
                preferred_element_type=F32)
    b_end = b[c - 1:c]
    k_end = k_up * jnp.exp(b_end)
    tr = jnp.concatenate([k_end, jnp.broadcast_to(b_end, (HG_DIM - c, HG_DIM))], axis=0).T
    upd = jnp.dot(tr[:, :c].astype(BF16), v_bf, preferred_element_type=F32)
    return o, state * jnp.exp(tr[:, c:c + 1]) + upd


def _hgrn_chunk(q, b, k, v, state):
    c = HG_CHUNK
    n_sub = c // HG_SUB
    state_t = state.T

    qe = (q * jnp.exp(b)).astype(BF16)
    o = lax.dot_general(qe, state_t.astype(BF16), (((1,), (1,)), ((), ())), preferred_element_type=F32)

    o_parts = []
    t_iota = lax.broadcasted_iota(jnp.int32, (HG_SUB, HG_DIM), 0)
    for i in range(n_sub):
        lo = i * HG_SUB
        q_i = q[lo:lo + HG_SUB]
        b_i = b[lo:lo + HG_SUB]
        o_i = o[lo:lo + HG_SUB]
        if i > 0:
            ref = b[lo - 1:lo]
            q_t = (q_i * jnp.exp(b_i - ref)).astype(BF16)
            k_t = (k[:lo] * jnp.exp(ref - b[:lo])).astype(BF16)
            sc = lax.dot_general(q_t, k_t, (((1,), (1,)), ((), ())), preferred_element_type=F32)
            o_i = o_i + jnp.dot(sc.astype(BF16), v[:lo].astype(BF16), preferred_element_type=F32)
        for s in range(HG_SUB):
            row = lo + s
            rel = jnp.where(t_iota >= s, b_i - b[row:row + 1], NEG_INF)
            p = q_i * jnp.exp(rel) * k[row:row + 1]
            o_i = o_i + jnp.sum(p, axis=-1, keepdims=True) * v[row:row + 1]
        o_parts.append(o_i)
    o = jnp.concatenate(o_parts, axis=0)

    b_end = b[c - 1:c]
    k_dec = (k * jnp.exp(b_end - b)).astype(BF16)
    upd = lax.dot_general(v.astype(BF16), k_dec, (((0,), (0,)), ((), ())), preferred_element_type=F32)
    return o, (state_t * jnp.exp(b_end) + upd).T


def _hgrn_kernel(lfmin_ref, q_ref, lf_ref, k_ref, v_ref, g_ref, gn_ref, o_ref, state_ref, *, n_chunks):
    @pl.when(pl.program_id(2) == 0)
    def _():
        state_ref[...] = jnp.zeros_like(state_ref)

    def run_chunk(chunk_fn, rows, b, state):
        load = lambda ref: ref[rows, :].astype(F32)
        o, new_state = chunk_fn(load(q_ref), b, load(k_ref), load(v_ref), state)
        y = o * lax.rsqrt(jnp.mean(o * o, axis=-1, keepdims=True) + EPS)
        o_ref[rows, :] = (y * gn_ref[...] * load(g_ref)).astype(o_ref.dtype)
        return new_state

    lf_min = lfmin_ref[pl.program_id(0) * pl.num_programs(1) + pl.program_id(1), pl.program_id(2)]
    single_ref_ok = lf_min * HG_CHUNK > -HG_MAX_EXPONENT

    @pl.when(single_ref_ok)
    def _():
        chunk_rows = [pl.ds(ci * HG_CHUNK, HG_CHUNK) for ci in range(n_chunks)]
        b_all = _chunk_cumsum(jnp.concatenate([lf_ref[rows, :] for rows in chunk_rows], axis=1))
        state = state_ref[...]
        for ci, rows in enumerate(chunk_rows):
            state = run_chunk(_hgrn_chunk_single_ref, rows, b_all[:, ci * HG_DIM:(ci + 1) * HG_DIM], state)
        state_ref[...] = state

    @pl.when(jnp.logical_not(single_ref_ok))
    def _():
        def body(ci, carry):
            rows = pl.ds(pl.multiple_of(ci * HG_CHUNK, HG_CHUNK), HG_CHUNK)
            state_ref[...] = run_chunk(_hgrn_chunk, rows, _chunk_cumsum(lf_ref[rows, :]), state_ref[...])
            return carry

        lax.fori_loop(0, n_chunks, body, 0)


def _hgrn(lf_min, q, lf, k, v, g, g_norm, tc):
    b, s, w = q.shape
    heads = w // HG_DIM
    spec = pl.BlockSpec((None, tc, HG_DIM), lambda bi, hi, ti: (bi, ti, hi))
    return pl.pallas_call(
        functools.partial(_hgrn_kernel, n_chunks=tc // HG_CHUNK),
        grid=(b, heads, s // tc),
        in_specs=[pl.BlockSpec(memory_space=pltpu.SMEM)] + [spec] * 5
                 + [pl.BlockSpec((1, HG_DIM), lambda bi, hi, ti: (0, 0))],
        out_specs=spec,
        out_shape=jax.ShapeDtypeStruct((b, s, w), BF16),
        scratch_shapes=[pltpu.VMEM((HG_DIM, HG_DIM), F32)],
        compiler_params=_cparams(("parallel", "parallel", "arbitrary")),
        name="hgrn2_mixer",
    )(lf_min, q, lf, k, v, g, g_norm.reshape(1, HG_DIM))


AUG_C = FOX_DIM
AUG_ONE = FOX_DIM + 3
AUG_QN = FOX_DIM + 6
NORM_MARGIN = 1.02
SKIP_LOG2 = 150.0


def _placement_constants(heads):
    pq = np.zeros((CP_WIDTH + heads, heads * LANES), np.float32)
    pk = np.zeros((CP_WIDTH, heads * LANES), np.float32)
    hs = np.zeros((heads * FOX_DIM, heads), np.float32)
    for h in range(heads):
        for i in range(3):
            pq[i * heads + h, h * LANES + AUG_C + i] = 1.0
            pk[i * heads + h, h * LANES + AUG_ONE + i] = -1.0
        pq[3 * heads, h * LANES + AUG_ONE:h * LANES + AUG_ONE + 3] = 1.0
        pk[3 * heads, h * LANES + AUG_C:h * LANES + AUG_C + 3] = 1.0
        pq[CP_WIDTH + h, h * LANES + AUG_QN] = 1.0
        hs[h * FOX_DIM:(h + 1) * FOX_DIM, h] = 1.0
    return tuple(jnp.asarray(a, BF16) for a in (pq, pk, hs))


def _fox_pack_kernel(q_ref, k_ref, cp_ref, c_ref, pq_ref, pk_ref, hs_ref, qa_ref, ka_ref, st_ref):
    q = q_ref[...]
    k = k_ref[...]
    cp = cp_ref[...]
    tm = q.shape[0]
    heads = qa_ref.shape[0]
    hs = hs_ref[...]
    qn = jnp.sqrt(jnp.dot(q * q, hs, preferred_element_type=F32)) * NORM_MARGIN
    kn = jnp.sqrt(jnp.dot(k * k, hs, preferred_element_type=F32)) * NORM_MARGIN
    tail_q = jnp.dot(jnp.concatenate([cp, qn.astype(BF16)], axis=1), pq_ref[...],
                     preferred_element_type=F32).astype(BF16)
    tail_k = jnp.dot(cp, pk_ref[...], preferred_element_type=F32).astype(BF16)
    lane = lax.broadcasted_iota(jnp.int32, (tm, LANES), 1)
    in_data = lane < FOX_DIM
    for h in range(heads):
        pair = slice((h // 2) * LANES, (h // 2 + 1) * LANES)
        mine = slice(h * LANES, (h + 1) * LANES)
        data = (lambda x: x[:, pair]) if h % 2 == 0 else (lambda x: pltpu.roll(x[:, pair], FOX_DIM, 1))
        qa_ref[h] = jnp.where(in_data, data(q), tail_q[:, mine])
        ka_ref[h] = jnp.where(in_data, data(k), tail_k[:, mine])
    st_ref[0:1, :] = jnp.max(kn, axis=0, keepdims=True)
    st_ref[1:2, :] = c_ref[tm - 1:tm, :]


def _fox_pack(q, k, cp, c, tm):
    b, s, w = q.shape
    heads = c.shape[-1]
    consts = _placement_constants(heads)
    row = lambda width: pl.BlockSpec((None, tm, width), lambda bi, ti: (bi, ti, 0))
    full = lambda a: pl.BlockSpec(a.shape, lambda bi, ti: (0, 0))
    out = pl.BlockSpec((None, heads, tm, LANES), lambda bi, ti: (bi, 0, ti, 0))
    oshape = jax.ShapeDtypeStruct((b, heads, s, LANES), BF16)
    return pl.pallas_call(
        _fox_pack_kernel,
        grid=(b, s // tm),
        in_specs=[row(w), row(w), row(CP_WIDTH), row(heads)] + [full(a) for a in consts],
        out_specs=[out] * 2 + [pl.BlockSpec((None, None, 2, heads), lambda bi, ti: (bi, ti, 0, 0))],
        out_shape=[oshape] * 2 + [jax.ShapeDtypeStruct((b, s // tm, 2, heads), F32)],
        compiler_params=_cparams(("parallel", "parallel")),
        name="fox_pack",
    )(q, k, cp, c, *consts)


def _fox_kernel(kmax_ref, clast_ref, q_ref, k_ref, v_ref, o_ref, m_ref, acc_ref, *, tq, tk):
    qi = pl.program_id(2)
    row_a = pl.program_id(0) * (2 * pl.num_programs(1)) + 2 * pl.program_id(1)
    m_ref[...] = jnp.full_like(m_ref, NEG_INF)
    acc_ref[...] = jnp.zeros_like(acc_ref)

    def scores(row0, n_rows, kv0, masked):
        rows = pl.ds(row0, n_rows)
        cols = pl.ds(kv0, tk)
        if masked:
            causal = (lax.broadcasted_iota(jnp.int32, (n_rows, tk), 1)
                      <= lax.broadcasted_iota(jnp.int32, (n_rows, tk), 0))
        out = []
        for a in range(2):
            s = lax.dot_general(q_ref[a, rows, :], k_ref[a, cols, :], (((1,), (1,)), ((), ())),
                                preferred_element_type=F32)
            out.append(jnp.where(causal, s, NEG_INF) if masked else s)
        return out

    def update(row0, n_rows, kv0, s_pair):
        rows = pl.ds(row0, n_rows)
        cols = pl.ds(kv0, tk)
        for a, s in enumerate(s_pair):
            m_prev = m_ref[a, rows, :]
            m_new = jnp.maximum(m_prev, jnp.max(s, axis=-1, keepdims=True))
            alpha = jnp.exp2(m_prev - m_new)
            p = jnp.exp2(s - jnp.tile(m_new, (1, tk // LANES)))
            acc_ref[a, rows, :] = alpha * acc_ref[a, rows, :] + jnp.dot(
                p.astype(BF16), v_ref[a, cols, :], preferred_element_type=F32)
            m_ref[a, rows, :] = m_new

    n_diag = tq // tk
    n_full = qi * n_diag

    def block(row0, n_rows, kv0, masked):
        update(row0, n_rows, kv0, scores(row0, n_rows, kv0, masked))

    diag = [(j * tk, tq - j * tk, pl.multiple_of((n_full + j) * tk, tk)) for j in range(n_diag)]
    s_next = scores(*diag[0], True)
    for j in range(n_diag):
        s_cur = s_next
        if j + 1 < n_diag:
            s_next = scores(*diag[j + 1], True)
        update(*diag[j], s_cur)

    n_tiles = kmax_ref.shape[1]
    slack = []
    for a in range(2):
        kmax = kmax_ref[row_a + a, 0]
        for j in range(1, n_tiles):
            kmax = jnp.maximum(kmax, kmax_ref[row_a + a, j])
        sel_row = lax.broadcasted_iota(jnp.int32, (LANES, LANES), 0)
        sel = jnp.where(sel_row == AUG_QN, kmax * NORM_MARGIN,
                        jnp.where((sel_row >= AUG_C) & (sel_row < AUG_ONE), 1.0, 0.0)).astype(BF16)
        gap = jnp.dot(q_ref[a], sel, preferred_element_type=F32) - m_ref[a]
        slack.append([jnp.max(gap[g * tk:(g + 1) * tk]) for g in range(n_diag)])

    def live_groups(j):
        n = jnp.int32(0)
        for a in range(2):
            c_last = clast_ref[row_a + a, jnp.maximum(j, 0)]
            for g in range(n_diag):
                n = jnp.where(slack[a][g] - c_last > -SKIP_LOG2, jnp.maximum(n, g + 1), n)
        return jnp.where(j >= 0, n, 0)

    def body(carry):
        j, n_live = carry
        kv0 = pl.multiple_of(j * tk, tk)
        for n in range(1, n_diag + 1):
            @pl.when(n_live == n)
            def _():
                block(0, n * tk, kv0, False)
        return j - 1, live_groups(j - 1)

    lax.while_loop(lambda carry: carry[1] > 0, body, (n_full - 1, live_groups(n_full - 1)))

    lane = lax.broadcasted_iota(jnp.int32, (tq, LANES), 1)
    acc_a = acc_ref[0]
    acc_b = acc_ref[1]
    o_a = acc_a / acc_a[:, FOX_DIM:FOX_DIM + 1]
    o_b = acc_b / acc_b[:, FOX_DIM:FOX_DIM + 1]
    o_ref[...] = jnp.where(lane < FOX_DIM, o_a, pltpu.roll(o_b, FOX_DIM, 1)).astype(o_ref.dtype)


def _fox_attention(q, k, v, kmax, clast, tq, tk):
    b, h, s, _ = q.shape
    kv_spec = pl.BlockSpec((None, 2, s, LANES), lambda bi, hi, qi: (bi, hi, 0, 0))
    smem = pl.BlockSpec(memory_space=pltpu.SMEM)
    return pl.pallas_call(
        functools.partial(_fox_kernel, tq=tq, tk=tk),
        grid=(b, h // 2, s // tq),
        in_specs=[smem, smem,
                  pl.BlockSpec((None, 2, tq, LANES), lambda bi, hi, qi: (bi, hi, qi, 0)), kv_spec, kv_spec],
        out_specs=pl.BlockSpec((None, tq, LANES), lambda bi, hi, qi: (bi, qi, hi)),
        out_shape=jax.ShapeDtypeStruct((b, s, (h // 2) * LANES), BF16),
        scratch_shapes=[pltpu.VMEM((2, tq, LANES), F32), pltpu.VMEM((2, tq, LANES), F32)],
        compiler_params=_cparams(("parallel", "parallel", "arbitrary")),
        name="fox_attention",
    )(kmax, clast, q, k, v)


def _merge_kernel(x_ref, oa_ref, ob_ref, ga_ref, gb_ref, wa_ref, wb_ref, wo_ref, gn_ref, h_ref, n_ref, mg_ref,
                  *, tn):
    oa = oa_ref[...]
    ob = ob_ref[...]
    for c in range(mg_ref.shape[1] // tn):
        cols = slice(c * tn, (c + 1) * tn)
        ya = jnp.dot(oa, wa_ref[:, cols], preferred_element_type=F32)
        yb = jnp.dot(ob, wb_ref[:, cols], preferred_element_type=F32)
        mg_ref[:, cols] = (ga_ref[:, cols] * ya + gb_ref[:, cols] * yb).astype(mg_ref.dtype)
    h = x_ref[...] + jnp.dot(mg_ref[...], wo_ref[...], preferred_element_type=F32)
    h_ref[...] = h
    y = h * lax.rsqrt(jnp.mean(h * h, axis=-1, keepdims=True) + EPS)
    n_ref[...] = (y * gn_ref[...]).astype(n_ref.dtype)


def _merge(x2d, oa, ob, ga, gb, wa, wb, wo, gain, tm, tn):
    t, d = x2d.shape
    row = lambda width: pl.BlockSpec((tm, width), lambda i: (i, 0))
    once = lambda a: pl.BlockSpec(a.shape, lambda i: (0,) * a.ndim, pipeline_mode=pl.Buffered(1))
    gain2 = gain.reshape(1, d)
    return pl.pallas_call(
        functools.partial(_merge_kernel, tn=tn),
        grid=(t // tm,),
        in_specs=[row(d), row(oa.shape[1]), row(ob.shape[1]), row(d), row(d), once(wa), once(wb), once(wo),
                  once(gain2)],
        out_specs=[row(d), row(d)],
        out_shape=[jax.ShapeDtypeStruct((t, d), F32), jax.ShapeDtypeStruct((t, d), BF16)],
        scratch_shapes=[pltpu.VMEM((tm, d), BF16)],
        compiler_params=_cparams(("parallel",)),
        name="merge_outproj",
    )(x2d, oa, ob, ga, gb, wa, wb, wo, gain2)


def _ffn_kernel(n_ref, halo_ref, h_ref, wu_ref, cw_ref, cb_ref, wd_ref, gn_ref, o_ref, next_ref, act_ref,
                *, tm, seq, tf):
    i = pl.program_id(0)
    at_start = (i * tm) % seq == 0
    next_ref[0:HALO, :] = jnp.where(at_start, jnp.zeros_like(halo_ref[...]), halo_ref[...])
    next_ref[HALO:, :] = n_ref[...]
    n_ext = next_ref[...]
    d_ff = act_ref.shape[1]

    def conv(col0):
        cols = pl.ds(col0, tf)
        u = jnp.dot(n_ext, wu_ref[:, cols], preferred_element_type=F32)
        out = cb_ref[:, cols]
        for tap in range(CONV_WIDTH):
            lo = HALO - (CONV_WIDTH - 1) + tap
            out = out + cw_ref[tap:tap + 1, cols] * u[lo:lo + tm]
        return out

    for c in range(d_ff // tf):
        gate = conv(c * tf)
        val = conv(d_ff + c * tf)
        act = 0.5 * gate * (1.0 + lax.erf(gate * (2.0 ** -0.5))) * val
        act_ref[:, c * tf:(c + 1) * tf] = act.astype(act_ref.dtype)

    h = h_ref[...] + jnp.dot(act_ref[...], wd_ref[...], preferred_element_type=F32)
    y = h * lax.rsqrt(jnp.mean(h * h, axis=-1, keepdims=True) + EPS)
    o_ref[...] = (y * gn_ref[...]).astype(o_ref.dtype)


def _ffn(n2d, h2d, w_up, conv_w, conv_b, w_down, gain, seq, tm, tf):
    t, d = n2d.shape
    d_ff = w_down.shape[0]
    halo_blocks = tm // HALO
    cb = conv_b.reshape(1, 2 * d_ff)
    once = lambda a: pl.BlockSpec(a.shape, lambda i: (0,) * a.ndim, pipeline_mode=pl.Buffered(1))
    return pl.pallas_call(
        functools.partial(_ffn_kernel, tm=tm, seq=seq, tf=tf),
        grid=(t // tm,),
        in_specs=[pl.BlockSpec((tm, d), lambda i: (i, 0)),
                  pl.BlockSpec((HALO, d), lambda i: (jnp.maximum(i * halo_blocks - 1, 0), 0)),
                  pl.BlockSpec((tm, d), lambda i: (i, 0)),
                  once(w_up), once(conv_w), once(cb), once(w_down),
                  pl.BlockSpec((1, d), lambda i: (0, 0))],
        out_specs=pl.BlockSpec((tm, d), lambda i: (i, 0)),
        out_shape=jax.ShapeDtypeStruct((t, d), F32),
        scratch_shapes=[pltpu.VMEM((HALO + tm, d), BF16), pltpu.VMEM((tm, d_ff), BF16)],
        compiler_params=_cparams(("parallel",)),
        name="conv_glu_ffn",
    )(n2d, n2d, h2d, w_up, conv_w, cb, w_down, gain.reshape(1, d))


class _Tiles(NamedTuple):
    rows: int
    cols: int
    hgrn: int
    query: int


def _tiles(seq):
    fit = lambda pref, n=seq: pref if n % pref == 0 else n
    rows = fit(512)
    return _Tiles(rows=rows, cols=256, hgrn=fit(2 * rows), query=fit(4 * rows))


def kernel(x, norm_mix, w_in, fox_f_bias, hg_lb_logits, hg_norm, w_branch_a, w_branch_b, w_out, norm_ffn, w_up,
           conv_w, conv_b, w_down, norm_final):
    b, s, d = x.shape
    assert norm_mix.shape[0] == 1, "single-layer trunk"
    t = b * s
    tiles = _tiles(s)
    assert HG_HEADS * HG_DIM == d and FOX_HEADS * FOX_DIM == d, "the projection kernel walks equal-width segments"
    f_lo, f_hi = N_MAIN * d, N_MAIN * d + FOX_HEADS
    assert w_in.shape[-1] == f_hi + 2 * d
    w_main, w_forget, w_gates = (w_in[0, :, lo:hi].astype(BF16) for lo, hi in ((0, f_lo), (f_lo, f_hi), (f_hi, None)))

    x2d = x.reshape(t, d)
    hq, hlf, hk, hv, hgate, fq, fk, v_aug, ga, gb, c2, c_pieces, lf_colmin = _in_proj(
        x2d, norm_mix[0], w_main, w_gates, w_forget, hg_lb_logits, fox_f_bias[0], s, tiles.rows, tiles.cols)

    r3 = lambda a: a.reshape(b, s, a.shape[-1])
    lf_min = lf_colmin[:, 0, :].reshape(b, s // tiles.hgrn, tiles.hgrn // tiles.rows, HG_HEADS, HG_DIM)
    lf_min = lf_min.min(axis=(2, 4)).transpose(0, 2, 1).reshape(b * HG_HEADS, s // tiles.hgrn)
    o_a = _hgrn(lf_min, r3(hq), r3(hlf), r3(hk), r3(hv), r3(hgate), hg_norm[0], tiles.hgrn)

    q_aug, k_aug, kv_stats = _fox_pack(r3(fq), r3(fk), r3(c_pieces), r3(c2), tiles.rows)
    kv_stats = kv_stats.transpose(2, 0, 3, 1).reshape(2, b * FOX_HEADS, s // tiles.rows)
    o_b = _fox_attention(q_aug, k_aug, v_aug, kv_stats[0], kv_stats[1], tiles.query, tiles.rows)

    h1, n_ffn = _merge(x2d, o_a.reshape(t, d), o_b.reshape(t, d), ga, gb, w_branch_a[0].astype(BF16),
                       w_branch_b[0].astype(BF16), w_out[0].astype(BF16), norm_ffn[0], tiles.rows, tiles.cols)
    out = _ffn(n_ffn, h1, w_up[0].astype(BF16), conv_w[0], conv_b[0], w_down[0].astype(BF16), norm_final,
               s, tiles.rows, tiles.cols)
    return out.reshape(b, s, d)
```

```python
import functools
import math
from typing import NamedTuple

import jax
import jax.numpy as jnp
import numpy as np
from jax import lax
from jax.experimental import pallas as pl
from jax.experimental.pallas import tpu as pltpu

F32 = jnp.float32
BF16 = jnp.bfloat16

EPS = 1e-6
HG_HEADS = 8
HG_DIM = 128
HG_CHUNK = 64
HG_SUB = 16
HG_MAX_EXPONENT = 64.0
FOX_HEADS = 16
FOX_DIM = 64
CONV_WIDTH = 3
HALO = 8
NEG_INF = float("-inf")
LOG2E = math.log2(math.e)
LANES = 128
SUBLANES = 8

VMEM_LIMIT = 56 * 1024 * 1024


def _cparams(sem):
    return pltpu.CompilerParams(dimension_semantics=sem, vmem_limit_bytes=VMEM_LIMIT)


def _sigmoid(x):
    return 1.0 / (1.0 + jnp.exp(-x))


def _silu(x):
    return x * _sigmoid(x)


def _log_sigmoid(x):
    return jnp.minimum(x, 0.0) - jnp.log1p(jnp.exp(-jnp.abs(x)))


def _split3(x):
    x1 = x.astype(BF16)
    r1 = x - x1.astype(F32)
    x2 = r1.astype(BF16)
    x3 = (r1 - x2.astype(F32)).astype(BF16)
    return x1, x2, x3


SEG_HQ, SEG_HF, SEG_HI, SEG_HG, SEG_FQ, SEG_FK, SEG_FV, SEG_GA, SEG_GB = range(9)
N_MAIN = 7


def _in_proj_kernel(x_ref, gain_ref, w_ref, wgate_ref, wf_ref, lbl_ref, fbias_ref, spread_ref, one_ref,
                    hq_ref, hlf_ref, hk_ref, hv_ref, hg_ref, fq_ref, fk_ref, va_ref, ga_ref, gb_ref, c_ref, cp_ref,
                    lfmin_ref, n_ref, carry_ref, *, tn, seq):
    x = x_ref[...]
    n_ref[...] = (x * lax.rsqrt(jnp.mean(x * x, axis=-1, keepdims=True) + EPS) * gain_ref[...]).astype(BF16)
    n = n_ref[...]
    width = hq_ref.shape[1]
    lbl = lbl_ref[...]
    e = jnp.exp(lbl - jnp.max(lbl, axis=0, keepdims=True))
    lb = e[0:1] / jnp.sum(e, axis=0, keepdims=True)

    def chunks(seg):
        ref, first = (w_ref, seg * width) if seg < N_MAIN else (wgate_ref, (seg - N_MAIN) * width)
        for c in range(width // tn):
            cols = slice(c * tn, (c + 1) * tn)
            yield cols, jnp.dot(n, ref[:, first + c * tn:first + (c + 1) * tn], preferred_element_type=F32)

    def emit(seg, ref, fn):
        for cols, acc in chunks(seg):
            ref[:, cols] = fn(acc).astype(ref.dtype)

    @pl.when((pl.program_id(0) * x.shape[0]) % seq == 0)
    def _():
        carry_ref[...] = jnp.zeros_like(carry_ref)

    lf = _log_sigmoid(jnp.dot(n, wf_ref[...], preferred_element_type=F32) + fbias_ref[...])
    tm = lf.shape[0]
    lower = (lax.broadcasted_iota(jnp.int32, (tm, tm), 1)
             <= lax.broadcasted_iota(jnp.int32, (tm, tm), 0)).astype(BF16)
    cs = carry_ref[...]
    for piece in _split3(lf):
        cs = cs + jnp.dot(lower, piece, preferred_element_type=F32)
    carry_ref[...] = cs[tm - 1:tm, :]
    c2 = cs * LOG2E
    c_ref[...] = c2
    cp = one_ref[...]
    for i, piece in enumerate(_split3(c2)):
        cp = cp + jnp.dot(piece, spread_ref[i], preferred_element_type=F32)
    cp_ref[...] = cp.astype(cp_ref.dtype)

    emit(SEG_HQ, hq_ref, _silu)
    for cols, acc in chunks(SEG_HF):
        f = lb[:, cols] + (1.0 - lb[:, cols]) * _sigmoid(acc)
        lf_h = jnp.log(f)
        hlf_ref[:, cols] = lf_h
        hk_ref[:, cols] = (1.0 - f).astype(hk_ref.dtype)
        lfmin_ref[:, cols] = jnp.broadcast_to(jnp.min(lf_h, axis=0, keepdims=True), (lfmin_ref.shape[0], tn))
    emit(SEG_HI, hv_ref, lambda a: a)
    emit(SEG_HG, hg_ref, _silu)
    emit(SEG_FQ, fq_ref, lambda a: a * (FOX_DIM ** -0.5 * LOG2E))
    emit(SEG_FK, fk_ref, lambda a: a)
    lane = lax.broadcasted_iota(jnp.int32, (tm, LANES), 1)
    tail_v = jnp.where(lane == FOX_DIM, 1.0, 0.0)
    for cols, acc in chunks(SEG_FV):
        for j in range(tn // FOX_DIM):
            pair = acc[:, (j // 2) * LANES:(j // 2 + 1) * LANES]
            data = pair if j % 2 == 0 else pltpu.roll(pair, FOX_DIM, 1)
            va_ref[cols.start // FOX_DIM + j] = jnp.where(lane < FOX_DIM, data, tail_v).astype(va_ref.dtype)
    emit(SEG_GA, ga_ref, _sigmoid)
    emit(SEG_GB, gb_ref, _sigmoid)


CP_WIDTH = 4 * FOX_HEADS


def _in_proj(x2d, gain, w_main, w_gates, w_forget, lb_logits, f_bias, seq, tm, tn):
    t, d = x2d.shape
    heads = f_bias.shape[0]
    spread = np.zeros((3, heads, CP_WIDTH), np.float32)
    for i in range(3):
        spread[i, np.arange(heads), i * heads + np.arange(heads)] = 1.0
    one = np.zeros((1, CP_WIDTH), np.float32)
    one[0, 3 * heads] = 1.0
    once = lambda a: pl.BlockSpec(a.shape, lambda i: (0,) * a.ndim, pipeline_mode=pl.Buffered(1))
    row = lambda width: pl.BlockSpec((tm, width), lambda i: (i, 0))
    flat = lambda dt: (row(d), jax.ShapeDtypeStruct((t, d), dt))
    tiles_per_seq = seq // tm
    v_aug = (pl.BlockSpec((None, heads, tm, LANES), lambda i: (i // tiles_per_seq, 0, i % tiles_per_seq, 0)),
             jax.ShapeDtypeStruct((t // seq, heads, seq, LANES), BF16))
    outs = [flat(BF16), flat(F32), flat(BF16), flat(BF16), flat(BF16),
            flat(BF16), flat(BF16), v_aug, flat(BF16), flat(BF16),
            (row(heads), jax.ShapeDtypeStruct((t, heads), F32)),
            (row(CP_WIDTH), jax.ShapeDtypeStruct((t, CP_WIDTH), BF16)),
            (pl.BlockSpec((None, SUBLANES, d), lambda i: (i, 0, 0)),
             jax.ShapeDtypeStruct((t // tm, SUBLANES, d), F32))]
    consts = (gain.reshape(1, d), w_main, w_gates, w_forget, lb_logits, f_bias.reshape(1, heads),
              jnp.asarray(spread, BF16), jnp.asarray(one))
    return pl.pallas_call(
        functools.partial(_in_proj_kernel, tn=tn, seq=seq),
        grid=(t // tm,),
        in_specs=[row(d)] + [once(a) for a in consts],
        out_specs=[spec for spec, _ in outs],
        out_shape=[shape for _, shape in outs],
        scratch_shapes=[pltpu.VMEM((tm, d), BF16), pltpu.VMEM((1, heads), F32)],
        compiler_params=_cparams(("arbitrary",)),
        name="in_proj",
    )(x2d, *consts)


def _causal_chunk():
    r_i = lax.broadcasted_iota(jnp.int32, (HG_CHUNK, HG_CHUNK), 0)
    c_i = lax.broadcasted_iota(jnp.int32, (HG_CHUNK, HG_CHUNK), 1)
    return c_i <= r_i


def _chunk_cumsum(lf):
    lower = _causal_chunk().astype(BF16)
    b = jnp.zeros(lf.shape, F32)
    for piece in _split3(lf):
        b = b + jnp.dot(lower, piece, preferred_element_type=F32)
    return b


def _hgrn_chunk_single_ref(q, b, k, v, state):
    c = HG_CHUNK
    q_t = (q * jnp.exp(b)).astype(BF16)
    k_up = k * jnp.exp(-b)
    sc = lax.dot_general(q_t, k_up.astype(BF16), (((1,), (1,)), ((), ())), preferred_element_type=F32)
    sc = jnp.where(_causal_chunk(), sc, 0.0).astype(BF16)
    v_bf = v.astype(BF16)
    o = jnp.dot(jnp.concatenate([q_t, sc], axis=1), jnp.concatenate([state.astype(BF16), v_bf], axis=0),
                preferred_element_type=F32)
    b_end = b[c - 1:c]
    k_end = k_up * jnp.exp(b_end)
    tr = jnp.concatenate([k_end, jnp.broadcast_to(b_end, (HG_DIM - c, HG_DIM))], axis=0).T
    upd = jnp.dot(tr[:, :c].astype(BF16), v_bf, preferred_element_type=F32)
    return o, state * jnp.exp(tr[:, c:c + 1]) + upd


def _hgrn_chunk(q, b, k, v, state):
    c = HG_CHUNK
    n_sub = c // HG_SUB
    state_t = state.T

    qe = (q * jnp.exp(b)).astype(BF16)
    o = lax.dot_general(qe, state_t.astype(BF16), (((1,), (1,)), ((), ())), preferred_element_type=F32)

    o_parts = []
    t_iota = lax.broadcasted_iota(jnp.int32, (HG_SUB, HG_DIM), 0)
    for i in range(n_sub):
        lo = i * HG_SUB
        q_i = q[lo:lo + HG_SUB]
        b_i = b[lo:lo + HG_SUB]
        o_i = o[lo:lo + HG_SUB]
        if i > 0:
            ref = b[lo - 1:lo]
            q_t = (q_i * jnp.exp(b_i - ref)).astype(BF16)
            k_t = (k[:lo] * jnp.exp(ref - b[:lo])).astype(BF16)
            sc = lax.dot_general(q_t, k_t, (((1,), (1,)), ((), ())), preferred_element_type=F32)
            o_i = o_i + jnp.dot(sc.astype(BF16), v[:lo].astype(BF16), preferred_element_type=F32)
        for s in range(HG_SUB):
            row = lo + s
            rel = jnp.where(t_iota >= s, b_i - b[row:row + 1], NEG_INF)
            p = q_i * jnp.exp(rel) * k[row:row + 1]
            o_i = o_i + jnp.sum(p, axis=-1, keepdims=True) * v[row:row + 1]
        o_parts.append(o_i)
    o = jnp.concatenate(o_parts, axis=0)

    b_end = b[c - 1:c]
    k_dec = (k * jnp.exp(b_end - b)).astype(BF16)
    upd = lax.dot_general(v.astype(BF16), k_dec, (((0,), (0,)), ((), ())), preferred_element_type=F32)
    return o, (state_t * jnp.exp(b_end) + upd).T


def _hgrn_kernel(lfmin_ref, q_ref, lf_ref, k_ref, v_ref, g_ref, gn_ref, o_ref, state_ref, *, n_chunks):
    @pl.when(pl.program_id(2) == 0)
    def _():
        state_ref[...] = jnp.zeros_like(state_ref)

    def run_chunk(chunk_fn, rows, b, state):
        load = lambda ref: ref[rows, :].astype(F32)
        o, new_state = chunk_fn(load(q_ref), b, load(k_ref), load(v_ref), state)
        y = o * lax.rsqrt(jnp.mean(o * o, axis=-1, keepdims=True) + EPS)
        o_ref[rows, :] = (y * gn_ref[...] * load(g_ref)).astype(o_ref.dtype)
        return new_state

    lf_min = lfmin_ref[pl.program_id(0) * pl.num_programs(1) + pl.program_id(1), pl.program_id(2)]
    single_ref_ok = lf_min * HG_CHUNK > -HG_MAX_EXPONENT

    @pl.when(single_ref_ok)
    def _():
        chunk_rows = [pl.ds(ci * HG_CHUNK, HG_CHUNK) for ci in range(n_chunks)]
        b_all = _chunk_cumsum(jnp.concatenate([lf_ref[rows, :] for rows in chunk_rows], axis=1))
        state = state_ref[...]
        for ci, rows in enumerate(chunk_rows):
            state = run_chunk(_hgrn_chunk_single_ref, rows, b_all[:, ci * HG_DIM:(ci + 1) * HG_DIM], state)
        state_ref[...] = state

    @pl.when(jnp.logical_not(single_ref_ok))
    def _():
        def body(ci, carry):
            rows = pl.ds(pl.multiple_of(ci * HG_CHUNK, HG_CHUNK), HG_CHUNK)
            state_ref[...] = run_chunk(_hgrn_chunk, rows, _chunk_cumsum(lf_ref[rows, :]), state_ref[...])
            return carry

        lax.fori_loop(0, n_chunks, body, 0)


def _hgrn(lf_min, q, lf, k, v, g, g_norm, tc):
    b, s, w = q.shape
    heads = w // HG_DIM
    spec = pl.BlockSpec((None, tc, HG_DIM), lambda bi, hi, ti: (bi, ti, hi))
    return pl.pallas_call(
        functools.partial(_hgrn_kernel, n_chunks=tc // HG_CHUNK),
        grid=(b, heads, s // tc),
        in_specs=[pl.BlockSpec(memory_space=pltpu.SMEM)] + [spec] * 5
                 + [pl.BlockSpec((1, HG_DIM), lambda bi, hi, ti: (0, 0))],
        out_specs=spec,
        out_shape=jax.ShapeDtypeStruct((b, s, w), BF16),
        scratch_shapes=[pltpu.VMEM((HG_DIM, HG_DIM), F32)],
        compiler_params=_cparams(("parallel", "parallel", "arbitrary")),
        name="hgrn2_mixer",
    )(lf_min, q, lf, k, v, g, g_norm.reshape(1, HG_DIM))


AUG_C = FOX_DIM
AUG_ONE = FOX_DIM + 3
AUG_QN = FOX_DIM + 6
NORM_MARGIN = 1.02
SKIP_LOG2 = 150.0


def _placement_constants(heads):
    pq = np.zeros((CP_WIDTH + heads, heads * LANES), np.float32)
    pk = np.zeros((CP_WIDTH, heads * LANES), np.float32)
    hs = np.zeros((heads * FOX_DIM, heads), np.float32)
    for h in range(heads):
        for i in range(3):
            pq[i * heads + h, h * LANES + AUG_C + i] = 1.0
            pk[i * heads + h, h * LANES + AUG_ONE + i] = -1.0
        pq[3 * heads, h * LANES + AUG_ONE:h * LANES + AUG_ONE + 3] = 1.0
        pk[3 * heads, h * LANES + AUG_C:h * LANES + AUG_C + 3] = 1.0
        pq[CP_WIDTH + h, h * LANES + AUG_QN] = 1.0
        hs[h * FOX_DIM:(h + 1) * FOX_DIM, h] = 1.0
    return tuple(jnp.asarray(a, BF16) for a in (pq, pk, hs))


def _fox_pack_kernel(q_ref, k_ref, cp_ref, c_ref, pq_ref, pk_ref, hs_ref, qa_ref, ka_ref, st_ref):
    q = q_ref[...]
    k = k_ref[...]
    cp = cp_ref[...]
    tm = q.shape[0]
    heads = qa_ref.shape[0]
    hs = hs_ref[...]
    qn = jnp.sqrt(jnp.dot(q * q, hs, preferred_element_type=F32)) * NORM_MARGIN
    kn = jnp.sqrt(jnp.dot(k * k, hs, preferred_element_type=F32)) * NORM_MARGIN
    tail_q = jnp.dot(jnp.concatenate([cp, qn.astype(BF16)], axis=1), pq_ref[...],
                     preferred_element_type=F32).astype(BF16)
    tail_k = jnp.dot(cp, pk_ref[...], preferred_element_type=F32).astype(BF16)
    lane = lax.broadcasted_iota(jnp.int32, (tm, LANES), 1)
    in_data = lane < FOX_DIM
    for h in range(heads):
        pair = slice((h // 2) * LANES, (h // 2 + 1) * LANES)
        mine = slice(h * LANES, (h + 1) * LANES)
        data = (lambda x: x[:, pair]) if h % 2 == 0 else (lambda x: pltpu.roll(x[:, pair], FOX_DIM, 1))
        qa_ref[h] = jnp.where(in_data, data(q), tail_q[:, mine])
        ka_ref[h] = jnp.where(in_data, data(k), tail_k[:, mine])
    st_ref[0:1, :] = jnp.max(kn, axis=0, keepdims=True)
    st_ref[1:2, :] = c_ref[tm - 1:tm, :]


def _fox_pack(q, k, cp, c, tm):
    b, s, w = q.shape
    heads = c.shape[-1]
    consts = _placement_constants(heads)
    row = lambda width: pl.BlockSpec((None, tm, width), lambda bi, ti: (bi, ti, 0))
    full = lambda a: pl.BlockSpec(a.shape, lambda bi, ti: (0, 0))
    out = pl.BlockSpec((None, heads, tm, LANES), lambda bi, ti: (bi, 0, ti, 0))
    oshape = jax.ShapeDtypeStruct((b, heads, s, LANES), BF16)
    return pl.pallas_call(
        _fox_pack_kernel,
        grid=(b, s // tm),
        in_specs=[row(w), row(w), row(CP_WIDTH), row(heads)] + [full(a) for a in consts],
        out_specs=[out] * 2 + [pl.BlockSpec((None, None, 2, heads), lambda bi, ti: (bi, ti, 0, 0))],
        out_shape=[oshape] * 2 + [jax.ShapeDtypeStruct((b, s // tm, 2, heads), F32)],
        compiler_params=_cparams(("parallel", "parallel")),
        name="fox_pack",
    )(q, k, cp, c, *consts)


def _fox_kernel(kmax_ref, clast_ref, q_ref, k_ref, v_ref, o_ref, m_ref, acc_ref, *, tq, tk, group):
    qi = pl.program_id(2)
    row_a = pl.program_id(0) * (2 * pl.num_programs(1)) + 2 * pl.program_id(1)
    m_ref[...] = jnp.full_like(m_ref, NEG_INF)
    acc_ref[...] = jnp.zeros_like(acc_ref)

    def scores(row0, n_rows, kv0, masked):
        rows = pl.ds(row0, n_rows)
        cols = pl.ds(kv0, tk)
        if masked:
            causal = (lax.broadcasted_iota(jnp.int32, (n_rows, tk), 1)
                      <= lax.broadcasted_iota(jnp.int32, (n_rows, tk), 0))
        out = []
        for a in range(2):
            s = lax.dot_general(q_ref[a, rows, :], k_ref[a, cols, :], (((1,), (1,)), ((), ())),
                                preferred_element_type=F32)
            out.append(jnp.where(causal, s, NEG_INF) if masked else s)
        return out

    def update(row0, n_rows, kv0, s_pair):
        rows = pl.ds(row0, n_rows)
        cols = pl.ds(kv0, tk)
        for a, s in enumerate(s_pair):
            m_prev = m_ref[a, rows, :]
            m_new = jnp.maximum(m_prev, jnp.max(s, axis=-1, keepdims=True))
            alpha = jnp.exp2(m_prev - m_new)
            p = jnp.exp2(s - jnp.tile(m_new, (1, tk // LANES)))
            acc_ref[a, rows, :] = alpha * acc_ref[a, rows, :] + jnp.dot(
                p.astype(BF16), v_ref[a, cols, :], preferred_element_type=F32)
            m_ref[a, rows, :] = m_new

    n_diag = tq // tk
    n_full = qi * n_diag

    def block(row0, n_rows, kv0, masked):
        update(row0, n_rows, kv0, scores(row0, n_rows, kv0, masked))

    diag = [(j * tk, tq - j * tk, pl.multiple_of((n_full + j) * tk, tk)) for j in range(n_diag)]
    s_next = scores(*diag[0], True)
    for j in range(n_diag):
        s_cur = s_next
        if j + 1 < n_diag:
            s_next = scores(*diag[j + 1], True)
        update(*diag[j], s_cur)

    n_tiles = kmax_ref.shape[1]
    n_groups = tq // group
    slack = []
    for a in range(2):
        kmax = kmax_ref[row_a + a, 0]
        for j in range(1, n_tiles):
            kmax = jnp.maximum(kmax, kmax_ref[row_a + a, j])
        sel_row = lax.broadcasted_iota(jnp.int32, (LANES, LANES), 0)
        sel = jnp.where(sel_row == AUG_QN, kmax * NORM_MARGIN,
                        jnp.where((sel_row >= AUG_C) & (sel_row < AUG_ONE), 1.0, 0.0)).astype(BF16)
        gap = jnp.dot(q_ref[a], sel, preferred_element_type=F32) - m_ref[a]
        slack.append([jnp.max(gap[g * group:(g + 1) * group]) for g in range(n_groups)])

    def live_groups(j):
        n = jnp.int32(0)
        for a in range(2):
            c_last = clast_ref[row_a + a, jnp.maximum(j, 0)]
            for g in range(n_groups):
                n = jnp.where(slack[a][g] - c_last > -SKIP_LOG2, jnp.maximum(n, g + 1), n)
        return jnp.where(j >= 0, n, 0)

    def body(carry):
        j, n_live = carry
        kv0 = pl.multiple_of(j * tk, tk)
        for n in range(1, n_groups + 1):
            @pl.when(n_live == n)
            def _():
                block(0, n * group, kv0, False)
        return j - 1, live_groups(j - 1)

    lax.while_loop(lambda carry: carry[1] > 0, body, (n_full - 1, live_groups(n_full - 1)))

    lane = lax.broadcasted_iota(jnp.int32, (tq, LANES), 1)
    acc_a = acc_ref[0]
    acc_b = acc_ref[1]
    o_a = acc_a / acc_a[:, FOX_DIM:FOX_DIM + 1]
    o_b = acc_b / acc_b[:, FOX_DIM:FOX_DIM + 1]
    o_ref[...] = jnp.where(lane < FOX_DIM, o_a, pltpu.roll(o_b, FOX_DIM, 1)).astype(o_ref.dtype)


def _fox_attention(q, k, v, kmax, clast, tq, tk, group):
    b, h, s, _ = q.shape
    kv_spec = pl.BlockSpec((None, 2, s, LANES), lambda bi, hi, qi: (bi, hi, 0, 0))
    smem = pl.BlockSpec(memory_space=pltpu.SMEM)
    return pl.pallas_call(
        functools.partial(_fox_kernel, tq=tq, tk=tk, group=group),
        grid=(b, h // 2, s // tq),
        in_specs=[smem, smem,
                  pl.BlockSpec((None, 2, tq, LANES), lambda bi, hi, qi: (bi, hi, qi, 0)), kv_spec, kv_spec],
        out_specs=pl.BlockSpec((None, tq, LANES), lambda bi, hi, qi: (bi, qi, hi)),
        out_shape=jax.ShapeDtypeStruct((b, s, (h // 2) * LANES), BF16),
        scratch_shapes=[pltpu.VMEM((2, tq, LANES), F32), pltpu.VMEM((2, tq, LANES), F32)],
        compiler_params=_cparams(("parallel", "parallel", "arbitrary")),
        name="fox_attention",
    )(kmax, clast, q, k, v)


def _merge_kernel(x_ref, oa_ref, ob_ref, ga_ref, gb_ref, wa_ref, wb_ref, wo_ref, gn_ref, h_ref, n_ref, mg_ref,
                  *, tn):
    oa = oa_ref[...]
    ob = ob_ref[...]
    for c in range(mg_ref.shape[1] // tn):
        cols = slice(c * tn, (c + 1) * tn)
        ya = jnp.dot(oa, wa_ref[:, cols], preferred_element_type=F32)
        yb = jnp.dot(ob, wb_ref[:, cols], preferred_element_type=F32)
        mg_ref[:, cols] = (ga_ref[:, cols] * ya + gb_ref[:, cols] * yb).astype(mg_ref.dtype)
    h = x_ref[...] + jnp.dot(mg_ref[...], wo_ref[...], preferred_element_type=F32)
    h_ref[...] = h
    y = h * lax.rsqrt(jnp.mean(h * h, axis=-1, keepdims=True) + EPS)
    n_ref[...] = (y * gn_ref[...]).astype(n_ref.dtype)


def _merge(x2d, oa, ob, ga, gb, wa, wb, wo, gain, tm, tn):
    t, d = x2d.shape
    row = lambda width: pl.BlockSpec((tm, width), lambda i: (i, 0))
    once = lambda a: pl.BlockSpec(a.shape, lambda i: (0,) * a.ndim, pipeline_mode=pl.Buffered(1))
    gain2 = gain.reshape(1, d)
    return pl.pallas_call(
        functools.partial(_merge_kernel, tn=tn),
        grid=(t // tm,),
        in_specs=[row(d), row(oa.shape[1]), row(ob.shape[1]), row(d), row(d), once(wa), once(wb), once(wo),
                  once(gain2)],
        out_specs=[row(d), row(d)],
        out_shape=[jax.ShapeDtypeStruct((t, d), F32), jax.ShapeDtypeStruct((t, d), BF16)],
        scratch_shapes=[pltpu.VMEM((tm, d), BF16)],
        compiler_params=_cparams(("parallel",)),
        name="merge_outproj",
    )(x2d, oa, ob, ga, gb, wa, wb, wo, gain2)


def _ffn_kernel(n_ref, halo_ref, h_ref, wu_ref, cw_ref, cb_ref, wd_ref, gn_ref, o_ref, next_ref, act_ref,
                *, tm, seq, tf):
    i = pl.program_id(0)
    at_start = (i * tm) % seq == 0
    next_ref[0:HALO, :] = jnp.where(at_start, jnp.zeros_like(halo_ref[...]), halo_ref[...])
    next_ref[HALO:, :] = n_ref[...]
    n_ext = next_ref[...]
    d_ff = act_ref.shape[1]

    def conv(col0):
        cols = pl.ds(col0, tf)
        u = jnp.dot(n_ext, wu_ref[:, cols], preferred_element_type=F32)
        out = cb_ref[:, cols]
        for tap in range(CONV_WIDTH):
            lo = HALO - (CONV_WIDTH - 1) + tap
            out = out + cw_ref[tap:tap + 1, cols] * u[lo:lo + tm]
        return out

    for c in range(d_ff // tf):
        gate = conv(c * tf)
        val = conv(d_ff + c * tf)
        act = 0.5 * gate * (1.0 + lax.erf(gate * (2.0 ** -0.5))) * val
        act_ref[:, c * tf:(c + 1) * tf] = act.astype(act_ref.dtype)

    h = h_ref[...] + jnp.dot(act_ref[...], wd_ref[...], preferred_element_type=F32)
    y = h * lax.rsqrt(jnp.mean(h * h, axis=-1, keepdims=True) + EPS)
    o_ref[...] = (y * gn_ref[...]).astype(o_ref.dtype)


def _ffn(n2d, h2d, w_up, conv_w, conv_b, w_down, gain, seq, tm, tf):
    t, d = n2d.shape
    d_ff = w_down.shape[0]
    halo_blocks = tm // HALO
    cb = conv_b.reshape(1, 2 * d_ff)
    once = lambda a: pl.BlockSpec(a.shape, lambda i: (0,) * a.ndim, pipeline_mode=pl.Buffered(1))
    return pl.pallas_call(
        functools.partial(_ffn_kernel, tm=tm, seq=seq, tf=tf),
        grid=(t // tm,),
        in_specs=[pl.BlockSpec((tm, d), lambda i: (i, 0)),
                  pl.BlockSpec((HALO, d), lambda i: (jnp.maximum(i * halo_blocks - 1, 0), 0)),
                  pl.BlockSpec((tm, d), lambda i: (i, 0)),
                  once(w_up), once(conv_w), once(cb), once(w_down),
                  pl.BlockSpec((1, d), lambda i: (0, 0))],
        out_specs=pl.BlockSpec((tm, d), lambda i: (i, 0)),
        out_shape=jax.ShapeDtypeStruct((t, d), F32),
        scratch_shapes=[pltpu.VMEM((HALO + tm, d), BF16), pltpu.VMEM((tm, d_ff), BF16)],
        compiler_params=_cparams(("parallel",)),
        name="conv_glu_ffn",
    )(n2d, n2d, h2d, w_up, conv_w, cb, w_down, gain.reshape(1, d))


class _Tiles(NamedTuple):
    rows: int
    cols: int
    hgrn: int
    query: int
    group: int


def _tiles(seq):
    fit = lambda pref, n=seq: pref if n % pref == 0 else n
    rows = fit(512)
    return _Tiles(rows=rows, cols=256, hgrn=fit(2 * rows), query=fit(4 * rows), group=fit(256, rows))


def kernel(x, norm_mix, w_in, fox_f_bias, hg_lb_logits, hg_norm, w_branch_a, w_branch_b, w_out, norm_ffn, w_up,
           conv_w, conv_b, w_down, norm_final):
    b, s, d = x.shape
    assert norm_mix.shape[0] == 1, "single-layer trunk"
    t = b * s
    tiles = _tiles(s)
    assert HG_HEADS * HG_DIM == d and FOX_HEADS * FOX_DIM == d, "the projection kernel walks equal-width segments"
    f_lo, f_hi = N_MAIN * d, N_MAIN * d + FOX_HEADS
    assert w_in.shape[-1] == f_hi + 2 * d
    w_main, w_forget, w_gates = (w_in[0, :, lo:hi].astype(BF16) for lo, hi in ((0, f_lo), (f_lo, f_hi), (f_hi, None)))

    x2d = x.reshape(t, d)
    hq, hlf, hk, hv, hgate, fq, fk, v_aug, ga, gb, c2, c_pieces, lf_colmin = _in_proj(
        x2d, norm_mix[0], w_main, w_gates, w_forget, hg_lb_logits, fox_f_bias[0], s, tiles.rows, tiles.cols)

    r3 = lambda a: a.reshape(b, s, a.shape[-1])
    lf_min = lf_colmin[:, 0, :].reshape(b, s // tiles.hgrn, tiles.hgrn // tiles.rows, HG_HEADS, HG_DIM)
    lf_min = lf_min.min(axis=(2, 4)).transpose(0, 2, 1).reshape(b * HG_HEADS, s // tiles.hgrn)
    o_a = _hgrn(lf_min, r3(hq), r3(hlf), r3(hk), r3(hv), r3(hgate), hg_norm[0], tiles.hgrn)

    q_aug, k_aug, kv_stats = _fox_pack(r3(fq), r3(fk), r3(c_pieces), r3(c2), tiles.rows)
    kv_stats = kv_stats.transpose(2, 0, 3, 1).reshape(2, b * FOX_HEADS, s // tiles.rows)
    o_b = _fox_attention(q_aug, k_aug, v_aug, kv_stats[0], kv_stats[1], tiles.query, tiles.rows,
                          tiles.group)

    h1, n_ffn = _merge(x2d, o_a.reshape(t, d), o_b.reshape(t, d), ga, gb, w_branch_a[0].astype(BF16),
                       w_branch_b[0].astype(BF16), w_out[0].astype(BF16), norm_ffn[0], tiles.rows, tiles.cols)
    out = _ffn(n_ffn, h1, w_up[0].astype(BF16), conv_w[0], conv_b[0], w_down[0].astype(BF16), norm_final,
               s, tiles.rows, tiles.cols)
    return out.reshape(b, s, d)
```

```python
import functools
import math
from typing import NamedTuple

import jax
import jax.numpy as jnp
import numpy as np
from jax import lax
from jax.experimental import pallas as pl
from jax.experimental.pallas import tpu as pltpu

F32 = jnp.float32
BF16 = jnp.bfloat16

EPS = 1e-6
HG_HEADS = 8
HG_DIM = 128
HG_CHUNK = 64
HG_SUB = 16
HG_MAX_EXPONENT = 64.0
FOX_HEADS = 16
FOX_DIM = 64
CONV_WIDTH = 3
HALO = 8
NEG_INF = float("-inf")
LOG2E = math.log2(math.e)
LANES = 128
SUBLANES = 8

VMEM_LIMIT = 56 * 1024 * 1024


def _cparams(sem):
    return pltpu.CompilerParams(dimension_semantics=sem, vmem_limit_bytes=VMEM_LIMIT)


def _sigmoid(x):
    return 1.0 / (1.0 + jnp.exp(-x))


def _silu(x):
    return x * _sigmoid(x)


def _log_sigmoid(x):
    return jnp.minimum(x, 0.0) - jnp.log1p(jnp.exp(-jnp.abs(x)))


def _split3(x):
    x1 = x.astype(BF16)
    r1 = x - x1.astype(F32)
    x2 = r1.astype(BF16)
    x3 = (r1 - x2.astype(F32)).astype(BF16)
    return x1, x2, x3


SEG_HQ, SEG_HF, SEG_HI, SEG_HG, SEG_FQ, SEG_FK, SEG_FV, SEG_GA, SEG_GB = range(9)
N_MAIN = 7


def _in_proj_kernel(x_ref, gain_ref, w_ref, wgate_ref, wf_ref, lbl_ref, fbias_ref, spread_ref, one_ref,
                    hq_ref, hlf_ref, hk_ref, hv_ref, hg_ref, fq_ref, fk_ref, va_ref, ga_ref, gb_ref, c_ref, cp_ref,
                    lfmin_ref, n_ref, carry_ref, *, tn, seq):
    x = x_ref[...]
    n_ref[...] = (x * lax.rsqrt(jnp.mean(x * x, axis=-1, keepdims=True) + EPS) * gain_ref[...]).astype(BF16)
    n = n_ref[...]
    width = hq_ref.shape[1]
    lbl = lbl_ref[...]
    e = jnp.exp(lbl - jnp.max(lbl, axis=0, keepdims=True))
    lb = e[0:1] / jnp.sum(e, axis=0, keepdims=True)

    def chunks(seg):
        ref, first = (w_ref, seg * width) if seg < N_MAIN else (wgate_ref, (seg - N_MAIN) * width)
        for c in range(width // tn):
            cols = slice(c * tn, (c + 1) * tn)
            yield cols, jnp.dot(n, ref[:, first + c * tn:first + (c + 1) * tn], preferred_element_type=F32)

    def emit(seg, ref, fn):
        for cols, acc in chunks(seg):
            ref[:, cols] = fn(acc).astype(ref.dtype)

    @pl.when((pl.program_id(0) * x.shape[0]) % seq == 0)
    def _():
        carry_ref[...] = jnp.zeros_like(carry_ref)

    lf = _log_sigmoid(jnp.dot(n, wf_ref[...], preferred_element_type=F32) + fbias_ref[...])
    tm = lf.shape[0]
    lower = (lax.broadcasted_iota(jnp.int32, (tm, tm), 1)
             <= lax.broadcasted_iota(jnp.int32, (tm, tm), 0)).astype(BF16)
    cs = carry_ref[...]
    for piece in _split3(lf):
        cs = cs + jnp.dot(lower, piece, preferred_element_type=F32)
    carry_ref[...] = cs[tm - 1:tm, :]
    c2 = cs * LOG2E
    c_ref[...] = c2
    cp = one_ref[...]
    for i, piece in enumerate(_split3(c2)):
        cp = cp + jnp.dot(piece, spread_ref[i], preferred_element_type=F32)
    cp_ref[...] = cp.astype(cp_ref.dtype)

    emit(SEG_HQ, hq_ref, _silu)
    for cols, acc in chunks(SEG_HF):
        f = lb[:, cols] + (1.0 - lb[:, cols]) * _sigmoid(acc)
        lf_h = jnp.log(f)
        hlf_ref[:, cols] = lf_h
        hk_ref[:, cols] = (1.0 - f).astype(hk_ref.dtype)
        lfmin_ref[:, cols] = jnp.broadcast_to(jnp.min(lf_h, axis=0, keepdims=True), (lfmin_ref.shape[0], tn))
    emit(SEG_HI, hv_ref, lambda a: a)
    emit(SEG_HG, hg_ref, _silu)
    emit(SEG_FQ, fq_ref, lambda a: a * (FOX_DIM ** -0.5 * LOG2E))
    emit(SEG_FK, fk_ref, lambda a: a)
    lane = lax.broadcasted_iota(jnp.int32, (tm, LANES), 1)
    tail_v = jnp.where(lane == FOX_DIM, 1.0, 0.0)
    for cols, acc in chunks(SEG_FV):
        for j in range(tn // FOX_DIM):
            pair = acc[:, (j // 2) * LANES:(j // 2 + 1) * LANES]
            data = pair if j % 2 == 0 else pltpu.roll(pair, FOX_DIM, 1)
            va_ref[cols.start // FOX_DIM + j] = jnp.where(lane < FOX_DIM, data, tail_v).astype(va_ref.dtype)
    emit(SEG_GA, ga_ref, _sigmoid)
    emit(SEG_GB, gb_ref, _sigmoid)


CP_WIDTH = 4 * FOX_HEADS


def _in_proj(x2d, gain, w_main, w_gates, w_forget, lb_logits, f_bias, seq, tm, tn):
    t, d = x2d.shape
    heads = f_bias.shape[0]
    spread = np.zeros((3, heads, CP_WIDTH), np.float32)
    for i in range(3):
        spread[i, np.arange(heads), i * heads + np.arange(heads)] = 1.0
    one = np.zeros((1, CP_WIDTH), np.float32)
    one[0, 3 * heads] = 1.0
    once = lambda a: pl.BlockSpec(a.shape, lambda i: (0,) * a.ndim, pipeline_mode=pl.Buffered(1))
    row = lambda width: pl.BlockSpec((tm, width), lambda i: (i, 0))
    flat = lambda dt: (row(d), jax.ShapeDtypeStruct((t, d), dt))
    tiles_per_seq = seq // tm
    v_aug = (pl.BlockSpec((None, heads, tm, LANES), lambda i: (i // tiles_per_seq, 0, i % tiles_per_seq, 0)),
             jax.ShapeDtypeStruct((t // seq, heads, seq, LANES), BF16))
    outs = [flat(BF16), flat(F32), flat(BF16), flat(BF16), flat(BF16),
            flat(BF16), flat(BF16), v_aug, flat(BF16), flat(BF16),
            (row(heads), jax.ShapeDtypeStruct((t, heads), F32)),
            (row(CP_WIDTH), jax.ShapeDtypeStruct((t, CP_WIDTH), BF16)),
            (pl.BlockSpec((None, SUBLANES, d), lambda i: (i, 0, 0)),
             jax.ShapeDtypeStruct((t // tm, SUBLANES, d), F32))]
    consts = (gain.reshape(1, d), w_main, w_gates, w_forget, lb_logits, f_bias.reshape(1, heads),
              jnp.asarray(spread, BF16), jnp.asarray(one))
    return pl.pallas_call(
        functools.partial(_in_proj_kernel, tn=tn, seq=seq),
        grid=(t // tm,),
        in_specs=[row(d)] + [once(a) for a in consts],
        out_specs=[spec for spec, _ in outs],
        out_shape=[shape for _, shape in outs],
        scratch_shapes=[pltpu.VMEM((tm, d), BF16), pltpu.VMEM((1, heads), F32)],
        compiler_params=_cparams(("arbitrary",)),
        name="in_proj",
    )(x2d, *consts)


def _causal_chunk():
    r_i = lax.broadcasted_iota(jnp.int32, (HG_CHUNK, HG_CHUNK), 0)
    c_i = lax.broadcasted_iota(jnp.int32, (HG_CHUNK, HG_CHUNK), 1)
    return c_i <= r_i


def _chunk_cumsum(lf):
    lower = _causal_chunk().astype(BF16)
    b = jnp.zeros(lf.shape, F32)
    for piece in _split3(lf):
        b = b + jnp.dot(lower, piece, preferred_element_type=F32)
    return b


def _hgrn_chunk_single_ref(q, b, k, v, state):
    c = HG_CHUNK
    q_t = (q * jnp.exp(b)).astype(BF16)
    k_up = k * jnp.exp(-b)
    sc = lax.dot_general(q_t, k_up.astype(BF16), (((1,), (1,)), ((), ())), preferred_element_type=F32)
    sc = jnp.where(_causal_chunk(), sc, 0.0).astype(BF16)
    v_bf = v.astype(BF16)
    o = jnp.dot(jnp.concatenate([q_t, sc], axis=1), jnp.concatenate([state.astype(BF16), v_bf], axis=0),
                preferred_element_type=F32)
    b_end = b[c - 1:c]
    k_end = k_up * jnp.exp(b_end)
    tr = jnp.concatenate([k_end, jnp.broadcast_to(b_end, (HG_DIM - c, HG_DIM))], axis=0).T
    upd = jnp.dot(tr[:, :c].astype(BF16), v_bf, preferred_element_type=F32)
    return o, state * jnp.exp(tr[:, c:c + 1]) + upd


def _hgrn_chunk(q, b, k, v, state):
    c = HG_CHUNK
    n_sub = c // HG_SUB
    state_t = state.T

    qe = (q * jnp.exp(b)).astype(BF16)
    o = lax.dot_general(qe, state_t.astype(BF16), (((1,), (1,)), ((), ())), preferred_element_type=F32)

    o_parts = []
    t_iota = lax.broadcasted_iota(jnp.int32, (HG_SUB, HG_DIM), 0)
    for i in range(n_sub):
        lo = i * HG_SUB
        q_i = q[lo:lo + HG_SUB]
        b_i = b[lo:lo + HG_SUB]
        o_i = o[lo:lo + HG_SUB]
        if i > 0:
            ref = b[lo - 1:lo]
            q_t = (q_i * jnp.exp(b_i - ref)).astype(BF16)
            k_t = (k[:lo] * jnp.exp(ref - b[:lo])).astype(BF16)
            sc = lax.dot_general(q_t, k_t, (((1,), (1,)), ((), ())), preferred_element_type=F32)
            o_i = o_i + jnp.dot(sc.astype(BF16), v[:lo].astype(BF16), preferred_element_type=F32)
        for s in range(HG_SUB):
            row = lo + s
            rel = jnp.where(t_iota >= s, b_i - b[row:row + 1], NEG_INF)
            p = q_i * jnp.exp(rel) * k[row:row + 1]
            o_i = o_i + jnp.sum(p, axis=-1, keepdims=True) * v[row:row + 1]
        o_parts.append(o_i)
    o = jnp.concatenate(o_parts, axis=0)

    b_end = b[c - 1:c]
    k_dec = (k * jnp.exp(b_end - b)).astype(BF16)
    upd = lax.dot_general(v.astype(BF16), k_dec, (((0,), (0,)), ((), ())), preferred_element_type=F32)
    return o, (state_t * jnp.exp(b_end) + upd).T


def _hgrn_kernel(lfmin_ref, q_ref, lf_ref, k_ref, v_ref, g_ref, gn_ref, o_ref, state_ref, *, n_chunks):
    @pl.when(pl.program_id(2) == 0)
    def _():
        state_ref[...] = jnp.zeros_like(state_ref)

    def run_chunk(chunk_fn, rows, b, state):
        load = lambda ref: ref[rows, :].astype(F32)
        o, new_state = chunk_fn(load(q_ref), b, load(k_ref), load(v_ref), state)
        y = o * lax.rsqrt(jnp.mean(o * o, axis=-1, keepdims=True) + EPS)
        o_ref[rows, :] = (y * gn_ref[...] * load(g_ref)).astype(o_ref.dtype)
        return new_state

    lf_min = lfmin_ref[pl.program_id(0) * pl.num_programs(1) + pl.program_id(1), pl.program_id(2)]
    single_ref_ok = lf_min * HG_CHUNK > -HG_MAX_EXPONENT

    @pl.when(single_ref_ok)
    def _():
        chunk_rows = [pl.ds(ci * HG_CHUNK, HG_CHUNK) for ci in range(n_chunks)]
        b_all = _chunk_cumsum(jnp.concatenate([lf_ref[rows, :] for rows in chunk_rows], axis=1))
        state = state_ref[...]
        for ci, rows in enumerate(chunk_rows):
            state = run_chunk(_hgrn_chunk_single_ref, rows, b_all[:, ci * HG_DIM:(ci + 1) * HG_DIM], state)
        state_ref[...] = state

    @pl.when(jnp.logical_not(single_ref_ok))
    def _():
        def body(ci, carry):
            rows = pl.ds(pl.multiple_of(ci * HG_CHUNK, HG_CHUNK), HG_CHUNK)
            state_ref[...] = run_chunk(_hgrn_chunk, rows, _chunk_cumsum(lf_ref[rows, :]), state_ref[...])
            return carry

        lax.fori_loop(0, n_chunks, body, 0)


def _hgrn(lf_min, q, lf, k, v, g, g_norm, tc):
    b, s, w = q.shape
    heads = w // HG_DIM
    spec = pl.BlockSpec((None, tc, HG_DIM), lambda bi, hi, ti: (bi, ti, hi))
    return pl.pallas_call(
        functools.partial(_hgrn_kernel, n_chunks=tc // HG_CHUNK),
        grid=(b, heads, s // tc),
        in_specs=[pl.BlockSpec(memory_space=pltpu.SMEM)] + [spec] * 5
                 + [pl.BlockSpec((1, HG_DIM), lambda bi, hi, ti: (0, 0))],
        out_specs=spec,
        out_shape=jax.ShapeDtypeStruct((b, s, w), BF16),
        scratch_shapes=[pltpu.VMEM((HG_DIM, HG_DIM), F32)],
        compiler_params=_cparams(("parallel", "parallel", "arbitrary")),
        name="hgrn2_mixer",
    )(lf_min, q, lf, k, v, g, g_norm.reshape(1, HG_DIM))


AUG_C = FOX_DIM
AUG_ONE = FOX_DIM + 3
AUG_QN = FOX_DIM + 6
NORM_MARGIN = 1.02
SKIP_LOG2 = 150.0


def _placement_constants(heads):
    pq = np.zeros((CP_WIDTH + heads, heads * LANES), np.float32)
    pk = np.zeros((CP_WIDTH, heads * LANES), np.float32)
    hs = np.zeros((heads * FOX_DIM, heads), np.float32)
    for h in range(heads):
        for i in range(3):
            pq[i * heads + h, h * LANES + AUG_C + i] = 1.0
            pk[i * heads + h, h * LANES + AUG_ONE + i] = -1.0
        pq[3 * heads, h * LANES + AUG_ONE:h * LANES + AUG_ONE + 3] = 1.0
        pk[3 * heads, h * LANES + AUG_C:h * LANES + AUG_C + 3] = 1.0
        pq[CP_WIDTH + h, h * LANES + AUG_QN] = 1.0
        hs[h * FOX_DIM:(h + 1) * FOX_DIM, h] = 1.0
    return tuple(jnp.asarray(a, BF16) for a in (pq, pk, hs))


def _fox_pack_kernel(q_ref, k_ref, cp_ref, c_ref, pq_ref, pk_ref, hs_ref, qa_ref, ka_ref, st_ref):
    q = q_ref[...]
    k = k_ref[...]
    cp = cp_ref[...]
    tm = q.shape[0]
    heads = qa_ref.shape[0]
    hs = hs_ref[...]
    qn = jnp.sqrt(jnp.dot(q * q, hs, preferred_element_type=F32)) * NORM_MARGIN
    kn = jnp.sqrt(jnp.dot(k * k, hs, preferred_element_type=F32)) * NORM_MARGIN
    tail_q = jnp.dot(jnp.concatenate([cp, qn.astype(BF16)], axis=1), pq_ref[...],
                     preferred_element_type=F32).astype(BF16)
    tail_k = jnp.dot(cp, pk_ref[...], preferred_element_type=F32).astype(BF16)
    lane = lax.broadcasted_iota(jnp.int32, (tm, LANES), 1)
    in_data = lane < FOX_DIM
    for h in range(heads):
        pair = slice((h // 2) * LANES, (h // 2 + 1) * LANES)
        mine = slice(h * LANES, (h + 1) * LANES)
        data = (lambda x: x[:, pair]) if h % 2 == 0 else (lambda x: pltpu.roll(x[:, pair], FOX_DIM, 1))
        qa_ref[h] = jnp.where(in_data, data(q), tail_q[:, mine])
        ka_ref[h] = jnp.where(in_data, data(k), tail_k[:, mine])
    st_ref[0:1, :] = jnp.max(kn, axis=0, keepdims=True)
    st_ref[1:2, :] = c_ref[tm - 1:tm, :]


def _fox_pack(q, k, cp, c, tm):
    b, s, w = q.shape
    heads = c.shape[-1]
    consts = _placement_constants(heads)
    row = lambda width: pl.BlockSpec((None, tm, width), lambda bi, ti: (bi, ti, 0))
    full = lambda a: pl.BlockSpec(a.shape, lambda bi, ti: (0, 0))
    out = pl.BlockSpec((None, heads, tm, LANES), lambda bi, ti: (bi, 0, ti, 0))
    oshape = jax.ShapeDtypeStruct((b, heads, s, LANES), BF16)
    return pl.pallas_call(
        _fox_pack_kernel,
        grid=(b, s // tm),
        in_specs=[row(w), row(w), row(CP_WIDTH), row(heads)] + [full(a) for a in consts],
        out_specs=[out] * 2 + [pl.BlockSpec((None, None, 2, heads), lambda bi, ti: (bi, ti, 0, 0))],
        out_shape=[oshape] * 2 + [jax.ShapeDtypeStruct((b, s // tm, 2, heads), F32)],
        compiler_params=_cparams(("parallel", "parallel")),
        name="fox_pack",
    )(q, k, cp, c, *consts)


def _fox_kernel(kmax_ref, clast_ref, q_ref, k_ref, v_ref, o_ref, m_ref, acc_ref, *, tq, tk, group):
    qi = pl.program_id(2)
    row_a = pl.program_id(0) * (2 * pl.num_programs(1)) + 2 * pl.program_id(1)
    m_ref[...] = jnp.full_like(m_ref, NEG_INF)
    acc_ref[...] = jnp.zeros_like(acc_ref)

    def scores(row0, n_rows, kv0, masked):
        rows = pl.ds(row0, n_rows)
        cols = pl.ds(kv0, tk)
        if masked:
            causal = (lax.broadcasted_iota(jnp.int32, (n_rows, tk), 1)
                      <= lax.broadcasted_iota(jnp.int32, (n_rows, tk), 0))
        out = []
        for a in range(2):
            s = lax.dot_general(q_ref[a, rows, :], k_ref[a, cols, :], (((1,), (1,)), ((), ())),
                                preferred_element_type=F32)
            out.append(jnp.where(causal, s, NEG_INF) if masked else s)
        return out

    def update(row0, n_rows, kv0, s_pair):
        rows = pl.ds(row0, n_rows)
        cols = pl.ds(kv0, tk)
        for a, s in enumerate(s_pair):
            m_prev = m_ref[a, rows, :]
            m_new = jnp.maximum(m_prev, jnp.max(s, axis=-1, keepdims=True))
            alpha = jnp.exp2(m_prev - m_new)
            p = jnp.exp2(s - jnp.tile(m_new, (1, tk // LANES)))
            acc_ref[a, rows, :] = alpha * acc_ref[a, rows, :] + jnp.dot(
                p.astype(BF16), v_ref[a, cols, :], preferred_element_type=F32)
            m_ref[a, rows, :] = m_new

    n_diag = tq // tk
    n_full = qi * n_diag

    def block(row0, n_rows, kv0, masked):
        update(row0, n_rows, kv0, scores(row0, n_rows, kv0, masked))

    diag = [(j * tk, tq - j * tk, pl.multiple_of((n_full + j) * tk, tk)) for j in range(n_diag)]
    s_next = scores(*diag[0], True)
    for j in range(n_diag):
        s_cur = s_next
        if j + 1 < n_diag:
            s_next = scores(*diag[j + 1], True)
        update(*diag[j], s_cur)

    n_tiles = kmax_ref.shape[1]
    n_groups = tq // group
    slack = []
    for a in range(2):
        kmax = kmax_ref[row_a + a, 0]
        for j in range(1, n_tiles):
            kmax = jnp.maximum(kmax, kmax_ref[row_a + a, j])
        sel_row = lax.broadcasted_iota(jnp.int32, (LANES, LANES), 0)
        sel = jnp.where(sel_row == AUG_QN, kmax * NORM_MARGIN,
                        jnp.where((sel_row >= AUG_C) & (sel_row < AUG_ONE), 1.0, 0.0)).astype(BF16)
        gap = jnp.dot(q_ref[a], sel, preferred_element_type=F32) - m_ref[a]
        slack.append([jnp.max(gap[g * group:(g + 1) * group]) for g in range(n_groups)])

    def live_groups(j):
        n = jnp.int32(0)
        for a in range(2):
            c_last = clast_ref[row_a + a, jnp.maximum(j, 0)]
            for g in range(n_groups):
                n = jnp.where(slack[a][g] - c_last > -SKIP_LOG2, jnp.maximum(n, g + 1), n)
        return jnp.where(j >= 0, n, 0)

    def body(carry):
        j, n_live = carry
        kv0 = pl.multiple_of(j * tk, tk)
        for n in range(1, n_groups + 1):
            @pl.when(n_live == n)
            def _():
                block(0, n * group, kv0, False)
        return j - 1, live_groups(j - 1)

    lax.while_loop(lambda carry: carry[1] > 0, body, (n_full - 1, live_groups(n_full - 1)))

    lane = lax.broadcasted_iota(jnp.int32, (tq, LANES), 1)
    acc_a = acc_ref[0]
    acc_b = acc_ref[1]
    o_a = acc_a / acc_a[:, FOX_DIM:FOX_DIM + 1]
    o_b = acc_b / acc_b[:, FOX_DIM:FOX_DIM + 1]
    o_ref[...] = jnp.where(lane < FOX_DIM, o_a, pltpu.roll(o_b, FOX_DIM, 1)).astype(o_ref.dtype)


def _fox_attention(q, k, v, kmax, clast, tq, tk, group):
    b, h, s, _ = q.shape
    kv_spec = pl.BlockSpec((None, 2, s, LANES), lambda bi, hi, qi: (bi, hi, 0, 0))
    smem = pl.BlockSpec(memory_space=pltpu.SMEM)
    return pl.pallas_call(
        functools.partial(_fox_kernel, tq=tq, tk=tk, group=group),
        grid=(b, h // 2, s // tq),
        in_specs=[smem, smem,
                  pl.BlockSpec((None, 2, tq, LANES), lambda bi, hi, qi: (bi, hi, qi, 0)), kv_spec, kv_spec],
        out_specs=pl.BlockSpec((None, tq, LANES), lambda bi, hi, qi: (bi, qi, hi)),
        out_shape=jax.ShapeDtypeStruct((b, s, (h // 2) * LANES), BF16),
        scratch_shapes=[pltpu.VMEM((2, tq, LANES), F32), pltpu.VMEM((2, tq, LANES), F32)],
        compiler_params=_cparams(("parallel", "parallel", "arbitrary")),
        name="fox_attention",
    )(kmax, clast, q, k, v)


def _merge_kernel(x_ref, oa_ref, ob_ref, ga_ref, gb_ref, wa_ref, wb_ref, wo_ref, gn_ref, h_ref, n_ref, mg_ref,
                  *, tn):
    oa = oa_ref[...]
    ob = ob_ref[...]
    for c in range(mg_ref.shape[1] // tn):
        cols = slice(c * tn, (c + 1) * tn)
        ya = jnp.dot(oa, wa_ref[:, cols], preferred_element_type=F32)
        yb = jnp.dot(ob, wb_ref[:, cols], preferred_element_type=F32)
        mg_ref[:, cols] = (ga_ref[:, cols] * ya + gb_ref[:, cols] * yb).astype(mg_ref.dtype)
    h = x_ref[...] + jnp.dot(mg_ref[...], wo_ref[...], preferred_element_type=F32)
    h_ref[...] = h
    y = h * lax.rsqrt(jnp.mean(h * h, axis=-1, keepdims=True) + EPS)
    n_ref[...] = (y * gn_ref[...]).astype(n_ref.dtype)


def _merge(x2d, oa, ob, ga, gb, wa, wb, wo, gain, tm, tn):
    t, d = x2d.shape
    row = lambda width: pl.BlockSpec((tm, width), lambda i: (i, 0))
    once = lambda a: pl.BlockSpec(a.shape, lambda i: (0,) * a.ndim, pipeline_mode=pl.Buffered(1))
    gain2 = gain.reshape(1, d)
    return pl.pallas_call(
        functools.partial(_merge_kernel, tn=tn),
        grid=(t // tm,),
        in_specs=[row(d), row(oa.shape[1]), row(ob.shape[1]), row(d), row(d), once(wa), once(wb), once(wo),
                  once(gain2)],
        out_specs=[row(d), row(d)],
        out_shape=[jax.ShapeDtypeStruct((t, d), F32), jax.ShapeDtypeStruct((t, d), BF16)],
        scratch_shapes=[pltpu.VMEM((tm, d), BF16)],
        compiler_params=_cparams(("parallel",)),
        name="merge_outproj",
    )(x2d, oa, ob, ga, gb, wa, wb, wo, gain2)


def _ffn_kernel(n_ref, halo_ref, h_ref, wu_ref, cw_ref, cb_ref, wd_ref, gn_ref, o_ref, next_ref, act_ref,
                *, tm, seq, tf):
    i = pl.program_id(0)
    at_start = (i * tm) % seq == 0
    next_ref[0:HALO, :] = jnp.where(at_start, jnp.zeros_like(halo_ref[...]), halo_ref[...])
    next_ref[HALO:, :] = n_ref[...]
    n_ext = next_ref[...]
    d_ff = act_ref.shape[1]

    def conv(col0):
        cols = pl.ds(col0, tf)
        u = jnp.dot(n_ext, wu_ref[:, cols], preferred_element_type=F32)
        out = cb_ref[:, cols]
        for tap in range(CONV_WIDTH):
            lo = HALO - (CONV_WIDTH - 1) + tap
            out = out + cw_ref[tap:tap + 1, cols] * u[lo:lo + tm]
        return out

    for c in range(d_ff // tf):
        gate = conv(c * tf)
        val = conv(d_ff + c * tf)
        act = 0.5 * gate * (1.0 + lax.erf(gate * (2.0 ** -0.5))) * val
        act_ref[:, c * tf:(c + 1) * tf] = act.astype(act_ref.dtype)

    h = h_ref[...] + jnp.dot(act_ref[...], wd_ref[...], preferred_element_type=F32)
    y = h * lax.rsqrt(jnp.mean(h * h, axis=-1, keepdims=True) + EPS)
    o_ref[...] = (y * gn_ref[...]).astype(o_ref.dtype)


def _ffn(n2d, h2d, w_up, conv_w, conv_b, w_down, gain, seq, tm, tf):
    t, d = n2d.shape
    d_ff = w_down.shape[0]
    halo_blocks = tm // HALO
    cb = conv_b.reshape(1, 2 * d_ff)
    once = lambda a: pl.BlockSpec(a.shape, lambda i: (0,) * a.ndim, pipeline_mode=pl.Buffered(1))
    return pl.pallas_call(
        functools.partial(_ffn_kernel, tm=tm, seq=seq, tf=tf),
        grid=(t // tm,),
        in_specs=[pl.BlockSpec((tm, d), lambda i: (i, 0)),
                  pl.BlockSpec((HALO, d), lambda i: (jnp.maximum(i * halo_blocks - 1, 0), 0)),
                  pl.BlockSpec((tm, d), lambda i: (i, 0)),
                  once(w_up), once(conv_w), once(cb), once(w_down),
                  pl.BlockSpec((1, d), lambda i: (0, 0))],
        out_specs=pl.BlockSpec((tm, d), lambda i: (i, 0)),
        out_shape=jax.ShapeDtypeStruct((t, d), F32),
        scratch_shapes=[pltpu.VMEM((HALO + tm, d), BF16), pltpu.VMEM((tm, d_ff), BF16)],
        compiler_params=_cparams(("parallel",)),
        name="conv_glu_ffn",
    )(n2d, n2d, h2d, w_up, conv_w, cb, w_down, gain.reshape(1, d))


class _Tiles(NamedTuple):
    rows: int
    wide_rows: int
    cols: int
    hgrn: int
    query: int
    group: int


def _tiles(seq):
    fit = lambda pref, n=seq: pref if n % pref == 0 else n
    rows = fit(512)
    return _Tiles(rows=rows, wide_rows=fit(2 * rows), cols=256, hgrn=fit(2 * rows), query=fit(4 * rows),
                  group=fit(256, rows))


def kernel(x, norm_mix, w_in, fox_f_bias, hg_lb_logits, hg_norm, w_branch_a, w_branch_b, w_out, norm_ffn, w_up,
           conv_w, conv_b, w_down, norm_final):
    b, s, d = x.shape
    assert norm_mix.shape[0] == 1, "single-layer trunk"
    t = b * s
    tiles = _tiles(s)
    assert HG_HEADS * HG_DIM == d and FOX_HEADS * FOX_DIM == d, "the projection kernel walks equal-width segments"
    f_lo, f_hi = N_MAIN * d, N_MAIN * d + FOX_HEADS
    assert w_in.shape[-1] == f_hi + 2 * d
    w_main, w_forget, w_gates = (w_in[0, :, lo:hi].astype(BF16) for lo, hi in ((0, f_lo), (f_lo, f_hi), (f_hi, None)))

    x2d = x.reshape(t, d)
    hq, hlf, hk, hv, hgate, fq, fk, v_aug, ga, gb, c2, c_pieces, lf_colmin = _in_proj(
        x2d, norm_mix[0], w_main, w_gates, w_forget, hg_lb_logits, fox_f_bias[0], s, tiles.rows, tiles.cols)

    r3 = lambda a: a.reshape(b, s, a.shape[-1])
    lf_min = lf_colmin[:, 0, :].reshape(b, s // tiles.hgrn, tiles.hgrn // tiles.rows, HG_HEADS, HG_DIM)
    lf_min = lf_min.min(axis=(2, 4)).transpose(0, 2, 1).reshape(b * HG_HEADS, s // tiles.hgrn)
    o_a = _hgrn(lf_min, r3(hq), r3(hlf), r3(hk), r3(hv), r3(hgate), hg_norm[0], tiles.hgrn)

    q_aug, k_aug, kv_stats = _fox_pack(r3(fq), r3(fk), r3(c_pieces), r3(c2), tiles.rows)
    kv_stats = kv_stats.transpose(2, 0, 3, 1).reshape(2, b * FOX_HEADS, s // tiles.rows)
    o_b = _fox_attention(q_aug, k_aug, v_aug, kv_stats[0], kv_stats[1], tiles.query, tiles.rows,
                          tiles.group)

    h1, n_ffn = _merge(x2d, o_a.reshape(t, d), o_b.reshape(t, d), ga, gb, w_branch_a[0].astype(BF16),
                       w_branch_b[0].astype(BF16), w_out[0].astype(BF16), norm_ffn[0], tiles.wide_rows, tiles.cols)
    out = _ffn(n_ffn, h1, w_up[0].astype(BF16), conv_w[0], conv_b[0], w_down[0].astype(BF16), norm_final,
               s, tiles.wide_rows, tiles.cols)
    return out.reshape(b, s, d)
```

```python
import functools
import math
from typing import NamedTuple

import jax
import jax.numpy as jnp
import numpy as np
from jax import lax
from jax.experimental import pallas as pl
from jax.experimental.pallas import tpu as pltpu

F32 = jnp.float32
BF16 = jnp.bfloat16

EPS = 1e-6
HG_HEADS = 8
HG_DIM = 128
HG_CHUNK = 64
HG_SUB = 16
HG_MAX_EXPONENT = 64.0
FOX_HEADS = 16
FOX_DIM = 64
CONV_WIDTH = 3
HALO = 8
NEG_INF = float("-inf")
LOG2E = math.log2(math.e)
LANES = 128
SUBLANES = 8

VMEM_LIMIT = 56 * 1024 * 1024


def _cparams(sem):
    return pltpu.CompilerParams(dimension_semantics=sem, vmem_limit_bytes=VMEM_LIMIT)


def _sigmoid(x):
    return 1.0 / (1.0 + jnp.exp(-x))


def _silu(x):
    return x * _sigmoid(x)


def _log_sigmoid(x):
    return jnp.minimum(x, 0.0) - jnp.log1p(jnp.exp(-jnp.abs(x)))


def _split3(x):
    x1 = x.astype(BF16)
    r1 = x - x1.astype(F32)
    x2 = r1.astype(BF16)
    x3 = (r1 - x2.astype(F32)).astype(BF16)
    return x1, x2, x3


SEG_HQ, SEG_HF, SEG_HI, SEG_HG, SEG_FQ, SEG_FK, SEG_FV, SEG_GA, SEG_GB = range(9)
N_MAIN = 7


def _in_proj_kernel(x_ref, gain_ref, w_ref, wgate_ref, wf_ref, lbl_ref, fbias_ref, spread_ref, one_ref,
                    hq_ref, hlf_ref, hk_ref, hv_ref, hg_ref, fq_ref, fk_ref, va_ref, ga_ref, gb_ref, c_ref, cp_ref,
                    lfmin_ref, n_ref, carry_ref, *, tn, seq):
    x = x_ref[...]
    n_ref[...] = (x * lax.rsqrt(jnp.mean(x * x, axis=-1, keepdims=True) + EPS) * gain_ref[...]).astype(BF16)
    n = n_ref[...]
    width = hq_ref.shape[1]
    lbl = lbl_ref[...]
    e = jnp.exp(lbl - jnp.max(lbl, axis=0, keepdims=True))
    lb = e[0:1] / jnp.sum(e, axis=0, keepdims=True)

    def chunks(seg):
        ref, first = (w_ref, seg * width) if seg < N_MAIN else (wgate_ref, (seg - N_MAIN) * width)
        for c in range(width // tn):
            cols = slice(c * tn, (c + 1) * tn)
            yield cols, jnp.dot(n, ref[:, first + c * tn:first + (c + 1) * tn], preferred_element_type=F32)

    def emit(seg, ref, fn):
        for cols, acc in chunks(seg):
            ref[:, cols] = fn(acc).astype(ref.dtype)

    @pl.when((pl.program_id(0) * x.shape[0]) % seq == 0)
    def _():
        carry_ref[...] = jnp.zeros_like(carry_ref)

    lf = _log_sigmoid(jnp.dot(n, wf_ref[...], preferred_element_type=F32) + fbias_ref[...])
    tm = lf.shape[0]
    lower = (lax.broadcasted_iota(jnp.int32, (tm, tm), 1)
             <= lax.broadcasted_iota(jnp.int32, (tm, tm), 0)).astype(BF16)
    cs = carry_ref[...]
    for piece in _split3(lf):
        cs = cs + jnp.dot(lower, piece, preferred_element_type=F32)
    carry_ref[...] = cs[tm - 1:tm, :]
    c2 = cs * LOG2E
    c_ref[...] = c2
    cp = one_ref[...]
    for i, piece in enumerate(_split3(c2)):
        cp = cp + jnp.dot(piece, spread_ref[i], preferred_element_type=F32)
    cp_ref[...] = cp.astype(cp_ref.dtype)

    emit(SEG_HQ, hq_ref, _silu)
    for cols, acc in chunks(SEG_HF):
        f = lb[:, cols] + (1.0 - lb[:, cols]) * _sigmoid(acc)
        lf_h = jnp.log(f)
        hlf_ref[:, cols] = lf_h
        hk_ref[:, cols] = (1.0 - f).astype(hk_ref.dtype)
        lfmin_ref[:, cols] = jnp.broadcast_to(jnp.min(lf_h, axis=0, keepdims=True), (lfmin_ref.shape[0], tn))
    emit(SEG_HI, hv_ref, lambda a: a)
    emit(SEG_HG, hg_ref, _silu)
    emit(SEG_FQ, fq_ref, lambda a: a * (FOX_DIM ** -0.5 * LOG2E))
    emit(SEG_FK, fk_ref, lambda a: a)
    lane = lax.broadcasted_iota(jnp.int32, (tm, LANES), 1)
    tail_v = jnp.where(lane == FOX_DIM, 1.0, 0.0)
    for cols, acc in chunks(SEG_FV):
        for j in range(tn // FOX_DIM):
            pair = acc[:, (j // 2) * LANES:(j // 2 + 1) * LANES]
            data = pair if j % 2 == 0 else pltpu.roll(pair, FOX_DIM, 1)
            va_ref[cols.start // FOX_DIM + j] = jnp.where(lane < FOX_DIM, data, tail_v).astype(va_ref.dtype)
    emit(SEG_GA, ga_ref, _sigmoid)
    emit(SEG_GB, gb_ref, _sigmoid)


CP_WIDTH = 4 * FOX_HEADS


def _in_proj(x2d, gain, w_main, w_gates, w_forget, lb_logits, f_bias, seq, tm, tn):
    t, d = x2d.shape
    heads = f_bias.shape[0]
    spread = np.zeros((3, heads, CP_WIDTH), np.float32)
    for i in range(3):
        spread[i, np.arange(heads), i * heads + np.arange(heads)] = 1.0
    one = np.zeros((1, CP_WIDTH), np.float32)
    one[0, 3 * heads] = 1.0
    once = lambda a: pl.BlockSpec(a.shape, lambda i: (0,) * a.ndim, pipeline_mode=pl.Buffered(1))
    row = lambda width: pl.BlockSpec((tm, width), lambda i: (i, 0))
    flat = lambda dt: (row(d), jax.ShapeDtypeStruct((t, d), dt))
    tiles_per_seq = seq // tm
    v_aug = (pl.BlockSpec((None, heads, tm, LANES), lambda i: (i // tiles_per_seq, 0, i % tiles_per_seq, 0)),
             jax.ShapeDtypeStruct((t // seq, heads, seq, LANES), BF16))
    outs = [flat(BF16), flat(F32), flat(BF16), flat(BF16), flat(BF16),
            flat(BF16), flat(BF16), v_aug, flat(BF16), flat(BF16),
            (row(heads), jax.ShapeDtypeStruct((t, heads), F32)),
            (row(CP_WIDTH), jax.ShapeDtypeStruct((t, CP_WIDTH), BF16)),
            (pl.BlockSpec((None, SUBLANES, d), lambda i: (i, 0, 0)),
             jax.ShapeDtypeStruct((t // tm, SUBLANES, d), F32))]
    consts = (gain.reshape(1, d), w_main, w_gates, w_forget, lb_logits, f_bias.reshape(1, heads),
              jnp.asarray(spread, BF16), jnp.asarray(one))
    return pl.pallas_call(
        functools.partial(_in_proj_kernel, tn=tn, seq=seq),
        grid=(t // tm,),
        in_specs=[row(d)] + [once(a) for a in consts],
        out_specs=[spec for spec, _ in outs],
        out_shape=[shape for _, shape in outs],
        scratch_shapes=[pltpu.VMEM((tm, d), BF16), pltpu.VMEM((1, heads), F32)],
        compiler_params=_cparams(("arbitrary",)),
        name="in_proj",
    )(x2d, *consts)


def _causal_chunk():
    r_i = lax.broadcasted_iota(jnp.int32, (HG_CHUNK, HG_CHUNK), 0)
    c_i = lax.broadcasted_iota(jnp.int32, (HG_CHUNK, HG_CHUNK), 1)
    return c_i <= r_i


def _chunk_cumsum(lf):
    lower = _causal_chunk().astype(BF16)
    b = jnp.zeros(lf.shape, F32)
    for piece in _split3(lf):
        b = b + jnp.dot(lower, piece, preferred_element_type=F32)
    return b


def _hgrn_chunk_single_ref(q, b, k, v, state):
    c = HG_CHUNK
    q_t = (q * jnp.exp(b)).astype(BF16)
    k_up = k * jnp.exp(-b)
    sc = lax.dot_general(q_t, k_up.astype(BF16), (((1,), (1,)), ((), ())), preferred_element_type=F32)
    sc = jnp.where(_causal_chunk(), sc, 0.0).astype(BF16)
    v_bf = v.astype(BF16)
    o = jnp.dot(jnp.concatenate([q_t, sc], axis=1), jnp.concatenate([state.astype(BF16), v_bf], axis=0),
                preferred_element_type=F32)
    b_end = b[c - 1:c]
    k_end = k_up * jnp.exp(b_end)
    tr = jnp.concatenate([k_end, jnp.broadcast_to(b_end, (HG_DIM - c, HG_DIM))], axis=0).T
    upd = jnp.dot(tr[:, :c].astype(BF16), v_bf, preferred_element_type=F32)
    return o, state * jnp.exp(tr[:, c:c + 1]) + upd


def _hgrn_chunk(q, b, k, v, state):
    c = HG_CHUNK
    n_sub = c // HG_SUB
    state_t = state.T

    qe = (q * jnp.exp(b)).astype(BF16)
    o = lax.dot_general(qe, state_t.astype(BF16), (((1,), (1,)), ((), ())), preferred_element_type=F32)

    o_parts = []
    t_iota = lax.broadcasted_iota(jnp.int32, (HG_SUB, HG_DIM), 0)
    for i in range(n_sub):
        lo = i * HG_SUB
        q_i = q[lo:lo + HG_SUB]
        b_i = b[lo:lo + HG_SUB]
        o_i = o[lo:lo + HG_SUB]
        if i > 0:
            ref = b[lo - 1:lo]
            q_t = (q_i * jnp.exp(b_i - ref)).astype(BF16)
            k_t = (k[:lo] * jnp.exp(ref - b[:lo])).astype(BF16)
            sc = lax.dot_general(q_t, k_t, (((1,), (1,)), ((), ())), preferred_element_type=F32)
            o_i = o_i + jnp.dot(sc.astype(BF16), v[:lo].astype(BF16), preferred_element_type=F32)
        for s in range(HG_SUB):
            row = lo + s
            rel = jnp.where(t_iota >= s, b_i - b[row:row + 1], NEG_INF)
            p = q_i * jnp.exp(rel) * k[row:row + 1]
            o_i = o_i + jnp.sum(p, axis=-1, keepdims=True) * v[row:row + 1]
        o_parts.append(o_i)
    o = jnp.concatenate(o_parts, axis=0)

    b_end = b[c - 1:c]
    k_dec = (k * jnp.exp(b_end - b)).astype(BF16)
    upd = lax.dot_general(v.astype(BF16), k_dec, (((0,), (0,)), ((), ())), preferred_element_type=F32)
    return o, (state_t * jnp.exp(b_end) + upd).T


def _hgrn_kernel(lfmin_ref, q_ref, lf_ref, k_ref, v_ref, g_ref, gn_ref, o_ref, state_ref, *, n_sub, n_chunks):
    @pl.when(pl.program_id(2) == 0)
    def _():
        state_ref[...] = jnp.zeros_like(state_ref)

    def run_chunk(chunk_fn, rows, b, state):
        load = lambda ref: ref[rows, :].astype(F32)
        o, new_state = chunk_fn(load(q_ref), b, load(k_ref), load(v_ref), state)
        y = o * lax.rsqrt(jnp.mean(o * o, axis=-1, keepdims=True) + EPS)
        o_ref[rows, :] = (y * gn_ref[...] * load(g_ref)).astype(o_ref.dtype)
        return new_state

    head_row = pl.program_id(0) * pl.num_programs(1) + pl.program_id(1)

    def sub_tile(si, carry):
        base = si * (n_chunks * HG_CHUNK)
        chunk_rows = [pl.ds(pl.multiple_of(base + ci * HG_CHUNK, HG_CHUNK), HG_CHUNK) for ci in range(n_chunks)]
        lf_min = lfmin_ref[head_row, pl.program_id(2) * n_sub + si]
        single_ref_ok = lf_min * HG_CHUNK > -HG_MAX_EXPONENT

        @pl.when(single_ref_ok)
        def _():
            b_all = _chunk_cumsum(jnp.concatenate([lf_ref[rows, :] for rows in chunk_rows], axis=1))
            state = state_ref[...]
            for ci, rows in enumerate(chunk_rows):
                state = run_chunk(_hgrn_chunk_single_ref, rows, b_all[:, ci * HG_DIM:(ci + 1) * HG_DIM], state)
            state_ref[...] = state

        @pl.when(jnp.logical_not(single_ref_ok))
        def _():
            def body(ci, inner):
                rows = pl.ds(pl.multiple_of(base + ci * HG_CHUNK, HG_CHUNK), HG_CHUNK)
                state_ref[...] = run_chunk(_hgrn_chunk, rows, _chunk_cumsum(lf_ref[rows, :]), state_ref[...])
                return inner

            lax.fori_loop(0, n_chunks, body, 0)

        return carry

    lax.fori_loop(0, n_sub, sub_tile, 0)


def _hgrn(lf_min, q, lf, k, v, g, g_norm, tc, n_sub):
    b, s, w = q.shape
    heads = w // HG_DIM
    step = tc * n_sub
    spec = pl.BlockSpec((None, step, HG_DIM), lambda bi, hi, ti: (bi, ti, hi))
    return pl.pallas_call(
        functools.partial(_hgrn_kernel, n_sub=n_sub, n_chunks=tc // HG_CHUNK),
        grid=(b, heads, s // step),
        in_specs=[pl.BlockSpec(memory_space=pltpu.SMEM)] + [spec] * 5
                 + [pl.BlockSpec((1, HG_DIM), lambda bi, hi, ti: (0, 0))],
        out_specs=spec,
        out_shape=jax.ShapeDtypeStruct((b, s, w), BF16),
        scratch_shapes=[pltpu.VMEM((HG_DIM, HG_DIM), F32)],
        compiler_params=_cparams(("parallel", "parallel", "arbitrary")),
        name="hgrn2_mixer",
    )(lf_min, q, lf, k, v, g, g_norm.reshape(1, HG_DIM))


AUG_C = FOX_DIM
AUG_ONE = FOX_DIM + 3
AUG_QN = FOX_DIM + 6
NORM_MARGIN = 1.02
SKIP_LOG2 = 150.0


def _placement_constants(heads):
    pq = np.zeros((CP_WIDTH + heads, heads * LANES), np.float32)
    pk = np.zeros((CP_WIDTH, heads * LANES), np.float32)
    hs = np.zeros((heads * FOX_DIM, heads), np.float32)
    for h in range(heads):
        for i in range(3):
            pq[i * heads + h, h * LANES + AUG_C + i] = 1.0
            pk[i * heads + h, h * LANES + AUG_ONE + i] = -1.0
        pq[3 * heads, h * LANES + AUG_ONE:h * LANES + AUG_ONE + 3] = 1.0
        pk[3 * heads, h * LANES + AUG_C:h * LANES + AUG_C + 3] = 1.0
        pq[CP_WIDTH + h, h * LANES + AUG_QN] = 1.0
        hs[h * FOX_DIM:(h + 1) * FOX_DIM, h] = 1.0
    return tuple(jnp.asarray(a, BF16) for a in (pq, pk, hs))


def _fox_pack_kernel(q_ref, k_ref, cp_ref, c_ref, pq_ref, pk_ref, hs_ref, qa_ref, ka_ref, st_ref):
    q = q_ref[...]
    k = k_ref[...]
    cp = cp_ref[...]
    tm = q.shape[0]
    heads = qa_ref.shape[0]
    hs = hs_ref[...]
    qn = jnp.sqrt(jnp.dot(q * q, hs, preferred_element_type=F32)) * NORM_MARGIN
    kn = jnp.sqrt(jnp.dot(k * k, hs, preferred_element_type=F32)) * NORM_MARGIN
    tail_q = jnp.dot(jnp.concatenate([cp, qn.astype(BF16)], axis=1), pq_ref[...],
                     preferred_element_type=F32).astype(BF16)
    tail_k = jnp.dot(cp, pk_ref[...], preferred_element_type=F32).astype(BF16)
    lane = lax.broadcasted_iota(jnp.int32, (tm, LANES), 1)
    in_data = lane < FOX_DIM
    for h in range(heads):
        pair = slice((h // 2) * LANES, (h // 2 + 1) * LANES)
        mine = slice(h * LANES, (h + 1) * LANES)
        data = (lambda x: x[:, pair]) if h % 2 == 0 else (lambda x: pltpu.roll(x[:, pair], FOX_DIM, 1))
        qa_ref[h] = jnp.where(in_data, data(q), tail_q[:, mine])
        ka_ref[h] = jnp.where(in_data, data(k), tail_k[:, mine])
    st_ref[0:1, :] = jnp.max(kn, axis=0, keepdims=True)
    st_ref[1:2, :] = c_ref[tm - 1:tm, :]


def _fox_pack(q, k, cp, c, tm):
    b, s, w = q.shape
    heads = c.shape[-1]
    consts = _placement_constants(heads)
    row = lambda width: pl.BlockSpec((None, tm, width), lambda bi, ti: (bi, ti, 0))
    full = lambda a: pl.BlockSpec(a.shape, lambda bi, ti: (0, 0))
    out = pl.BlockSpec((None, heads, tm, LANES), lambda bi, ti: (bi, 0, ti, 0))
    oshape = jax.ShapeDtypeStruct((b, heads, s, LANES), BF16)
    return pl.pallas_call(
        _fox_pack_kernel,
        grid=(b, s // tm),
        in_specs=[row(w), row(w), row(CP_WIDTH), row(heads)] + [full(a) for a in consts],
        out_specs=[out] * 2 + [pl.BlockSpec((None, None, 2, heads), lambda bi, ti: (bi, ti, 0, 0))],
        out_shape=[oshape] * 2 + [jax.ShapeDtypeStruct((b, s // tm, 2, heads), F32)],
        compiler_params=_cparams(("parallel", "parallel")),
        name="fox_pack",
    )(q, k, cp, c, *consts)


def _fox_kernel(kmax_ref, clast_ref, q_ref, k_ref, v_ref, o_ref, m_ref, acc_ref, *, tq, tk, group):
    qi = pl.program_id(2)
    row_a = pl.program_id(0) * (2 * pl.num_programs(1)) + 2 * pl.program_id(1)
    m_ref[...] = jnp.full_like(m_ref, NEG_INF)
    acc_ref[...] = jnp.zeros_like(acc_ref)

    def scores(row0, n_rows, kv0, masked):
        rows = pl.ds(row0, n_rows)
        cols = pl.ds(kv0, tk)
        if masked:
            causal = (lax.broadcasted_iota(jnp.int32, (n_rows, tk), 1)
                      <= lax.broadcasted_iota(jnp.int32, (n_rows, tk), 0))
        out = []
        for a in range(2):
            s = lax.dot_general(q_ref[a, rows, :], k_ref[a, cols, :], (((1,), (1,)), ((), ())),
                                preferred_element_type=F32)
            out.append(jnp.where(causal, s, NEG_INF) if masked else s)
        return out

    def update(row0, n_rows, kv0, s_pair):
        rows = pl.ds(row0, n_rows)
        cols = pl.ds(kv0, tk)
        for a, s in enumerate(s_pair):
            m_prev = m_ref[a, rows, :]
            m_new = jnp.maximum(m_prev, jnp.max(s, axis=-1, keepdims=True))
            alpha = jnp.exp2(m_prev - m_new)
            p = jnp.exp2(s - jnp.tile(m_new, (1, tk // LANES)))
            acc_ref[a, rows, :] = alpha * acc_ref[a, rows, :] + jnp.dot(
                p.astype(BF16), v_ref[a, cols, :], preferred_element_type=F32)
            m_ref[a, rows, :] = m_new

    n_diag = tq // tk
    n_full = qi * n_diag

    def block(row0, n_rows, kv0, masked):
        update(row0, n_rows, kv0, scores(row0, n_rows, kv0, masked))

    diag = [(j * tk, tq - j * tk, pl.multiple_of((n_full + j) * tk, tk)) for j in range(n_diag)]
    s_next = scores(*diag[0], True)
    for j in range(n_diag):
        s_cur = s_next
        if j + 1 < n_diag:
            s_next = scores(*diag[j + 1], True)
        update(*diag[j], s_cur)

    n_tiles = kmax_ref.shape[1]
    n_groups = tq // group
    slack = []
    for a in range(2):
        kmax = kmax_ref[row_a + a, 0]
        for j in range(1, n_tiles):
            kmax = jnp.maximum(kmax, kmax_ref[row_a + a, j])
        sel_row = lax.broadcasted_iota(jnp.int32, (LANES, LANES), 0)
        sel = jnp.where(sel_row == AUG_QN, kmax * NORM_MARGIN,
                        jnp.where((sel_row >= AUG_C) & (sel_row < AUG_ONE), 1.0, 0.0)).astype(BF16)
        gap = jnp.dot(q_ref[a], sel, preferred_element_type=F32) - m_ref[a]
        slack.append([jnp.max(gap[g * group:(g + 1) * group]) for g in range(n_groups)])

    def live_groups(j):
        n = jnp.int32(0)
        for a in range(2):
            c_last = clast_ref[row_a + a, jnp.maximum(j, 0)]
            for g in range(n_groups):
                n = jnp.where(slack[a][g] - c_last > -SKIP_LOG2, jnp.maximum(n, g + 1), n)
        return jnp.where(j >= 0, n, 0)

    def body(carry):
        j, n_live = carry
        kv0 = pl.multiple_of(j * tk, tk)
        for n in range(1, n_groups + 1):
            @pl.when(n_live == n)
            def _():
                block(0, n * group, kv0, False)
        return j - 1, live_groups(j - 1)

    lax.while_loop(lambda carry: carry[1] > 0, body, (n_full - 1, live_groups(n_full - 1)))

    lane = lax.broadcasted_iota(jnp.int32, (tq, LANES), 1)
    acc_a = acc_ref[0]
    acc_b = acc_ref[1]
    o_a = acc_a / acc_a[:, FOX_DIM:FOX_DIM + 1]
    o_b = acc_b / acc_b[:, FOX_DIM:FOX_DIM + 1]
    o_ref[...] = jnp.where(lane < FOX_DIM, o_a, pltpu.roll(o_b, FOX_DIM, 1)).astype(o_ref.dtype)


def _fox_attention(q, k, v, kmax, clast, tq, tk, group):
    b, h, s, _ = q.shape
    kv_spec = pl.BlockSpec((None, 2, s, LANES), lambda bi, hi, qi: (bi, hi, 0, 0))
    smem = pl.BlockSpec(memory_space=pltpu.SMEM)
    return pl.pallas_call(
        functools.partial(_fox_kernel, tq=tq, tk=tk, group=group),
        grid=(b, h // 2, s // tq),
        in_specs=[smem, smem,
                  pl.BlockSpec((None, 2, tq, LANES), lambda bi, hi, qi: (bi, hi, qi, 0)), kv_spec, kv_spec],
        out_specs=pl.BlockSpec((None, tq, LANES), lambda bi, hi, qi: (bi, qi, hi)),
        out_shape=jax.ShapeDtypeStruct((b, s, (h // 2) * LANES), BF16),
        scratch_shapes=[pltpu.VMEM((2, tq, LANES), F32), pltpu.VMEM((2, tq, LANES), F32)],
        compiler_params=_cparams(("parallel", "parallel", "arbitrary")),
        name="fox_attention",
    )(kmax, clast, q, k, v)


def _merge_kernel(x_ref, oa_ref, ob_ref, ga_ref, gb_ref, wa_ref, wb_ref, wo_ref, gn_ref, h_ref, n_ref, mg_ref,
                  *, tn):
    oa = oa_ref[...]
    ob = ob_ref[...]
    for c in range(mg_ref.shape[1] // tn):
        cols = slice(c * tn, (c + 1) * tn)
        ya = jnp.dot(oa, wa_ref[:, cols], preferred_element_type=F32)
        yb = jnp.dot(ob, wb_ref[:, cols], preferred_element_type=F32)
        mg_ref[:, cols] = (ga_ref[:, cols] * ya + gb_ref[:, cols] * yb).astype(mg_ref.dtype)
    h = x_ref[...] + jnp.dot(mg_ref[...], wo_ref[...], preferred_element_type=F32)
    h_ref[...] = h
    y = h * lax.rsqrt(jnp.mean(h * h, axis=-1, keepdims=True) + EPS)
    n_ref[...] = (y * gn_ref[...]).astype(n_ref.dtype)


def _merge(x2d, oa, ob, ga, gb, wa, wb, wo, gain, tm, tn):
    t, d = x2d.shape
    row = lambda width: pl.BlockSpec((tm, width), lambda i: (i, 0))
    once = lambda a: pl.BlockSpec(a.shape, lambda i: (0,) * a.ndim, pipeline_mode=pl.Buffered(1))
    gain2 = gain.reshape(1, d)
    return pl.pallas_call(
        functools.partial(_merge_kernel, tn=tn),
        grid=(t // tm,),
        in_specs=[row(d), row(oa.shape[1]), row(ob.shape[1]), row(d), row(d), once(wa), once(wb), once(wo),
                  once(gain2)],
        out_specs=[row(d), row(d)],
        out_shape=[jax.ShapeDtypeStruct((t, d), F32), jax.ShapeDtypeStruct((t, d), BF16)],
        scratch_shapes=[pltpu.VMEM((tm, d), BF16)],
        compiler_params=_cparams(("parallel",)),
        name="merge_outproj",
    )(x2d, oa, ob, ga, gb, wa, wb, wo, gain2)


def _ffn_kernel(n_ref, halo_ref, h_ref, wu_ref, cw_ref, cb_ref, wd_ref, gn_ref, o_ref, next_ref, act_ref,
                *, tm, seq, tf):
    i = pl.program_id(0)
    at_start = (i * tm) % seq == 0
    next_ref[0:HALO, :] = jnp.where(at_start, jnp.zeros_like(halo_ref[...]), halo_ref[...])
    next_ref[HALO:, :] = n_ref[...]
    n_ext = next_ref[...]
    d_ff = act_ref.shape[1]

    def conv(col0):
        cols = pl.ds(col0, tf)
        u = jnp.dot(n_ext, wu_ref[:, cols], preferred_element_type=F32)
        out = cb_ref[:, cols]
        for tap in range(CONV_WIDTH):
            lo = HALO - (CONV_WIDTH - 1) + tap
            out = out + cw_ref[tap:tap + 1, cols] * u[lo:lo + tm]
        return out

    for c in range(d_ff // tf):
        gate = conv(c * tf)
        val = conv(d_ff + c * tf)
        act = 0.5 * gate * (1.0 + lax.erf(gate * (2.0 ** -0.5))) * val
        act_ref[:, c * tf:(c + 1) * tf] = act.astype(act_ref.dtype)

    h = h_ref[...] + jnp.dot(act_ref[...], wd_ref[...], preferred_element_type=F32)
    y = h * lax.rsqrt(jnp.mean(h * h, axis=-1, keepdims=True) + EPS)
    o_ref[...] = (y * gn_ref[...]).astype(o_ref.dtype)


def _ffn(n2d, h2d, w_up, conv_w, conv_b, w_down, gain, seq, tm, tf):
    t, d = n2d.shape
    d_ff = w_down.shape[0]
    halo_blocks = tm // HALO
    cb = conv_b.reshape(1, 2 * d_ff)
    once = lambda a: pl.BlockSpec(a.shape, lambda i: (0,) * a.ndim, pipeline_mode=pl.Buffered(1))
    return pl.pallas_call(
        functools.partial(_ffn_kernel, tm=tm, seq=seq, tf=tf),
        grid=(t // tm,),
        in_specs=[pl.BlockSpec((tm, d), lambda i: (i, 0)),
                  pl.BlockSpec((HALO, d), lambda i: (jnp.maximum(i * halo_blocks - 1, 0), 0)),
                  pl.BlockSpec((tm, d), lambda i: (i, 0)),
                  once(w_up), once(conv_w), once(cb), once(w_down),
                  pl.BlockSpec((1, d), lambda i: (0, 0))],
        out_specs=pl.BlockSpec((tm, d), lambda i: (i, 0)),
        out_shape=jax.ShapeDtypeStruct((t, d), F32),
        scratch_shapes=[pltpu.VMEM((HALO + tm, d), BF16), pltpu.VMEM((tm, d_ff), BF16)],
        compiler_params=_cparams(("parallel",)),
        name="conv_glu_ffn",
    )(n2d, n2d, h2d, w_up, conv_w, cb, w_down, gain.reshape(1, d))


class _Tiles(NamedTuple):
    rows: int
    wide_rows: int
    cols: int
    hgrn: int
    hgrn_sub: int
    query: int
    group: int


def _tiles(seq):
    fit = lambda pref, n=seq: pref if n % pref == 0 else n
    rows = fit(512)
    hgrn = fit(2 * rows)
    return _Tiles(rows=rows, wide_rows=fit(2 * rows), cols=256, hgrn=hgrn, hgrn_sub=4 if (seq // hgrn) % 4 == 0 else 1,
                  query=fit(4 * rows), group=fit(256, rows))


def kernel(x, norm_mix, w_in, fox_f_bias, hg_lb_logits, hg_norm, w_branch_a, w_branch_b, w_out, norm_ffn, w_up,
           conv_w, conv_b, w_down, norm_final):
    b, s, d = x.shape
    assert norm_mix.shape[0] == 1, "single-layer trunk"
    t = b * s
    tiles = _tiles(s)
    assert HG_HEADS * HG_DIM == d and FOX_HEADS * FOX_DIM == d, "the projection kernel walks equal-width segments"
    f_lo, f_hi = N_MAIN * d, N_MAIN * d + FOX_HEADS
    assert w_in.shape[-1] == f_hi + 2 * d
    w_main, w_forget, w_gates = (w_in[0, :, lo:hi].astype(BF16) for lo, hi in ((0, f_lo), (f_lo, f_hi), (f_hi, None)))

    x2d = x.reshape(t, d)
    hq, hlf, hk, hv, hgate, fq, fk, v_aug, ga, gb, c2, c_pieces, lf_colmin = _in_proj(
        x2d, norm_mix[0], w_main, w_gates, w_forget, hg_lb_logits, fox_f_bias[0], s, tiles.rows, tiles.cols)

    r3 = lambda a: a.reshape(b, s, a.shape[-1])
    lf_min = lf_colmin[:, 0, :].reshape(b, s // tiles.hgrn, tiles.hgrn // tiles.rows, HG_HEADS, HG_DIM)
    lf_min = lf_min.min(axis=(2, 4)).transpose(0, 2, 1).reshape(b * HG_HEADS, s // tiles.hgrn)
    o_a = _hgrn(lf_min, r3(hq), r3(hlf), r3(hk), r3(hv), r3(hgate), hg_norm[0], tiles.hgrn, tiles.hgrn_sub)

    q_aug, k_aug, kv_stats = _fox_pack(r3(fq), r3(fk), r3(c_pieces), r3(c2), tiles.rows)
    kv_stats = kv_stats.transpose(2, 0, 3, 1).reshape(2, b * FOX_HEADS, s // tiles.rows)
    o_b = _fox_attention(q_aug, k_aug, v_aug, kv_stats[0], kv_stats[1], tiles.query, tiles.rows,
                          tiles.group)

    h1, n_ffn = _merge(x2d, o_a.reshape(t, d), o_b.reshape(t, d), ga, gb, w_branch_a[0].astype(BF16),
                       w_branch_b[0].astype(BF16), w_out[0].astype(BF16), norm_ffn[0], tiles.wide_rows, tiles.cols)
    out = _ffn(n_ffn, h1, w_up[0].astype(BF16), conv_w[0], conv_b[0], w_down[0].astype(BF16), norm_final,
               s, tiles.wide_rows, tiles.cols)
    return out.reshape(b, s, d)
```

```python
import functools
import math
from typing import NamedTuple

import jax
import jax.numpy as jnp
import numpy as np
from jax import lax
from jax.experimental import pallas as pl
from jax.experimental.pallas import tpu as pltpu

F32 = jnp.float32
BF16 = jnp.bfloat16

EPS = 1e-6
HG_HEADS = 8
HG_DIM = 128
HG_CHUNK = 64
HG_SUB = 16
HG_MAX_EXPONENT = 64.0
FOX_HEADS = 16
FOX_DIM = 64
CONV_WIDTH = 3
HALO = 8
NEG_INF = float("-inf")
LOG2E = math.log2(math.e)
LANES = 128
SUBLANES = 8

VMEM_LIMIT = 56 * 1024 * 1024


def _cparams(sem):
    return pltpu.CompilerParams(dimension_semantics=sem, vmem_limit_bytes=VMEM_LIMIT)


def _sigmoid(x):
    return 1.0 / (1.0 + jnp.exp(-x))


def _silu(x):
    return x * _sigmoid(x)


def _log_sigmoid(x):
    return jnp.minimum(x, 0.0) - jnp.log1p(jnp.exp(-jnp.abs(x)))


def _split3(x):
    x1 = x.astype(BF16)
    r1 = x - x1.astype(F32)
    x2 = r1.astype(BF16)
    x3 = (r1 - x2.astype(F32)).astype(BF16)
    return x1, x2, x3


SEG_HQ, SEG_HF, SEG_HI, SEG_HG, SEG_FQ, SEG_FK, SEG_FV, SEG_GA, SEG_GB = range(9)
N_MAIN = 7


def _in_proj_kernel(x_ref, gain_ref, w_ref, wgate_ref, wf_ref, lbl_ref, fbias_ref, spread_ref, one_ref,
                    hq_ref, hlf_ref, hk_ref, hv_ref, hg_ref, fq_ref, fk_ref, va_ref, ga_ref, gb_ref, c_ref, cp_ref,
                    lfmin_ref, n_ref, carry_ref, *, tn, seq):
    x = x_ref[...]
    n_ref[...] = (x * lax.rsqrt(jnp.mean(x * x, axis=-1, keepdims=True) + EPS) * gain_ref[...]).astype(BF16)
    n = n_ref[...]
    width = hq_ref.shape[1]
    lbl = lbl_ref[...]
    e = jnp.exp(lbl - jnp.max(lbl, axis=0, keepdims=True))
    lb = e[0:1] / jnp.sum(e, axis=0, keepdims=True)

    def chunks(seg):
        ref, first = (w_ref, seg * width) if seg < N_MAIN else (wgate_ref, (seg - N_MAIN) * width)
        for c in range(width // tn):
            cols = slice(c * tn, (c + 1) * tn)
            yield cols, jnp.dot(n, ref[:, first + c * tn:first + (c + 1) * tn], preferred_element_type=F32)

    def emit(seg, ref, fn):
        for cols, acc in chunks(seg):
            ref[:, cols] = fn(acc).astype(ref.dtype)

    @pl.when((pl.program_id(0) * x.shape[0]) % seq == 0)
    def _():
        carry_ref[...] = jnp.zeros_like(carry_ref)

    lf = _log_sigmoid(jnp.dot(n, wf_ref[...], preferred_element_type=F32) + fbias_ref[...])
    tm = lf.shape[0]
    lower = (lax.broadcasted_iota(jnp.int32, (tm, tm), 1)
             <= lax.broadcasted_iota(jnp.int32, (tm, tm), 0)).astype(BF16)
    cs = carry_ref[...]
    for piece in _split3(lf):
        cs = cs + jnp.dot(lower, piece, preferred_element_type=F32)
    carry_ref[...] = cs[tm - 1:tm, :]
    c2 = cs * LOG2E
    c_ref[...] = c2
    cp = one_ref[...]
    for i, piece in enumerate(_split3(c2)):
        cp = cp + jnp.dot(piece, spread_ref[i], preferred_element_type=F32)
    cp_ref[...] = cp.astype(cp_ref.dtype)

    emit(SEG_HQ, hq_ref, _silu)
    for cols, acc in chunks(SEG_HF):
        f = lb[:, cols] + (1.0 - lb[:, cols]) * _sigmoid(acc)
        lf_h = jnp.log(f)
        hlf_ref[:, cols] = lf_h
        hk_ref[:, cols] = (1.0 - f).astype(hk_ref.dtype)
        lfmin_ref[:, cols] = jnp.broadcast_to(jnp.min(lf_h, axis=0, keepdims=True), (lfmin_ref.shape[0], tn))
    emit(SEG_HI, hv_ref, lambda a: a)
    emit(SEG_HG, hg_ref, _silu)
    emit(SEG_FQ, fq_ref, lambda a: a * (FOX_DIM ** -0.5 * LOG2E))
    emit(SEG_FK, fk_ref, lambda a: a)
    lane = lax.broadcasted_iota(jnp.int32, (tm, LANES), 1)
    tail_v = jnp.where(lane == FOX_DIM, 1.0, 0.0)
    for cols, acc in chunks(SEG_FV):
        for j in range(tn // FOX_DIM):
            pair = acc[:, (j // 2) * LANES:(j // 2 + 1) * LANES]
            data = pair if j % 2 == 0 else pltpu.roll(pair, FOX_DIM, 1)
            va_ref[cols.start // FOX_DIM + j] = jnp.where(lane < FOX_DIM, data, tail_v).astype(va_ref.dtype)
    emit(SEG_GA, ga_ref, _sigmoid)
    emit(SEG_GB, gb_ref, _sigmoid)


CP_WIDTH = 4 * FOX_HEADS


def _in_proj(x2d, gain, w_main, w_gates, w_forget, lb_logits, f_bias, seq, tm, tn):
    t, d = x2d.shape
    heads = f_bias.shape[0]
    spread = np.zeros((3, heads, CP_WIDTH), np.float32)
    for i in range(3):
        spread[i, np.arange(heads), i * heads + np.arange(heads)] = 1.0
    one = np.zeros((1, CP_WIDTH), np.float32)
    one[0, 3 * heads] = 1.0
    once = lambda a: pl.BlockSpec(a.shape, lambda i: (0,) * a.ndim, pipeline_mode=pl.Buffered(1))
    row = lambda width: pl.BlockSpec((tm, width), lambda i: (i, 0))
    flat = lambda dt: (row(d), jax.ShapeDtypeStruct((t, d), dt))
    tiles_per_seq = seq // tm
    v_aug = (pl.BlockSpec((None, heads, tm, LANES), lambda i: (i // tiles_per_seq, 0, i % tiles_per_seq, 0)),
             jax.ShapeDtypeStruct((t // seq, heads, seq, LANES), BF16))
    outs = [flat(BF16), flat(F32), flat(BF16), flat(BF16), flat(BF16),
            flat(BF16), flat(BF16), v_aug, flat(BF16), flat(BF16),
            (row(heads), jax.ShapeDtypeStruct((t, heads), F32)),
            (row(CP_WIDTH), jax.ShapeDtypeStruct((t, CP_WIDTH), BF16)),
            (pl.BlockSpec((None, SUBLANES, d), lambda i: (i, 0, 0)),
             jax.ShapeDtypeStruct((t // tm, SUBLANES, d), F32))]
    consts = (gain.reshape(1, d), w_main, w_gates, w_forget, lb_logits, f_bias.reshape(1, heads),
              jnp.asarray(spread, BF16), jnp.asarray(one))
    return pl.pallas_call(
        functools.partial(_in_proj_kernel, tn=tn, seq=seq),
        grid=(t // tm,),
        in_specs=[row(d)] + [once(a) for a in consts],
        out_specs=[spec for spec, _ in outs],
        out_shape=[shape for _, shape in outs],
        scratch_shapes=[pltpu.VMEM((tm, d), BF16), pltpu.VMEM((1, heads), F32)],
        compiler_params=_cparams(("arbitrary",)),
        name="in_proj",
    )(x2d, *consts)


def _causal_chunk():
    r_i = lax.broadcasted_iota(jnp.int32, (HG_CHUNK, HG_CHUNK), 0)
    c_i = lax.broadcasted_iota(jnp.int32, (HG_CHUNK, HG_CHUNK), 1)
    return c_i <= r_i


def _chunk_cumsum(lf):
    lower = _causal_chunk().astype(BF16)
    b = jnp.zeros(lf.shape, F32)
    for piece in _split3(lf):
        b = b + jnp.dot(lower, piece, preferred_element_type=F32)
    return b


def _hgrn_chunk_single_ref(q, b, k, v, state):
    c = HG_CHUNK
    q_t = (q * jnp.exp(b)).astype(BF16)
    k_up = k * jnp.exp(-b)
    sc = lax.dot_general(q_t, k_up.astype(BF16), (((1,), (1,)), ((), ())), preferred_element_type=F32)
    sc = jnp.where(_causal_chunk(), sc, 0.0).astype(BF16)
    v_bf = v.astype(BF16)
    o = jnp.dot(jnp.concatenate([q_t, sc], axis=1), jnp.concatenate([state.astype(BF16), v_bf], axis=0),
                preferred_element_type=F32)
    b_end = b[c - 1:c]
    k_end = k_up * jnp.exp(b_end)
    tr = jnp.concatenate([k_end, jnp.broadcast_to(b_end, (HG_DIM - c, HG_DIM))], axis=0).T
    upd = jnp.dot(tr[:, :c].astype(BF16), v_bf, preferred_element_type=F32)
    return o, state * jnp.exp(tr[:, c:c + 1]) + upd


def _hgrn_chunk(q, b, k, v, state):
    c = HG_CHUNK
    n_sub = c // HG_SUB
    state_t = state.T

    qe = (q * jnp.exp(b)).astype(BF16)
    o = lax.dot_general(qe, state_t.astype(BF16), (((1,), (1,)), ((), ())), preferred_element_type=F32)

    o_parts = []
    t_iota = lax.broadcasted_iota(jnp.int32, (HG_SUB, HG_DIM), 0)
    for i in range(n_sub):
        lo = i * HG_SUB
        q_i = q[lo:lo + HG_SUB]
        b_i = b[lo:lo + HG_SUB]
        o_i = o[lo:lo + HG_SUB]
        if i > 0:
            ref = b[lo - 1:lo]
            q_t = (q_i * jnp.exp(b_i - ref)).astype(BF16)
            k_t = (k[:lo] * jnp.exp(ref - b[:lo])).astype(BF16)
            sc = lax.dot_general(q_t, k_t, (((1,), (1,)), ((), ())), preferred_element_type=F32)
            o_i = o_i + jnp.dot(sc.astype(BF16), v[:lo].astype(BF16), preferred_element_type=F32)
        for s in range(HG_SUB):
            row = lo + s
            rel = jnp.where(t_iota >= s, b_i - b[row:row + 1], NEG_INF)
            p = q_i * jnp.exp(rel) * k[row:row + 1]
            o_i = o_i + jnp.sum(p, axis=-1, keepdims=True) * v[row:row + 1]
        o_parts.append(o_i)
    o = jnp.concatenate(o_parts, axis=0)

    b_end = b[c - 1:c]
    k_dec = (k * jnp.exp(b_end - b)).astype(BF16)
    upd = lax.dot_general(v.astype(BF16), k_dec, (((0,), (0,)), ((), ())), preferred_element_type=F32)
    return o, (state_t * jnp.exp(b_end) + upd).T


def _hgrn_kernel(lfmin_ref, q_ref, lf_ref, k_ref, v_ref, g_ref, gn_ref, o_ref, state_ref, *, n_sub, n_chunks):
    @pl.when(pl.program_id(2) == 0)
    def _():
        state_ref[...] = jnp.zeros_like(state_ref)

    def run_chunk(chunk_fn, rows, b, state):
        load = lambda ref: ref[rows, :].astype(F32)
        o, new_state = chunk_fn(load(q_ref), b, load(k_ref), load(v_ref), state)
        y = o * lax.rsqrt(jnp.mean(o * o, axis=-1, keepdims=True) + EPS)
        o_ref[rows, :] = (y * gn_ref[...] * load(g_ref)).astype(o_ref.dtype)
        return new_state

    head_row = pl.program_id(0) * pl.num_programs(1) + pl.program_id(1)

    def sub_tile(si, carry):
        base = si * (n_chunks * HG_CHUNK)
        chunk_rows = [pl.ds(pl.multiple_of(base + ci * HG_CHUNK, HG_CHUNK), HG_CHUNK) for ci in range(n_chunks)]
        lf_min = lfmin_ref[head_row, pl.program_id(2) * n_sub + si]
        single_ref_ok = lf_min * HG_CHUNK > -HG_MAX_EXPONENT

        @pl.when(single_ref_ok)
        def _():
            b_all = _chunk_cumsum(jnp.concatenate([lf_ref[rows, :] for rows in chunk_rows], axis=1))
            state = state_ref[...]
            for ci, rows in enumerate(chunk_rows):
                state = run_chunk(_hgrn_chunk_single_ref, rows, b_all[:, ci * HG_DIM:(ci + 1) * HG_DIM], state)
            state_ref[...] = state

        @pl.when(jnp.logical_not(single_ref_ok))
        def _():
            def body(ci, inner):
                rows = pl.ds(pl.multiple_of(base + ci * HG_CHUNK, HG_CHUNK), HG_CHUNK)
                state_ref[...] = run_chunk(_hgrn_chunk, rows, _chunk_cumsum(lf_ref[rows, :]), state_ref[...])
                return inner

            lax.fori_loop(0, n_chunks, body, 0)

        return carry

    lax.fori_loop(0, n_sub, sub_tile, 0)


def _hgrn(lf_min, q, lf, k, v, g, g_norm, tc, n_sub):
    b, s, w = q.shape
    heads = w // HG_DIM
    step = tc * n_sub
    spec = pl.BlockSpec((None, step, HG_DIM), lambda bi, hi, ti: (bi, ti, hi))
    return pl.pallas_call(
        functools.partial(_hgrn_kernel, n_sub=n_sub, n_chunks=tc // HG_CHUNK),
        grid=(b, heads, s // step),
        in_specs=[pl.BlockSpec(memory_space=pltpu.SMEM)] + [spec] * 5
                 + [pl.BlockSpec((1, HG_DIM), lambda bi, hi, ti: (0, 0))],
        out_specs=spec,
        out_shape=jax.ShapeDtypeStruct((b, s, w), BF16),
        scratch_shapes=[pltpu.VMEM((HG_DIM, HG_DIM), F32)],
        compiler_params=_cparams(("parallel", "parallel", "arbitrary")),
        name="hgrn2_mixer",
    )(lf_min, q, lf, k, v, g, g_norm.reshape(1, HG_DIM))


AUG_C = FOX_DIM
AUG_ONE = FOX_DIM + 3
AUG_QN = FOX_DIM + 6
NORM_MARGIN = 1.02
SKIP_LOG2 = 150.0


def _placement_constants(heads):
    pq = np.zeros((CP_WIDTH + heads, heads * LANES), np.float32)
    pk = np.zeros((CP_WIDTH, heads * LANES), np.float32)
    hs = np.zeros((heads * FOX_DIM, heads), np.float32)
    for h in range(heads):
        for i in range(3):
            pq[i * heads + h, h * LANES + AUG_C + i] = 1.0
            pk[i * heads + h, h * LANES + AUG_ONE + i] = -1.0
        pq[3 * heads, h * LANES + AUG_ONE:h * LANES + AUG_ONE + 3] = 1.0
        pk[3 * heads, h * LANES + AUG_C:h * LANES + AUG_C + 3] = 1.0
        pq[CP_WIDTH + h, h * LANES + AUG_QN] = 1.0
        hs[h * FOX_DIM:(h + 1) * FOX_DIM, h] = 1.0
    return tuple(jnp.asarray(a, BF16) for a in (pq, pk, hs))


def _fox_pack_kernel(q_ref, k_ref, cp_ref, c_ref, pq_ref, pk_ref, hs_ref, qa_ref, ka_ref, st_ref):
    q = q_ref[...]
    k = k_ref[...]
    cp = cp_ref[...]
    tm = q.shape[0]
    heads = qa_ref.shape[0]
    hs = hs_ref[...]
    qn = jnp.sqrt(jnp.dot(q * q, hs, preferred_element_type=F32)) * NORM_MARGIN
    kn = jnp.sqrt(jnp.dot(k * k, hs, preferred_element_type=F32)) * NORM_MARGIN
    tail_q = jnp.dot(jnp.concatenate([cp, qn.astype(BF16)], axis=1), pq_ref[...],
                     preferred_element_type=F32).astype(BF16)
    tail_k = jnp.dot(cp, pk_ref[...], preferred_element_type=F32).astype(BF16)
    lane = lax.broadcasted_iota(jnp.int32, (tm, LANES), 1)
    in_data = lane < FOX_DIM
    for h in range(heads):
        pair = slice((h // 2) * LANES, (h // 2 + 1) * LANES)
        mine = slice(h * LANES, (h + 1) * LANES)
        data = (lambda x: x[:, pair]) if h % 2 == 0 else (lambda x: pltpu.roll(x[:, pair], FOX_DIM, 1))
        qa_ref[h] = jnp.where(in_data, data(q), tail_q[:, mine])
        ka_ref[h] = jnp.where(in_data, data(k), tail_k[:, mine])
    tk = tm // st_ref.shape[0]
    for j in range(st_ref.shape[0]):
        st_ref[j, 0:1, :] = jnp.max(kn[j * tk:(j + 1) * tk], axis=0, keepdims=True)
        st_ref[j, 1:2, :] = c_ref[(j + 1) * tk - 1:(j + 1) * tk, :]


def _fox_pack(q, k, cp, c, tm, tk):
    b, s, w = q.shape
    heads = c.shape[-1]
    consts = _placement_constants(heads)
    row = lambda width: pl.BlockSpec((None, tm, width), lambda bi, ti: (bi, ti, 0))
    full = lambda a: pl.BlockSpec(a.shape, lambda bi, ti: (0, 0))
    out = pl.BlockSpec((None, heads, tm, LANES), lambda bi, ti: (bi, 0, ti, 0))
    oshape = jax.ShapeDtypeStruct((b, heads, s, LANES), BF16)
    return pl.pallas_call(
        _fox_pack_kernel,
        grid=(b, s // tm),
        in_specs=[row(w), row(w), row(CP_WIDTH), row(heads)] + [full(a) for a in consts],
        out_specs=[out] * 2 + [pl.BlockSpec((None, tm // tk, 2, heads), lambda bi, ti: (bi, ti, 0, 0))],
        out_shape=[oshape] * 2 + [jax.ShapeDtypeStruct((b, s // tk, 2, heads), F32)],
        compiler_params=_cparams(("parallel", "parallel")),
        name="fox_pack",
    )(q, k, cp, c, *consts)


def _fox_kernel(kmax_ref, clast_ref, q_ref, k_ref, v_ref, o_ref, m_ref, acc_ref, *, tq, tk, group):
    qi = pl.program_id(2)
    row_a = pl.program_id(0) * (2 * pl.num_programs(1)) + 2 * pl.program_id(1)
    m_ref[...] = jnp.full_like(m_ref, NEG_INF)
    acc_ref[...] = jnp.zeros_like(acc_ref)

    def scores(row0, n_rows, kv0, masked):
        rows = pl.ds(row0, n_rows)
        cols = pl.ds(kv0, tk)
        if masked:
            causal = (lax.broadcasted_iota(jnp.int32, (n_rows, tk), 1)
                      <= lax.broadcasted_iota(jnp.int32, (n_rows, tk), 0))
        out = []
        for a in range(2):
            s = lax.dot_general(q_ref[a, rows, :], k_ref[a, cols, :], (((1,), (1,)), ((), ())),
                                preferred_element_type=F32)
            out.append(jnp.where(causal, s, NEG_INF) if masked else s)
        return out

    def update(row0, n_rows, kv0, s_pair):
        rows = pl.ds(row0, n_rows)
        cols = pl.ds(kv0, tk)
        for a, s in enumerate(s_pair):
            m_prev = m_ref[a, rows, :]
            m_new = jnp.maximum(m_prev, jnp.max(s, axis=-1, keepdims=True))
            alpha = jnp.exp2(m_prev - m_new)
            p = jnp.exp2(s - jnp.tile(m_new, (1, tk // LANES)))
            acc_ref[a, rows, :] = alpha * acc_ref[a, rows, :] + jnp.dot(
                p.astype(BF16), v_ref[a, cols, :], preferred_element_type=F32)
            m_ref[a, rows, :] = m_new

    n_diag = tq // tk
    n_full = qi * n_diag

    def block(row0, n_rows, kv0, masked):
        update(row0, n_rows, kv0, scores(row0, n_rows, kv0, masked))

    n_tiles = kmax_ref.shape[1]
    bound = []
    for a in range(2):
        kmax = kmax_ref[row_a + a, 0]
        for j in range(1, n_tiles):
            kmax = jnp.maximum(kmax, kmax_ref[row_a + a, j])
        sel_row = lax.broadcasted_iota(jnp.int32, (LANES, LANES), 0)
        sel = jnp.where(sel_row == AUG_QN, kmax * NORM_MARGIN,
                        jnp.where((sel_row >= AUG_C) & (sel_row < AUG_ONE), 1.0, 0.0)).astype(BF16)
        bound.append(jnp.dot(q_ref[a], sel, preferred_element_type=F32))

    diag = [(j * tk, tq - j * tk, pl.multiple_of((n_full + j) * tk, tk)) for j in range(n_diag)]
    s_next = scores(*diag[0], True)
    for j in range(n_diag):
        s_cur = s_next
        if j + 1 < n_diag:
            s_next = scores(*diag[j + 1], True)
        update(*diag[j], s_cur)

    n_groups = tq // group
    slack = []
    for a in range(2):
        gap = bound[a] - m_ref[a]
        slack.append([jnp.max(gap[g * group:(g + 1) * group]) for g in range(n_groups)])

    def live_groups(j):
        n = jnp.int32(0)
        for a in range(2):
            c_last = clast_ref[row_a + a, jnp.maximum(j, 0)]
            for g in range(n_groups):
                n = jnp.where(slack[a][g] - c_last > -SKIP_LOG2, jnp.maximum(n, g + 1), n)
        return jnp.where(j >= 0, n, 0)

    def body(carry):
        j, n_live = carry
        kv0 = pl.multiple_of(j * tk, tk)
        for n in range(1, n_groups + 1):
            @pl.when(n_live == n)
            def _():
                block(0, n * group, kv0, False)
        return j - 1, live_groups(j - 1)

    lax.while_loop(lambda carry: carry[1] > 0, body, (n_full - 1, live_groups(n_full - 1)))

    lane = lax.broadcasted_iota(jnp.int32, (tq, LANES), 1)
    acc_a = acc_ref[0]
    acc_b = acc_ref[1]
    o_a = acc_a / acc_a[:, FOX_DIM:FOX_DIM + 1]
    o_b = acc_b / acc_b[:, FOX_DIM:FOX_DIM + 1]
    o_ref[...] = jnp.where(lane < FOX_DIM, o_a, pltpu.roll(o_b, FOX_DIM, 1)).astype(o_ref.dtype)


def _fox_attention(q, k, v, kmax, clast, tq, tk, group):
    b, h, s, _ = q.shape
    kv_spec = pl.BlockSpec((None, 2, s, LANES), lambda bi, hi, qi: (bi, hi, 0, 0))
    smem = pl.BlockSpec(memory_space=pltpu.SMEM)
    return pl.pallas_call(
        functools.partial(_fox_kernel, tq=tq, tk=tk, group=group),
        grid=(b, h // 2, s // tq),
        in_specs=[smem, smem,
                  pl.BlockSpec((None, 2, tq, LANES), lambda bi, hi, qi: (bi, hi, qi, 0)), kv_spec, kv_spec],
        out_specs=pl.BlockSpec((None, tq, LANES), lambda bi, hi, qi: (bi, qi, hi)),
        out_shape=jax.ShapeDtypeStruct((b, s, (h // 2) * LANES), BF16),
        scratch_shapes=[pltpu.VMEM((2, tq, LANES), F32), pltpu.VMEM((2, tq, LANES), F32)],
        compiler_params=_cparams(("parallel", "parallel", "arbitrary")),
        name="fox_attention",
    )(kmax, clast, q, k, v)


def _merge_kernel(x_ref, oa_ref, ob_ref, ga_ref, gb_ref, wa_ref, wb_ref, wo_ref, gn_ref, h_ref, n_ref, mg_ref,
                  *, tn):
    oa = oa_ref[...]
    ob = ob_ref[...]
    for c in range(mg_ref.shape[1] // tn):
        cols = slice(c * tn, (c + 1) * tn)
        ya = jnp.dot(oa, wa_ref[:, cols], preferred_element_type=F32)
        yb = jnp.dot(ob, wb_ref[:, cols], preferred_element_type=F32)
        mg_ref[:, cols] = (ga_ref[:, cols] * ya + gb_ref[:, cols] * yb).astype(mg_ref.dtype)
    h = x_ref[...] + jnp.dot(mg_ref[...], wo_ref[...], preferred_element_type=F32)
    h_ref[...] = h
    y = h * lax.rsqrt(jnp.mean(h * h, axis=-1, keepdims=True) + EPS)
    n_ref[...] = (y * gn_ref[...]).astype(n_ref.dtype)


def _merge(x2d, oa, ob, ga, gb, wa, wb, wo, gain, tm, tn):
    t, d = x2d.shape
    row = lambda width: pl.BlockSpec((tm, width), lambda i: (i, 0))
    once = lambda a: pl.BlockSpec(a.shape, lambda i: (0,) * a.ndim, pipeline_mode=pl.Buffered(1))
    gain2 = gain.reshape(1, d)
    return pl.pallas_call(
        functools.partial(_merge_kernel, tn=tn),
        grid=(t // tm,),
        in_specs=[row(d), row(oa.shape[1]), row(ob.shape[1]), row(d), row(d), once(wa), once(wb), once(wo),
                  once(gain2)],
        out_specs=[row(d), row(d)],
        out_shape=[jax.ShapeDtypeStruct((t, d), F32), jax.ShapeDtypeStruct((t, d), BF16)],
        scratch_shapes=[pltpu.VMEM((tm, d), BF16)],
        compiler_params=_cparams(("parallel",)),
        name="merge_outproj",
    )(x2d, oa, ob, ga, gb, wa, wb, wo, gain2)


def _ffn_kernel(n_ref, halo_ref, h_ref, wu_ref, cw_ref, cb_ref, wd_ref, gn_ref, o_ref, next_ref, act_ref,
                *, tm, seq, tf):
    i = pl.program_id(0)
    at_start = (i * tm) % seq == 0
    next_ref[0:HALO, :] = jnp.where(at_start, jnp.zeros_like(halo_ref[...]), halo_ref[...])
    next_ref[HALO:, :] = n_ref[...]
    n_ext = next_ref[...]
    d_ff = act_ref.shape[1]

    def conv(col0):
        cols = pl.ds(col0, tf)
        u = jnp.dot(n_ext, wu_ref[:, cols], preferred_element_type=F32)
        out = cb_ref[:, cols]
        for tap in range(CONV_WIDTH):
            lo = HALO - (CONV_WIDTH - 1) + tap
            out = out + cw_ref[tap:tap + 1, cols] * u[lo:lo + tm]
        return out

    for c in range(d_ff // tf):
        gate = conv(c * tf)
        val = conv(d_ff + c * tf)
        act = 0.5 * gate * (1.0 + lax.erf(gate * (2.0 ** -0.5))) * val
        act_ref[:, c * tf:(c + 1) * tf] = act.astype(act_ref.dtype)

    h = h_ref[...] + jnp.dot(act_ref[...], wd_ref[...], preferred_element_type=F32)
    y = h * lax.rsqrt(jnp.mean(h * h, axis=-1, keepdims=True) + EPS)
    o_ref[...] = (y * gn_ref[...]).astype(o_ref.dtype)


def _ffn(n2d, h2d, w_up, conv_w, conv_b, w_down, gain, seq, tm, tf):
    t, d = n2d.shape
    d_ff = w_down.shape[0]
    halo_blocks = tm // HALO
    cb = conv_b.reshape(1, 2 * d_ff)
    once = lambda a: pl.BlockSpec(a.shape, lambda i: (0,) * a.ndim, pipeline_mode=pl.Buffered(1))
    return pl.pallas_call(
        functools.partial(_ffn_kernel, tm=tm, seq=seq, tf=tf),
        grid=(t // tm,),
        in_specs=[pl.BlockSpec((tm, d), lambda i: (i, 0)),
                  pl.BlockSpec((HALO, d), lambda i: (jnp.maximum(i * halo_blocks - 1, 0), 0)),
                  pl.BlockSpec((tm, d), lambda i: (i, 0)),
                  once(w_up), once(conv_w), once(cb), once(w_down),
                  pl.BlockSpec((1, d), lambda i: (0, 0))],
        out_specs=pl.BlockSpec((tm, d), lambda i: (i, 0)),
        out_shape=jax.ShapeDtypeStruct((t, d), F32),
        scratch_shapes=[pltpu.VMEM((HALO + tm, d), BF16), pltpu.VMEM((tm, d_ff), BF16)],
        compiler_params=_cparams(("parallel",)),
        name="conv_glu_ffn",
    )(n2d, n2d, h2d, w_up, conv_w, cb, w_down, gain.reshape(1, d))


class _Tiles(NamedTuple):
    rows: int
    wide_rows: int
    cols: int
    hgrn: int
    hgrn_sub: int
    query: int
    group: int


def _tiles(seq):
    fit = lambda pref, n=seq: pref if n % pref == 0 else n
    rows = fit(512)
    hgrn = fit(2 * rows)
    return _Tiles(rows=rows, wide_rows=fit(2 * rows), cols=256, hgrn=hgrn, hgrn_sub=4 if (seq // hgrn) % 4 == 0 else 1,
                  query=fit(4 * rows), group=fit(256, rows))


def kernel(x, norm_mix, w_in, fox_f_bias, hg_lb_logits, hg_norm, w_branch_a, w_branch_b, w_out, norm_ffn, w_up,
           conv_w, conv_b, w_down, norm_final):
    b, s, d = x.shape
    assert norm_mix.shape[0] == 1, "single-layer trunk"
    t = b * s
    tiles = _tiles(s)
    assert HG_HEADS * HG_DIM == d and FOX_HEADS * FOX_DIM == d, "the projection kernel walks equal-width segments"
    f_lo, f_hi = N_MAIN * d, N_MAIN * d + FOX_HEADS
    assert w_in.shape[-1] == f_hi + 2 * d
    w_main, w_forget, w_gates = (w_in[0, :, lo:hi].astype(BF16) for lo, hi in ((0, f_lo), (f_lo, f_hi), (f_hi, None)))

    x2d = x.reshape(t, d)
    hq, hlf, hk, hv, hgate, fq, fk, v_aug, ga, gb, c2, c_pieces, lf_colmin = _in_proj(
        x2d, norm_mix[0], w_main, w_gates, w_forget, hg_lb_logits, fox_f_bias[0], s, tiles.rows, tiles.cols)

    r3 = lambda a: a.reshape(b, s, a.shape[-1])
    lf_min = lf_colmin[:, 0, :].reshape(b, s // tiles.hgrn, tiles.hgrn // tiles.rows, HG_HEADS, HG_DIM)
    lf_min = lf_min.min(axis=(2, 4)).transpose(0, 2, 1).reshape(b * HG_HEADS, s // tiles.hgrn)
    o_a = _hgrn(lf_min, r3(hq), r3(hlf), r3(hk), r3(hv), r3(hgate), hg_norm[0], tiles.hgrn, tiles.hgrn_sub)

    q_aug, k_aug, kv_stats = _fox_pack(r3(fq), r3(fk), r3(c_pieces), r3(c2), tiles.wide_rows, tiles.rows)
    kv_stats = kv_stats.transpose(2, 0, 3, 1).reshape(2, b * FOX_HEADS, s // tiles.rows)
    o_b = _fox_attention(q_aug, k_aug, v_aug, kv_stats[0], kv_stats[1], tiles.query, tiles.rows,
                          tiles.group)

    h1, n_ffn = _merge(x2d, o_a.reshape(t, d), o_b.reshape(t, d), ga, gb, w_branch_a[0].astype(BF16),
                       w_branch_b[0].astype(BF16), w_out[0].astype(BF16), norm_ffn[0], tiles.wide_rows, tiles.cols)
    out = _ffn(n_ffn, h1, w_up[0].astype(BF16), conv_w[0], conv_b[0], w_down[0].astype(BF16), norm_final,
               s, tiles.wide_rows, tiles.cols)
    return out.reshape(b, s, d)
```

```python
import functools
import math
from typing import NamedTuple

import jax
import jax.numpy as jnp
import numpy as np
from jax import lax
from jax.experimental import pallas as pl
from jax.experimental.pallas import tpu as pltpu

F32 = jnp.float32
BF16 = jnp.bfloat16

EPS = 1e-6
HG_HEADS = 8
HG_DIM = 128
HG_CHUNK = 64
HG_SUB = 16
HG_MAX_EXPONENT = 64.0
FOX_HEADS = 16
FOX_DIM = 64
CONV_WIDTH = 3
HALO = 8
NEG_INF = float("-inf")
LOG2E = math.log2(math.e)
LANES = 128
SUBLANES = 8

VMEM_LIMIT = 56 * 1024 * 1024


def _cparams(sem):
    return pltpu.CompilerParams(dimension_semantics=sem, vmem_limit_bytes=VMEM_LIMIT)


def _sigmoid(x):
    return 1.0 / (1.0 + jnp.exp(-x))


def _silu(x):
    return x * _sigmoid(x)


def _log_sigmoid(x):
    return jnp.minimum(x, 0.0) - jnp.log1p(jnp.exp(-jnp.abs(x)))


def _split3(x):
    x1 = x.astype(BF16)
    r1 = x - x1.astype(F32)
    x2 = r1.astype(BF16)
    x3 = (r1 - x2.astype(F32)).astype(BF16)
    return x1, x2, x3


SEG_HQ, SEG_HF, SEG_HI, SEG_HG, SEG_FQ, SEG_FK, SEG_FV, SEG_GA, SEG_GB = range(9)
N_MAIN = 7


def _in_proj_kernel(slot_ref, x_ref, gain_ref, w_ref, wgate_ref, wf_ref, lbl_ref, fbias_ref, spread_ref, one_ref,
                    hq_ref, hlf_ref, hk_ref, hv_ref, hg_ref, fq_ref, fk_ref, va_ref, ga_ref, gb_ref, c_ref, cp_ref,
                    lfmin_ref, n_ref, carry_ref, *, tn, seq):
    x = x_ref[...]
    n_ref[...] = (x * lax.rsqrt(jnp.mean(x * x, axis=-1, keepdims=True) + EPS) * gain_ref[...]).astype(BF16)
    n = n_ref[...]
    width = hq_ref.shape[1]
    lbl = lbl_ref[...]
    e = jnp.exp(lbl - jnp.max(lbl, axis=0, keepdims=True))
    lb = e[0:1] / jnp.sum(e, axis=0, keepdims=True)

    def chunks(seg):
        ref, first = (w_ref, seg * width) if seg < N_MAIN else (wgate_ref, (seg - N_MAIN) * width)
        for c in range(width // tn):
            cols = slice(c * tn, (c + 1) * tn)
            yield cols, jnp.dot(n, ref[:, first + c * tn:first + (c + 1) * tn], preferred_element_type=F32)

    def emit(seg, ref, fn):
        for cols, acc in chunks(seg):
            ref[:, cols] = fn(acc).astype(ref.dtype)

    @pl.when((pl.program_id(0) * x.shape[0]) % seq == 0)
    def _():
        carry_ref[...] = jnp.zeros_like(carry_ref)

    lf = _log_sigmoid(jnp.dot(n, wf_ref[...], preferred_element_type=F32) + fbias_ref[...])
    tm = lf.shape[0]
    lower = (lax.broadcasted_iota(jnp.int32, (tm, tm), 1)
             <= lax.broadcasted_iota(jnp.int32, (tm, tm), 0)).astype(BF16)
    cs = carry_ref[...]
    for piece in _split3(lf):
        cs = cs + jnp.dot(lower, piece, preferred_element_type=F32)
    carry_ref[...] = cs[tm - 1:tm, :]
    c2 = cs * LOG2E
    c_ref[...] = c2
    cp = one_ref[...]
    for i, piece in enumerate(_split3(c2)):
        cp = cp + jnp.dot(piece, spread_ref[i], preferred_element_type=F32)
    cp_ref[...] = cp.astype(cp_ref.dtype)

    emit(SEG_HQ, hq_ref, _silu)
    for cols, acc in chunks(SEG_HF):
        f = lb[:, cols] + (1.0 - lb[:, cols]) * _sigmoid(acc)
        lf_h = jnp.log(f)
        hlf_ref[:, cols] = lf_h
        hk_ref[:, cols] = (1.0 - f).astype(hk_ref.dtype)
        lfmin_ref[:, cols] = jnp.broadcast_to(jnp.min(lf_h, axis=0, keepdims=True), (lfmin_ref.shape[0], tn))
    emit(SEG_HI, hv_ref, lambda a: a)
    emit(SEG_HG, hg_ref, _silu)
    emit(SEG_FQ, fq_ref, lambda a: a * (FOX_DIM ** -0.5 * LOG2E))
    emit(SEG_FK, fk_ref, lambda a: a)
    lane = lax.broadcasted_iota(jnp.int32, (tm, LANES), 1)
    tail_v = jnp.where(lane == FOX_DIM, 1.0, 0.0)
    for cols, acc in chunks(SEG_FV):
        for j in range(tn // FOX_DIM):
            pair = acc[:, (j // 2) * LANES:(j // 2 + 1) * LANES]
            data = pair if j % 2 == 0 else pltpu.roll(pair, FOX_DIM, 1)
            slot = slot_ref[cols.start // FOX_DIM + j]
            va_ref[slot] = jnp.where(lane < FOX_DIM, data, tail_v).astype(va_ref.dtype)
    emit(SEG_GA, ga_ref, _sigmoid)
    emit(SEG_GB, gb_ref, _sigmoid)


CP_WIDTH = 4 * FOX_HEADS


def _in_proj(head_slot, x2d, gain, w_main, w_gates, w_forget, lb_logits, f_bias, seq, tm, tn):
    t, d = x2d.shape
    heads = f_bias.shape[0]
    spread = np.zeros((3, heads, CP_WIDTH), np.float32)
    for i in range(3):
        spread[i, np.arange(heads), i * heads + np.arange(heads)] = 1.0
    one = np.zeros((1, CP_WIDTH), np.float32)
    one[0, 3 * heads] = 1.0
    once = lambda a: pl.BlockSpec(a.shape, lambda i: (0,) * a.ndim, pipeline_mode=pl.Buffered(1))
    row = lambda width: pl.BlockSpec((tm, width), lambda i: (i, 0))
    flat = lambda dt: (row(d), jax.ShapeDtypeStruct((t, d), dt))
    tiles_per_seq = seq // tm
    v_aug = (pl.BlockSpec((None, heads, tm, LANES), lambda i: (i // tiles_per_seq, 0, i % tiles_per_seq, 0)),
             jax.ShapeDtypeStruct((t // seq, heads, seq, LANES), BF16))
    outs = [flat(BF16), flat(F32), flat(BF16), flat(BF16), flat(BF16),
            flat(BF16), flat(BF16), v_aug, flat(BF16), flat(BF16),
            (row(heads), jax.ShapeDtypeStruct((t, heads), F32)),
            (row(CP_WIDTH), jax.ShapeDtypeStruct((t, CP_WIDTH), BF16)),
            (pl.BlockSpec((None, SUBLANES, d), lambda i: (i, 0, 0)),
             jax.ShapeDtypeStruct((t // tm, SUBLANES, d), F32))]
    consts = (gain.reshape(1, d), w_main, w_gates, w_forget, lb_logits, f_bias.reshape(1, heads),
              jnp.asarray(spread, BF16), jnp.asarray(one))
    return pl.pallas_call(
        functools.partial(_in_proj_kernel, tn=tn, seq=seq),
        grid=(t // tm,),
        in_specs=[pl.BlockSpec(memory_space=pltpu.SMEM), row(d)] + [once(a) for a in consts],
        out_specs=[spec for spec, _ in outs],
        out_shape=[shape for _, shape in outs],
        scratch_shapes=[pltpu.VMEM((tm, d), BF16), pltpu.VMEM((1, heads), F32)],
        compiler_params=_cparams(("arbitrary",)),
        name="in_proj",
    )(head_slot, x2d, *consts)


def _causal_chunk():
    r_i = lax.broadcasted_iota(jnp.int32, (HG_CHUNK, HG_CHUNK), 0)
    c_i = lax.broadcasted_iota(jnp.int32, (HG_CHUNK, HG_CHUNK), 1)
    return c_i <= r_i


def _chunk_cumsum(lf):
    lower = _causal_chunk().astype(BF16)
    b = jnp.zeros(lf.shape, F32)
    for piece in _split3(lf):
        b = b + jnp.dot(lower, piece, preferred_element_type=F32)
    return b


def _hgrn_chunk_single_ref(q, b, k, v, state):
    c = HG_CHUNK
    q_t = (q * jnp.exp(b)).astype(BF16)
    k_up = k * jnp.exp(-b)
    sc = lax.dot_general(q_t, k_up.astype(BF16), (((1,), (1,)), ((), ())), preferred_element_type=F32)
    sc = jnp.where(_causal_chunk(), sc, 0.0).astype(BF16)
    v_bf = v.astype(BF16)
    o = jnp.dot(jnp.concatenate([q_t, sc], axis=1), jnp.concatenate([state.astype(BF16), v_bf], axis=0),
                preferred_element_type=F32)
    b_end = b[c - 1:c]
    k_end = k_up * jnp.exp(b_end)
    tr = jnp.concatenate([k_end, jnp.broadcast_to(b_end, (HG_DIM - c, HG_DIM))], axis=0).T
    upd = jnp.dot(tr[:, :c].astype(BF16), v_bf, preferred_element_type=F32)
    return o, state * jnp.exp(tr[:, c:c + 1]) + upd


def _hgrn_chunk(q, b, k, v, state):
    c = HG_CHUNK
    n_sub = c // HG_SUB
    state_t = state.T

    qe = (q * jnp.exp(b)).astype(BF16)
    o = lax.dot_general(qe, state_t.astype(BF16), (((1,), (1,)), ((), ())), preferred_element_type=F32)

    o_parts = []
    t_iota = lax.broadcasted_iota(jnp.int32, (HG_SUB, HG_DIM), 0)
    for i in range(n_sub):
        lo = i * HG_SUB
        q_i = q[lo:lo + HG_SUB]
        b_i = b[lo:lo + HG_SUB]
        o_i = o[lo:lo + HG_SUB]
        if i > 0:
            ref = b[lo - 1:lo]
            q_t = (q_i * jnp.exp(b_i - ref)).astype(BF16)
            k_t = (k[:lo] * jnp.exp(ref - b[:lo])).astype(BF16)
            sc = lax.dot_general(q_t, k_t, (((1,), (1,)), ((), ())), preferred_element_type=F32)
            o_i = o_i + jnp.dot(sc.astype(BF16), v[:lo].astype(BF16), preferred_element_type=F32)
        for s in range(HG_SUB):
            row = lo + s
            rel = jnp.where(t_iota >= s, b_i - b[row:row + 1], NEG_INF)
            p = q_i * jnp.exp(rel) * k[row:row + 1]
            o_i = o_i + jnp.sum(p, axis=-1, keepdims=True) * v[row:row + 1]
        o_parts.append(o_i)
    o = jnp.concatenate(o_parts, axis=0)

    b_end = b[c - 1:c]
    k_dec = (k * jnp.exp(b_end - b)).astype(BF16)
    upd = lax.dot_general(v.astype(BF16), k_dec, (((0,), (0,)), ((), ())), preferred_element_type=F32)
    return o, (state_t * jnp.exp(b_end) + upd).T


def _hgrn_kernel(lfmin_ref, q_ref, lf_ref, k_ref, v_ref, g_ref, gn_ref, o_ref, state_ref, *, n_sub, n_chunks):
    @pl.when(pl.program_id(2) == 0)
    def _():
        state_ref[...] = jnp.zeros_like(state_ref)

    def run_chunk(chunk_fn, rows, b, state):
        load = lambda ref: ref[rows, :].astype(F32)
        o, new_state = chunk_fn(load(q_ref), b, load(k_ref), load(v_ref), state)
        y = o * lax.rsqrt(jnp.mean(o * o, axis=-1, keepdims=True) + EPS)
        o_ref[rows, :] = (y * gn_ref[...] * load(g_ref)).astype(o_ref.dtype)
        return new_state

    head_row = pl.program_id(0) * pl.num_programs(1) + pl.program_id(1)

    def sub_tile(si, carry):
        base = si * (n_chunks * HG_CHUNK)
        chunk_rows = [pl.ds(pl.multiple_of(base + ci * HG_CHUNK, HG_CHUNK), HG_CHUNK) for ci in range(n_chunks)]
        lf_min = lfmin_ref[head_row, pl.program_id(2) * n_sub + si]
        single_ref_ok = lf_min * HG_CHUNK > -HG_MAX_EXPONENT

        @pl.when(single_ref_ok)
        def _():
            b_all = _chunk_cumsum(jnp.concatenate([lf_ref[rows, :] for rows in chunk_rows], axis=1))
            state = state_ref[...]
            for ci, rows in enumerate(chunk_rows):
                state = run_chunk(_hgrn_chunk_single_ref, rows, b_all[:, ci * HG_DIM:(ci + 1) * HG_DIM], state)
            state_ref[...] = state

        @pl.when(jnp.logical_not(single_ref_ok))
        def _():
            def body(ci, inner):
                rows = pl.ds(pl.multiple_of(base + ci * HG_CHUNK, HG_CHUNK), HG_CHUNK)
                state_ref[...] = run_chunk(_hgrn_chunk, rows, _chunk_cumsum(lf_ref[rows, :]), state_ref[...])
                return inner

            lax.fori_loop(0, n_chunks, body, 0)

        return carry

    lax.fori_loop(0, n_sub, sub_tile, 0)


def _hgrn(lf_min, q, lf, k, v, g, g_norm, tc, n_sub):
    b, s, w = q.shape
    heads = w // HG_DIM
    step = tc * n_sub
    spec = pl.BlockSpec((None, step, HG_DIM), lambda bi, hi, ti: (bi, ti, hi))
    return pl.pallas_call(
        functools.partial(_hgrn_kernel, n_sub=n_sub, n_chunks=tc // HG_CHUNK),
        grid=(b, heads, s // step),
        in_specs=[pl.BlockSpec(memory_space=pltpu.SMEM)] + [spec] * 5
                 + [pl.BlockSpec((1, HG_DIM), lambda bi, hi, ti: (0, 0))],
        out_specs=spec,
        out_shape=jax.ShapeDtypeStruct((b, s, w), BF16),
        scratch_shapes=[pltpu.VMEM((HG_DIM, HG_DIM), F32)],
        compiler_params=_cparams(("parallel", "parallel", "arbitrary")),
        name="hgrn2_mixer",
    )(lf_min, q, lf, k, v, g, g_norm.reshape(1, HG_DIM))


AUG_C = FOX_DIM
AUG_ONE = FOX_DIM + 3
AUG_QN = FOX_DIM + 6
NORM_MARGIN = 1.02
SKIP_LOG2 = 150.0


def _placement_constants(heads):
    pq = np.zeros((CP_WIDTH + heads, heads * LANES), np.float32)
    pk = np.zeros((CP_WIDTH, heads * LANES), np.float32)
    hs = np.zeros((heads * FOX_DIM, heads), np.float32)
    for h in range(heads):
        for i in range(3):
            pq[i * heads + h, h * LANES + AUG_C + i] = 1.0
            pk[i * heads + h, h * LANES + AUG_ONE + i] = -1.0
        pq[3 * heads, h * LANES + AUG_ONE:h * LANES + AUG_ONE + 3] = 1.0
        pk[3 * heads, h * LANES + AUG_C:h * LANES + AUG_C + 3] = 1.0
        pq[CP_WIDTH + h, h * LANES + AUG_QN] = 1.0
        hs[h * FOX_DIM:(h + 1) * FOX_DIM, h] = 1.0
    return tuple(jnp.asarray(a, BF16) for a in (pq, pk, hs))


def _fox_pack_kernel(slot_ref, q_ref, k_ref, cp_ref, c_ref, pq_ref, pk_ref, hs_ref, qa_ref, ka_ref, st_ref):
    q = q_ref[...]
    k = k_ref[...]
    cp = cp_ref[...]
    tm = q.shape[0]
    heads = qa_ref.shape[0]
    hs = hs_ref[...]
    qn = jnp.sqrt(jnp.dot(q * q, hs, preferred_element_type=F32)) * NORM_MARGIN
    kn = jnp.sqrt(jnp.dot(k * k, hs, preferred_element_type=F32)) * NORM_MARGIN
    tail_q = jnp.dot(jnp.concatenate([cp, qn.astype(BF16)], axis=1), pq_ref[...],
                     preferred_element_type=F32).astype(BF16)
    tail_k = jnp.dot(cp, pk_ref[...], preferred_element_type=F32).astype(BF16)
    lane = lax.broadcasted_iota(jnp.int32, (tm, LANES), 1)
    in_data = lane < FOX_DIM
    for h in range(heads):
        pair = slice((h // 2) * LANES, (h // 2 + 1) * LANES)
        mine = slice(h * LANES, (h + 1) * LANES)
        data = (lambda x: x[:, pair]) if h % 2 == 0 else (lambda x: pltpu.roll(x[:, pair], FOX_DIM, 1))
        qa_ref[slot_ref[h]] = jnp.where(in_data, data(q), tail_q[:, mine])
        ka_ref[slot_ref[h]] = jnp.where(in_data, data(k), tail_k[:, mine])
    tk = tm // st_ref.shape[0]
    for j in range(st_ref.shape[0]):
        st_ref[j, 0:1, :] = jnp.max(kn[j * tk:(j + 1) * tk], axis=0, keepdims=True)
        st_ref[j, 1:2, :] = c_ref[(j + 1) * tk - 1:(j + 1) * tk, :]


def _fox_pack(head_slot, q, k, cp, c, tm, tk):
    b, s, w = q.shape
    heads = c.shape[-1]
    consts = _placement_constants(heads)
    row = lambda width: pl.BlockSpec((None, tm, width), lambda bi, ti: (bi, ti, 0))
    full = lambda a: pl.BlockSpec(a.shape, lambda bi, ti: (0, 0))
    out = pl.BlockSpec((None, heads, tm, LANES), lambda bi, ti: (bi, 0, ti, 0))
    oshape = jax.ShapeDtypeStruct((b, heads, s, LANES), BF16)
    return pl.pallas_call(
        _fox_pack_kernel,
        grid=(b, s // tm),
        in_specs=[pl.BlockSpec(memory_space=pltpu.SMEM), row(w), row(w), row(CP_WIDTH), row(heads)]
                 + [full(a) for a in consts],
        out_specs=[out] * 2 + [pl.BlockSpec((None, tm // tk, 2, heads), lambda bi, ti: (bi, ti, 0, 0))],
        out_shape=[oshape] * 2 + [jax.ShapeDtypeStruct((b, s // tk, 2, heads), F32)],
        compiler_params=_cparams(("parallel", "parallel")),
        name="fox_pack",
    )(head_slot, q, k, cp, c, *consts)


def _fox_kernel(kmax_ref, clast_ref, q_ref, k_ref, v_ref, o_ref, m_ref, acc_ref, *, tq, tk, group):
    qi = pl.program_id(2)
    row_a = pl.program_id(0) * (2 * pl.num_programs(1)) + 2 * pl.program_id(1)
    m_ref[...] = jnp.full_like(m_ref, NEG_INF)
    acc_ref[...] = jnp.zeros_like(acc_ref)

    def scores(row0, n_rows, kv0, masked):
        rows = pl.ds(row0, n_rows)
        cols = pl.ds(kv0, tk)
        if masked:
            causal = (lax.broadcasted_iota(jnp.int32, (n_rows, tk), 1)
                      <= lax.broadcasted_iota(jnp.int32, (n_rows, tk), 0))
        out = []
        for a in range(2):
            s = lax.dot_general(q_ref[a, rows, :], k_ref[a, cols, :], (((1,), (1,)), ((), ())),
                                preferred_element_type=F32)
            out.append(jnp.where(causal, s, NEG_INF) if masked else s)
        return out

    def update(row0, n_rows, kv0, s_pair):
        rows = pl.ds(row0, n_rows)
        cols = pl.ds(kv0, tk)
        for a, s in enumerate(s_pair):
            m_prev = m_ref[a, rows, :]
            m_new = jnp.maximum(m_prev, jnp.max(s, axis=-1, keepdims=True))
            alpha = jnp.exp2(m_prev - m_new)
            p = jnp.exp2(s - jnp.tile(m_new, (1, tk // LANES)))
            acc_ref[a, rows, :] = alpha * acc_ref[a, rows, :] + jnp.dot(
                p.astype(BF16), v_ref[a, cols, :], preferred_element_type=F32)
            m_ref[a, rows, :] = m_new

    n_diag = tq // tk
    n_full = qi * n_diag

    def block(row0, n_rows, kv0, masked):
        update(row0, n_rows, kv0, scores(row0, n_rows, kv0, masked))

    n_tiles = kmax_ref.shape[1]
    bound = []
    for a in range(2):
        kmax = kmax_ref[row_a + a, 0]
        for j in range(1, n_tiles):
            kmax = jnp.maximum(kmax, kmax_ref[row_a + a, j])
        sel_row = lax.broadcasted_iota(jnp.int32, (LANES, LANES), 0)
        sel = jnp.where(sel_row == AUG_QN, kmax * NORM_MARGIN,
                        jnp.where((sel_row >= AUG_C) & (sel_row < AUG_ONE), 1.0, 0.0)).astype(BF16)
        bound.append(jnp.dot(q_ref[a], sel, preferred_element_type=F32))

    diag = [(j * tk, tq - j * tk, pl.multiple_of((n_full + j) * tk, tk)) for j in range(n_diag)]
    s_next = scores(*diag[0], True)
    for j in range(n_diag):
        s_cur = s_next
        if j + 1 < n_diag:
            s_next = scores(*diag[j + 1], True)
        update(*diag[j], s_cur)

    n_groups = tq // group
    slack = []
    for a in range(2):
        gap = bound[a] - m_ref[a]
        slack.append([jnp.max(gap[g * group:(g + 1) * group]) for g in range(n_groups)])

    def live_groups(j):
        n = jnp.int32(0)
        for a in range(2):
            c_last = clast_ref[row_a + a, jnp.maximum(j, 0)]
            for g in range(n_groups):
                n = jnp.where(slack[a][g] - c_last > -SKIP_LOG2, jnp.maximum(n, g + 1), n)
        return jnp.where(j >= 0, n, 0)

    def body(carry):
        j, n_live = carry
        kv0 = pl.multiple_of(j * tk, tk)
        for n in range(1, n_groups + 1):
            @pl.when(n_live == n)
            def _():
                block(0, n * group, kv0, False)
        return j - 1, live_groups(j - 1)

    lax.while_loop(lambda carry: carry[1] > 0, body, (n_full - 1, live_groups(n_full - 1)))

    lane = lax.broadcasted_iota(jnp.int32, (tq, LANES), 1)
    acc_a = acc_ref[0]
    acc_b = acc_ref[1]
    o_a = acc_a / acc_a[:, FOX_DIM:FOX_DIM + 1]
    o_b = acc_b / acc_b[:, FOX_DIM:FOX_DIM + 1]
    o_ref[...] = jnp.where(lane < FOX_DIM, o_a, pltpu.roll(o_b, FOX_DIM, 1)).astype(o_ref.dtype)


def _fox_attention(q, k, v, kmax, clast, tq, tk, group):
    b, h, s, _ = q.shape
    kv_spec = pl.BlockSpec((None, 2, s, LANES), lambda bi, hi, qi: (bi, hi, 0, 0))
    smem = pl.BlockSpec(memory_space=pltpu.SMEM)
    return pl.pallas_call(
        functools.partial(_fox_kernel, tq=tq, tk=tk, group=group),
        grid=(b, h // 2, s // tq),
        in_specs=[smem, smem,
                  pl.BlockSpec((None, 2, tq, LANES), lambda bi, hi, qi: (bi, hi, qi, 0)), kv_spec, kv_spec],
        out_specs=pl.BlockSpec((None, tq, LANES), lambda bi, hi, qi: (bi, qi, hi)),
        out_shape=jax.ShapeDtypeStruct((b, s, (h // 2) * LANES), BF16),
        scratch_shapes=[pltpu.VMEM((2, tq, LANES), F32), pltpu.VMEM((2, tq, LANES), F32)],
        compiler_params=_cparams(("parallel", "parallel", "arbitrary")),
        name="fox_attention",
    )(kmax, clast, q, k, v)


def _merge_kernel(x_ref, oa_ref, ob_ref, ga_ref, gb_ref, wa_ref, wb_ref, wo_ref, gn_ref, h_ref, n_ref, mg_ref,
                  *, tn):
    oa = oa_ref[...]
    ob = ob_ref[...]
    for c in range(mg_ref.shape[1] // tn):
        cols = slice(c * tn, (c + 1) * tn)
        ya = jnp.dot(oa, wa_ref[:, cols], preferred_element_type=F32)
        yb = jnp.dot(ob, wb_ref[:, cols], preferred_element_type=F32)
        mg_ref[:, cols] = (ga_ref[:, cols] * ya + gb_ref[:, cols] * yb).astype(mg_ref.dtype)
    h = x_ref[...] + jnp.dot(mg_ref[...], wo_ref[...], preferred_element_type=F32)
    h_ref[...] = h
    y = h * lax.rsqrt(jnp.mean(h * h, axis=-1, keepdims=True) + EPS)
    n_ref[...] = (y * gn_ref[...]).astype(n_ref.dtype)


def _merge(x2d, oa, ob, ga, gb, wa, wb, wo, gain, tm, tn):
    t, d = x2d.shape
    row = lambda width: pl.BlockSpec((tm, width), lambda i: (i, 0))
    once = lambda a: pl.BlockSpec(a.shape, lambda i: (0,) * a.ndim, pipeline_mode=pl.Buffered(1))
    gain2 = gain.reshape(1, d)
    return pl.pallas_call(
        functools.partial(_merge_kernel, tn=tn),
        grid=(t // tm,),
        in_specs=[row(d), row(oa.shape[1]), row(ob.shape[1]), row(d), row(d), once(wa), once(wb), once(wo),
                  once(gain2)],
        out_specs=[row(d), row(d)],
        out_shape=[jax.ShapeDtypeStruct((t, d), F32), jax.ShapeDtypeStruct((t, d), BF16)],
        scratch_shapes=[pltpu.VMEM((tm, d), BF16)],
        compiler_params=_cparams(("parallel",)),
        name="merge_outproj",
    )(x2d, oa, ob, ga, gb, wa, wb, wo, gain2)


def _ffn_kernel(n_ref, halo_ref, h_ref, wu_ref, cw_ref, cb_ref, wd_ref, gn_ref, o_ref, next_ref, act_ref,
                *, tm, seq, tf):
    i = pl.program_id(0)
    at_start = (i * tm) % seq == 0
    next_ref[0:HALO, :] = jnp.where(at_start, jnp.zeros_like(halo_ref[...]), halo_ref[...])
    next_ref[HALO:, :] = n_ref[...]
    n_ext = next_ref[...]
    d_ff = act_ref.shape[1]

    def conv(col0):
        cols = pl.ds(col0, tf)
        u = jnp.dot(n_ext, wu_ref[:, cols], preferred_element_type=F32)
        out = cb_ref[:, cols]
        for tap in range(CONV_WIDTH):
            lo = HALO - (CONV_WIDTH - 1) + tap
            out = out + cw_ref[tap:tap + 1, cols] * u[lo:lo + tm]
        return out

    for c in range(d_ff // tf):
        gate = conv(c * tf)
        val = conv(d_ff + c * tf)
        act = 0.5 * gate * (1.0 + lax.erf(gate * (2.0 ** -0.5))) * val
        act_ref[:, c * tf:(c + 1) * tf] = act.astype(act_ref.dtype)

    h = h_ref[...] + jnp.dot(act_ref[...], wd_ref[...], preferred_element_type=F32)
    y = h * lax.rsqrt(jnp.mean(h * h, axis=-1, keepdims=True) + EPS)
    o_ref[...] = (y * gn_ref[...]).astype(o_ref.dtype)


def _ffn(n2d, h2d, w_up, conv_w, conv_b, w_down, gain, seq, tm, tf):
    t, d = n2d.shape
    d_ff = w_down.shape[0]
    halo_blocks = tm // HALO
    cb = conv_b.reshape(1, 2 * d_ff)
    once = lambda a: pl.BlockSpec(a.shape, lambda i: (0,) * a.ndim, pipeline_mode=pl.Buffered(1))
    return pl.pallas_call(
        functools.partial(_ffn_kernel, tm=tm, seq=seq, tf=tf),
        grid=(t // tm,),
        in_specs=[pl.BlockSpec((tm, d), lambda i: (i, 0)),
                  pl.BlockSpec((HALO, d), lambda i: (jnp.maximum(i * halo_blocks - 1, 0), 0)),
                  pl.BlockSpec((tm, d), lambda i: (i, 0)),
                  once(w_up), once(conv_w), once(cb), once(w_down),
                  pl.BlockSpec((1, d), lambda i: (0, 0))],
        out_specs=pl.BlockSpec((tm, d), lambda i: (i, 0)),
        out_shape=jax.ShapeDtypeStruct((t, d), F32),
        scratch_shapes=[pltpu.VMEM((HALO + tm, d), BF16), pltpu.VMEM((tm, d_ff), BF16)],
        compiler_params=_cparams(("parallel",)),
        name="conv_glu_ffn",
    )(n2d, n2d, h2d, w_up, conv_w, cb, w_down, gain.reshape(1, d))


class _Tiles(NamedTuple):
    rows: int
    wide_rows: int
    cols: int
    hgrn: int
    hgrn_sub: int
    query: int
    group: int


def _tiles(seq):
    fit = lambda pref, n=seq: pref if n % pref == 0 else n
    rows = fit(512)
    hgrn = fit(2 * rows)
    return _Tiles(rows=rows, wide_rows=fit(2 * rows), cols=256, hgrn=hgrn, hgrn_sub=4 if (seq // hgrn) % 4 == 0 else 1,
                  query=fit(4 * rows), group=fit(256, rows))


def kernel(x, norm_mix, w_in, fox_f_bias, hg_lb_logits, hg_norm, w_branch_a, w_branch_b, w_out, norm_ffn, w_up,
           conv_w, conv_b, w_down, norm_final):
    b, s, d = x.shape
    assert norm_mix.shape[0] == 1, "single-layer trunk"
    t = b * s
    tiles = _tiles(s)
    assert HG_HEADS * HG_DIM == d and FOX_HEADS * FOX_DIM == d, "the projection kernel walks equal-width segments"
    f_lo, f_hi = N_MAIN * d, N_MAIN * d + FOX_HEADS
    assert w_in.shape[-1] == f_hi + 2 * d
    w_main, w_forget, w_gates = (w_in[0, :, lo:hi].astype(BF16) for lo, hi in ((0, f_lo), (f_lo, f_hi), (f_hi, None)))

    head_order = jnp.argsort(fox_f_bias[0])
    head_slot = jnp.argsort(head_order).astype(jnp.int32)

    x2d = x.reshape(t, d)
    hq, hlf, hk, hv, hgate, fq, fk, v_aug, ga, gb, c2, c_pieces, lf_colmin = _in_proj(
        head_slot, x2d, norm_mix[0], w_main, w_gates, w_forget, hg_lb_logits, fox_f_bias[0], s, tiles.rows,
        tiles.cols)

    r3 = lambda a: a.reshape(b, s, a.shape[-1])
    lf_min = lf_colmin[:, 0, :].reshape(b, s // tiles.hgrn, tiles.hgrn // tiles.rows, HG_HEADS, HG_DIM)
    lf_min = lf_min.min(axis=(2, 4)).transpose(0, 2, 1).reshape(b * HG_HEADS, s // tiles.hgrn)
    o_a = _hgrn(lf_min, r3(hq), r3(hlf), r3(hk), r3(hv), r3(hgate), hg_norm[0], tiles.hgrn, tiles.hgrn_sub)

    q_aug, k_aug, kv_stats = _fox_pack(head_slot, r3(fq), r3(fk), r3(c_pieces), r3(c2), tiles.wide_rows, tiles.rows)
    kv_stats = kv_stats[..., head_order]
    kv_stats = kv_stats.transpose(2, 0, 3, 1).reshape(2, b * FOX_HEADS, s // tiles.rows)
    o_b = _fox_attention(q_aug, k_aug, v_aug, kv_stats[0], kv_stats[1], tiles.query, tiles.rows,
                          tiles.group)
    w_b = w_branch_b[0].reshape(FOX_HEADS, FOX_DIM, d)[head_order].reshape(d, d)

    h1, n_ffn = _merge(x2d, o_a.reshape(t, d), o_b.reshape(t, d), ga, gb, w_branch_a[0].astype(BF16),
                       w_b.astype(BF16), w_out[0].astype(BF16), norm_ffn[0], tiles.wide_rows, tiles.cols)
    out = _ffn(n_ffn, h1, w_up[0].astype(BF16), conv_w[0], conv_b[0], w_down[0].astype(BF16), norm_final,
               s, tiles.wide_rows, tiles.cols)
    return out.reshape(b, s, d)
```

```python
import functools
import math
from typing import NamedTuple

import jax
import jax.numpy as jnp
import numpy as np
from jax import lax
from jax.experimental import pallas as pl
from jax.experimental.pallas import tpu as pltpu

F32 = jnp.float32
BF16 = jnp.bfloat16

EPS = 1e-6
HG_HEADS = 8
HG_DIM = 128
HG_CHUNK = 64
HG_SUB = 16
HG_MAX_EXPONENT = 64.0
FOX_HEADS = 16
FOX_DIM = 64
CONV_WIDTH = 3
HALO = 8
NEG_INF = float("-inf")
LOG2E = math.log2(math.e)
LANES = 128
SUBLANES = 8

VMEM_LIMIT = 56 * 1024 * 1024


def _cparams(sem):
    return pltpu.CompilerParams(dimension_semantics=sem, vmem_limit_bytes=VMEM_LIMIT)


def _sigmoid(x):
    return 1.0 / (1.0 + jnp.exp(-x))


def _silu(x):
    return x * _sigmoid(x)


def _log_sigmoid(x):
    return jnp.minimum(x, 0.0) - jnp.log1p(jnp.exp(-jnp.abs(x)))


def _split3(x):
    x1 = x.astype(BF16)
    r1 = x - x1.astype(F32)
    x2 = r1.astype(BF16)
    x3 = (r1 - x2.astype(F32)).astype(BF16)
    return x1, x2, x3


SEG_HQ, SEG_HF, SEG_HI, SEG_HG, SEG_FQ, SEG_FK, SEG_FV, SEG_GA, SEG_GB = range(9)
N_MAIN = 7


def _in_proj_kernel(x_ref, gain_ref, w_ref, wgate_ref, wf_ref, lbl_ref, fbias_ref, spread_ref, one_ref,
                    hq_ref, hlf_ref, hk_ref, hv_ref, hg_ref, fq_ref, fk_ref, va_ref, ga_ref, gb_ref, c_ref, cp_ref,
                    lfmin_ref, n_ref, carry_ref, *, tn, seq):
    x = x_ref[...]
    n_ref[...] = (x * lax.rsqrt(jnp.mean(x * x, axis=-1, keepdims=True) + EPS) * gain_ref[...]).astype(BF16)
    n = n_ref[...]
    width = hq_ref.shape[1]
    lbl = lbl_ref[...]
    e = jnp.exp(lbl - jnp.max(lbl, axis=0, keepdims=True))
    lb = e[0:1] / jnp.sum(e, axis=0, keepdims=True)

    def chunks(seg):
        ref, first = (w_ref, seg * width) if seg < N_MAIN else (wgate_ref, (seg - N_MAIN) * width)
        for c in range(width // tn):
            cols = slice(c * tn, (c + 1) * tn)
            yield cols, jnp.dot(n, ref[:, first + c * tn:first + (c + 1) * tn], preferred_element_type=F32)

    def emit(seg, ref, fn):
        for cols, acc in chunks(seg):
            ref[:, cols] = fn(acc).astype(ref.dtype)

    @pl.when((pl.program_id(0) * x.shape[0]) % seq == 0)
    def _():
        carry_ref[...] = jnp.zeros_like(carry_ref)

    lf = _log_sigmoid(jnp.dot(n, wf_ref[...], preferred_element_type=F32) + fbias_ref[...])
    tm = lf.shape[0]
    lower = (lax.broadcasted_iota(jnp.int32, (tm, tm), 1)
             <= lax.broadcasted_iota(jnp.int32, (tm, tm), 0)).astype(BF16)
    cs = carry_ref[...]
    for piece in _split3(lf):
        cs = cs + jnp.dot(lower, piece, preferred_element_type=F32)
    carry_ref[...] = cs[tm - 1:tm, :]
    c2 = cs * LOG2E
    c_ref[...] = c2
    cp = one_ref[...]
    for i, piece in enumerate(_split3(c2)):
        cp = cp + jnp.dot(piece, spread_ref[i], preferred_element_type=F32)
    cp_ref[...] = cp.astype(cp_ref.dtype)

    emit(SEG_HQ, hq_ref, _silu)
    for cols, acc in chunks(SEG_HF):
        f = lb[:, cols] + (1.0 - lb[:, cols]) * _sigmoid(acc)
        lf_h = jnp.log(f)
        hlf_ref[:, cols] = lf_h
        hk_ref[:, cols] = (1.0 - f).astype(hk_ref.dtype)
        lfmin_ref[:, cols] = jnp.broadcast_to(jnp.min(lf_h, axis=0, keepdims=True), (lfmin_ref.shape[0], tn))
    emit(SEG_HI, hv_ref, lambda a: a)
    emit(SEG_HG, hg_ref, _silu)
    emit(SEG_FQ, fq_ref, lambda a: a * (FOX_DIM ** -0.5 * LOG2E))
    emit(SEG_FK, fk_ref, lambda a: a)
    lane = lax.broadcasted_iota(jnp.int32, (tm, LANES), 1)
    tail_v = jnp.where(lane == FOX_DIM, 1.0, 0.0)
    for cols, acc in chunks(SEG_FV):
        for j in range(tn // FOX_DIM):
            pair = acc[:, (j // 2) * LANES:(j // 2 + 1) * LANES]
            data = pair if j % 2 == 0 else pltpu.roll(pair, FOX_DIM, 1)
            va_ref[cols.start // FOX_DIM + j] = jnp.where(lane < FOX_DIM, data, tail_v).astype(va_ref.dtype)
    emit(SEG_GA, ga_ref, _sigmoid)
    emit(SEG_GB, gb_ref, _sigmoid)


CP_WIDTH = 4 * FOX_HEADS


def _in_proj(x2d, gain, w_main, w_gates, w_forget, lb_logits, f_bias, seq, tm, tn):
    t, d = x2d.shape
    heads = f_bias.shape[0]
    spread = np.zeros((3, heads, CP_WIDTH), np.float32)
    for i in range(3):
        spread[i, np.arange(heads), i * heads + np.arange(heads)] = 1.0
    one = np.zeros((1, CP_WIDTH), np.float32)
    one[0, 3 * heads] = 1.0
    once = lambda a: pl.BlockSpec(a.shape, lambda i: (0,) * a.ndim, pipeline_mode=pl.Buffered(1))
    row = lambda width: pl.BlockSpec((tm, width), lambda i: (i, 0))
    flat = lambda dt: (row(d), jax.ShapeDtypeStruct((t, d), dt))
    tiles_per_seq = seq // tm
    v_aug = (pl.BlockSpec((None, heads, tm, LANES), lambda i: (i // tiles_per_seq, 0, i % tiles_per_seq, 0)),
             jax.ShapeDtypeStruct((t // seq, heads, seq, LANES), BF16))
    outs = [flat(BF16), flat(F32), flat(BF16), flat(BF16), flat(BF16),
            flat(BF16), flat(BF16), v_aug, flat(BF16), flat(BF16),
            (row(heads), jax.ShapeDtypeStruct((t, heads), F32)),
            (row(CP_WIDTH), jax.ShapeDtypeStruct((t, CP_WIDTH), BF16)),
            (pl.BlockSpec((None, SUBLANES, d), lambda i: (i, 0, 0)),
             jax.ShapeDtypeStruct((t // tm, SUBLANES, d), F32))]
    consts = (gain.reshape(1, d), w_main, w_gates, w_forget, lb_logits, f_bias.reshape(1, heads),
              jnp.asarray(spread, BF16), jnp.asarray(one))
    return pl.pallas_call(
        functools.partial(_in_proj_kernel, tn=tn, seq=seq),
        grid=(t // tm,),
        in_specs=[row(d)] + [once(a) for a in consts],
        out_specs=[spec for spec, _ in outs],
        out_shape=[shape for _, shape in outs],
        scratch_shapes=[pltpu.VMEM((tm, d), BF16), pltpu.VMEM((1, heads), F32)],
        compiler_params=_cparams(("arbitrary",)),
        name="in_proj",
    )(x2d, *consts)


def _causal_chunk():
    r_i = lax.broadcasted_iota(jnp.int32, (HG_CHUNK, HG_CHUNK), 0)
    c_i = lax.broadcasted_iota(jnp.int32, (HG_CHUNK, HG_CHUNK), 1)
    return c_i <= r_i


def _chunk_cumsum(lf):
    lower = _causal_chunk().astype(BF16)
    b = jnp.zeros(lf.shape, F32)
    for piece in _split3(lf):
        b = b + jnp.dot(lower, piece, preferred_element_type=F32)
    return b


def _hgrn_chunk_single_ref(q, b, k, v, state):
    c = HG_CHUNK
    q_t = (q * jnp.exp(b)).astype(BF16)
    k_up = k * jnp.exp(-b)
    sc = lax.dot_general(q_t, k_up.astype(BF16), (((1,), (1,)), ((), ())), preferred_element_type=F32)
    sc = jnp.where(_causal_chunk(), sc, 0.0).astype(BF16)
    v_bf = v.astype(BF16)
    o = jnp.dot(jnp.concatenate([q_t, sc], axis=1), jnp.concatenate([state.astype(BF16), v_bf], axis=0),
                preferred_element_type=F32)
    b_end = b[c - 1:c]
    k_end = k_up * jnp.exp(b_end)
    tr = jnp.concatenate([k_end, jnp.broadcast_to(b_end, (HG_DIM - c, HG_DIM))], axis=0).T
    upd = jnp.dot(tr[:, :c].astype(BF16), v_bf, preferred_element_type=F32)
    return o, state * jnp.exp(tr[:, c:c + 1]) + upd


def _hgrn_chunk(q, b, k, v, state):
    c = HG_CHUNK
    n_sub = c // HG_SUB
    state_t = state.T

    qe = (q * jnp.exp(b)).astype(BF16)
    o = lax.dot_general(qe, state_t.astype(BF16), (((1,), (1,)), ((), ())), preferred_element_type=F32)

    o_parts = []
    t_iota = lax.broadcasted_iota(jnp.int32, (HG_SUB, HG_DIM), 0)
    for i in range(n_sub):
        lo = i * HG_SUB
        q_i = q[lo:lo + HG_SUB]
        b_i = b[lo:lo + HG_SUB]
        o_i = o[lo:lo + HG_SUB]
        if i > 0:
            ref = b[lo - 1:lo]
            q_t = (q_i * jnp.exp(b_i - ref)).astype(BF16)
            k_t = (k[:lo] * jnp.exp(ref - b[:lo])).astype(BF16)
            sc = lax.dot_general(q_t, k_t, (((1,), (1,)), ((), ())), preferred_element_type=F32)
            o_i = o_i + jnp.dot(sc.astype(BF16), v[:lo].astype(BF16), preferred_element_type=F32)
        for s in range(HG_SUB):
            row = lo + s
            rel = jnp.where(t_iota >= s, b_i - b[row:row + 1], NEG_INF)
            p = q_i * jnp.exp(rel) * k[row:row + 1]
            o_i = o_i + jnp.sum(p, axis=-1, keepdims=True) * v[row:row + 1]
        o_parts.append(o_i)
    o = jnp.concatenate(o_parts, axis=0)

    b_end = b[c - 1:c]
    k_dec = (k * jnp.exp(b_end - b)).astype(BF16)
    upd = lax.dot_general(v.astype(BF16), k_dec, (((0,), (0,)), ((), ())), preferred_element_type=F32)
    return o, (state_t * jnp.exp(b_end) + upd).T


def _hgrn_kernel(lfmin_ref, q_ref, lf_ref, k_ref, v_ref, g_ref, gn_ref, o_ref, state_ref, *, n_sub, n_chunks):
    @pl.when(pl.program_id(2) == 0)
    def _():
        state_ref[...] = jnp.zeros_like(state_ref)

    def run_chunk(chunk_fn, rows, b, state):
        load = lambda ref: ref[rows, :].astype(F32)
        o, new_state = chunk_fn(load(q_ref), b, load(k_ref), load(v_ref), state)
        y = o * lax.rsqrt(jnp.mean(o * o, axis=-1, keepdims=True) + EPS)
        o_ref[rows, :] = (y * gn_ref[...] * load(g_ref)).astype(o_ref.dtype)
        return new_state

    head_row = pl.program_id(0) * pl.num_programs(1) + pl.program_id(1)

    def sub_tile(si, carry):
        base = si * (n_chunks * HG_CHUNK)
        chunk_rows = [pl.ds(pl.multiple_of(base + ci * HG_CHUNK, HG_CHUNK), HG_CHUNK) for ci in range(n_chunks)]
        lf_min = lfmin_ref[head_row, pl.program_id(2) * n_sub + si]
        single_ref_ok = lf_min * HG_CHUNK > -HG_MAX_EXPONENT

        @pl.when(single_ref_ok)
        def _():
            b_all = _chunk_cumsum(jnp.concatenate([lf_ref[rows, :] for rows in chunk_rows], axis=1))
            state = state_ref[...]
            for ci, rows in enumerate(chunk_rows):
                state = run_chunk(_hgrn_chunk_single_ref, rows, b_all[:, ci * HG_DIM:(ci + 1) * HG_DIM], state)
            state_ref[...] = state

        @pl.when(jnp.logical_not(single_ref_ok))
        def _():
            def body(ci, inner):
                rows = pl.ds(pl.multiple_of(base + ci * HG_CHUNK, HG_CHUNK), HG_CHUNK)
                state_ref[...] = run_chunk(_hgrn_chunk, rows, _chunk_cumsum(lf_ref[rows, :]), state_ref[...])
                return inner

            lax.fori_loop(0, n_chunks, body, 0)

        return carry

    lax.fori_loop(0, n_sub, sub_tile, 0)


def _hgrn(lf_min, q, lf, k, v, g, g_norm, tc, n_sub):
    b, s, w = q.shape
    heads = w // HG_DIM
    step = tc * n_sub
    spec = pl.BlockSpec((None, step, HG_DIM), lambda bi, hi, ti: (bi, ti, hi))
    return pl.pallas_call(
        functools.partial(_hgrn_kernel, n_sub=n_sub, n_chunks=tc // HG_CHUNK),
        grid=(b, heads, s // step),
        in_specs=[pl.BlockSpec(memory_space=pltpu.SMEM)] + [spec] * 5
                 + [pl.BlockSpec((1, HG_DIM), lambda bi, hi, ti: (0, 0))],
        out_specs=spec,
        out_shape=jax.ShapeDtypeStruct((b, s, w), BF16),
        scratch_shapes=[pltpu.VMEM((HG_DIM, HG_DIM), F32)],
        compiler_params=_cparams(("parallel", "parallel", "arbitrary")),
        name="hgrn2_mixer",
    )(lf_min, q, lf, k, v, g, g_norm.reshape(1, HG_DIM))


AUG_C = FOX_DIM
AUG_ONE = FOX_DIM + 3
AUG_QN = FOX_DIM + 6
NORM_MARGIN = 1.02
SKIP_LOG2 = 150.0


def _placement_constants(heads):
    pq = np.zeros((CP_WIDTH + heads, heads * LANES), np.float32)
    pk = np.zeros((CP_WIDTH, heads * LANES), np.float32)
    hs = np.zeros((heads * FOX_DIM, heads), np.float32)
    for h in range(heads):
        for i in range(3):
            pq[i * heads + h, h * LANES + AUG_C + i] = 1.0
            pk[i * heads + h, h * LANES + AUG_ONE + i] = -1.0
        pq[3 * heads, h * LANES + AUG_ONE:h * LANES + AUG_ONE + 3] = 1.0
        pk[3 * heads, h * LANES + AUG_C:h * LANES + AUG_C + 3] = 1.0
        pq[CP_WIDTH + h, h * LANES + AUG_QN] = 1.0
        hs[h * FOX_DIM:(h + 1) * FOX_DIM, h] = 1.0
    return tuple(jnp.asarray(a, BF16) for a in (pq, pk, hs))


def _fox_pack_kernel(q_ref, k_ref, cp_ref, c_ref, pq_ref, pk_ref, hs_ref, qa_ref, ka_ref, st_ref):
    q = q_ref[...]
    k = k_ref[...]
    cp = cp_ref[...]
    tm = q.shape[0]
    heads = qa_ref.shape[0]
    hs = hs_ref[...]
    qn = jnp.sqrt(jnp.dot(q * q, hs, preferred_element_type=F32)) * NORM_MARGIN
    kn = jnp.sqrt(jnp.dot(k * k, hs, preferred_element_type=F32)) * NORM_MARGIN
    tail_q = jnp.dot(jnp.concatenate([cp, qn.astype(BF16)], axis=1), pq_ref[...],
                     preferred_element_type=F32).astype(BF16)
    tail_k = jnp.dot(cp, pk_ref[...], preferred_element_type=F32).astype(BF16)
    lane = lax.broadcasted_iota(jnp.int32, (tm, LANES), 1)
    in_data = lane < FOX_DIM
    for h in range(heads):
        pair = slice((h // 2) * LANES, (h // 2 + 1) * LANES)
        mine = slice(h * LANES, (h + 1) * LANES)
        data = (lambda x: x[:, pair]) if h % 2 == 0 else (lambda x: pltpu.roll(x[:, pair], FOX_DIM, 1))
        qa_ref[h] = jnp.where(in_data, data(q), tail_q[:, mine])
        ka_ref[h] = jnp.where(in_data, data(k), tail_k[:, mine])
    tk = tm // st_ref.shape[0]
    for j in range(st_ref.shape[0]):
        st_ref[j, 0:1, :] = jnp.max(kn[j * tk:(j + 1) * tk], axis=0, keepdims=True)
        st_ref[j, 1:2, :] = c_ref[(j + 1) * tk - 1:(j + 1) * tk, :]


def _fox_pack(q, k, cp, c, tm, tk):
    b, s, w = q.shape
    heads = c.shape[-1]
    consts = _placement_constants(heads)
    row = lambda width: pl.BlockSpec((None, tm, width), lambda bi, ti: (bi, ti, 0))
    full = lambda a: pl.BlockSpec(a.shape, lambda bi, ti: (0, 0))
    out = pl.BlockSpec((None, heads, tm, LANES), lambda bi, ti: (bi, 0, ti, 0))
    oshape = jax.ShapeDtypeStruct((b, heads, s, LANES), BF16)
    return pl.pallas_call(
        _fox_pack_kernel,
        grid=(b, s // tm),
        in_specs=[row(w), row(w), row(CP_WIDTH), row(heads)] + [full(a) for a in consts],
        out_specs=[out] * 2 + [pl.BlockSpec((None, tm // tk, 2, heads), lambda bi, ti: (bi, ti, 0, 0))],
        out_shape=[oshape] * 2 + [jax.ShapeDtypeStruct((b, s // tk, 2, heads), F32)],
        compiler_params=_cparams(("parallel", "parallel")),
        name="fox_pack",
    )(q, k, cp, c, *consts)


def _fox_kernel(order_ref, kmax_ref, clast_ref, qa_ref, qb_ref, ka_ref, kb_ref, va_ref, vb_ref, o_ref, m_ref, acc_ref,
                *, tq, tk, group):
    del order_ref
    q_refs, k_refs, v_refs = (qa_ref, qb_ref), (ka_ref, kb_ref), (va_ref, vb_ref)
    qi = pl.program_id(2)
    row_a = pl.program_id(0) * (2 * pl.num_programs(1)) + 2 * pl.program_id(1)
    m_ref[...] = jnp.full_like(m_ref, NEG_INF)
    acc_ref[...] = jnp.zeros_like(acc_ref)

    def scores(row0, n_rows, kv0, masked):
        rows = pl.ds(row0, n_rows)
        cols = pl.ds(kv0, tk)
        if masked:
            causal = (lax.broadcasted_iota(jnp.int32, (n_rows, tk), 1)
                      <= lax.broadcasted_iota(jnp.int32, (n_rows, tk), 0))
        out = []
        for a in range(2):
            s = lax.dot_general(q_refs[a][rows, :], k_refs[a][cols, :], (((1,), (1,)), ((), ())),
                                preferred_element_type=F32)
            out.append(jnp.where(causal, s, NEG_INF) if masked else s)
        return out

    def update(row0, n_rows, kv0, s_pair):
        rows = pl.ds(row0, n_rows)
        cols = pl.ds(kv0, tk)
        for a, s in enumerate(s_pair):
            m_prev = m_ref[a, rows, :]
            m_new = jnp.maximum(m_prev, jnp.max(s, axis=-1, keepdims=True))
            alpha = jnp.exp2(m_prev - m_new)
            p = jnp.exp2(s - jnp.tile(m_new, (1, tk // LANES)))
            acc_ref[a, rows, :] = alpha * acc_ref[a, rows, :] + jnp.dot(
                p.astype(BF16), v_refs[a][cols, :], preferred_element_type=F32)
            m_ref[a, rows, :] = m_new

    n_diag = tq // tk
    n_full = qi * n_diag

    def block(row0, n_rows, kv0, masked):
        update(row0, n_rows, kv0, scores(row0, n_rows, kv0, masked))

    n_tiles = kmax_ref.shape[1]
    bound = []
    for a in range(2):
        kmax = kmax_ref[row_a + a, 0]
        for j in range(1, n_tiles):
            kmax = jnp.maximum(kmax, kmax_ref[row_a + a, j])
        sel_row = lax.broadcasted_iota(jnp.int32, (LANES, LANES), 0)
        sel = jnp.where(sel_row == AUG_QN, kmax * NORM_MARGIN,
                        jnp.where((sel_row >= AUG_C) & (sel_row < AUG_ONE), 1.0, 0.0)).astype(BF16)
        bound.append(jnp.dot(q_refs[a][...], sel, preferred_element_type=F32))

    diag = [(j * tk, tq - j * tk, pl.multiple_of((n_full + j) * tk, tk)) for j in range(n_diag)]
    s_next = scores(*diag[0], True)
    for j in range(n_diag):
        s_cur = s_next
        if j + 1 < n_diag:
            s_next = scores(*diag[j + 1], True)
        update(*diag[j], s_cur)

    n_groups = tq // group
    slack = []
    for a in range(2):
        gap = bound[a] - m_ref[a]
        slack.append([jnp.max(gap[g * group:(g + 1) * group]) for g in range(n_groups)])

    def live_groups(j):
        n = jnp.int32(0)
        for a in range(2):
            c_last = clast_ref[row_a + a, jnp.maximum(j, 0)]
            for g in range(n_groups):
                n = jnp.where(slack[a][g] - c_last > -SKIP_LOG2, jnp.maximum(n, g + 1), n)
        return jnp.where(j >= 0, n, 0)

    def body(carry):
        j, n_live = carry
        kv0 = pl.multiple_of(j * tk, tk)
        for n in range(1, n_groups + 1):
            @pl.when(n_live == n)
            def _():
                block(0, n * group, kv0, False)
        return j - 1, live_groups(j - 1)

    lax.while_loop(lambda carry: carry[1] > 0, body, (n_full - 1, live_groups(n_full - 1)))

    lane = lax.broadcasted_iota(jnp.int32, (tq, LANES), 1)
    acc_a = acc_ref[0]
    acc_b = acc_ref[1]
    o_a = acc_a / acc_a[:, FOX_DIM:FOX_DIM + 1]
    o_b = acc_b / acc_b[:, FOX_DIM:FOX_DIM + 1]
    o_ref[...] = jnp.where(lane < FOX_DIM, o_a, pltpu.roll(o_b, FOX_DIM, 1)).astype(o_ref.dtype)


def _fox_attention(head_order, q, k, v, kmax, clast, tq, tk, group):
    b, h, s, _ = q.shape
    smem = pl.BlockSpec(memory_space=pltpu.SMEM)
    q_spec = lambda a: pl.BlockSpec((None, None, tq, LANES), lambda bi, hi, qi, order: (bi, order[2 * hi + a], qi, 0))
    kv_spec = lambda a: pl.BlockSpec((None, None, s, LANES), lambda bi, hi, qi, order: (bi, order[2 * hi + a], 0, 0))
    return pl.pallas_call(
        functools.partial(_fox_kernel, tq=tq, tk=tk, group=group),
        grid_spec=pltpu.PrefetchScalarGridSpec(
            num_scalar_prefetch=1,
            grid=(b, h // 2, s // tq),
            in_specs=[smem, smem, q_spec(0), q_spec(1), kv_spec(0), kv_spec(1), kv_spec(0), kv_spec(1)],
            out_specs=pl.BlockSpec((None, tq, LANES), lambda bi, hi, qi, order: (bi, qi, hi)),
            scratch_shapes=[pltpu.VMEM((2, tq, LANES), F32), pltpu.VMEM((2, tq, LANES), F32)]),
        out_shape=jax.ShapeDtypeStruct((b, s, (h // 2) * LANES), BF16),
        compiler_params=_cparams(("parallel", "parallel", "arbitrary")),
        name="fox_attention",
    )(head_order, kmax, clast, q, q, k, k, v, v)


def _merge_kernel(x_ref, oa_ref, ob_ref, ga_ref, gb_ref, wa_ref, wb_ref, wo_ref, gn_ref, h_ref, n_ref, mg_ref,
                  *, tn):
    oa = oa_ref[...]
    ob = ob_ref[...]
    for c in range(mg_ref.shape[1] // tn):
        cols = slice(c * tn, (c + 1) * tn)
        ya = jnp.dot(oa, wa_ref[:, cols], preferred_element_type=F32)
        yb = jnp.dot(ob, wb_ref[:, cols], preferred_element_type=F32)
        mg_ref[:, cols] = (ga_ref[:, cols] * ya + gb_ref[:, cols] * yb).astype(mg_ref.dtype)
    h = x_ref[...] + jnp.dot(mg_ref[...], wo_ref[...], preferred_element_type=F32)
    h_ref[...] = h
    y = h * lax.rsqrt(jnp.mean(h * h, axis=-1, keepdims=True) + EPS)
    n_ref[...] = (y * gn_ref[...]).astype(n_ref.dtype)


def _merge(x2d, oa, ob, ga, gb, wa, wb, wo, gain, tm, tn):
    t, d = x2d.shape
    row = lambda width: pl.BlockSpec((tm, width), lambda i: (i, 0))
    once = lambda a: pl.BlockSpec(a.shape, lambda i: (0,) * a.ndim, pipeline_mode=pl.Buffered(1))
    gain2 = gain.reshape(1, d)
    return pl.pallas_call(
        functools.partial(_merge_kernel, tn=tn),
        grid=(t // tm,),
        in_specs=[row(d), row(oa.shape[1]), row(ob.shape[1]), row(d), row(d), once(wa), once(wb), once(wo),
                  once(gain2)],
        out_specs=[row(d), row(d)],
        out_shape=[jax.ShapeDtypeStruct((t, d), F32), jax.ShapeDtypeStruct((t, d), BF16)],
        scratch_shapes=[pltpu.VMEM((tm, d), BF16)],
        compiler_params=_cparams(("parallel",)),
        name="merge_outproj",
    )(x2d, oa, ob, ga, gb, wa, wb, wo, gain2)


def _ffn_kernel(n_ref, halo_ref, h_ref, wu_ref, cw_ref, cb_ref, wd_ref, gn_ref, o_ref, next_ref, act_ref,
                *, tm, seq, tf):
    i = pl.program_id(0)
    at_start = (i * tm) % seq == 0
    next_ref[0:HALO, :] = jnp.where(at_start, jnp.zeros_like(halo_ref[...]), halo_ref[...])
    next_ref[HALO:, :] = n_ref[...]
    n_ext = next_ref[...]
    d_ff = act_ref.shape[1]

    def conv(col0):
        cols = pl.ds(col0, tf)
        u = jnp.dot(n_ext, wu_ref[:, cols], preferred_element_type=F32)
        out = cb_ref[:, cols]
        for tap in range(CONV_WIDTH):
            lo = HALO - (CONV_WIDTH - 1) + tap
            out = out + cw_ref[tap:tap + 1, cols] * u[lo:lo + tm]
        return out

    for c in range(d_ff // tf):
        gate = conv(c * tf)
        val = conv(d_ff + c * tf)
        act = 0.5 * gate * (1.0 + lax.erf(gate * (2.0 ** -0.5))) * val
        act_ref[:, c * tf:(c + 1) * tf] = act.astype(act_ref.dtype)

    h = h_ref[...] + jnp.dot(act_ref[...], wd_ref[...], preferred_element_type=F32)
    y = h * lax.rsqrt(jnp.mean(h * h, axis=-1, keepdims=True) + EPS)
    o_ref[...] = (y * gn_ref[...]).astype(o_ref.dtype)


def _ffn(n2d, h2d, w_up, conv_w, conv_b, w_down, gain, seq, tm, tf):
    t, d = n2d.shape
    d_ff = w_down.shape[0]
    halo_blocks = tm // HALO
    cb = conv_b.reshape(1, 2 * d_ff)
    once = lambda a: pl.BlockSpec(a.shape, lambda i: (0,) * a.ndim, pipeline_mode=pl.Buffered(1))
    return pl.pallas_call(
        functools.partial(_ffn_kernel, tm=tm, seq=seq, tf=tf),
        grid=(t // tm,),
        in_specs=[pl.BlockSpec((tm, d), lambda i: (i, 0)),
                  pl.BlockSpec((HALO, d), lambda i: (jnp.maximum(i * halo_blocks - 1, 0), 0)),
                  pl.BlockSpec((tm, d), lambda i: (i, 0)),
                  once(w_up), once(conv_w), once(cb), once(w_down),
                  pl.BlockSpec((1, d), lambda i: (0, 0))],
        out_specs=pl.BlockSpec((tm, d), lambda i: (i, 0)),
        out_shape=jax.ShapeDtypeStruct((t, d), F32),
        scratch_shapes=[pltpu.VMEM((HALO + tm, d), BF16), pltpu.VMEM((tm, d_ff), BF16)],
        compiler_params=_cparams(("parallel",)),
        name="conv_glu_ffn",
    )(n2d, n2d, h2d, w_up, conv_w, cb, w_down, gain.reshape(1, d))


class _Tiles(NamedTuple):
    rows: int
    wide_rows: int
    cols: int
    hgrn: int
    hgrn_sub: int
    query: int
    group: int


def _tiles(seq):
    fit = lambda pref, n=seq: pref if n % pref == 0 else n
    rows = fit(512)
    hgrn = fit(2 * rows)
    return _Tiles(rows=rows, wide_rows=fit(2 * rows), cols=256, hgrn=hgrn, hgrn_sub=4 if (seq // hgrn) % 4 == 0 else 1,
                  query=fit(4 * rows), group=fit(256, rows))


def kernel(x, norm_mix, w_in, fox_f_bias, hg_lb_logits, hg_norm, w_branch_a, w_branch_b, w_out, norm_ffn, w_up,
           conv_w, conv_b, w_down, norm_final):
    b, s, d = x.shape
    assert norm_mix.shape[0] == 1, "single-layer trunk"
    t = b * s
    tiles = _tiles(s)
    assert HG_HEADS * HG_DIM == d and FOX_HEADS * FOX_DIM == d, "the projection kernel walks equal-width segments"
    f_lo, f_hi = N_MAIN * d, N_MAIN * d + FOX_HEADS
    assert w_in.shape[-1] == f_hi + 2 * d
    w_main, w_forget, w_gates = (w_in[0, :, lo:hi].astype(BF16) for lo, hi in ((0, f_lo), (f_lo, f_hi), (f_hi, None)))

    head_order = jnp.argsort(fox_f_bias[0]).astype(jnp.int32)

    x2d = x.reshape(t, d)
    hq, hlf, hk, hv, hgate, fq, fk, v_aug, ga, gb, c2, c_pieces, lf_colmin = _in_proj(
        x2d, norm_mix[0], w_main, w_gates, w_forget, hg_lb_logits, fox_f_bias[0], s, tiles.rows, tiles.cols)

    r3 = lambda a: a.reshape(b, s, a.shape[-1])
    lf_min = lf_colmin[:, 0, :].reshape(b, s // tiles.hgrn, tiles.hgrn // tiles.rows, HG_HEADS, HG_DIM)
    lf_min = lf_min.min(axis=(2, 4)).transpose(0, 2, 1).reshape(b * HG_HEADS, s // tiles.hgrn)
    o_a = _hgrn(lf_min, r3(hq), r3(hlf), r3(hk), r3(hv), r3(hgate), hg_norm[0], tiles.hgrn, tiles.hgrn_sub)

    q_aug, k_aug, kv_stats = _fox_pack(r3(fq), r3(fk), r3(c_pieces), r3(c2), tiles.wide_rows, tiles.rows)
    kv_stats = kv_stats[..., head_order]
    kv_stats = kv_stats.transpose(2, 0, 3, 1).reshape(2, b * FOX_HEADS, s // tiles.rows)
    o_b = _fox_attention(head_order, q_aug, k_aug, v_aug, kv_stats[0], kv_stats[1], tiles.query, tiles.rows,
                          tiles.group)
    w_b = w_branch_b[0].reshape(FOX_HEADS, FOX_DIM, d)[head_order].reshape(d, d)

    h1, n_ffn = _merge(x2d, o_a.reshape(t, d), o_b.reshape(t, d), ga, gb, w_branch_a[0].astype(BF16),
                       w_b.astype(BF16), w_out[0].astype(BF16), norm_ffn[0], tiles.wide_rows, tiles.cols)
    out = _ffn(n_ffn, h1, w_up[0].astype(BF16), conv_w[0], conv_b[0], w_down[0].astype(BF16), norm_final,
               s, tiles.wide_rows, tiles.cols)
    return out.reshape(b, s, d)
```
